```python
import math
import jax, jax.numpy as jnp
from jax import lax
import numpy as np

D_MODEL = 1024
BATCH = 8
SEQ = 2048
DEPTH = 2
DEC_BATCH = 128
DEC_SEQ = 8
PAST_LEN = 16384
PAGE_SIZE = 128

RET_HEADS = 8
RET_DK = 64
RET_DV = 128
RET_QK = RET_HEADS * RET_DK
RET_V = RET_HEADS * RET_DV
RET_CHUNK = 128
ROPE_BASE = 10000.0
D_RNN = 1024
RNN_BLOCKS = 16
RNN_BS = D_RNN // RNN_BLOCKS
CONV_W = 4
RG_C = 8.0
IN_SIZES = (RET_QK, RET_QK, RET_V, RET_V, D_RNN, D_RNN, D_MODEL, D_MODEL)
N_IN = sum(IN_SIZES)
PEER_HEADS = 8
N_KEYS = 128
N_EXPERTS = N_KEYS * N_KEYS
PEER_DKEY = 256
PEER_TOPK = 16
PEER_BLOCK = 256
EPS = 1e-6

kernel_name = 'hybrid_retention_rglru_peer_step'


def rmsnorm(x, g):
    xf = x.astype(jnp.float32)
    y = xf * lax.rsqrt(jnp.mean(xf * xf, axis=-1, keepdims=True) + EPS)
    return (y * g.astype(jnp.float32)).astype(x.dtype)


def rotary(x, pos):
    half = x.shape[-1] // 2
    freq = ROPE_BASE ** (-jnp.arange(half, dtype=jnp.float32) / half)
    ang = pos[:, None] * freq[None, :]
    cos = jnp.cos(ang)[None, :, None, :]
    sin = jnp.sin(ang)[None, :, None, :]
    x1, x2 = x[..., :half], x[..., half:]
    return jnp.concatenate([x1 * cos - x2 * sin, x1 * sin + x2 * cos], axis=-1)


def retention(q, k, v, r0):
    b, t = q.shape[0], q.shape[1]
    c = RET_CHUNK if t % RET_CHUNK == 0 else t
    nc = t // c
    log_g = jnp.log1p(-(2.0 ** (-5.0 - jnp.arange(RET_HEADS, dtype=jnp.float32))))
    idx = jnp.arange(c, dtype=jnp.float32)
    diff = idx[:, None] - idx[None, :]
    mask = jnp.where(diff[None] >= 0, jnp.exp(jnp.maximum(diff, 0.0)[None] * log_g[:, None, None]), 0.0)
    qc = q.reshape(b, nc, c, RET_HEADS, RET_DK)
    kc = k.reshape(b, nc, c, RET_HEADS, RET_DK)
    vc = v.reshape(b, nc, c, RET_HEADS, RET_DV)
    s = jnp.einsum('bnjhd,bnmhd->bnhjm', qc, kc) * mask
    intra = jnp.einsum('bnhjm,bnmhe->bnjhe', s, vc)
    k_w = jnp.exp((c - 1 - idx)[:, None] * log_g[None, :])
    kv = jnp.einsum('bnmhd,bnmhe,mh->nbhde', kc, vc, k_w)
    g_c = jnp.exp(c * log_g)[None, :, None, None]

    def step(r, kv_n):
        return g_c * r + kv_n, r

    r_last, r_prev = lax.scan(step, r0, kv)
    q_w = jnp.exp((idx + 1.0)[:, None] * log_g[None, :])
    cross = jnp.einsum('bnjhd,nbhde,jh->bnjhe', qc, r_prev, q_w)
    o = (intra + cross).reshape(b, t, RET_HEADS, RET_DV)
    return o, r_last


def rglru_branch(xr, buf, h0, conv_w, conv_b, wa, ba, wx, bx, lam):
    b, t = xr.shape[0], xr.shape[1]
    f32 = jnp.float32
    xcat = jnp.concatenate([buf.astype(f32), xr], axis=1)
    cw = conv_w.astype(f32)
    xc = conv_b.astype(f32)[None, None, :]
    for w in range(CONV_W):
        xc = xc + xcat[:, w:w + t] * cw[w]
    new_buf = xcat[:, -(CONV_W - 1):]
    xb = xc.reshape(b, t, RNN_BLOCKS, RNN_BS)
    r = jax.nn.sigmoid(jnp.einsum('btnc,ncd->btnd', xb, wa.astype(f32)).reshape(b, t, D_RNN) + ba.astype(f32))
    i = jax.nn.sigmoid(jnp.einsum('btnc,ncd->btnd', xb, wx.astype(f32)).reshape(b, t, D_RNN) + bx.astype(f32))
    log_a = -RG_C * r * jax.nn.softplus(-lam.astype(f32))
    a = jnp.exp(log_a)
    u = jnp.sqrt(-jnp.expm1(2.0 * log_a)) * (i * xc)
    u = u.at[:, 0].add(a[:, 0] * h0)

    def comb(left, right):
        a1, b1 = left
        a2, b2 = right
        return a1 * a2, a2 * b1 + b2

    _, h = lax.associative_scan(comb, (a, u), axis=1)
    return h, h[:, -1], new_buf


def peer(xn, wq, keys, u_tab, v_tab):
    f32 = jnp.float32
    shp = xn.shape
    xf = xn.reshape(-1, D_MODEL)
    n = xf.shape[0]
    q = (xf @ wq).astype(f32).reshape(n, PEER_HEADS, 2, PEER_DKEY // 2)
    s = jnp.einsum('nhpd,hpkd->nhpk', q, keys.astype(f32))
    top_s, top_i = lax.top_k(s, PEER_TOPK)
    cand = top_s[:, :, 0, :, None] + top_s[:, :, 1, None, :]
    best_s, best_c = lax.top_k(cand.reshape(n, PEER_HEADS, PEER_TOPK * PEER_TOPK), PEER_TOPK)
    i1 = jnp.take_along_axis(top_i[:, :, 0], best_c // PEER_TOPK, axis=-1)
    i2 = jnp.take_along_axis(top_i[:, :, 1], best_c % PEER_TOPK, axis=-1)
    eidx = (i1 * N_KEYS + i2).reshape(n, PEER_HEADS * PEER_TOPK)
    gate = jax.nn.softmax(best_s, axis=-1).reshape(n, PEER_HEADS * PEER_TOPK)
    blk = min(PEER_BLOCK, n)
    nb = -(-n // blk)
    pad = nb * blk - n
    xp = jnp.pad(xf, ((0, pad), (0, 0))).reshape(nb, blk, D_MODEL)
    ep = jnp.pad(eidx, ((0, pad), (0, 0))).reshape(nb, blk, PEER_HEADS * PEER_TOPK)
    gp = jnp.pad(gate, ((0, pad), (0, 0))).reshape(nb, blk, PEER_HEADS * PEER_TOPK)

    def block(args):
        xb, eb, gb = args
        u = jnp.take(u_tab, eb, axis=0)
        hid = jax.nn.gelu(jnp.einsum('nd,nkd->nk', xb, u).astype(f32)) * gb
        v = jnp.take(v_tab, eb, axis=0)
        return jnp.einsum('nk,nkd->nd', hid.astype(xb.dtype), v)

    out = lax.map(block, (xp, ep, gp)).reshape(nb * blk, D_MODEL)[:n]
    return out.reshape(shp)


def trunk(x, r0, h0, buf0, pos0, norm1_g, norm2_g, normf_g, w_in, ret_gn_g, w_ret_out, conv_w, conv_b,
          rg_wa, rg_ba, rg_wx, rg_bx, rg_lambda, w_rnn_out, w_o, peer_wq, peer_keys, peer_u, peer_v):
    f32 = jnp.float32
    b, t = x.shape[0], x.shape[1]
    pos = pos0 + jnp.arange(t, dtype=f32)
    splits = np.cumsum(IN_SIZES)[:-1].tolist()
    new_r, new_h, new_buf = [], [], []
    for l in range(DEPTH):
        xn = rmsnorm(x, norm1_g[l])
        p = xn @ w_in[l]
        q, k, v, g_ret, xr, g_rnn, ga, gb = jnp.split(p, splits, axis=-1)
        qh = rotary(q.reshape(b, t, RET_HEADS, RET_DK).astype(f32), pos)
        kh = rotary(k.reshape(b, t, RET_HEADS, RET_DK).astype(f32), pos) * (RET_DK ** -0.5)
        vh = v.reshape(b, t, RET_HEADS, RET_DV).astype(f32)
        o, r_l = retention(qh, kh, vh, r0[l].astype(f32))
        mu = jnp.mean(o, axis=-1, keepdims=True)
        var = jnp.mean(jnp.square(o - mu), axis=-1, keepdims=True)
        o = ((o - mu) * lax.rsqrt(var + EPS)).reshape(b, t, RET_V) * ret_gn_g[l].astype(f32)
        ret_out = (jax.nn.silu(g_ret.astype(f32)) * o).astype(x.dtype) @ w_ret_out[l]
        h, h_l, buf_l = rglru_branch(xr.astype(f32), buf0[l], h0[l].astype(f32), conv_w[l], conv_b[l],
                                     rg_wa[l], rg_ba[l], rg_wx[l], rg_bx[l], rg_lambda[l])
        rnn_out = (h * jax.nn.gelu(g_rnn.astype(f32))).astype(x.dtype) @ w_rnn_out[l]
        merged = jax.nn.sigmoid(ga) * ret_out + jax.nn.sigmoid(gb) * rnn_out
        x = x + merged @ w_o[l]
        x = x + peer(rmsnorm(x, norm2_g[l]), peer_wq[l], peer_keys[l], peer_u[l], peer_v[l])
        new_r.append(r_l.astype(x.dtype))
        new_h.append(h_l.astype(x.dtype))
        new_buf.append(buf_l.astype(x.dtype))
    y = rmsnorm(x, normf_g)
    return y, jnp.stack(new_r), jnp.stack(new_h), jnp.stack(new_buf)


def setup_inputs(seed: int = 0) -> dict:
    key = jax.random.key(seed)
    ks = jax.random.split(key, 32)
    nrm = jax.random.normal
    f32 = jnp.float32
    lam_u = jax.random.uniform(ks[17], (DEPTH, D_RNN), f32, 0.9, 0.999)
    a_base = lam_u ** (1.0 / RG_C)
    rg_lambda = jnp.log(a_base) - jnp.log1p(-a_base)
    return {
        'x_prompt': nrm(ks[0], (BATCH, SEQ, D_MODEL), f32),
        'x_sample': nrm(ks[1], (DEC_BATCH, DEC_SEQ, D_MODEL), f32),
        'state_ret': 0.5 * nrm(ks[2], (DEPTH, DEC_BATCH, RET_HEADS, RET_DK, RET_DV), f32),
        'state_rnn': 0.5 * nrm(ks[3], (DEPTH, DEC_BATCH, D_RNN), f32),
        'state_conv': nrm(ks[4], (DEPTH, DEC_BATCH, CONV_W - 1, D_RNN), f32),
        'norm1_g': 1.0 + 0.05 * nrm(ks[5], (DEPTH, D_MODEL), f32),
        'norm2_g': 1.0 + 0.05 * nrm(ks[6], (DEPTH, D_MODEL), f32),
        'normf_g': 1.0 + 0.05 * nrm(ks[7], (D_MODEL,), f32),
        'w_in': nrm(ks[8], (DEPTH, D_MODEL, N_IN), f32) * D_MODEL ** -0.5,
        'ret_gn_g': 1.0 + 0.05 * nrm(ks[9], (DEPTH, RET_V), f32),
        'w_ret_out': nrm(ks[10], (DEPTH, RET_V, D_MODEL), f32) * RET_V ** -0.5,
        'conv_w': nrm(ks[11], (DEPTH, CONV_W, D_RNN), f32) * CONV_W ** -0.5,
        'conv_b': 0.01 * nrm(ks[12], (DEPTH, D_RNN), f32),
        'rg_wa': nrm(ks[13], (DEPTH, RNN_BLOCKS, RNN_BS, RNN_BS), f32) * RNN_BS ** -0.5,
        'rg_ba': 0.01 * nrm(ks[14], (DEPTH, D_RNN), f32),
        'rg_wx': nrm(ks[15], (DEPTH, RNN_BLOCKS, RNN_BS, RNN_BS), f32) * RNN_BS ** -0.5,
        'rg_bx': 0.01 * nrm(ks[16], (DEPTH, D_RNN), f32),
        'rg_lambda': rg_lambda,
        'w_rnn_out': nrm(ks[18], (DEPTH, D_RNN, D_MODEL), f32) * D_RNN ** -0.5,
        'w_o': nrm(ks[19], (DEPTH, D_MODEL, D_MODEL), f32) * D_MODEL ** -0.5,
        'peer_wq': nrm(ks[20], (DEPTH, D_MODEL, PEER_HEADS * PEER_DKEY), f32) * D_MODEL ** -0.5,
        'peer_keys': nrm(ks[21], (DEPTH, PEER_HEADS, 2, N_KEYS, PEER_DKEY // 2), f32) * (PEER_DKEY // 2) ** -0.5,
        'peer_u': nrm(ks[22], (DEPTH, N_EXPERTS, D_MODEL), f32) * D_MODEL ** -0.5,
        'peer_v': nrm(ks[23], (DEPTH, N_EXPERTS, D_MODEL), f32) * (PEER_HEADS * PEER_TOPK) ** -0.5,
    }


def reference(x_prompt, x_sample, state_ret, state_rnn, state_conv, norm1_g, norm2_g, normf_g, w_in,
              ret_gn_g, w_ret_out, conv_w, conv_b, rg_wa, rg_ba, rg_wx, rg_bx, rg_lambda, w_rnn_out, w_o,
              peer_wq, peer_keys, peer_u, peer_v):
    bp = x_prompt.shape[0]
    dt = x_prompt.dtype
    zr = jnp.zeros((DEPTH, bp, RET_HEADS, RET_DK, RET_DV), dt)
    zh = jnp.zeros((DEPTH, bp, D_RNN), dt)
    zc = jnp.zeros((DEPTH, bp, CONV_W - 1, D_RNN), dt)
    y_prompt, sr_p, sh_p, sc_p = trunk(x_prompt, zr, zh, zc, 0, norm1_g, norm2_g, normf_g, w_in, ret_gn_g,
                                       w_ret_out, conv_w, conv_b, rg_wa, rg_ba, rg_wx, rg_bx, rg_lambda,
                                       w_rnn_out, w_o, peer_wq, peer_keys, peer_u, peer_v)
    y_sample, sr_s, sh_s, sc_s = trunk(x_sample, state_ret, state_rnn, state_conv, PAST_LEN, norm1_g, norm2_g,
                                       normf_g, w_in, ret_gn_g, w_ret_out, conv_w, conv_b, rg_wa, rg_ba, rg_wx,
                                       rg_bx, rg_lambda, w_rnn_out, w_o, peer_wq, peer_keys, peer_u, peer_v)
    return (y_prompt, y_sample, sr_p, sh_p, sc_p, sr_s, sh_s, sc_s)
```

```python
import functools
import math

import jax
import jax.numpy as jnp
import numpy as np
from jax import lax
from jax.experimental import pallas as pl
from jax.experimental.pallas import tpu as pltpu

D_MODEL = 1024
DEPTH = 2
PAST_LEN = 16384
RET_HEADS = 8
RET_DK = 64
RET_DV = 128
RET_QK = RET_HEADS * RET_DK
RET_V = RET_HEADS * RET_DV
RET_CHUNK = 128
ROPE_BASE = 10000.0
D_RNN = 1024
RNN_BLOCKS = 16
RNN_BS = D_RNN // RNN_BLOCKS
CONV_W = 4
RG_C = 8.0
IN_SIZES = (RET_QK, RET_QK, RET_V, RET_V, D_RNN, D_RNN, D_MODEL, D_MODEL)
N_IN = sum(IN_SIZES)
PEER_HEADS = 8
N_KEYS = 128
N_EXPERTS = N_KEYS * N_KEYS
PEER_DKEY = 256
PEER_HALF = PEER_DKEY // 2
PEER_TOPK = 16
EPS = 1e-6

SUBLANES = 8
LANES = 128
VMEM_LIMIT = 56 * 1024 * 1024

F32 = jnp.float32
BF16 = jnp.bfloat16


def _params(*sem):
    return pltpu.CompilerParams(dimension_semantics=sem, vmem_limit_bytes=VMEM_LIMIT)


def _pick(n, prefs):
    for p in prefs:
        if n % p == 0:
            return p
    return n


def _norm_matmul_kernel(x_ref, g_ref, w_ref, y_ref, xn_scr):
    @pl.when(pl.program_id(1) == 0)
    def _():
        x = x_ref[...]
        ms = jnp.mean(x * x, axis=-1, keepdims=True)
        xn_scr[...] = ((x * lax.rsqrt(ms + EPS)) * g_ref[...]).astype(BF16)

    y_ref[...] = jnp.dot(xn_scr[...], w_ref[...], preferred_element_type=F32)


def norm_matmul(x, g, w_bf16):
    n, d = x.shape
    m = w_bf16.shape[1]
    tm = _pick(n, (1024, 512, 256, 128, 64, 32, 16, 8))
    tn = _pick(m, (1024, 512, 256, 128))
    return pl.pallas_call(
        _norm_matmul_kernel,
        grid=(n // tm, m // tn),
        in_specs=[
            pl.BlockSpec((tm, d), lambda i, j: (i, 0)),
            pl.BlockSpec((1, d), lambda i, j: (0, 0)),
            pl.BlockSpec((d, tn), lambda i, j: (0, j)),
        ],
        out_specs=pl.BlockSpec((tm, tn), lambda i, j: (i, j)),
        out_shape=jax.ShapeDtypeStruct((n, m), F32),
        scratch_shapes=[pltpu.VMEM((tm, d), BF16)],
        compiler_params=_params("parallel", "arbitrary"),
        name="norm_in_proj",
    )(x, g.reshape(1, d), w_bf16)


def _rot_half(x):
    n = x.shape[-1]
    half = RET_DK // 2
    fwd = pltpu.roll(x, half, axis=1)
    bwd = pltpu.roll(x, n - half, axis=1)
    lane = lax.broadcasted_iota(jnp.int32, x.shape, 1)
    return jnp.where((lane % RET_DK) < half, bwd, fwd)


def _retention_kernel(q_ref, k_ref, v_ref, g_ref, r0_ref, cos_ref, sin_ref, mask_ref, qw_ref, kw_ref,
                      gc_ref, gn_ref, o_ref, r_out_ref, r_scr):
    c = pl.program_id(1)

    @pl.when(c == 0)
    def _():
        r_scr[...] = r0_ref[0]

    cos = cos_ref[...]
    sin = sin_ref[...]
    q = q_ref[...]
    k = k_ref[...]
    qr = q * cos + _rot_half(q) * sin
    kr = (k * cos + _rot_half(k) * sin) * (RET_DK ** -0.5)
    qd = (qr * qw_ref[...]).astype(BF16)
    kd = (kr * kw_ref[...]).astype(BF16)
    qb = qr.astype(BF16)
    kb = kr.astype(BF16)
    v = v_ref[...].astype(BF16)
    g = g_ref[...]
    for h in range(RET_HEADS):
        ks = slice(h * RET_DK, (h + 1) * RET_DK)
        vs = slice(h * RET_DV, (h + 1) * RET_DV)
        vh = v[:, vs]
        s = lax.dot_general(qb[:, ks], kb[:, ks], (((1,), (1,)), ((), ())),
                            preferred_element_type=F32) * mask_ref[h]
        r_h = r_scr[h]
        o = jnp.dot(s.astype(BF16), vh, preferred_element_type=F32)
        o = o + jnp.dot(qd[:, ks], r_h.astype(BF16), preferred_element_type=F32)
        kv = lax.dot_general(kd[:, ks], vh, (((0,), (0,)), ((), ())), preferred_element_type=F32)
        r_scr[h] = gc_ref[h] * r_h + kv
        mu = jnp.mean(o, axis=-1, keepdims=True)
        var = jnp.mean(jnp.square(o - mu), axis=-1, keepdims=True)
        on = ((o - mu) * lax.rsqrt(var + EPS)) * gn_ref[:, vs]
        gh = g[:, vs]
        o_ref[:, vs] = ((gh * jax.nn.sigmoid(gh)) * on).astype(BF16)

    @pl.when(c == pl.num_programs(1) - 1)
    def _():
        r_out_ref[0] = r_scr[...]


def retention_path(p, row0, b, t, pos0, r0, gn_g):
    c = RET_CHUNK if t % RET_CHUNK == 0 else t
    nc = t // c
    blk0 = row0 // c
    assert row0 % c == 0
    log_g = jnp.log1p(-(2.0 ** (-5.0 - jnp.arange(RET_HEADS, dtype=F32))))
    idx = jnp.arange(c, dtype=F32)
    diff = idx[:, None] - idx[None, :]
    mask = jnp.where(diff[None] >= 0, jnp.exp(jnp.maximum(diff, 0.0)[None] * log_g[:, None, None]), 0.0)
    k_w = jnp.exp((c - 1 - idx)[:, None] * log_g[None, :])
    q_w = jnp.exp((idx + 1.0)[:, None] * log_g[None, :])
    g_c = jnp.exp(c * log_g)
    qw_tab = jnp.repeat(q_w, RET_DK, axis=1)
    kw_tab = jnp.repeat(k_w, RET_DK, axis=1)
    gc_tab = jnp.broadcast_to(g_c[:, None, None], (RET_HEADS, 1, RET_DV))
    half = RET_DK // 2
    pos = pos0 + jnp.arange(t, dtype=F32)
    freq = ROPE_BASE ** (-jnp.arange(half, dtype=F32) / half)
    ang = pos[:, None] * freq[None, :]
    cos_h = jnp.concatenate([jnp.cos(ang), jnp.cos(ang)], axis=1)
    sin_h = jnp.concatenate([-jnp.sin(ang), jnp.sin(ang)], axis=1)
    cos_tab = jnp.tile(cos_h, (1, RET_HEADS))
    sin_tab = jnp.tile(sin_h, (1, RET_HEADS))

    rows = lambda bi, ci: blk0 + bi * nc + ci
    o, r_new = pl.pallas_call(
        _retention_kernel,
        grid=(b, nc),
        in_specs=[
            pl.BlockSpec((c, RET_QK), lambda bi, ci: (rows(bi, ci), 0)),
            pl.BlockSpec((c, RET_QK), lambda bi, ci: (rows(bi, ci), 1)),
            pl.BlockSpec((c, RET_V), lambda bi, ci: (rows(bi, ci), 1)),
            pl.BlockSpec((c, RET_V), lambda bi, ci: (rows(bi, ci), 2)),
            pl.BlockSpec((1, RET_HEADS, RET_DK, RET_DV), lambda bi, ci: (bi, 0, 0, 0)),
            pl.BlockSpec((c, RET_QK), lambda bi, ci: (ci, 0)),
            pl.BlockSpec((c, RET_QK), lambda bi, ci: (ci, 0)),
            pl.BlockSpec((RET_HEADS, c, c), lambda bi, ci: (0, 0, 0)),
            pl.BlockSpec((c, RET_QK), lambda bi, ci: (0, 0)),
            pl.BlockSpec((c, RET_QK), lambda bi, ci: (0, 0)),
            pl.BlockSpec((RET_HEADS, 1, RET_DV), lambda bi, ci: (0, 0, 0)),
            pl.BlockSpec((1, RET_V), lambda bi, ci: (0, 0)),
        ],
        out_specs=[
            pl.BlockSpec((c, RET_V), lambda bi, ci: (bi * nc + ci, 0)),
            pl.BlockSpec((1, RET_HEADS, RET_DK, RET_DV), lambda bi, ci: (bi, 0, 0, 0)),
        ],
        out_shape=[
            jax.ShapeDtypeStruct((b * t, RET_V), BF16),
            jax.ShapeDtypeStruct((b, RET_HEADS, RET_DK, RET_DV), F32),
        ],
        scratch_shapes=[pltpu.VMEM((RET_HEADS, RET_DK, RET_DV), F32)],
        compiler_params=_params("parallel", "arbitrary"),
        name="retention",
    )(p, p, p, p, r0, cos_tab, sin_tab, mask, qw_tab, kw_tab, gc_tab, gn_g.reshape(1, RET_V))
    return o, r_new


def _gelu_tanh(x):
    return x * (0.5 * (1.0 + jnp.tanh(math.sqrt(2.0 / math.pi) * (x + 0.044715 * (x * x * x)))))


def _rglru_kernel(bsz, tc, xr_ref, g_ref, buf_ref, h0_ref, cw_ref, cb_ref, wa_ref, ba_ref, wx_ref, bx_ref,
                  lam_ref, o_ref, hl_ref, nb_ref, xcat_scr, a_scr, u_scr, hs_scr):
    rows = tc * bsz
    hist = (CONV_W - 1) * bsz
    step = pl.program_id(0)

    @pl.when(step == 0)
    def _():
        xcat_scr[pl.ds(rows, hist), :] = buf_ref[...]
        hs_scr[pl.ds(rows, bsz), :] = h0_ref[...]

    xcat_scr[pl.ds(0, hist), :] = xcat_scr[pl.ds(rows, hist), :]
    hs_scr[pl.ds(0, bsz), :] = hs_scr[pl.ds(rows, bsz), :]
    xcat_scr[pl.ds(hist, rows), :] = xr_ref[...]

    xc = cb_ref[...] + xcat_scr[pl.ds(0, rows), :] * cw_ref[0:1, :]
    for w in range(1, CONV_W):
        xc = xc + xcat_scr[pl.ds(w * bsz, rows), :] * cw_ref[w:w + 1, :]
    xcb = xc.astype(BF16)
    nblk = wa_ref.shape[0]
    wdt = wa_ref.shape[1]
    ra = jnp.concatenate(
        [jnp.dot(xcb[:, j * wdt:(j + 1) * wdt], wa_ref[j], preferred_element_type=F32) for j in range(nblk)],
        axis=1)
    ri = jnp.concatenate(
        [jnp.dot(xcb[:, j * wdt:(j + 1) * wdt], wx_ref[j], preferred_element_type=F32) for j in range(nblk)],
        axis=1)
    r = jax.nn.sigmoid(ra + ba_ref[...])
    i = jax.nn.sigmoid(ri + bx_ref[...])
    z = -lam_ref[...]
    softplus = jnp.maximum(z, 0.0) + jnp.log1p(jnp.exp(-jnp.abs(z)))
    log_a = (-RG_C * r) * softplus
    a = jnp.exp(log_a)
    one_m_a2 = -jnp.tanh(log_a) * (a * a + 1.0)
    a_scr[...] = a
    u_scr[...] = jnp.sqrt(one_m_a2) * (i * xc)

    def scan_step(t, carry):
        prev = hs_scr[pl.ds(pl.multiple_of(t * bsz, bsz), bsz), :]
        cur = pl.ds(pl.multiple_of(t * bsz, bsz), bsz)
        h = a_scr[cur, :] * prev + u_scr[cur, :]
        hs_scr[pl.ds(pl.multiple_of((t + 1) * bsz, bsz), bsz), :] = h
        return carry

    lax.fori_loop(0, tc, scan_step, 0)
    hs = hs_scr[pl.ds(bsz, rows), :]
    o_ref[...] = (hs * _gelu_tanh(g_ref[...])).astype(BF16)

    @pl.when(step == pl.num_programs(0) - 1)
    def _():
        hl_ref[...] = hs_scr[pl.ds(rows, bsz), :]
        nb_ref[...] = xcat_scr[pl.ds(rows, hist), :]


def rglru_path(xr_tm, g_tm, buf_tm, h0, b, t, conv_w, conv_b, wa4, ba, wx4, bx, lam):
    assert t >= CONV_W - 1
    rows_target = 256
    tc = max(1, min(t, rows_target // b))
    while t % tc:
        tc -= 1
    rows = tc * b
    hist = (CONV_W - 1) * b
    d = D_RNN
    nblk, wdt = wa4.shape[0], wa4.shape[1]
    const2 = lambda s: (0, 0)
    out, h_last, new_buf = pl.pallas_call(
        functools.partial(_rglru_kernel, b, tc),
        grid=(t // tc,),
        in_specs=[
            pl.BlockSpec((rows, d), lambda s: (s, 0)),
            pl.BlockSpec((rows, d), lambda s: (s, 0)),
            pl.BlockSpec((hist, d), const2),
            pl.BlockSpec((b, d), const2),
            pl.BlockSpec((CONV_W, d), const2),
            pl.BlockSpec((1, d), const2),
            pl.BlockSpec((nblk, wdt, wdt), lambda s: (0, 0, 0)),
            pl.BlockSpec((1, d), const2),
            pl.BlockSpec((nblk, wdt, wdt), lambda s: (0, 0, 0)),
            pl.BlockSpec((1, d), const2),
            pl.BlockSpec((1, d), const2),
        ],
        out_specs=[
            pl.BlockSpec((rows, d), lambda s: (s, 0)),
            pl.BlockSpec((b, d), const2),
            pl.BlockSpec((hist, d), const2),
        ],
        out_shape=[
            jax.ShapeDtypeStruct((t * b, d), BF16),
            jax.ShapeDtypeStruct((b, d), F32),
            jax.ShapeDtypeStruct((hist, d), F32),
        ],
        scratch_shapes=[
            pltpu.VMEM((rows + hist, d), F32),
            pltpu.VMEM((rows, d), F32),
            pltpu.VMEM((rows, d), F32),
            pltpu.VMEM((rows + b, d), F32),
        ],
        compiler_params=_params("arbitrary"),
        name="rglru",
    )(xr_tm, g_tm, buf_tm, h0, conv_w, conv_b.reshape(1, d), wa4, ba.reshape(1, d), wx4, bx.reshape(1, d),
      lam.reshape(1, d))
    return out, h_last, new_buf


def _block_diag_tiles(w, tile):
    nb, bs, _ = w.shape
    per = tile // bs
    w = w.reshape(nb // per, per, bs, bs)
    eye = jnp.eye(per, dtype=w.dtype)
    dense = jnp.einsum("gpcd,pq->gpcqd", w, eye).reshape(nb // per, tile, tile)
    return dense.astype(BF16)


def _merge_kernel(x_ref, ro_ref, rn_ref, ga_ref, gb_ref, wr_ref, wn_ref, wo_ref, y_ref):
    ret_out = jnp.dot(ro_ref[...], wr_ref[...], preferred_element_type=F32)
    rnn_out = jnp.dot(rn_ref[...], wn_ref[...], preferred_element_type=F32)
    merged = jax.nn.sigmoid(ga_ref[...]) * ret_out + jax.nn.sigmoid(gb_ref[...]) * rnn_out
    y_ref[...] = x_ref[...] + jnp.dot(merged.astype(BF16), wo_ref[...], preferred_element_type=F32)


def merge_proj(x, ret_g, rnn_g, p, w_ret_out, w_rnn_out, w_o):
    n, d = x.shape
    tm = _pick(n, (512, 256, 128, 64, 32, 16, 8))
    row = lambda i: (i, 0)
    const = lambda i: (0, 0)
    return pl.pallas_call(
        _merge_kernel,
        grid=(n // tm,),
        in_specs=[
            pl.BlockSpec((tm, d), row),
            pl.BlockSpec((tm, d), row),
            pl.BlockSpec((tm, d), row),
            pl.BlockSpec((tm, d), lambda i: (i, 5)),
            pl.BlockSpec((tm, d), lambda i: (i, 6)),
            pl.BlockSpec((d, d), const),
            pl.BlockSpec((d, d), const),
            pl.BlockSpec((d, d), const),
        ],
        out_specs=pl.BlockSpec((tm, d), row),
        out_shape=jax.ShapeDtypeStruct((n, d), F32),
        compiler_params=_params("parallel"),
        name="merge_out_proj",
    )(x, ret_g, rnn_g, p, p, w_ret_out, w_rnn_out, w_o)


def _peer_query_kernel(xt_ref, g_ref, wq_ref, xn_ref, q_ref):
    x = xt_ref[...]
    ms = jnp.mean(x * x, axis=0, keepdims=True)
    xn = ((x * lax.rsqrt(ms + EPS)) * g_ref[...]).astype(BF16)
    xn_ref[...] = xn
    q_ref[...] = jnp.dot(wq_ref[...], xn, preferred_element_type=F32)


def peer_query(xt, g, wq_t):
    d, n = xt.shape
    m = wq_t.shape[0]
    tn = _pick(n, (512, 256, 128))
    return pl.pallas_call(
        _peer_query_kernel,
        grid=(n // tn,),
        in_specs=[
            pl.BlockSpec((d, tn), lambda i: (0, i)),
            pl.BlockSpec((d, 1), lambda i: (0, 0)),
            pl.BlockSpec((m, d), lambda i: (0, 0)),
        ],
        out_specs=[pl.BlockSpec((d, tn), lambda i: (0, i)), pl.BlockSpec((m, tn), lambda i: (0, i))],
        out_shape=[jax.ShapeDtypeStruct((d, n), BF16), jax.ShapeDtypeStruct((m, n), F32)],
        compiler_params=_params("parallel"),
        name="peer_query",
    )(xt, g.reshape(d, 1), wq_t)


def _sort_pairs(n):
    def merge(lo, hi, r):
        step = r * 2
        if step < hi - lo:
            yield from merge(lo, hi, step)
            yield from merge(lo + r, hi, step)
            yield from [(i, i + r) for i in range(lo + r, hi - r, step)]
        else:
            yield (lo, lo + r)

    def sort(lo, hi):
        if hi - lo >= 1:
            mid = lo + (hi - lo) // 2
            yield from sort(lo, mid)
            yield from sort(mid + 1, hi)
            yield from merge(lo, hi, 1)

    return list(sort(0, n - 1))


_SORT16 = _sort_pairs(PEER_TOPK)


def _cmpx(vals, i, j):
    a, b = vals[i], vals[j]
    if b is None:
        return
    if a is None:
        vals[i], vals[j] = b, None
        return
    vals[i], vals[j] = jnp.maximum(a, b), jnp.minimum(a, b)


def _sort_desc(vals):
    vals = list(vals)
    for i, j in _SORT16:
        _cmpx(vals, i, j)
    return vals


def _merge_top(a, b):
    k = PEER_TOPK
    a = list(a) + [None] * (k - len(a))
    b = list(b) + [None] * (k - len(b))
    out = []
    for r in range(k):
        x, y = a[r], b[k - 1 - r]
        out.append(y if x is None else (x if y is None else jnp.maximum(x, y)))
    d = k // 2
    while d >= 1:
        for i in range(k):
            if not i & d:
                _cmpx(out, i, i + d)
        d //= 2
    return out


def _top_sorted(ref):
    groups = []
    for g0 in range(0, N_KEYS, PEER_TOPK):
        vals = [ref[pl.ds((g0 + j) * PEER_HEADS, PEER_HEADS), :] for j in range(PEER_TOPK)]
        groups.append(_sort_desc(vals))
    while len(groups) > 1:
        groups = [_merge_top(groups[i], groups[i + 1]) for i in range(0, len(groups), 2)]
    return groups[0]


def _peer_select_kernel(q_ref, k1_ref, k2_ref, th_ref, e1_ref, s2_ref, e2_ref, s1_scr, s2_scr, e2_scr):
    nh = PEER_HEADS
    half_rows = q_ref.shape[0] // 2
    tn = q_ref.shape[1]
    s1 = jnp.dot(k1_ref[...], q_ref[pl.ds(0, half_rows), :].astype(BF16), preferred_element_type=F32)
    s2 = jnp.dot(k2_ref[...], q_ref[pl.ds(half_rows, half_rows), :].astype(BF16), preferred_element_type=F32)
    for lt in range(tn // LANES):
        s1_scr[lt] = s1[:, lt * LANES:(lt + 1) * LANES]
        s2_scr[lt] = s2[:, lt * LANES:(lt + 1) * LANES]
    for lt in range(tn // LANES):
        lanes = pl.ds(lt * LANES, LANES)
        s1_t, s2_t, e2_t = s1_scr.at[lt], s2_scr.at[lt], e2_scr.at[lt]
        a = _top_sorted(s1_t)
        b = _top_sorted(s2_t)
        k = PEER_TOPK
        lists = []
        for j in range(1, k + 1):
            col = [a[r - 1] + b[j - 1] for r in range(j, k // j + 1)]
            row = [a[j - 1] + b[s - 1] for s in range(j + 1, k // j + 1)]
            if col:
                lists.append(col)
            if row:
                lists.append(row)
        top = lists[0]
        for other in lists[1:]:
            top = _merge_top(top, other)
        tau = top[k - 1]
        z = jnp.ones_like(tau)
        for r in range(1, k):
            z = z + jnp.exp(top[r] - top[0])
        zinv = 1.0 / z
        inf = jnp.full((nh, LANES), jnp.inf, F32)

        def per_key(kk, carry):
            rows = pl.ds(pl.multiple_of(kk * nh, nh), nh)
            s1k = s1_t[rows, :]
            th = inf
            for s in range(k):
                th = jnp.where(s1k + b[s] >= tau, b[s], th)
            th_ref[rows, lanes] = th
            e1_ref[rows, lanes] = jnp.exp(s1k - a[0])
            e2_t[rows, :] = jnp.exp(s2_t[rows, :] - b[0]) * zinv
            return carry

        lax.fori_loop(0, N_KEYS, per_key, 0)
        for h in range(nh):
            for kt in range(N_KEYS // SUBLANES):
                src = pl.ds(kt * SUBLANES * nh + h, SUBLANES, stride=nh)
                dst = pl.ds(h * N_KEYS + kt * SUBLANES, SUBLANES)
                s2_ref[dst, lanes] = s2_t[src, :]
                e2_ref[dst, lanes] = e2_t[src, :]


def peer_select(q_t, k1, k2):
    m, n = q_t.shape
    rows = PEER_HEADS * N_KEYS
    tn = _pick(n, (256, 128))
    tok = lambda i: (0, i)
    const = lambda i: (0, 0)
    out = jax.ShapeDtypeStruct((rows, n), F32)
    return pl.pallas_call(
        _peer_select_kernel,
        grid=(n // tn,),
        in_specs=[pl.BlockSpec((m, tn), tok), pl.BlockSpec(k1.shape, const), pl.BlockSpec(k2.shape, const)],
        out_specs=[pl.BlockSpec((rows, tn), tok)] * 4,
        out_shape=[out] * 4,
        scratch_shapes=[pltpu.VMEM((tn // LANES, rows, LANES), F32)] * 3,
        compiler_params=_params("parallel"),
        name="peer_select",
    )(q_t, k1, k2)


def _peer_dense_kernel(i1_per_blk, xt_ref, xn_ref, u_ref, vt_ref, th_ref, e1_ref, s2_ref, e2_ref, y_ref,
                       acc_scr, act_scr):
    j = pl.program_id(1)
    nh = PEER_HEADS

    @pl.when(j == 0)
    def _():
        acc_scr[...] = jnp.zeros_like(acc_scr)

    act_scr[...] = jnp.dot(u_ref[...], xn_ref[...], preferred_element_type=F32)
    tiles_per_i1 = N_KEYS // SUBLANES

    def per_i1(il, carry):
        hrows = pl.ds(pl.multiple_of(il * nh, nh), nh)
        th = th_ref[hrows, :]
        e1 = e1_ref[hrows, :]

        def per_tile(it, carry2):
            rows = pl.ds(pl.multiple_of((il * tiles_per_i1 + it) * SUBLANES, SUBLANES), SUBLANES)
            gate = jnp.zeros((SUBLANES, act_scr.shape[1]), F32)
            for h in range(nh):
                krows = pl.ds(pl.multiple_of(h * N_KEYS + it * SUBLANES, SUBLANES), SUBLANES)
                sel = jnp.where(s2_ref[krows, :] >= th[h:h + 1, :], e2_ref[krows, :], 0.0)
                gate = gate + sel * e1[h:h + 1, :]
            act_scr[rows, :] = _gelu_tanh(act_scr[rows, :]) * gate
            return carry2

        return lax.fori_loop(0, tiles_per_i1, per_tile, carry)

    lax.fori_loop(0, i1_per_blk, per_i1, 0)
    acc_scr[...] += jnp.dot(vt_ref[...], act_scr[...].astype(BF16), preferred_element_type=F32)

    @pl.when(j == pl.num_programs(1) - 1)
    def _():
        y_ref[...] = xt_ref[...] + acc_scr[...]


def peer_dense(xt, xn_t, u_bf16, vt_bf16, th, e1, s2, e2):
    d, n = xt.shape
    ne = u_bf16.shape[0]
    tn = _pick(n, (512, 256, 128))
    te = 1024
    i1_per_blk = te // N_KEYS
    tok = lambda i, j: (0, i)
    return pl.pallas_call(
        functools.partial(_peer_dense_kernel, i1_per_blk),
        grid=(n // tn, ne // te),
        in_specs=[
            pl.BlockSpec((d, tn), tok),
            pl.BlockSpec((d, tn), tok),
            pl.BlockSpec((te, d), lambda i, j: (j, 0)),
            pl.BlockSpec((d, te), lambda i, j: (0, j)),
            pl.BlockSpec((i1_per_blk * PEER_HEADS, tn), lambda i, j: (j, i)),
            pl.BlockSpec((i1_per_blk * PEER_HEADS, tn), lambda i, j: (j, i)),
            pl.BlockSpec((PEER_HEADS * N_KEYS, tn), tok),
            pl.BlockSpec((PEER_HEADS * N_KEYS, tn), tok),
        ],
        out_specs=pl.BlockSpec((d, tn), tok),
        out_shape=jax.ShapeDtypeStruct((d, n), F32),
        scratch_shapes=[pltpu.VMEM((d, tn), F32), pltpu.VMEM((te, tn), F32)],
        compiler_params=_params("parallel", "arbitrary"),
        name="peer_dense",
    )(xt, xn_t, u_bf16, vt_bf16, th, e1, s2, e2)


def _interleaved_keys(keys_p):
    nh, nk, half = keys_p.shape
    eye = jnp.eye(nh, dtype=keys_p.dtype)
    return jnp.einsum("hkd,hg->khgd", keys_p, eye).reshape(nk * nh, nh * half).astype(BF16)


def peer_layer(x, norm_g, wq, keys, u_tab, v_tab):
    n, d = x.shape
    xt = x.T
    wq_t = wq.reshape(d, PEER_HEADS, 2, PEER_HALF).transpose(2, 1, 3, 0).reshape(2 * PEER_HEADS * PEER_HALF, d)
    xn_t, q_t = peer_query(xt, norm_g, wq_t.astype(BF16))
    k1 = _interleaved_keys(keys[:, 0])
    k2 = _interleaved_keys(keys[:, 1])
    th, e1, s2, e2 = peer_select(q_t, k1, k2)
    y_t = peer_dense(xt, xn_t, u_tab.astype(BF16), v_tab.T.astype(BF16), th, e1, s2, e2)
    return y_t.T


def _rmsnorm_kernel(x_ref, g_ref, y_ref):
    x = x_ref[...]
    ms = jnp.mean(x * x, axis=-1, keepdims=True)
    y_ref[...] = (x * lax.rsqrt(ms + EPS)) * g_ref[...]


def final_norm(x, row0, rows, g):
    d = x.shape[1]
    tm = _pick(math.gcd(rows, row0) if row0 else rows, (512, 256, 128, 64, 32, 16, 8))
    blk0 = row0 // tm
    return pl.pallas_call(
        _rmsnorm_kernel,
        grid=(rows // tm,),
        in_specs=[pl.BlockSpec((tm, d), lambda i: (blk0 + i, 0)), pl.BlockSpec((1, d), lambda i: (0, 0))],
        out_specs=pl.BlockSpec((tm, d), lambda i: (i, 0)),
        out_shape=jax.ShapeDtypeStruct((rows, d), F32),
        compiler_params=_params("parallel"),
        name="final_norm",
    )(x, g.reshape(1, d))


def _to_time_major(p, row0, b, t, col):
    blk = lax.slice(p, (row0, col * D_RNN), (row0 + b * t, (col + 1) * D_RNN))
    return blk.reshape(b, t, D_RNN).transpose(1, 0, 2).reshape(t * b, D_RNN)


def _trunk(groups, norm1_g, norm2_g, normf_g, w_in, ret_gn_g, w_ret_out, conv_w, conv_b, rg_wa, rg_ba, rg_wx,
           rg_bx, rg_lambda, w_rnn_out, w_o, peer_wq, peer_keys, peer_u, peer_v):
    shapes = [(g[0].shape[0], g[0].shape[1]) for g in groups]
    x = jnp.concatenate([g[0].reshape(-1, D_MODEL) for g in groups], axis=0)
    row0s = np.cumsum([0] + [b * t for b, t in shapes]).tolist()
    states = [([], [], []) for _ in groups]
    for l in range(DEPTH):
        p = norm_matmul(x, norm1_g[l], w_in[l].astype(BF16))
        wa4 = _block_diag_tiles(rg_wa[l], 256)
        wx4 = _block_diag_tiles(rg_wx[l], 256)
        ret_parts, rnn_parts = [], []
        for gi, (xg, r0, h0, buf0, pos0) in enumerate(groups):
            b, t = shapes[gi]
            row0 = row0s[gi]
            o, r_new = retention_path(p, row0, b, t, pos0, r0[l], ret_gn_g[l])
            xr_tm = _to_time_major(p, row0, b, t, 3)
            g_tm = _to_time_major(p, row0, b, t, 4)
            buf_tm = buf0[l].transpose(1, 0, 2).reshape((CONV_W - 1) * b, D_RNN)
            hg, h_last, nb = rglru_path(xr_tm, g_tm, buf_tm, h0[l], b, t, conv_w[l], conv_b[l], wa4, rg_ba[l],
                                        wx4, rg_bx[l], rg_lambda[l])
            ret_parts.append(o)
            rnn_parts.append(hg.reshape(t, b, D_RNN).transpose(1, 0, 2).reshape(b * t, D_RNN))
            states[gi][0].append(r_new)
            states[gi][1].append(h_last)
            states[gi][2].append(nb.reshape(CONV_W - 1, b, D_RNN).transpose(1, 0, 2))
        ret_g = jnp.concatenate(ret_parts, axis=0)
        rnn_g = jnp.concatenate(rnn_parts, axis=0)
        x = merge_proj(x, ret_g, rnn_g, p, w_ret_out[l].astype(BF16), w_rnn_out[l].astype(BF16),
                       w_o[l].astype(BF16))
        x = peer_layer(x, norm2_g[l], peer_wq[l], peer_keys[l], peer_u[l], peer_v[l])
    outs = []
    for gi, (b, t) in enumerate(shapes):
        y = final_norm(x, row0s[gi], b * t, normf_g).reshape(b, t, D_MODEL)
        outs.append((y, jnp.stack(states[gi][0]), jnp.stack(states[gi][1]), jnp.stack(states[gi][2])))
    return outs


def kernel(x_prompt, x_sample, state_ret, state_rnn, state_conv, norm1_g, norm2_g, normf_g, w_in, ret_gn_g,
           w_ret_out, conv_w, conv_b, rg_wa, rg_ba, rg_wx, rg_bx, rg_lambda, w_rnn_out, w_o, peer_wq, peer_keys,
           peer_u, peer_v):
    bp = x_prompt.shape[0]
    dt = x_prompt.dtype
    zr = jnp.zeros((DEPTH, bp, RET_HEADS, RET_DK, RET_DV), dt)
    zh = jnp.zeros((DEPTH, bp, D_RNN), dt)
    zc = jnp.zeros((DEPTH, bp, CONV_W - 1, D_RNN), dt)
    groups = [(x_prompt, zr, zh, zc, 0.0), (x_sample, state_ret, state_rnn, state_conv, float(PAST_LEN))]
    (yp, rp, hp, cp), (ys, rs, hs, cs) = _trunk(
        groups, norm1_g, norm2_g, normf_g, w_in, ret_gn_g, w_ret_out, conv_w, conv_b, rg_wa, rg_ba, rg_wx, rg_bx,
        rg_lambda, w_rnn_out, w_o, peer_wq, peer_keys, peer_u, peer_v)
    return (yp, ys, rp, hp, cp, rs, hs, cs)
```

```python
import functools
import math

import jax
import jax.numpy as jnp
import numpy as np
from jax import lax
from jax.experimental import pallas as pl
from jax.experimental.pallas import tpu as pltpu

D_MODEL = 1024
DEPTH = 2
PAST_LEN = 16384
RET_HEADS = 8
RET_DK = 64
RET_DV = 128
RET_QK = RET_HEADS * RET_DK
RET_V = RET_HEADS * RET_DV
RET_CHUNK = 128
ROPE_BASE = 10000.0
D_RNN = 1024
RNN_BLOCKS = 16
RNN_BS = D_RNN // RNN_BLOCKS
CONV_W = 4
RG_C = 8.0
IN_SIZES = (RET_QK, RET_QK, RET_V, RET_V, D_RNN, D_RNN, D_MODEL, D_MODEL)
N_IN = sum(IN_SIZES)
PEER_HEADS = 8
N_KEYS = 128
N_EXPERTS = N_KEYS * N_KEYS
PEER_DKEY = 256
PEER_HALF = PEER_DKEY // 2
PEER_TOPK = 16
EPS = 1e-6

SUBLANES = 8
LANES = 128
VMEM_LIMIT = 56 * 1024 * 1024

F32 = jnp.float32
BF16 = jnp.bfloat16


def _params(*sem, flags=None):
    return pltpu.CompilerParams(dimension_semantics=sem, vmem_limit_bytes=VMEM_LIMIT, flags=flags)


def _pick(n, prefs):
    for p in prefs:
        if n % p == 0:
            return p
    return n


def _norm_matmul_kernel(x_ref, g_ref, w_ref, y_ref, xn_scr):
    @pl.when(pl.program_id(1) == 0)
    def _():
        x = x_ref[...]
        ms = jnp.mean(x * x, axis=-1, keepdims=True)
        xn_scr[...] = ((x * lax.rsqrt(ms + EPS)) * g_ref[...]).astype(BF16)

    y_ref[...] = jnp.dot(xn_scr[...], w_ref[...], preferred_element_type=F32)


def norm_matmul(x, g, w_bf16):
    n, d = x.shape
    m = w_bf16.shape[1]
    tm = _pick(n, (1024, 512, 256, 128, 64, 32, 16, 8))
    tn = _pick(m, (1024, 512, 256, 128))
    return pl.pallas_call(
        _norm_matmul_kernel,
        grid=(n // tm, m // tn),
        in_specs=[
            pl.BlockSpec((tm, d), lambda i, j: (i, 0)),
            pl.BlockSpec((1, d), lambda i, j: (0, 0)),
            pl.BlockSpec((d, tn), lambda i, j: (0, j)),
        ],
        out_specs=pl.BlockSpec((tm, tn), lambda i, j: (i, j)),
        out_shape=jax.ShapeDtypeStruct((n, m), F32),
        scratch_shapes=[pltpu.VMEM((tm, d), BF16)],
        compiler_params=_params("parallel", "arbitrary"),
        name="norm_in_proj",
    )(x, g.reshape(1, d), w_bf16)


def _rot_half(x):
    n = x.shape[-1]
    half = RET_DK // 2
    fwd = pltpu.roll(x, half, axis=1)
    bwd = pltpu.roll(x, n - half, axis=1)
    lane = lax.broadcasted_iota(jnp.int32, x.shape, 1)
    return jnp.where((lane % RET_DK) < half, bwd, fwd)


def _retention_kernel(q_ref, k_ref, v_ref, g_ref, r0_ref, cos_ref, sin_ref, mask_ref, qw_ref, kw_ref,
                      gc_ref, gn_ref, o_ref, r_out_ref, r_scr):
    c = pl.program_id(1)

    @pl.when(c == 0)
    def _():
        r_scr[...] = r0_ref[0]

    cos = cos_ref[...]
    sin = sin_ref[...]
    q = q_ref[...]
    k = k_ref[...]
    qr = q * cos + _rot_half(q) * sin
    kr = (k * cos + _rot_half(k) * sin) * (RET_DK ** -0.5)
    qd = (qr * qw_ref[...]).astype(BF16)
    kd = (kr * kw_ref[...]).astype(BF16)
    qb = qr.astype(BF16)
    kb = kr.astype(BF16)
    v = v_ref[...].astype(BF16)
    g = g_ref[...]
    for h in range(RET_HEADS):
        ks = slice(h * RET_DK, (h + 1) * RET_DK)
        vs = slice(h * RET_DV, (h + 1) * RET_DV)
        vh = v[:, vs]
        s = lax.dot_general(qb[:, ks], kb[:, ks], (((1,), (1,)), ((), ())),
                            preferred_element_type=F32) * mask_ref[h]
        r_h = r_scr[h]
        o = jnp.dot(s.astype(BF16), vh, preferred_element_type=F32)
        o = o + jnp.dot(qd[:, ks], r_h.astype(BF16), preferred_element_type=F32)
        kv = lax.dot_general(kd[:, ks], vh, (((0,), (0,)), ((), ())), preferred_element_type=F32)
        r_scr[h] = gc_ref[h] * r_h + kv
        mu = jnp.mean(o, axis=-1, keepdims=True)
        var = jnp.mean(jnp.square(o - mu), axis=-1, keepdims=True)
        on = ((o - mu) * lax.rsqrt(var + EPS)) * gn_ref[:, vs]
        gh = g[:, vs]
        o_ref[:, vs] = ((gh * jax.nn.sigmoid(gh)) * on).astype(BF16)

    @pl.when(c == pl.num_programs(1) - 1)
    def _():
        r_out_ref[0] = r_scr[...]


def retention_path(p, row0, b, t, pos0, r0, gn_g):
    c = RET_CHUNK if t % RET_CHUNK == 0 else t
    nc = t // c
    blk0 = row0 // c
    assert row0 % c == 0
    log_g = jnp.log1p(-(2.0 ** (-5.0 - jnp.arange(RET_HEADS, dtype=F32))))
    idx = jnp.arange(c, dtype=F32)
    diff = idx[:, None] - idx[None, :]
    mask = jnp.where(diff[None] >= 0, jnp.exp(jnp.maximum(diff, 0.0)[None] * log_g[:, None, None]), 0.0)
    k_w = jnp.exp((c - 1 - idx)[:, None] * log_g[None, :])
    q_w = jnp.exp((idx + 1.0)[:, None] * log_g[None, :])
    g_c = jnp.exp(c * log_g)
    qw_tab = jnp.repeat(q_w, RET_DK, axis=1)
    kw_tab = jnp.repeat(k_w, RET_DK, axis=1)
    gc_tab = jnp.broadcast_to(g_c[:, None, None], (RET_HEADS, 1, RET_DV))
    half = RET_DK // 2
    pos = pos0 + jnp.arange(t, dtype=F32)
    freq = ROPE_BASE ** (-jnp.arange(half, dtype=F32) / half)
    ang = pos[:, None] * freq[None, :]
    cos_h = jnp.concatenate([jnp.cos(ang), jnp.cos(ang)], axis=1)
    sin_h = jnp.concatenate([-jnp.sin(ang), jnp.sin(ang)], axis=1)
    cos_tab = jnp.tile(cos_h, (1, RET_HEADS))
    sin_tab = jnp.tile(sin_h, (1, RET_HEADS))

    rows = lambda bi, ci: blk0 + bi * nc + ci
    o, r_new = pl.pallas_call(
        _retention_kernel,
        grid=(b, nc),
        in_specs=[
            pl.BlockSpec((c, RET_QK), lambda bi, ci: (rows(bi, ci), 0)),
            pl.BlockSpec((c, RET_QK), lambda bi, ci: (rows(bi, ci), 1)),
            pl.BlockSpec((c, RET_V), lambda bi, ci: (rows(bi, ci), 1)),
            pl.BlockSpec((c, RET_V), lambda bi, ci: (rows(bi, ci), 2)),
            pl.BlockSpec((1, RET_HEADS, RET_DK, RET_DV), lambda bi, ci: (bi, 0, 0, 0)),
            pl.BlockSpec((c, RET_QK), lambda bi, ci: (ci, 0)),
            pl.BlockSpec((c, RET_QK), lambda bi, ci: (ci, 0)),
            pl.BlockSpec((RET_HEADS, c, c), lambda bi, ci: (0, 0, 0)),
            pl.BlockSpec((c, RET_QK), lambda bi, ci: (0, 0)),
            pl.BlockSpec((c, RET_QK), lambda bi, ci: (0, 0)),
            pl.BlockSpec((RET_HEADS, 1, RET_DV), lambda bi, ci: (0, 0, 0)),
            pl.BlockSpec((1, RET_V), lambda bi, ci: (0, 0)),
        ],
        out_specs=[
            pl.BlockSpec((c, RET_V), lambda bi, ci: (bi * nc + ci, 0)),
            pl.BlockSpec((1, RET_HEADS, RET_DK, RET_DV), lambda bi, ci: (bi, 0, 0, 0)),
        ],
        out_shape=[
            jax.ShapeDtypeStruct((b * t, RET_V), BF16),
            jax.ShapeDtypeStruct((b, RET_HEADS, RET_DK, RET_DV), F32),
        ],
        scratch_shapes=[pltpu.VMEM((RET_HEADS, RET_DK, RET_DV), F32)],
        compiler_params=_params("parallel", "arbitrary"),
        name="retention",
    )(p, p, p, p, r0, cos_tab, sin_tab, mask, qw_tab, kw_tab, gc_tab, gn_g.reshape(1, RET_V))
    return o, r_new


def _gelu_tanh(x):
    return x * (0.5 * (1.0 + jnp.tanh(math.sqrt(2.0 / math.pi) * (x + 0.044715 * (x * x * x)))))


def _rglru_kernel(bsz, tc, xr_ref, g_ref, buf_ref, h0_ref, cw_ref, cb_ref, wa_ref, ba_ref, wx_ref, bx_ref,
                  lam_ref, o_ref, hl_ref, nb_ref, xcat_scr, a_scr, u_scr, hs_scr):
    rows = tc * bsz
    hist = (CONV_W - 1) * bsz
    step = pl.program_id(0)

    @pl.when(step == 0)
    def _():
        xcat_scr[pl.ds(rows, hist), :] = buf_ref[...]
        hs_scr[pl.ds(rows, bsz), :] = h0_ref[...]

    xcat_scr[pl.ds(0, hist), :] = xcat_scr[pl.ds(rows, hist), :]
    hs_scr[pl.ds(0, bsz), :] = hs_scr[pl.ds(rows, bsz), :]
    xcat_scr[pl.ds(hist, rows), :] = xr_ref[...]

    xc = cb_ref[...] + xcat_scr[pl.ds(0, rows), :] * cw_ref[0:1, :]
    for w in range(1, CONV_W):
        xc = xc + xcat_scr[pl.ds(w * bsz, rows), :] * cw_ref[w:w + 1, :]
    xcb = xc.astype(BF16)
    nblk = wa_ref.shape[0]
    wdt = wa_ref.shape[1]
    ra = jnp.concatenate(
        [jnp.dot(xcb[:, j * wdt:(j + 1) * wdt], wa_ref[j], preferred_element_type=F32) for j in range(nblk)],
        axis=1)
    ri = jnp.concatenate(
        [jnp.dot(xcb[:, j * wdt:(j + 1) * wdt], wx_ref[j], preferred_element_type=F32) for j in range(nblk)],
        axis=1)
    r = jax.nn.sigmoid(ra + ba_ref[...])
    i = jax.nn.sigmoid(ri + bx_ref[...])
    z = -lam_ref[...]
    softplus = jnp.maximum(z, 0.0) + jnp.log1p(jnp.exp(-jnp.abs(z)))
    log_a = (-RG_C * r) * softplus
    a = jnp.exp(log_a)
    one_m_a2 = -jnp.tanh(log_a) * (a * a + 1.0)
    a_scr[...] = a
    u_scr[...] = jnp.sqrt(one_m_a2) * (i * xc)

    def scan_step(t, carry):
        prev = hs_scr[pl.ds(pl.multiple_of(t * bsz, bsz), bsz), :]
        cur = pl.ds(pl.multiple_of(t * bsz, bsz), bsz)
        h = a_scr[cur, :] * prev + u_scr[cur, :]
        hs_scr[pl.ds(pl.multiple_of((t + 1) * bsz, bsz), bsz), :] = h
        return carry

    lax.fori_loop(0, tc, scan_step, 0)
    hs = hs_scr[pl.ds(bsz, rows), :]
    o_ref[...] = (hs * _gelu_tanh(g_ref[...])).astype(BF16)

    @pl.when(step == pl.num_programs(0) - 1)
    def _():
        hl_ref[...] = hs_scr[pl.ds(rows, bsz), :]
        nb_ref[...] = xcat_scr[pl.ds(rows, hist), :]


def rglru_path(xr_tm, g_tm, buf_tm, h0, b, t, conv_w, conv_b, wa4, ba, wx4, bx, lam):
    assert t >= CONV_W - 1
    rows_target = 256
    tc = max(1, min(t, rows_target // b))
    while t % tc:
        tc -= 1
    rows = tc * b
    hist = (CONV_W - 1) * b
    d = D_RNN
    nblk, wdt = wa4.shape[0], wa4.shape[1]
    const2 = lambda s: (0, 0)
    out, h_last, new_buf = pl.pallas_call(
        functools.partial(_rglru_kernel, b, tc),
        grid=(t // tc,),
        in_specs=[
            pl.BlockSpec((rows, d), lambda s: (s, 0)),
            pl.BlockSpec((rows, d), lambda s: (s, 0)),
            pl.BlockSpec((hist, d), const2),
            pl.BlockSpec((b, d), const2),
            pl.BlockSpec((CONV_W, d), const2),
            pl.BlockSpec((1, d), const2),
            pl.BlockSpec((nblk, wdt, wdt), lambda s: (0, 0, 0)),
            pl.BlockSpec((1, d), const2),
            pl.BlockSpec((nblk, wdt, wdt), lambda s: (0, 0, 0)),
            pl.BlockSpec((1, d), const2),
            pl.BlockSpec((1, d), const2),
        ],
        out_specs=[
            pl.BlockSpec((rows, d), lambda s: (s, 0)),
            pl.BlockSpec((b, d), const2),
            pl.BlockSpec((hist, d), const2),
        ],
        out_shape=[
            jax.ShapeDtypeStruct((t * b, d), BF16),
            jax.ShapeDtypeStruct((b, d), F32),
            jax.ShapeDtypeStruct((hist, d), F32),
        ],
        scratch_shapes=[
            pltpu.VMEM((rows + hist, d), F32),
            pltpu.VMEM((rows, d), F32),
            pltpu.VMEM((rows, d), F32),
            pltpu.VMEM((rows + b, d), F32),
        ],
        compiler_params=_params("arbitrary"),
        name="rglru",
    )(xr_tm, g_tm, buf_tm, h0, conv_w, conv_b.reshape(1, d), wa4, ba.reshape(1, d), wx4, bx.reshape(1, d),
      lam.reshape(1, d))
    return out, h_last, new_buf


def _block_diag_tiles(w, tile):
    nb, bs, _ = w.shape
    per = tile // bs
    w = w.reshape(nb // per, per, bs, bs)
    eye = jnp.eye(per, dtype=w.dtype)
    dense = jnp.einsum("gpcd,pq->gpcqd", w, eye).reshape(nb // per, tile, tile)
    return dense.astype(BF16)


def _merge_kernel(x_ref, ro_ref, rn_ref, ga_ref, gb_ref, wr_ref, wn_ref, wo_ref, y_ref):
    ret_out = jnp.dot(ro_ref[...], wr_ref[...], preferred_element_type=F32)
    rnn_out = jnp.dot(rn_ref[...], wn_ref[...], preferred_element_type=F32)
    merged = jax.nn.sigmoid(ga_ref[...]) * ret_out + jax.nn.sigmoid(gb_ref[...]) * rnn_out
    y_ref[...] = x_ref[...] + jnp.dot(merged.astype(BF16), wo_ref[...], preferred_element_type=F32)


def merge_proj(x, ret_g, rnn_g, p, w_ret_out, w_rnn_out, w_o):
    n, d = x.shape
    tm = _pick(n, (512, 256, 128, 64, 32, 16, 8))
    row = lambda i: (i, 0)
    const = lambda i: (0, 0)
    return pl.pallas_call(
        _merge_kernel,
        grid=(n // tm,),
        in_specs=[
            pl.BlockSpec((tm, d), row),
            pl.BlockSpec((tm, d), row),
            pl.BlockSpec((tm, d), row),
            pl.BlockSpec((tm, d), lambda i: (i, 5)),
            pl.BlockSpec((tm, d), lambda i: (i, 6)),
            pl.BlockSpec((d, d), const),
            pl.BlockSpec((d, d), const),
            pl.BlockSpec((d, d), const),
        ],
        out_specs=pl.BlockSpec((tm, d), row),
        out_shape=jax.ShapeDtypeStruct((n, d), F32),
        compiler_params=_params("parallel"),
        name="merge_out_proj",
    )(x, ret_g, rnn_g, p, p, w_ret_out, w_rnn_out, w_o)


def _peer_query_kernel(xt_ref, g_ref, wq_ref, xn_ref, q_ref):
    x = xt_ref[...]
    ms = jnp.mean(x * x, axis=0, keepdims=True)
    xn = ((x * lax.rsqrt(ms + EPS)) * g_ref[...]).astype(BF16)
    xn_ref[...] = xn
    q_ref[...] = jnp.dot(wq_ref[...], xn, preferred_element_type=F32)


def peer_query(xt, g, wq_t):
    d, n = xt.shape
    m = wq_t.shape[0]
    tn = _pick(n, (512, 256, 128))
    return pl.pallas_call(
        _peer_query_kernel,
        grid=(n // tn,),
        in_specs=[
            pl.BlockSpec((d, tn), lambda i: (0, i)),
            pl.BlockSpec((d, 1), lambda i: (0, 0)),
            pl.BlockSpec((m, d), lambda i: (0, 0)),
        ],
        out_specs=[pl.BlockSpec((d, tn), lambda i: (0, i)), pl.BlockSpec((m, tn), lambda i: (0, i))],
        out_shape=[jax.ShapeDtypeStruct((d, n), BF16), jax.ShapeDtypeStruct((m, n), F32)],
        compiler_params=_params("parallel"),
        name="peer_query",
    )(xt, g.reshape(d, 1), wq_t)


def _sort_pairs(n):
    def merge(lo, hi, r):
        step = r * 2
        if step < hi - lo:
            yield from merge(lo, hi, step)
            yield from merge(lo + r, hi, step)
            yield from [(i, i + r) for i in range(lo + r, hi - r, step)]
        else:
            yield (lo, lo + r)

    def sort(lo, hi):
        if hi - lo >= 1:
            mid = lo + (hi - lo) // 2
            yield from sort(lo, mid)
            yield from sort(mid + 1, hi)
            yield from merge(lo, hi, 1)

    return list(sort(0, n - 1))


_SORT16 = _sort_pairs(PEER_TOPK)


def _cmpx(vals, i, j):
    a, b = vals[i], vals[j]
    if b is None:
        return
    if a is None:
        vals[i], vals[j] = b, None
        return
    vals[i], vals[j] = jnp.maximum(a, b), jnp.minimum(a, b)


def _sort_desc(vals):
    vals = list(vals)
    for i, j in _SORT16:
        _cmpx(vals, i, j)
    return vals


def _merge_top(a, b):
    k = PEER_TOPK
    a = list(a) + [None] * (k - len(a))
    b = list(b) + [None] * (k - len(b))
    out = []
    for r in range(k):
        x, y = a[r], b[k - 1 - r]
        out.append(y if x is None else (x if y is None else jnp.maximum(x, y)))
    d = k // 2
    while d >= 1:
        for i in range(k):
            if not i & d:
                _cmpx(out, i, i + d)
        d //= 2
    return out


def _top_sorted(ref):
    groups = []
    for g0 in range(0, N_KEYS, PEER_TOPK):
        vals = [ref[pl.ds((g0 + j) * PEER_HEADS, PEER_HEADS), :] for j in range(PEER_TOPK)]
        groups.append(_sort_desc(vals))
    while len(groups) > 1:
        groups = [_merge_top(groups[i], groups[i + 1]) for i in range(0, len(groups), 2)]
    return groups[0]


def _peer_select_kernel(q_ref, k1_ref, k2_ref, c1_ref, e1_ref, r2_ref, e2_ref, s1_scr, s2_scr, r2_scr, e2_scr):
    nh = PEER_HEADS
    half_rows = q_ref.shape[0] // 2
    tn = q_ref.shape[1]
    s1 = jnp.dot(k1_ref[...], q_ref[pl.ds(0, half_rows), :].astype(BF16), preferred_element_type=F32)
    s2 = jnp.dot(k2_ref[...], q_ref[pl.ds(half_rows, half_rows), :].astype(BF16), preferred_element_type=F32)
    for lt in range(tn // LANES):
        s1_scr[lt] = s1[:, lt * LANES:(lt + 1) * LANES]
        s2_scr[lt] = s2[:, lt * LANES:(lt + 1) * LANES]
    for lt in range(tn // LANES):
        lanes = pl.ds(lt * LANES, LANES)
        s1_t, s2_t, r2_t, e2_t = s1_scr.at[lt], s2_scr.at[lt], r2_scr.at[lt], e2_scr.at[lt]
        a = _top_sorted(s1_t)
        b = _top_sorted(s2_t)
        k = PEER_TOPK
        lists = []
        for j in range(1, k + 1):
            col = [a[r - 1] + b[j - 1] for r in range(j, k // j + 1)]
            row = [a[j - 1] + b[s - 1] for s in range(j + 1, k // j + 1)]
            if col:
                lists.append(col)
            if row:
                lists.append(row)
        top = lists[0]
        for other in lists[1:]:
            top = _merge_top(top, other)
        tau = top[k - 1]
        z = jnp.ones_like(tau)
        for r in range(1, k):
            z = z + jnp.exp(top[r] - top[0])
        zinv = 1.0 / z
        inf = jnp.full((nh, LANES), jnp.inf, F32)
        phi = []
        for s in range(1, k + 1):
            p = inf
            for r in range(1, k // s + 1):
                p = jnp.where(a[r - 1] + b[s - 1] >= tau, a[r - 1], p)
            phi.append(p)

        def per_key(kk, carry):
            rows = pl.ds(pl.multiple_of(kk * nh, nh), nh)
            s1k = s1_t[rows, :]
            s2k = s2_t[rows, :]
            cnt = jnp.zeros((nh, LANES), F32)
            for s in range(k):
                cnt = jnp.where(s1k >= phi[s], float(s + 1), cnt)
            rank = jnp.full((nh, LANES), float(k + 1), F32)
            for s in range(k - 1, -1, -1):
                rank = jnp.where(s2k >= b[s], float(s + 1), rank)
            c1_ref[rows, lanes] = cnt
            e1_ref[rows, lanes] = jnp.exp(s1k - a[0])
            r2_t[rows, :] = rank
            e2_t[rows, :] = jnp.exp(s2k - b[0]) * zinv
            return carry

        lax.fori_loop(0, N_KEYS, per_key, 0)
        pack = 2 * SUBLANES
        for h in range(nh):
            for kt in range(N_KEYS // pack):
                lo = pl.ds(kt * pack * nh + h, SUBLANES, stride=nh)
                hi = pl.ds((kt * pack + SUBLANES) * nh + h, SUBLANES, stride=nh)
                dst = pl.ds(h * N_KEYS + kt * pack, pack)
                r2_ref[dst, lanes] = jnp.concatenate([r2_t[lo, :], r2_t[hi, :]], axis=0).astype(BF16)
                e2_ref[dst, lanes] = jnp.concatenate([e2_t[lo, :], e2_t[hi, :]], axis=0).astype(BF16)


def peer_select(q_t, k1, k2):
    m, n = q_t.shape
    rows = PEER_HEADS * N_KEYS
    tn = _pick(n, (256, 128))
    tok = lambda i: (0, i)
    const = lambda i: (0, 0)
    return pl.pallas_call(
        _peer_select_kernel,
        grid=(n // tn,),
        in_specs=[pl.BlockSpec((m, tn), tok), pl.BlockSpec(k1.shape, const), pl.BlockSpec(k2.shape, const)],
        out_specs=[pl.BlockSpec((rows, tn), tok)] * 4,
        out_shape=[jax.ShapeDtypeStruct((rows, n), F32), jax.ShapeDtypeStruct((rows, n), F32),
                   jax.ShapeDtypeStruct((rows, n), BF16), jax.ShapeDtypeStruct((rows, n), BF16)],
        scratch_shapes=[pltpu.VMEM((tn // LANES, rows, LANES), F32)] * 4,
        compiler_params=_params("parallel"),
        name="peer_select",
    )(q_t, k1, k2)


PEER_EXPERT_BLOCK = 1024


def _peer_dense_kernel(xt_ref, xn_ref, u_ref, vt_ref, c1_ref, e1_ref, r2_ref, e2_ref, y_ref,
                       acc_scr, h_scr, act_scr, bc_scr):
    j = pl.program_id(1)
    nh = PEER_HEADS
    tn = xn_ref.shape[1]
    pack = 2 * SUBLANES
    i1_per_blk = u_ref.shape[0] // N_KEYS

    @pl.when(j == 0)
    def _():
        acc_scr[...] = jnp.zeros_like(acc_scr)

    h_scr[...] = jnp.dot(u_ref[...], xn_ref[...], preferred_element_type=F32)

    def per_i1(il, carry):
        erows = pl.ds(pl.multiple_of(il * N_KEYS, N_KEYS), N_KEYS)
        hrows = pl.ds(pl.multiple_of(il * nh, nh), nh)
        for lc in range(tn // LANES):
            lanes = pl.ds(lc * LANES, LANES)
            c1p = c1_ref[hrows, lanes] + 1.0
            e1 = e1_ref[hrows, lanes]
            for h in range(nh):
                bc_scr[lc, h] = jnp.broadcast_to(c1p[h:h + 1, :], (pack, LANES)).astype(BF16)
                bc_scr[lc, nh + h] = jnp.broadcast_to(e1[h:h + 1, :], (pack, LANES)).astype(BF16)
            act_scr[erows, lanes] = _gelu_tanh(h_scr[erows, lanes]).astype(BF16)
            c1b = [bc_scr[lc, h] for h in range(nh)]
            e1b = [bc_scr[lc, nh + h] for h in range(nh)]
            zero = jnp.zeros((pack, LANES), BF16)
            for it in range(N_KEYS // pack):
                rows = pl.ds(pl.multiple_of(il * N_KEYS + it * pack, pack), pack)
                gate = None
                for h in range(nh):
                    krows = pl.ds(h * N_KEYS + it * pack, pack)
                    sel = jnp.minimum(jnp.maximum(c1b[h] - r2_ref[krows, lanes], zero), e2_ref[krows, lanes])
                    term = sel * e1b[h]
                    gate = term if gate is None else gate + term
                act_scr[rows, lanes] = act_scr[rows, lanes] * gate
        return carry

    lax.fori_loop(0, i1_per_blk, per_i1, 0)
    acc_scr[...] += jnp.dot(vt_ref[...], act_scr[...], preferred_element_type=F32)

    @pl.when(j == pl.num_programs(1) - 1)
    def _():
        y_ref[...] = xt_ref[...] + acc_scr[...]


def peer_dense(xt, xn_t, u_bf16, vt_bf16, c1, e1, r2, e2):
    d, n = xt.shape
    ne = u_bf16.shape[0]
    tn = _pick(n, (512, 256, 128))
    te = PEER_EXPERT_BLOCK
    i1_per_blk = te // N_KEYS
    tok = lambda i, j: (0, i)
    return pl.pallas_call(
        _peer_dense_kernel,
        grid=(n // tn, ne // te),
        in_specs=[
            pl.BlockSpec((d, tn), tok),
            pl.BlockSpec((d, tn), tok),
            pl.BlockSpec((te, d), lambda i, j: (j, 0)),
            pl.BlockSpec((d, te), lambda i, j: (0, j)),
            pl.BlockSpec((i1_per_blk * PEER_HEADS, tn), lambda i, j: (j, i)),
            pl.BlockSpec((i1_per_blk * PEER_HEADS, tn), lambda i, j: (j, i)),
            pl.BlockSpec((PEER_HEADS * N_KEYS, tn), tok),
            pl.BlockSpec((PEER_HEADS * N_KEYS, tn), tok),
        ],
        out_specs=pl.BlockSpec((d, tn), tok),
        out_shape=jax.ShapeDtypeStruct((d, n), F32),
        scratch_shapes=[pltpu.VMEM((d, tn), F32), pltpu.VMEM((te, tn), F32), pltpu.VMEM((te, tn), BF16),
                        pltpu.VMEM((tn // LANES, 2 * PEER_HEADS, 2 * SUBLANES, LANES), BF16)],
        compiler_params=_params("parallel", "arbitrary"),
        name="peer_dense",
    )(xt, xn_t, u_bf16, vt_bf16, c1, e1, r2, e2)


def _interleaved_keys(keys_p):
    nh, nk, half = keys_p.shape
    eye = jnp.eye(nh, dtype=keys_p.dtype)
    return jnp.einsum("hkd,hg->khgd", keys_p, eye).reshape(nk * nh, nh * half).astype(BF16)


def peer_layer(x, norm_g, wq, keys, u_tab, v_tab):
    n, d = x.shape
    xt = x.T
    wq_t = wq.reshape(d, PEER_HEADS, 2, PEER_HALF).transpose(2, 1, 3, 0).reshape(2 * PEER_HEADS * PEER_HALF, d)
    xn_t, q_t = peer_query(xt, norm_g, wq_t.astype(BF16))
    k1 = _interleaved_keys(keys[:, 0])
    k2 = _interleaved_keys(keys[:, 1])
    c1, e1, r2, e2 = peer_select(q_t, k1, k2)
    y_t = peer_dense(xt, xn_t, u_tab.astype(BF16), v_tab.T.astype(BF16), c1, e1, r2, e2)
    return y_t.T


def _rmsnorm_kernel(x_ref, g_ref, y_ref):
    x = x_ref[...]
    ms = jnp.mean(x * x, axis=-1, keepdims=True)
    y_ref[...] = (x * lax.rsqrt(ms + EPS)) * g_ref[...]


def final_norm(x, row0, rows, g):
    d = x.shape[1]
    tm = _pick(math.gcd(rows, row0) if row0 else rows, (512, 256, 128, 64, 32, 16, 8))
    blk0 = row0 // tm
    return pl.pallas_call(
        _rmsnorm_kernel,
        grid=(rows // tm,),
        in_specs=[pl.BlockSpec((tm, d), lambda i: (blk0 + i, 0)), pl.BlockSpec((1, d), lambda i: (0, 0))],
        out_specs=pl.BlockSpec((tm, d), lambda i: (i, 0)),
        out_shape=jax.ShapeDtypeStruct((rows, d), F32),
        compiler_params=_params("parallel"),
        name="final_norm",
    )(x, g.reshape(1, d))


def _to_time_major(p, row0, b, t, col):
    blk = lax.slice(p, (row0, col * D_RNN), (row0 + b * t, (col + 1) * D_RNN))
    return blk.reshape(b, t, D_RNN).transpose(1, 0, 2).reshape(t * b, D_RNN)


def _trunk(groups, norm1_g, norm2_g, normf_g, w_in, ret_gn_g, w_ret_out, conv_w, conv_b, rg_wa, rg_ba, rg_wx,
           rg_bx, rg_lambda, w_rnn_out, w_o, peer_wq, peer_keys, peer_u, peer_v):
    shapes = [(g[0].shape[0], g[0].shape[1]) for g in groups]
    x = jnp.concatenate([g[0].reshape(-1, D_MODEL) for g in groups], axis=0)
    row0s = np.cumsum([0] + [b * t for b, t in shapes]).tolist()
    states = [([], [], []) for _ in groups]
    for l in range(DEPTH):
        p = norm_matmul(x, norm1_g[l], w_in[l].astype(BF16))
        wa4 = _block_diag_tiles(rg_wa[l], 256)
        wx4 = _block_diag_tiles(rg_wx[l], 256)
        ret_parts, rnn_parts = [], []
        for gi, (xg, r0, h0, buf0, pos0) in enumerate(groups):
            b, t = shapes[gi]
            row0 = row0s[gi]
            o, r_new = retention_path(p, row0, b, t, pos0, r0[l], ret_gn_g[l])
            xr_tm = _to_time_major(p, row0, b, t, 3)
            g_tm = _to_time_major(p, row0, b, t, 4)
            buf_tm = buf0[l].transpose(1, 0, 2).reshape((CONV_W - 1) * b, D_RNN)
            hg, h_last, nb = rglru_path(xr_tm, g_tm, buf_tm, h0[l], b, t, conv_w[l], conv_b[l], wa4, rg_ba[l],
                                        wx4, rg_bx[l], rg_lambda[l])
            ret_parts.append(o)
            rnn_parts.append(hg.reshape(t, b, D_RNN).transpose(1, 0, 2).reshape(b * t, D_RNN))
            states[gi][0].append(r_new)
            states[gi][1].append(h_last)
            states[gi][2].append(nb.reshape(CONV_W - 1, b, D_RNN).transpose(1, 0, 2))
        ret_g = jnp.concatenate(ret_parts, axis=0)
        rnn_g = jnp.concatenate(rnn_parts, axis=0)
        x = merge_proj(x, ret_g, rnn_g, p, w_ret_out[l].astype(BF16), w_rnn_out[l].astype(BF16),
                       w_o[l].astype(BF16))
        x = peer_layer(x, norm2_g[l], peer_wq[l], peer_keys[l], peer_u[l], peer_v[l])
    outs = []
    for gi, (b, t) in enumerate(shapes):
        y = final_norm(x, row0s[gi], b * t, normf_g).reshape(b, t, D_MODEL)
        outs.append((y, jnp.stack(states[gi][0]), jnp.stack(states[gi][1]), jnp.stack(states[gi][2])))
    return outs


def kernel(x_prompt, x_sample, state_ret, state_rnn, state_conv, norm1_g, norm2_g, normf_g, w_in, ret_gn_g,
           w_ret_out, conv_w, conv_b, rg_wa, rg_ba, rg_wx, rg_bx, rg_lambda, w_rnn_out, w_o, peer_wq, peer_keys,
           peer_u, peer_v):
    bp = x_prompt.shape[0]
    dt = x_prompt.dtype
    zr = jnp.zeros((DEPTH, bp, RET_HEADS, RET_DK, RET_DV), dt)
    zh = jnp.zeros((DEPTH, bp, D_RNN), dt)
    zc = jnp.zeros((DEPTH, bp, CONV_W - 1, D_RNN), dt)
    groups = [(x_prompt, zr, zh, zc, 0.0), (x_sample, state_ret, state_rnn, state_conv, float(PAST_LEN))]
    (yp, rp, hp, cp), (ys, rs, hs, cs) = _trunk(
        groups, norm1_g, norm2_g, normf_g, w_in, ret_gn_g, w_ret_out, conv_w, conv_b, rg_wa, rg_ba, rg_wx, rg_bx,
        rg_lambda, w_rnn_out, w_o, peer_wq, peer_keys, peer_u, peer_v)
    return (yp, ys, rp, hp, cp, rs, hs, cs)
```

```python
import functools
import math

import jax
import jax.numpy as jnp
import numpy as np
from jax import lax
from jax.experimental import pallas as pl
from jax.experimental.pallas import tpu as pltpu

D_MODEL = 1024
DEPTH = 2
PAST_LEN = 16384
RET_HEADS = 8
RET_DK = 64
RET_DV = 128
RET_QK = RET_HEADS * RET_DK
RET_V = RET_HEADS * RET_DV
RET_CHUNK = 128
ROPE_BASE = 10000.0
D_RNN = 1024
RNN_BLOCKS = 16
RNN_BS = D_RNN // RNN_BLOCKS
CONV_W = 4
RG_C = 8.0
IN_SIZES = (RET_QK, RET_QK, RET_V, RET_V, D_RNN, D_RNN, D_MODEL, D_MODEL)
N_IN = sum(IN_SIZES)
PEER_HEADS = 8
N_KEYS = 128
N_EXPERTS = N_KEYS * N_KEYS
PEER_DKEY = 256
PEER_HALF = PEER_DKEY // 2
PEER_TOPK = 16
EPS = 1e-6

SUBLANES = 8
LANES = 128
VMEM_LIMIT = 56 * 1024 * 1024

F32 = jnp.float32
BF16 = jnp.bfloat16


def _params(*sem, flags=None):
    return pltpu.CompilerParams(dimension_semantics=sem, vmem_limit_bytes=VMEM_LIMIT, flags=flags)


def _pick(n, prefs):
    for p in prefs:
        if n % p == 0:
            return p
    return n


def _norm_matmul_kernel(x_ref, g_ref, w_ref, y_ref, xn_scr):
    @pl.when(pl.program_id(1) == 0)
    def _():
        x = x_ref[...]
        ms = jnp.mean(x * x, axis=-1, keepdims=True)
        xn_scr[...] = ((x * lax.rsqrt(ms + EPS)) * g_ref[...]).astype(BF16)

    y_ref[...] = jnp.dot(xn_scr[...], w_ref[...], preferred_element_type=F32)


def norm_matmul(x, g, w_bf16):
    n, d = x.shape
    m = w_bf16.shape[1]
    tm = _pick(n, (1024, 512, 256, 128, 64, 32, 16, 8))
    tn = _pick(m, (1024, 512, 256, 128))
    return pl.pallas_call(
        _norm_matmul_kernel,
        grid=(n // tm, m // tn),
        in_specs=[
            pl.BlockSpec((tm, d), lambda i, j: (i, 0)),
            pl.BlockSpec((1, d), lambda i, j: (0, 0)),
            pl.BlockSpec((d, tn), lambda i, j: (0, j)),
        ],
        out_specs=pl.BlockSpec((tm, tn), lambda i, j: (i, j)),
        out_shape=jax.ShapeDtypeStruct((n, m), F32),
        scratch_shapes=[pltpu.VMEM((tm, d), BF16)],
        compiler_params=_params("parallel", "arbitrary"),
        name="norm_in_proj",
    )(x, g.reshape(1, d), w_bf16)


def _rot_half(x):
    n = x.shape[-1]
    half = RET_DK // 2
    fwd = pltpu.roll(x, half, axis=1)
    bwd = pltpu.roll(x, n - half, axis=1)
    lane = lax.broadcasted_iota(jnp.int32, x.shape, 1)
    return jnp.where((lane % RET_DK) < half, bwd, fwd)


def _retention_kernel(q_ref, k_ref, v_ref, g_ref, r0_ref, cos_ref, sin_ref, mask_ref, qw_ref, kw_ref,
                      gc_ref, gn_ref, o_ref, r_out_ref, r_scr):
    c = pl.program_id(1)

    @pl.when(c == 0)
    def _():
        r_scr[...] = r0_ref[0]

    cos = cos_ref[...]
    sin = sin_ref[...]
    q = q_ref[...]
    k = k_ref[...]
    qr = q * cos + _rot_half(q) * sin
    kr = (k * cos + _rot_half(k) * sin) * (RET_DK ** -0.5)
    qd = (qr * qw_ref[...]).astype(BF16)
    kd = (kr * kw_ref[...]).astype(BF16)
    qb = qr.astype(BF16)
    kb = kr.astype(BF16)
    v = v_ref[...].astype(BF16)
    g = g_ref[...]
    for h in range(RET_HEADS):
        ks = slice(h * RET_DK, (h + 1) * RET_DK)
        vs = slice(h * RET_DV, (h + 1) * RET_DV)
        vh = v[:, vs]
        s = lax.dot_general(qb[:, ks], kb[:, ks], (((1,), (1,)), ((), ())),
                            preferred_element_type=F32) * mask_ref[h]
        r_h = r_scr[h]
        o = jnp.dot(s.astype(BF16), vh, preferred_element_type=F32)
        o = o + jnp.dot(qd[:, ks], r_h.astype(BF16), preferred_element_type=F32)
        kv = lax.dot_general(kd[:, ks], vh, (((0,), (0,)), ((), ())), preferred_element_type=F32)
        r_scr[h] = gc_ref[h] * r_h + kv
        mu = jnp.mean(o, axis=-1, keepdims=True)
        var = jnp.mean(jnp.square(o - mu), axis=-1, keepdims=True)
        on = ((o - mu) * lax.rsqrt(var + EPS)) * gn_ref[:, vs]
        gh = g[:, vs]
        o_ref[:, vs] = ((gh * jax.nn.sigmoid(gh)) * on).astype(BF16)

    @pl.when(c == pl.num_programs(1) - 1)
    def _():
        r_out_ref[0] = r_scr[...]


def retention_path(p, row0, b, t, pos0, r0, gn_g):
    c = RET_CHUNK if t % RET_CHUNK == 0 else t
    nc = t // c
    blk0 = row0 // c
    assert row0 % c == 0
    log_g = jnp.log1p(-(2.0 ** (-5.0 - jnp.arange(RET_HEADS, dtype=F32))))
    idx = jnp.arange(c, dtype=F32)
    diff = idx[:, None] - idx[None, :]
    mask = jnp.where(diff[None] >= 0, jnp.exp(jnp.maximum(diff, 0.0)[None] * log_g[:, None, None]), 0.0)
    k_w = jnp.exp((c - 1 - idx)[:, None] * log_g[None, :])
    q_w = jnp.exp((idx + 1.0)[:, None] * log_g[None, :])
    g_c = jnp.exp(c * log_g)
    qw_tab = jnp.repeat(q_w, RET_DK, axis=1)
    kw_tab = jnp.repeat(k_w, RET_DK, axis=1)
    gc_tab = jnp.broadcast_to(g_c[:, None, None], (RET_HEADS, 1, RET_DV))
    half = RET_DK // 2
    pos = pos0 + jnp.arange(t, dtype=F32)
    freq = ROPE_BASE ** (-jnp.arange(half, dtype=F32) / half)
    ang = pos[:, None] * freq[None, :]
    cos_h = jnp.concatenate([jnp.cos(ang), jnp.cos(ang)], axis=1)
    sin_h = jnp.concatenate([-jnp.sin(ang), jnp.sin(ang)], axis=1)
    cos_tab = jnp.tile(cos_h, (1, RET_HEADS))
    sin_tab = jnp.tile(sin_h, (1, RET_HEADS))

    rows = lambda bi, ci: blk0 + bi * nc + ci
    o, r_new = pl.pallas_call(
        _retention_kernel,
        grid=(b, nc),
        in_specs=[
            pl.BlockSpec((c, RET_QK), lambda bi, ci: (rows(bi, ci), 0)),
            pl.BlockSpec((c, RET_QK), lambda bi, ci: (rows(bi, ci), 1)),
            pl.BlockSpec((c, RET_V), lambda bi, ci: (rows(bi, ci), 1)),
            pl.BlockSpec((c, RET_V), lambda bi, ci: (rows(bi, ci), 2)),
            pl.BlockSpec((1, RET_HEADS, RET_DK, RET_DV), lambda bi, ci: (bi, 0, 0, 0)),
            pl.BlockSpec((c, RET_QK), lambda bi, ci: (ci, 0)),
            pl.BlockSpec((c, RET_QK), lambda bi, ci: (ci, 0)),
            pl.BlockSpec((RET_HEADS, c, c), lambda bi, ci: (0, 0, 0)),
            pl.BlockSpec((c, RET_QK), lambda bi, ci: (0, 0)),
            pl.BlockSpec((c, RET_QK), lambda bi, ci: (0, 0)),
            pl.BlockSpec((RET_HEADS, 1, RET_DV), lambda bi, ci: (0, 0, 0)),
            pl.BlockSpec((1, RET_V), lambda bi, ci: (0, 0)),
        ],
        out_specs=[
            pl.BlockSpec((c, RET_V), lambda bi, ci: (bi * nc + ci, 0)),
            pl.BlockSpec((1, RET_HEADS, RET_DK, RET_DV), lambda bi, ci: (bi, 0, 0, 0)),
        ],
        out_shape=[
            jax.ShapeDtypeStruct((b * t, RET_V), BF16),
            jax.ShapeDtypeStruct((b, RET_HEADS, RET_DK, RET_DV), F32),
        ],
        scratch_shapes=[pltpu.VMEM((RET_HEADS, RET_DK, RET_DV), F32)],
        compiler_params=_params("parallel", "arbitrary"),
        name="retention",
    )(p, p, p, p, r0, cos_tab, sin_tab, mask, qw_tab, kw_tab, gc_tab, gn_g.reshape(1, RET_V))
    return o, r_new


GELU_C0 = math.sqrt(2.0 / math.pi)
GELU_C1 = GELU_C0 * 0.044715


def _gelu_tanh(x):
    return x * (0.5 * (1.0 + jnp.tanh(math.sqrt(2.0 / math.pi) * (x + 0.044715 * (x * x * x)))))


def _rglru_kernel(bsz, tc, xr_ref, g_ref, buf_ref, h0_ref, cw_ref, cb_ref, wa_ref, ba_ref, wx_ref, bx_ref,
                  lam_ref, o_ref, hl_ref, nb_ref, xcat_scr, a_scr, u_scr, hs_scr):
    rows = tc * bsz
    hist = (CONV_W - 1) * bsz
    step = pl.program_id(0)

    @pl.when(step == 0)
    def _():
        xcat_scr[pl.ds(rows, hist), :] = buf_ref[...]
        hs_scr[pl.ds(rows, bsz), :] = h0_ref[...]

    xcat_scr[pl.ds(0, hist), :] = xcat_scr[pl.ds(rows, hist), :]
    hs_scr[pl.ds(0, bsz), :] = hs_scr[pl.ds(rows, bsz), :]
    xcat_scr[pl.ds(hist, rows), :] = xr_ref[...]

    xc = cb_ref[...] + xcat_scr[pl.ds(0, rows), :] * cw_ref[0:1, :]
    for w in range(1, CONV_W):
        xc = xc + xcat_scr[pl.ds(w * bsz, rows), :] * cw_ref[w:w + 1, :]
    xcb = xc.astype(BF16)
    nblk = wa_ref.shape[0]
    wdt = wa_ref.shape[1]
    ra = jnp.concatenate(
        [jnp.dot(xcb[:, j * wdt:(j + 1) * wdt], wa_ref[j], preferred_element_type=F32) for j in range(nblk)],
        axis=1)
    ri = jnp.concatenate(
        [jnp.dot(xcb[:, j * wdt:(j + 1) * wdt], wx_ref[j], preferred_element_type=F32) for j in range(nblk)],
        axis=1)
    r = jax.nn.sigmoid(ra + ba_ref[...])
    i = jax.nn.sigmoid(ri + bx_ref[...])
    z = -lam_ref[...]
    softplus = jnp.maximum(z, 0.0) + jnp.log1p(jnp.exp(-jnp.abs(z)))
    log_a = (-RG_C * r) * softplus
    a = jnp.exp(log_a)
    one_m_a2 = -jnp.tanh(log_a) * (a * a + 1.0)
    a_scr[...] = a
    u_scr[...] = jnp.sqrt(one_m_a2) * (i * xc)

    def scan_step(t, carry):
        prev = hs_scr[pl.ds(pl.multiple_of(t * bsz, bsz), bsz), :]
        cur = pl.ds(pl.multiple_of(t * bsz, bsz), bsz)
        h = a_scr[cur, :] * prev + u_scr[cur, :]
        hs_scr[pl.ds(pl.multiple_of((t + 1) * bsz, bsz), bsz), :] = h
        return carry

    lax.fori_loop(0, tc, scan_step, 0)
    hs = hs_scr[pl.ds(bsz, rows), :]
    o_ref[...] = (hs * _gelu_tanh(g_ref[...])).astype(BF16)

    @pl.when(step == pl.num_programs(0) - 1)
    def _():
        hl_ref[...] = hs_scr[pl.ds(rows, bsz), :]
        nb_ref[...] = xcat_scr[pl.ds(rows, hist), :]


def rglru_path(xr_tm, g_tm, buf_tm, h0, b, t, conv_w, conv_b, wa4, ba, wx4, bx, lam):
    assert t >= CONV_W - 1
    rows_target = 256
    tc = max(1, min(t, rows_target // b))
    while t % tc:
        tc -= 1
    rows = tc * b
    hist = (CONV_W - 1) * b
    d = D_RNN
    nblk, wdt = wa4.shape[0], wa4.shape[1]
    const2 = lambda s: (0, 0)
    out, h_last, new_buf = pl.pallas_call(
        functools.partial(_rglru_kernel, b, tc),
        grid=(t // tc,),
        in_specs=[
            pl.BlockSpec((rows, d), lambda s: (s, 0)),
            pl.BlockSpec((rows, d), lambda s: (s, 0)),
            pl.BlockSpec((hist, d), const2),
            pl.BlockSpec((b, d), const2),
            pl.BlockSpec((CONV_W, d), const2),
            pl.BlockSpec((1, d), const2),
            pl.BlockSpec((nblk, wdt, wdt), lambda s: (0, 0, 0)),
            pl.BlockSpec((1, d), const2),
            pl.BlockSpec((nblk, wdt, wdt), lambda s: (0, 0, 0)),
            pl.BlockSpec((1, d), const2),
            pl.BlockSpec((1, d), const2),
        ],
        out_specs=[
            pl.BlockSpec((rows, d), lambda s: (s, 0)),
            pl.BlockSpec((b, d), const2),
            pl.BlockSpec((hist, d), const2),
        ],
        out_shape=[
            jax.ShapeDtypeStruct((t * b, d), BF16),
            jax.ShapeDtypeStruct((b, d), F32),
            jax.ShapeDtypeStruct((hist, d), F32),
        ],
        scratch_shapes=[
            pltpu.VMEM((rows + hist, d), F32),
            pltpu.VMEM((rows, d), F32),
            pltpu.VMEM((rows, d), F32),
            pltpu.VMEM((rows + b, d), F32),
        ],
        compiler_params=_params("arbitrary"),
        name="rglru",
    )(xr_tm, g_tm, buf_tm, h0, conv_w, conv_b.reshape(1, d), wa4, ba.reshape(1, d), wx4, bx.reshape(1, d),
      lam.reshape(1, d))
    return out, h_last, new_buf


def _block_diag_tiles(w, tile):
    nb, bs, _ = w.shape
    per = tile // bs
    w = w.reshape(nb // per, per, bs, bs)
    eye = jnp.eye(per, dtype=w.dtype)
    dense = jnp.einsum("gpcd,pq->gpcqd", w, eye).reshape(nb // per, tile, tile)
    return dense.astype(BF16)


def _merge_kernel(blocks_a, x_ref, roa_ref, rob_ref, rna_ref, rnb_ref, ga_ref, gb_ref, wr_ref, wn_ref, wo_ref,
                  yt_ref):
    first = pl.program_id(0) < blocks_a
    ro = jnp.where(first, roa_ref[...], rob_ref[...])
    rn = jnp.where(first, rna_ref[...], rnb_ref[...])
    ret_out = jnp.dot(ro, wr_ref[...], preferred_element_type=F32)
    rnn_out = jnp.dot(rn, wn_ref[...], preferred_element_type=F32)
    merged = jax.nn.sigmoid(ga_ref[...]) * ret_out + jax.nn.sigmoid(gb_ref[...]) * rnn_out
    y = x_ref[...] + jnp.dot(merged.astype(BF16), wo_ref[...], preferred_element_type=F32)
    yt_ref[...] = y.T


def merge_proj(x, ret_parts, rnn_parts, p, w_ret_out, w_rnn_out, w_o):
    n, d = x.shape
    (ret_a, ret_b), (rnn_a, rnn_b) = ret_parts, rnn_parts
    rows_a, rows_b = ret_a.shape[0], ret_b.shape[0]
    tm = _pick(math.gcd(rows_a, rows_b), (512, 256, 128))
    blocks_a = rows_a // tm
    row = lambda i: (i, 0)
    row_a = lambda i: (jnp.minimum(i, blocks_a - 1), 0)
    row_b = lambda i: (jnp.maximum(i - blocks_a, 0), 0)
    const = lambda i: (0, 0)
    return pl.pallas_call(
        functools.partial(_merge_kernel, blocks_a),
        grid=(n // tm,),
        in_specs=[
            pl.BlockSpec((tm, d), row),
            pl.BlockSpec((tm, d), row_a),
            pl.BlockSpec((tm, d), row_b),
            pl.BlockSpec((tm, d), row_a),
            pl.BlockSpec((tm, d), row_b),
            pl.BlockSpec((tm, d), lambda i: (i, 5)),
            pl.BlockSpec((tm, d), lambda i: (i, 6)),
            pl.BlockSpec((d, d), const),
            pl.BlockSpec((d, d), const),
            pl.BlockSpec((d, d), const),
        ],
        out_specs=pl.BlockSpec((d, tm), lambda i: (0, i)),
        out_shape=jax.ShapeDtypeStruct((d, n), F32),
        compiler_params=_params("parallel"),
        name="merge_out_proj",
    )(x, ret_a, ret_b, rnn_a, rnn_b, p, p, w_ret_out, w_rnn_out, w_o)


def _peer_query_kernel(xt_ref, g_ref, wq_ref, xn_ref, q_ref):
    x = xt_ref[...]
    ms = jnp.mean(x * x, axis=0, keepdims=True)
    xn = ((x * lax.rsqrt(ms + EPS)) * g_ref[...]).astype(BF16)
    xn_ref[...] = xn
    q_ref[...] = jnp.dot(wq_ref[...], xn, preferred_element_type=F32)


def peer_query(xt, g, wq_t):
    d, n = xt.shape
    m = wq_t.shape[0]
    tn = _pick(n, (512, 256, 128))
    return pl.pallas_call(
        _peer_query_kernel,
        grid=(n // tn,),
        in_specs=[
            pl.BlockSpec((d, tn), lambda i: (0, i)),
            pl.BlockSpec((d, 1), lambda i: (0, 0)),
            pl.BlockSpec((m, d), lambda i: (0, 0)),
        ],
        out_specs=[pl.BlockSpec((d, tn), lambda i: (0, i)), pl.BlockSpec((m, tn), lambda i: (0, i))],
        out_shape=[jax.ShapeDtypeStruct((d, n), BF16), jax.ShapeDtypeStruct((m, n), F32)],
        compiler_params=_params("parallel"),
        name="peer_query",
    )(xt, g.reshape(d, 1), wq_t)


def _sort_pairs(n):
    def merge(lo, hi, r):
        step = r * 2
        if step < hi - lo:
            yield from merge(lo, hi, step)
            yield from merge(lo + r, hi, step)
            yield from [(i, i + r) for i in range(lo + r, hi - r, step)]
        else:
            yield (lo, lo + r)

    def sort(lo, hi):
        if hi - lo >= 1:
            mid = lo + (hi - lo) // 2
            yield from sort(lo, mid)
            yield from sort(mid + 1, hi)
            yield from merge(lo, hi, 1)

    return list(sort(0, n - 1))


_SORT16 = _sort_pairs(PEER_TOPK)


def _cmpx(vals, i, j):
    a, b = vals[i], vals[j]
    if b is None:
        return
    if a is None:
        vals[i], vals[j] = b, None
        return
    vals[i], vals[j] = jnp.maximum(a, b), jnp.minimum(a, b)


def _sort_desc(vals):
    vals = list(vals)
    for i, j in _SORT16:
        _cmpx(vals, i, j)
    return vals


def _merge_top(a, b):
    k = PEER_TOPK
    a = list(a) + [None] * (k - len(a))
    b = list(b) + [None] * (k - len(b))
    out = []
    for r in range(k):
        x, y = a[r], b[k - 1 - r]
        out.append(y if x is None else (x if y is None else jnp.maximum(x, y)))
    d = k // 2
    while d >= 1:
        for i in range(k):
            if not i & d:
                _cmpx(out, i, i + d)
        d //= 2
    return out


def _top_sorted(ref):
    groups = []
    for g0 in range(0, N_KEYS, PEER_TOPK):
        vals = [ref[pl.ds((g0 + j) * PEER_HEADS, PEER_HEADS), :] for j in range(PEER_TOPK)]
        groups.append(_sort_desc(vals))
    while len(groups) > 1:
        groups = [_merge_top(groups[i], groups[i + 1]) for i in range(0, len(groups), 2)]
    return groups[0]


def _peer_select_kernel(q_ref, k1_ref, k2_ref, c1_ref, e1_ref, r2_ref, e2_ref, s1_scr, s2_scr, r2_scr, e2_scr):
    nh = PEER_HEADS
    half_rows = q_ref.shape[0] // 2
    tn = q_ref.shape[1]
    s1 = jnp.dot(k1_ref[...], q_ref[pl.ds(0, half_rows), :].astype(BF16), preferred_element_type=F32)
    s2 = jnp.dot(k2_ref[...], q_ref[pl.ds(half_rows, half_rows), :].astype(BF16), preferred_element_type=F32)
    for lt in range(tn // LANES):
        s1_scr[lt] = s1[:, lt * LANES:(lt + 1) * LANES]
        s2_scr[lt] = s2[:, lt * LANES:(lt + 1) * LANES]
    for lt in range(tn // LANES):
        lanes = pl.ds(lt * LANES, LANES)
        s1_t, s2_t, r2_t, e2_t = s1_scr.at[lt], s2_scr.at[lt], r2_scr.at[lt], e2_scr.at[lt]
        a = _top_sorted(s1_t)
        b = _top_sorted(s2_t)
        k = PEER_TOPK
        lists = []
        for j in range(1, k + 1):
            col = [a[r - 1] + b[j - 1] for r in range(j, k // j + 1)]
            row = [a[j - 1] + b[s - 1] for s in range(j + 1, k // j + 1)]
            if col:
                lists.append(col)
            if row:
                lists.append(row)
        top = lists[0]
        for other in lists[1:]:
            top = _merge_top(top, other)
        tau = top[k - 1]
        z = jnp.ones_like(tau)
        for r in range(1, k):
            z = z + jnp.exp(top[r] - top[0])
        zinv = 1.0 / z
        inf = jnp.full((nh, LANES), jnp.inf, F32)
        phi = []
        for s in range(1, k + 1):
            p = inf
            for r in range(1, k // s + 1):
                p = jnp.where(a[r - 1] + b[s - 1] >= tau, a[r - 1], p)
            phi.append(p)

        def per_key(kk, carry):
            rows = pl.ds(pl.multiple_of(kk * nh, nh), nh)
            s1k = s1_t[rows, :]
            s2k = s2_t[rows, :]
            cnt = jnp.zeros((nh, LANES), F32)
            for s in range(k):
                cnt = jnp.where(s1k >= phi[s], float(s + 1), cnt)
            rank = jnp.full((nh, LANES), float(k + 1), F32)
            for s in range(k - 1, -1, -1):
                rank = jnp.where(s2k >= b[s], float(s + 1), rank)
            c1_ref[rows, lanes] = cnt
            e1_ref[rows, lanes] = jnp.exp(s1k - a[0])
            r2_t[rows, :] = rank
            e2_t[rows, :] = jnp.exp(s2k - b[0]) * zinv
            return carry

        lax.fori_loop(0, N_KEYS, per_key, 0, unroll=2)
        pack = 2 * SUBLANES
        for h in range(nh):
            for kt in range(N_KEYS // pack):
                lo = pl.ds(kt * pack * nh + h, SUBLANES, stride=nh)
                hi = pl.ds((kt * pack + SUBLANES) * nh + h, SUBLANES, stride=nh)
                dst = pl.ds(h * N_KEYS + kt * pack, pack)
                r2_ref[dst, lanes] = jnp.concatenate([r2_t[lo, :], r2_t[hi, :]], axis=0).astype(BF16)
                e2_ref[dst, lanes] = jnp.concatenate([e2_t[lo, :], e2_t[hi, :]], axis=0).astype(BF16)


def peer_select(q_t, k1, k2):
    m, n = q_t.shape
    rows = PEER_HEADS * N_KEYS
    tn = _pick(n, (256, 128))
    tok = lambda i: (0, i)
    const = lambda i: (0, 0)
    return pl.pallas_call(
        _peer_select_kernel,
        grid=(n // tn,),
        in_specs=[pl.BlockSpec((m, tn), tok), pl.BlockSpec(k1.shape, const), pl.BlockSpec(k2.shape, const)],
        out_specs=[pl.BlockSpec((rows, tn), tok)] * 4,
        out_shape=[jax.ShapeDtypeStruct((rows, n), F32), jax.ShapeDtypeStruct((rows, n), F32),
                   jax.ShapeDtypeStruct((rows, n), BF16), jax.ShapeDtypeStruct((rows, n), BF16)],
        scratch_shapes=[pltpu.VMEM((tn // LANES, rows, LANES), F32)] * 4,
        compiler_params=_params("parallel"),
        name="peer_select",
    )(q_t, k1, k2)


PEER_EXPERT_BLOCK = 1024


def _peer_dense_kernel(xt_ref, xn_ref, u_ref, vt_ref, c1_ref, e1_ref, r2_ref, e2_ref, y_ref,
                       acc_scr, h_scr, act_scr, bc_scr):
    j = pl.program_id(1)
    nh = PEER_HEADS
    tn = xn_ref.shape[1]
    pack = 2 * SUBLANES
    i1_per_blk = u_ref.shape[0] // N_KEYS

    @pl.when(j == 0)
    def _():
        acc_scr[...] = jnp.zeros_like(acc_scr)

    h_scr[...] = jnp.dot(u_ref[...], xn_ref[...], preferred_element_type=F32)

    def per_i1(il, carry):
        erows = pl.ds(pl.multiple_of(il * N_KEYS, N_KEYS), N_KEYS)
        hrows = pl.ds(pl.multiple_of(il * nh, nh), nh)
        for lc in range(tn // LANES):
            lanes = pl.ds(lc * LANES, LANES)
            c1p = c1_ref[hrows, lanes] + 1.0
            e1 = e1_ref[hrows, lanes]
            for h in range(nh):
                bc_scr[lc, h] = jnp.broadcast_to(c1p[h:h + 1, :], (pack, LANES)).astype(BF16)
                bc_scr[lc, nh + h] = jnp.broadcast_to(e1[h:h + 1, :], (pack, LANES)).astype(BF16)
            x = h_scr[erows, lanes]
            t = jnp.tanh(x * (GELU_C0 + GELU_C1 * (x * x)))
            half = jnp.full((), 0.5, BF16)
            act_scr[erows, lanes] = x.astype(BF16) * (half + half * t.astype(BF16))
            c1b = [bc_scr[lc, h] for h in range(nh)]
            e1b = [bc_scr[lc, nh + h] for h in range(nh)]
            zero = jnp.zeros((pack, LANES), BF16)
            for it in range(N_KEYS // pack):
                rows = pl.ds(pl.multiple_of(il * N_KEYS + it * pack, pack), pack)
                gate = None
                for h in range(nh):
                    krows = pl.ds(h * N_KEYS + it * pack, pack)
                    sel = jnp.minimum(jnp.maximum(c1b[h] - r2_ref[krows, lanes], zero), e2_ref[krows, lanes])
                    term = sel * e1b[h]
                    gate = term if gate is None else gate + term
                act_scr[rows, lanes] = act_scr[rows, lanes] * gate
        return carry

    lax.fori_loop(0, i1_per_blk, per_i1, 0)
    acc_scr[...] += jnp.dot(vt_ref[...], act_scr[...], preferred_element_type=F32)

    @pl.when(j == pl.num_programs(1) - 1)
    def _():
        y_ref[...] = (xt_ref[...] + acc_scr[...]).T


def peer_dense(xt, xn_t, u_bf16, vt_bf16, c1, e1, r2, e2):
    d, n = xt.shape
    ne = u_bf16.shape[0]
    tn = _pick(n, (512, 256, 128))
    te = PEER_EXPERT_BLOCK
    i1_per_blk = te // N_KEYS
    tok = lambda i, j: (0, i)
    return pl.pallas_call(
        _peer_dense_kernel,
        grid=(n // tn, ne // te),
        in_specs=[
            pl.BlockSpec((d, tn), tok),
            pl.BlockSpec((d, tn), tok),
            pl.BlockSpec((te, d), lambda i, j: (j, 0)),
            pl.BlockSpec((d, te), lambda i, j: (0, j)),
            pl.BlockSpec((i1_per_blk * PEER_HEADS, tn), lambda i, j: (j, i)),
            pl.BlockSpec((i1_per_blk * PEER_HEADS, tn), lambda i, j: (j, i)),
            pl.BlockSpec((PEER_HEADS * N_KEYS, tn), tok),
            pl.BlockSpec((PEER_HEADS * N_KEYS, tn), tok),
        ],
        out_specs=pl.BlockSpec((tn, d), lambda i, j: (i, 0)),
        out_shape=jax.ShapeDtypeStruct((n, d), F32),
        scratch_shapes=[pltpu.VMEM((d, tn), F32), pltpu.VMEM((te, tn), F32), pltpu.VMEM((te, tn), BF16),
                        pltpu.VMEM((tn // LANES, 2 * PEER_HEADS, 2 * SUBLANES, LANES), BF16)],
        compiler_params=_params("parallel", "arbitrary"),
        name="peer_dense",
    )(xt, xn_t, u_bf16, vt_bf16, c1, e1, r2, e2)


def _interleaved_keys(keys_p):
    nh, nk, half = keys_p.shape
    eye = jnp.eye(nh, dtype=keys_p.dtype)
    return jnp.einsum("hkd,hg->khgd", keys_p, eye).reshape(nk * nh, nh * half).astype(BF16)


def peer_layer(xt, norm_g, wq, keys, u_tab, v_tab):
    d, n = xt.shape
    wq_t = wq.reshape(d, PEER_HEADS, 2, PEER_HALF).transpose(2, 1, 3, 0).reshape(2 * PEER_HEADS * PEER_HALF, d)
    xn_t, q_t = peer_query(xt, norm_g, wq_t.astype(BF16))
    k1 = _interleaved_keys(keys[:, 0])
    k2 = _interleaved_keys(keys[:, 1])
    c1, e1, r2, e2 = peer_select(q_t, k1, k2)
    return peer_dense(xt, xn_t, u_tab.astype(BF16), v_tab.T.astype(BF16), c1, e1, r2, e2)


def _rmsnorm_kernel(x_ref, g_ref, y_ref):
    x = x_ref[...]
    ms = jnp.mean(x * x, axis=-1, keepdims=True)
    y_ref[...] = (x * lax.rsqrt(ms + EPS)) * g_ref[...]


def final_norm(x, row0, rows, g):
    d = x.shape[1]
    tm = _pick(math.gcd(rows, row0) if row0 else rows, (512, 256, 128, 64, 32, 16, 8))
    blk0 = row0 // tm
    return pl.pallas_call(
        _rmsnorm_kernel,
        grid=(rows // tm,),
        in_specs=[pl.BlockSpec((tm, d), lambda i: (blk0 + i, 0)), pl.BlockSpec((1, d), lambda i: (0, 0))],
        out_specs=pl.BlockSpec((tm, d), lambda i: (i, 0)),
        out_shape=jax.ShapeDtypeStruct((rows, d), F32),
        compiler_params=_params("parallel"),
        name="final_norm",
    )(x, g.reshape(1, d))


def _to_time_major(p, row0, b, t, col):
    blk = lax.slice(p, (row0, col * D_RNN), (row0 + b * t, (col + 1) * D_RNN))
    return blk.reshape(b, t, D_RNN).transpose(1, 0, 2).reshape(t * b, D_RNN)


def _trunk(groups, norm1_g, norm2_g, normf_g, w_in, ret_gn_g, w_ret_out, conv_w, conv_b, rg_wa, rg_ba, rg_wx,
           rg_bx, rg_lambda, w_rnn_out, w_o, peer_wq, peer_keys, peer_u, peer_v):
    shapes = [(g[0].shape[0], g[0].shape[1]) for g in groups]
    x = jnp.concatenate([g[0].reshape(-1, D_MODEL) for g in groups], axis=0)
    row0s = np.cumsum([0] + [b * t for b, t in shapes]).tolist()
    states = [([], [], []) for _ in groups]
    for l in range(DEPTH):
        p = norm_matmul(x, norm1_g[l], w_in[l].astype(BF16))
        wa4 = _block_diag_tiles(rg_wa[l], 256)
        wx4 = _block_diag_tiles(rg_wx[l], 256)
        ret_parts, rnn_parts = [], []
        for gi, (xg, r0, h0, buf0, pos0) in enumerate(groups):
            b, t = shapes[gi]
            row0 = row0s[gi]
            o, r_new = retention_path(p, row0, b, t, pos0, r0[l], ret_gn_g[l])
            xr_tm = _to_time_major(p, row0, b, t, 3)
            g_tm = _to_time_major(p, row0, b, t, 4)
            buf_tm = buf0[l].transpose(1, 0, 2).reshape((CONV_W - 1) * b, D_RNN)
            hg, h_last, nb = rglru_path(xr_tm, g_tm, buf_tm, h0[l], b, t, conv_w[l], conv_b[l], wa4, rg_ba[l],
                                        wx4, rg_bx[l], rg_lambda[l])
            ret_parts.append(o)
            rnn_parts.append(hg.reshape(t, b, D_RNN).transpose(1, 0, 2).reshape(b * t, D_RNN))
            states[gi][0].append(r_new)
            states[gi][1].append(h_last)
            states[gi][2].append(nb.reshape(CONV_W - 1, b, D_RNN).transpose(1, 0, 2))
        xt = merge_proj(x, ret_parts, rnn_parts, p, w_ret_out[l].astype(BF16), w_rnn_out[l].astype(BF16),
                        w_o[l].astype(BF16))
        x = peer_layer(xt, norm2_g[l], peer_wq[l], peer_keys[l], peer_u[l], peer_v[l])
    outs = []
    for gi, (b, t) in enumerate(shapes):
        y = final_norm(x, row0s[gi], b * t, normf_g).reshape(b, t, D_MODEL)
        outs.append((y, jnp.stack(states[gi][0]), jnp.stack(states[gi][1]), jnp.stack(states[gi][2])))
    return outs


def kernel(x_prompt, x_sample, state_ret, state_rnn, state_conv, norm1_g, norm2_g, normf_g, w_in, ret_gn_g,
           w_ret_out, conv_w, conv_b, rg_wa, rg_ba, rg_wx, rg_bx, rg_lambda, w_rnn_out, w_o, peer_wq, peer_keys,
           peer_u, peer_v):
    bp = x_prompt.shape[0]
    dt = x_prompt.dtype
    zr = jnp.zeros((DEPTH, bp, RET_HEADS, RET_DK, RET_DV), dt)
    zh = jnp.zeros((DEPTH, bp, D_RNN), dt)
    zc = jnp.zeros((DEPTH, bp, CONV_W - 1, D_RNN), dt)
    groups = [(x_prompt, zr, zh, zc, 0.0), (x_sample, state_ret, state_rnn, state_conv, float(PAST_LEN))]
    (yp, rp, hp, cp), (ys, rs, hs, cs) = _trunk(
        groups, norm1_g, norm2_g, normf_g, w_in, ret_gn_g, w_ret_out, conv_w, conv_b, rg_wa, rg_ba, rg_wx, rg_bx,
        rg_lambda, w_rnn_out, w_o, peer_wq, peer_keys, peer_u, peer_v)
    return (yp, ys, rp, hp, cp, rs, hs, cs)
```

```python
import functools
import math

import jax
import jax.numpy as jnp
import numpy as np
from jax import lax
from jax.experimental import pallas as pl
from jax.experimental.pallas import tpu as pltpu

D_MODEL = 1024
DEPTH = 2
PAST_LEN = 16384
RET_HEADS = 8
RET_DK = 64
RET_DV = 128
RET_QK = RET_HEADS * RET_DK
RET_V = RET_HEADS * RET_DV
RET_CHUNK = 128
ROPE_BASE = 10000.0
D_RNN = 1024
RNN_BLOCKS = 16
RNN_BS = D_RNN // RNN_BLOCKS
CONV_W = 4
RG_C = 8.0
IN_SIZES = (RET_QK, RET_QK, RET_V, RET_V, D_RNN, D_RNN, D_MODEL, D_MODEL)
N_IN = sum(IN_SIZES)
PEER_HEADS = 8
N_KEYS = 128
N_EXPERTS = N_KEYS * N_KEYS
PEER_DKEY = 256
PEER_HALF = PEER_DKEY // 2
PEER_TOPK = 16
EPS = 1e-6

SUBLANES = 8
LANES = 128
VMEM_LIMIT = 56 * 1024 * 1024

F32 = jnp.float32
BF16 = jnp.bfloat16


def _params(*sem, flags=None):
    return pltpu.CompilerParams(dimension_semantics=sem, vmem_limit_bytes=VMEM_LIMIT, flags=flags)


def _pick(n, prefs):
    for p in prefs:
        if n % p == 0:
            return p
    return n


def _norm_matmul_kernel(x_ref, g_ref, w_ref, y_ref, xn_scr):
    @pl.when(pl.program_id(1) == 0)
    def _():
        x = x_ref[...]
        ms = jnp.mean(x * x, axis=-1, keepdims=True)
        xn_scr[...] = ((x * lax.rsqrt(ms + EPS)) * g_ref[...]).astype(BF16)

    y_ref[...] = jnp.dot(xn_scr[...], w_ref[...], preferred_element_type=F32)


def norm_matmul(x, g, w_bf16):
    n, d = x.shape
    m = w_bf16.shape[1]
    tm = _pick(n, (1024, 512, 256, 128, 64, 32, 16, 8))
    tn = _pick(m, (1024, 512, 256, 128))
    return pl.pallas_call(
        _norm_matmul_kernel,
        grid=(n // tm, m // tn),
        in_specs=[
            pl.BlockSpec((tm, d), lambda i, j: (i, 0)),
            pl.BlockSpec((1, d), lambda i, j: (0, 0)),
            pl.BlockSpec((d, tn), lambda i, j: (0, j)),
        ],
        out_specs=pl.BlockSpec((tm, tn), lambda i, j: (i, j)),
        out_shape=jax.ShapeDtypeStruct((n, m), F32),
        scratch_shapes=[pltpu.VMEM((tm, d), BF16)],
        compiler_params=_params("parallel", "arbitrary"),
        name="norm_in_proj",
    )(x, g.reshape(1, d), w_bf16)


def _rot_half(x):
    n = x.shape[-1]
    half = RET_DK // 2
    fwd = pltpu.roll(x, half, axis=1)
    bwd = pltpu.roll(x, n - half, axis=1)
    lane = lax.broadcasted_iota(jnp.int32, x.shape, 1)
    return jnp.where((lane % RET_DK) < half, bwd, fwd)


def _retention_kernel(q_ref, k_ref, v_ref, g_ref, r0_ref, cos_ref, sin_ref, mask_ref, qw_ref, kw_ref,
                      gc_ref, gn_ref, o_ref, r_out_ref, r_scr):
    c = pl.program_id(1)

    @pl.when(c == 0)
    def _():
        r_scr[...] = r0_ref[0]

    cos = cos_ref[...]
    sin = sin_ref[...]
    q = q_ref[...]
    k = k_ref[...]
    qr = q * cos + _rot_half(q) * sin
    kr = (k * cos + _rot_half(k) * sin) * (RET_DK ** -0.5)
    qd = (qr * qw_ref[...]).astype(BF16)
    kd = (kr * kw_ref[...]).astype(BF16)
    qb = qr.astype(BF16)
    kb = kr.astype(BF16)
    v = v_ref[...].astype(BF16)
    g = g_ref[...]
    for h in range(RET_HEADS):
        ks = slice(h * RET_DK, (h + 1) * RET_DK)
        vs = slice(h * RET_DV, (h + 1) * RET_DV)
        vh = v[:, vs]
        s = lax.dot_general(qb[:, ks], kb[:, ks], (((1,), (1,)), ((), ())),
                            preferred_element_type=F32) * mask_ref[h]
        r_h = r_scr[h]
        o = jnp.dot(s.astype(BF16), vh, preferred_element_type=F32)
        o = o + jnp.dot(qd[:, ks], r_h.astype(BF16), preferred_element_type=F32)
        kv = lax.dot_general(kd[:, ks], vh, (((0,), (0,)), ((), ())), preferred_element_type=F32)
        r_scr[h] = gc_ref[h] * r_h + kv
        mu = jnp.mean(o, axis=-1, keepdims=True)
        var = jnp.mean(jnp.square(o - mu), axis=-1, keepdims=True)
        on = ((o - mu) * lax.rsqrt(var + EPS)) * gn_ref[:, vs]
        gh = g[:, vs]
        o_ref[:, vs] = ((gh * jax.nn.sigmoid(gh)) * on).astype(BF16)

    @pl.when(c == pl.num_programs(1) - 1)
    def _():
        r_out_ref[0] = r_scr[...]


def retention_path(p, row0, b, t, pos0, r0, gn_g):
    c = RET_CHUNK if t % RET_CHUNK == 0 else t
    nc = t // c
    blk0 = row0 // c
    assert row0 % c == 0
    log_g = jnp.log1p(-(2.0 ** (-5.0 - jnp.arange(RET_HEADS, dtype=F32))))
    idx = jnp.arange(c, dtype=F32)
    diff = idx[:, None] - idx[None, :]
    mask = jnp.where(diff[None] >= 0, jnp.exp(jnp.maximum(diff, 0.0)[None] * log_g[:, None, None]), 0.0)
    k_w = jnp.exp((c - 1 - idx)[:, None] * log_g[None, :])
    q_w = jnp.exp((idx + 1.0)[:, None] * log_g[None, :])
    g_c = jnp.exp(c * log_g)
    qw_tab = jnp.repeat(q_w, RET_DK, axis=1)
    kw_tab = jnp.repeat(k_w, RET_DK, axis=1)
    gc_tab = jnp.broadcast_to(g_c[:, None, None], (RET_HEADS, 1, RET_DV))
    half = RET_DK // 2
    pos = pos0 + jnp.arange(t, dtype=F32)
    freq = ROPE_BASE ** (-jnp.arange(half, dtype=F32) / half)
    ang = pos[:, None] * freq[None, :]
    cos_h = jnp.concatenate([jnp.cos(ang), jnp.cos(ang)], axis=1)
    sin_h = jnp.concatenate([-jnp.sin(ang), jnp.sin(ang)], axis=1)
    cos_tab = jnp.tile(cos_h, (1, RET_HEADS))
    sin_tab = jnp.tile(sin_h, (1, RET_HEADS))

    rows = lambda bi, ci: blk0 + bi * nc + ci
    o, r_new = pl.pallas_call(
        _retention_kernel,
        grid=(b, nc),
        in_specs=[
            pl.BlockSpec((c, RET_QK), lambda bi, ci: (rows(bi, ci), 0)),
            pl.BlockSpec((c, RET_QK), lambda bi, ci: (rows(bi, ci), 1)),
            pl.BlockSpec((c, RET_V), lambda bi, ci: (rows(bi, ci), 1)),
            pl.BlockSpec((c, RET_V), lambda bi, ci: (rows(bi, ci), 2)),
            pl.BlockSpec((1, RET_HEADS, RET_DK, RET_DV), lambda bi, ci: (bi, 0, 0, 0)),
            pl.BlockSpec((c, RET_QK), lambda bi, ci: (ci, 0)),
            pl.BlockSpec((c, RET_QK), lambda bi, ci: (ci, 0)),
            pl.BlockSpec((RET_HEADS, c, c), lambda bi, ci: (0, 0, 0)),
            pl.BlockSpec((c, RET_QK), lambda bi, ci: (0, 0)),
            pl.BlockSpec((c, RET_QK), lambda bi, ci: (0, 0)),
            pl.BlockSpec((RET_HEADS, 1, RET_DV), lambda bi, ci: (0, 0, 0)),
            pl.BlockSpec((1, RET_V), lambda bi, ci: (0, 0)),
        ],
        out_specs=[
            pl.BlockSpec((c, RET_V), lambda bi, ci: (bi * nc + ci, 0)),
            pl.BlockSpec((1, RET_HEADS, RET_DK, RET_DV), lambda bi, ci: (bi, 0, 0, 0)),
        ],
        out_shape=[
            jax.ShapeDtypeStruct((b * t, RET_V), BF16),
            jax.ShapeDtypeStruct((b, RET_HEADS, RET_DK, RET_DV), F32),
        ],
        scratch_shapes=[pltpu.VMEM((RET_HEADS, RET_DK, RET_DV), F32)],
        compiler_params=_params("parallel", "arbitrary"),
        name="retention",
    )(p, p, p, p, r0, cos_tab, sin_tab, mask, qw_tab, kw_tab, gc_tab, gn_g.reshape(1, RET_V))
    return o, r_new


GELU_C0 = math.sqrt(2.0 / math.pi)
GELU_C1 = GELU_C0 * 0.044715


def _gelu_tanh(x):
    return x * (0.5 * (1.0 + jnp.tanh(math.sqrt(2.0 / math.pi) * (x + 0.044715 * (x * x * x)))))


def _rglru_kernel(bsz, tc, xr_ref, g_ref, buf_ref, h0_ref, cw_ref, cb_ref, wa_ref, ba_ref, wx_ref, bx_ref,
                  lam_ref, o_ref, hl_ref, nb_ref, xcat_scr, a_scr, u_scr, hs_scr):
    rows = tc * bsz
    hist = (CONV_W - 1) * bsz
    step = pl.program_id(0)

    @pl.when(step == 0)
    def _():
        xcat_scr[pl.ds(rows, hist), :] = buf_ref[...]
        hs_scr[pl.ds(rows, bsz), :] = h0_ref[...]

    xcat_scr[pl.ds(0, hist), :] = xcat_scr[pl.ds(rows, hist), :]
    hs_scr[pl.ds(0, bsz), :] = hs_scr[pl.ds(rows, bsz), :]
    xcat_scr[pl.ds(hist, rows), :] = xr_ref[...]

    xc = cb_ref[...] + xcat_scr[pl.ds(0, rows), :] * cw_ref[0:1, :]
    for w in range(1, CONV_W):
        xc = xc + xcat_scr[pl.ds(w * bsz, rows), :] * cw_ref[w:w + 1, :]
    xcb = xc.astype(BF16)
    nblk = wa_ref.shape[0]
    wdt = wa_ref.shape[1]
    ra = jnp.concatenate(
        [jnp.dot(xcb[:, j * wdt:(j + 1) * wdt], wa_ref[j], preferred_element_type=F32) for j in range(nblk)],
        axis=1)
    ri = jnp.concatenate(
        [jnp.dot(xcb[:, j * wdt:(j + 1) * wdt], wx_ref[j], preferred_element_type=F32) for j in range(nblk)],
        axis=1)
    r = jax.nn.sigmoid(ra + ba_ref[...])
    i = jax.nn.sigmoid(ri + bx_ref[...])
    z = -lam_ref[...]
    softplus = jnp.maximum(z, 0.0) + jnp.log1p(jnp.exp(-jnp.abs(z)))
    log_a = (-RG_C * r) * softplus
    a = jnp.exp(log_a)
    one_m_a2 = -jnp.tanh(log_a) * (a * a + 1.0)
    a_scr[...] = a
    u_scr[...] = jnp.sqrt(one_m_a2) * (i * xc)

    def scan_step(t, carry):
        prev = hs_scr[pl.ds(pl.multiple_of(t * bsz, bsz), bsz), :]
        cur = pl.ds(pl.multiple_of(t * bsz, bsz), bsz)
        h = a_scr[cur, :] * prev + u_scr[cur, :]
        hs_scr[pl.ds(pl.multiple_of((t + 1) * bsz, bsz), bsz), :] = h
        return carry

    lax.fori_loop(0, tc, scan_step, 0)
    hs = hs_scr[pl.ds(bsz, rows), :]
    o_ref[...] = (hs * _gelu_tanh(g_ref[...])).astype(BF16)

    @pl.when(step == pl.num_programs(0) - 1)
    def _():
        hl_ref[...] = hs_scr[pl.ds(rows, bsz), :]
        nb_ref[...] = xcat_scr[pl.ds(rows, hist), :]


def rglru_path(xr_tm, g_tm, buf_tm, h0, b, t, conv_w, conv_b, wa4, ba, wx4, bx, lam):
    assert t >= CONV_W - 1
    rows_target = 256
    tc = max(1, min(t, rows_target // b))
    while t % tc:
        tc -= 1
    rows = tc * b
    hist = (CONV_W - 1) * b
    d = D_RNN
    nblk, wdt = wa4.shape[0], wa4.shape[1]
    const2 = lambda s: (0, 0)
    out, h_last, new_buf = pl.pallas_call(
        functools.partial(_rglru_kernel, b, tc),
        grid=(t // tc,),
        in_specs=[
            pl.BlockSpec((rows, d), lambda s: (s, 0)),
            pl.BlockSpec((rows, d), lambda s: (s, 0)),
            pl.BlockSpec((hist, d), const2),
            pl.BlockSpec((b, d), const2),
            pl.BlockSpec((CONV_W, d), const2),
            pl.BlockSpec((1, d), const2),
            pl.BlockSpec((nblk, wdt, wdt), lambda s: (0, 0, 0)),
            pl.BlockSpec((1, d), const2),
            pl.BlockSpec((nblk, wdt, wdt), lambda s: (0, 0, 0)),
            pl.BlockSpec((1, d), const2),
            pl.BlockSpec((1, d), const2),
        ],
        out_specs=[
            pl.BlockSpec((rows, d), lambda s: (s, 0)),
            pl.BlockSpec((b, d), const2),
            pl.BlockSpec((hist, d), const2),
        ],
        out_shape=[
            jax.ShapeDtypeStruct((t * b, d), BF16),
            jax.ShapeDtypeStruct((b, d), F32),
            jax.ShapeDtypeStruct((hist, d), F32),
        ],
        scratch_shapes=[
            pltpu.VMEM((rows + hist, d), F32),
            pltpu.VMEM((rows, d), F32),
            pltpu.VMEM((rows, d), F32),
            pltpu.VMEM((rows + b, d), F32),
        ],
        compiler_params=_params("arbitrary"),
        name="rglru",
    )(xr_tm, g_tm, buf_tm, h0, conv_w, conv_b.reshape(1, d), wa4, ba.reshape(1, d), wx4, bx.reshape(1, d),
      lam.reshape(1, d))
    return out, h_last, new_buf


def _block_diag_tiles(w, tile):
    nb, bs, _ = w.shape
    per = tile // bs
    w = w.reshape(nb // per, per, bs, bs)
    eye = jnp.eye(per, dtype=w.dtype)
    dense = jnp.einsum("gpcd,pq->gpcqd", w, eye).reshape(nb // per, tile, tile)
    return dense.astype(BF16)


def _merge_kernel(blocks_a, x_ref, roa_ref, rob_ref, rna_ref, rnb_ref, ga_ref, gb_ref, wr_ref, wn_ref, wo_ref,
                  yt_ref):
    first = pl.program_id(0) < blocks_a
    ro = jnp.where(first, roa_ref[...], rob_ref[...])
    rn = jnp.where(first, rna_ref[...], rnb_ref[...])
    ret_out = jnp.dot(ro, wr_ref[...], preferred_element_type=F32)
    rnn_out = jnp.dot(rn, wn_ref[...], preferred_element_type=F32)
    merged = jax.nn.sigmoid(ga_ref[...]) * ret_out + jax.nn.sigmoid(gb_ref[...]) * rnn_out
    y = x_ref[...] + jnp.dot(merged.astype(BF16), wo_ref[...], preferred_element_type=F32)
    yt_ref[...] = y.T


def merge_proj(x, ret_parts, rnn_parts, p, w_ret_out, w_rnn_out, w_o):
    n, d = x.shape
    (ret_a, ret_b), (rnn_a, rnn_b) = ret_parts, rnn_parts
    rows_a, rows_b = ret_a.shape[0], ret_b.shape[0]
    tm = _pick(math.gcd(rows_a, rows_b), (512, 256, 128))
    blocks_a = rows_a // tm
    row = lambda i: (i, 0)
    row_a = lambda i: (jnp.minimum(i, blocks_a - 1), 0)
    row_b = lambda i: (jnp.maximum(i - blocks_a, 0), 0)
    const = lambda i: (0, 0)
    return pl.pallas_call(
        functools.partial(_merge_kernel, blocks_a),
        grid=(n // tm,),
        in_specs=[
            pl.BlockSpec((tm, d), row),
            pl.BlockSpec((tm, d), row_a),
            pl.BlockSpec((tm, d), row_b),
            pl.BlockSpec((tm, d), row_a),
            pl.BlockSpec((tm, d), row_b),
            pl.BlockSpec((tm, d), lambda i: (i, 5)),
            pl.BlockSpec((tm, d), lambda i: (i, 6)),
            pl.BlockSpec((d, d), const),
            pl.BlockSpec((d, d), const),
            pl.BlockSpec((d, d), const),
        ],
        out_specs=pl.BlockSpec((d, tm), lambda i: (0, i)),
        out_shape=jax.ShapeDtypeStruct((d, n), F32),
        compiler_params=_params("parallel"),
        name="merge_out_proj",
    )(x, ret_a, ret_b, rnn_a, rnn_b, p, p, w_ret_out, w_rnn_out, w_o)


def _peer_query_kernel(xt_ref, g_ref, wq_ref, xn_ref, q_ref):
    x = xt_ref[...]
    ms = jnp.mean(x * x, axis=0, keepdims=True)
    xn = ((x * lax.rsqrt(ms + EPS)) * g_ref[...]).astype(BF16)
    xn_ref[...] = xn
    q_ref[...] = jnp.dot(wq_ref[...], xn, preferred_element_type=F32)


def peer_query(xt, g, wq_t):
    d, n = xt.shape
    m = wq_t.shape[0]
    tn = _pick(n, (512, 256, 128))
    return pl.pallas_call(
        _peer_query_kernel,
        grid=(n // tn,),
        in_specs=[
            pl.BlockSpec((d, tn), lambda i: (0, i)),
            pl.BlockSpec((d, 1), lambda i: (0, 0)),
            pl.BlockSpec((m, d), lambda i: (0, 0)),
        ],
        out_specs=[pl.BlockSpec((d, tn), lambda i: (0, i)), pl.BlockSpec((m, tn), lambda i: (0, i))],
        out_shape=[jax.ShapeDtypeStruct((d, n), BF16), jax.ShapeDtypeStruct((m, n), F32)],
        compiler_params=_params("parallel"),
        name="peer_query",
    )(xt, g.reshape(d, 1), wq_t)


def _sort_pairs(n):
    def merge(lo, hi, r):
        step = r * 2
        if step < hi - lo:
            yield from merge(lo, hi, step)
            yield from merge(lo + r, hi, step)
            yield from [(i, i + r) for i in range(lo + r, hi - r, step)]
        else:
            yield (lo, lo + r)

    def sort(lo, hi):
        if hi - lo >= 1:
            mid = lo + (hi - lo) // 2
            yield from sort(lo, mid)
            yield from sort(mid + 1, hi)
            yield from merge(lo, hi, 1)

    return list(sort(0, n - 1))


_SORT16 = _sort_pairs(PEER_TOPK)


def _cmpx(vals, i, j):
    a, b = vals[i], vals[j]
    if b is None:
        return
    if a is None:
        vals[i], vals[j] = b, None
        return
    vals[i], vals[j] = jnp.maximum(a, b), jnp.minimum(a, b)


def _sort_desc(vals):
    vals = list(vals)
    for i, j in _SORT16:
        _cmpx(vals, i, j)
    return vals


def _merge_top(a, b):
    k = PEER_TOPK
    a = list(a) + [None] * (k - len(a))
    b = list(b) + [None] * (k - len(b))
    out = []
    for r in range(k):
        x, y = a[r], b[k - 1 - r]
        out.append(y if x is None else (x if y is None else jnp.maximum(x, y)))
    d = k // 2
    while d >= 1:
        for i in range(k):
            if not i & d:
                _cmpx(out, i, i + d)
        d //= 2
    return out


def _top_sorted(ref):
    groups = []
    for g0 in range(0, N_KEYS, PEER_TOPK):
        vals = [ref[pl.ds((g0 + j) * PEER_HEADS, PEER_HEADS), :] for j in range(PEER_TOPK)]
        groups.append(_sort_desc(vals))
    while len(groups) > 1:
        groups = [_merge_top(groups[i], groups[i + 1]) for i in range(0, len(groups), 2)]
    return groups[0]


def _peer_select_kernel(q_ref, k1_ref, k2_ref, c1_ref, e1_ref, r2_ref, e2_ref, s1_scr, s2_scr, r2_scr, e2_scr):
    nh = PEER_HEADS
    half_rows = q_ref.shape[0] // 2
    tn = q_ref.shape[1]
    s1 = jnp.dot(k1_ref[...], q_ref[pl.ds(0, half_rows), :].astype(BF16), preferred_element_type=F32)
    s2 = jnp.dot(k2_ref[...], q_ref[pl.ds(half_rows, half_rows), :].astype(BF16), preferred_element_type=F32)
    for lt in range(tn // LANES):
        s1_scr[lt] = s1[:, lt * LANES:(lt + 1) * LANES]
        s2_scr[lt] = s2[:, lt * LANES:(lt + 1) * LANES]
    for lt in range(tn // LANES):
        lanes = pl.ds(lt * LANES, LANES)
        s1_t, s2_t, r2_t, e2_t = s1_scr.at[lt], s2_scr.at[lt], r2_scr.at[lt], e2_scr.at[lt]
        a = _top_sorted(s1_t)
        b = _top_sorted(s2_t)
        k = PEER_TOPK
        lists = []
        for j in range(1, k + 1):
            col = [a[r - 1] + b[j - 1] for r in range(j, k // j + 1)]
            row = [a[j - 1] + b[s - 1] for s in range(j + 1, k // j + 1)]
            if col:
                lists.append(col)
            if row:
                lists.append(row)
        top = lists[0]
        for other in lists[1:]:
            top = _merge_top(top, other)
        tau = top[k - 1]
        z = jnp.ones_like(tau)
        for r in range(1, k):
            z = z + jnp.exp(top[r] - top[0])
        zinv = 1.0 / z
        inf = jnp.full((nh, LANES), jnp.inf, F32)
        phi = []
        for s in range(1, k + 1):
            p = inf
            for r in range(1, k // s + 1):
                p = jnp.where(a[r - 1] + b[s - 1] >= tau, a[r - 1], p)
            phi.append(p)

        def per_key(kk, carry):
            rows = pl.ds(pl.multiple_of(kk * nh, nh), nh)
            s1k = s1_t[rows, :]
            s2k = s2_t[rows, :]
            cnt = jnp.zeros((nh, LANES), F32)
            for s in range(k):
                cnt = jnp.where(s1k >= phi[s], float(s + 1), cnt)
            rank = jnp.full((nh, LANES), float(k + 1), F32)
            for s in range(k - 1, -1, -1):
                rank = jnp.where(s2k >= b[s], float(s + 1), rank)
            c1_ref[rows, lanes] = cnt
            e1_ref[rows, lanes] = jnp.exp(s1k - a[0])
            r2_t[rows, :] = rank
            e2_t[rows, :] = jnp.exp(s2k - b[0]) * zinv
            return carry

        lax.fori_loop(0, N_KEYS, per_key, 0, unroll=2)
        pack = 2 * SUBLANES
        for h in range(nh):
            for kt in range(N_KEYS // pack):
                lo = pl.ds(kt * pack * nh + h, SUBLANES, stride=nh)
                hi = pl.ds((kt * pack + SUBLANES) * nh + h, SUBLANES, stride=nh)
                dst = pl.ds(h * N_KEYS + kt * pack, pack)
                r2_ref[dst, lanes] = jnp.concatenate([r2_t[lo, :], r2_t[hi, :]], axis=0).astype(BF16)
                e2_ref[dst, lanes] = jnp.concatenate([e2_t[lo, :], e2_t[hi, :]], axis=0).astype(BF16)


def peer_select(q_t, k1, k2):
    m, n = q_t.shape
    rows = PEER_HEADS * N_KEYS
    tn = _pick(n, (256, 128))
    tok = lambda i: (0, i)
    const = lambda i: (0, 0)
    return pl.pallas_call(
        _peer_select_kernel,
        grid=(n // tn,),
        in_specs=[pl.BlockSpec((m, tn), tok), pl.BlockSpec(k1.shape, const), pl.BlockSpec(k2.shape, const)],
        out_specs=[pl.BlockSpec((rows, tn), tok)] * 4,
        out_shape=[jax.ShapeDtypeStruct((rows, n), F32), jax.ShapeDtypeStruct((rows, n), F32),
                   jax.ShapeDtypeStruct((rows, n), BF16), jax.ShapeDtypeStruct((rows, n), BF16)],
        scratch_shapes=[pltpu.VMEM((tn // LANES, rows, LANES), F32)] * 4,
        compiler_params=_params("parallel"),
        name="peer_select",
    )(q_t, k1, k2)


PEER_EXPERT_BLOCK = 1024


def _peer_dense_kernel(xt_ref, xn_ref, u_ref, vt_ref, c1_ref, e1_ref, r2_ref, e2_ref, y_ref,
                       acc_scr, h_scr, act_scr, bc_scr):
    j = pl.program_id(1)
    nh = PEER_HEADS
    tn = xn_ref.shape[1]
    pack = 2 * SUBLANES
    i1_per_blk = u_ref.shape[0] // N_KEYS

    @pl.when(j == 0)
    def _():
        acc_scr[...] = jnp.zeros_like(acc_scr)

    h_scr[...] = jnp.dot(u_ref[...], xn_ref[...], preferred_element_type=F32)

    def per_i1(il, carry):
        erows = pl.ds(pl.multiple_of(il * N_KEYS, N_KEYS), N_KEYS)
        hrows = pl.ds(pl.multiple_of(il * nh, nh), nh)
        for lc in range(tn // LANES):
            lanes = pl.ds(lc * LANES, LANES)
            c1p = c1_ref[hrows, lanes] + 1.0
            e1 = e1_ref[hrows, lanes]
            c1b = [jnp.broadcast_to(c1p[h:h + 1, :], (pack, LANES)).astype(BF16) for h in range(nh)]
            e1b = [jnp.broadcast_to(e1[h:h + 1, :], (pack, LANES)).astype(BF16) for h in range(nh)]
            zero = jnp.zeros((pack, LANES), BF16)
            half = jnp.full((), 0.5, BF16)
            for it in range(N_KEYS // pack):
                rows = pl.ds(pl.multiple_of(il * N_KEYS + it * pack, pack), pack)
                gate = None
                for h in range(nh):
                    krows = pl.ds(h * N_KEYS + it * pack, pack)
                    sel = jnp.minimum(jnp.maximum(c1b[h] - r2_ref[krows, lanes], zero), e2_ref[krows, lanes])
                    term = sel * e1b[h]
                    gate = term if gate is None else gate + term
                x = h_scr[rows, lanes]
                t = jnp.tanh(x * (GELU_C0 + GELU_C1 * (x * x)))
                act_scr[rows, lanes] = (x.astype(BF16) * (half + half * t.astype(BF16))) * gate
        return carry

    lax.fori_loop(0, i1_per_blk, per_i1, 0)
    acc_scr[...] += jnp.dot(vt_ref[...], act_scr[...], preferred_element_type=F32)

    @pl.when(j == pl.num_programs(1) - 1)
    def _():
        y_ref[...] = (xt_ref[...] + acc_scr[...]).T


def peer_dense(xt, xn_t, u_bf16, vt_bf16, c1, e1, r2, e2):
    d, n = xt.shape
    ne = u_bf16.shape[0]
    tn = _pick(n, (512, 256, 128))
    te = PEER_EXPERT_BLOCK
    i1_per_blk = te // N_KEYS
    tok = lambda i, j: (0, i)
    return pl.pallas_call(
        _peer_dense_kernel,
        grid=(n // tn, ne // te),
        in_specs=[
            pl.BlockSpec((d, tn), tok),
            pl.BlockSpec((d, tn), tok),
            pl.BlockSpec((te, d), lambda i, j: (j, 0)),
            pl.BlockSpec((d, te), lambda i, j: (0, j)),
            pl.BlockSpec((i1_per_blk * PEER_HEADS, tn), lambda i, j: (j, i)),
            pl.BlockSpec((i1_per_blk * PEER_HEADS, tn), lambda i, j: (j, i)),
            pl.BlockSpec((PEER_HEADS * N_KEYS, tn), tok),
            pl.BlockSpec((PEER_HEADS * N_KEYS, tn), tok),
        ],
        out_specs=pl.BlockSpec((tn, d), lambda i, j: (i, 0)),
        out_shape=jax.ShapeDtypeStruct((n, d), F32),
        scratch_shapes=[pltpu.VMEM((d, tn), F32), pltpu.VMEM((te, tn), F32), pltpu.VMEM((te, tn), BF16),
                        pltpu.VMEM((tn // LANES, 2 * PEER_HEADS, 2 * SUBLANES, LANES), BF16)],
        compiler_params=_params("parallel", "arbitrary"),
        name="peer_dense",
    )(xt, xn_t, u_bf16, vt_bf16, c1, e1, r2, e2)


def _interleaved_keys(keys_p):
    nh, nk, half = keys_p.shape
    eye = jnp.eye(nh, dtype=keys_p.dtype)
    return jnp.einsum("hkd,hg->khgd", keys_p, eye).reshape(nk * nh, nh * half).astype(BF16)


def peer_layer(xt, norm_g, wq, keys, u_tab, v_tab):
    d, n = xt.shape
    wq_t = wq.reshape(d, PEER_HEADS, 2, PEER_HALF).transpose(2, 1, 3, 0).reshape(2 * PEER_HEADS * PEER_HALF, d)
    xn_t, q_t = peer_query(xt, norm_g, wq_t.astype(BF16))
    k1 = _interleaved_keys(keys[:, 0])
    k2 = _interleaved_keys(keys[:, 1])
    c1, e1, r2, e2 = peer_select(q_t, k1, k2)
    return peer_dense(xt, xn_t, u_tab.astype(BF16), v_tab.T.astype(BF16), c1, e1, r2, e2)


def _rmsnorm_kernel(x_ref, g_ref, y_ref):
    x = x_ref[...]
    ms = jnp.mean(x * x, axis=-1, keepdims=True)
    y_ref[...] = (x * lax.rsqrt(ms + EPS)) * g_ref[...]


def final_norm(x, row0, rows, g):
    d = x.shape[1]
    tm = _pick(math.gcd(rows, row0) if row0 else rows, (512, 256, 128, 64, 32, 16, 8))
    blk0 = row0 // tm
    return pl.pallas_call(
        _rmsnorm_kernel,
        grid=(rows // tm,),
        in_specs=[pl.BlockSpec((tm, d), lambda i: (blk0 + i, 0)), pl.BlockSpec((1, d), lambda i: (0, 0))],
        out_specs=pl.BlockSpec((tm, d), lambda i: (i, 0)),
        out_shape=jax.ShapeDtypeStruct((rows, d), F32),
        compiler_params=_params("parallel"),
        name="final_norm",
    )(x, g.reshape(1, d))


def _to_time_major(p, row0, b, t, col):
    blk = lax.slice(p, (row0, col * D_RNN), (row0 + b * t, (col + 1) * D_RNN))
    return blk.reshape(b, t, D_RNN).transpose(1, 0, 2).reshape(t * b, D_RNN)


def _trunk(groups, norm1_g, norm2_g, normf_g, w_in, ret_gn_g, w_ret_out, conv_w, conv_b, rg_wa, rg_ba, rg_wx,
           rg_bx, rg_lambda, w_rnn_out, w_o, peer_wq, peer_keys, peer_u, peer_v):
    shapes = [(g[0].shape[0], g[0].shape[1]) for g in groups]
    x = jnp.concatenate([g[0].reshape(-1, D_MODEL) for g in groups], axis=0)
    row0s = np.cumsum([0] + [b * t for b, t in shapes]).tolist()
    states = [([], [], []) for _ in groups]
    for l in range(DEPTH):
        p = norm_matmul(x, norm1_g[l], w_in[l].astype(BF16))
        wa4 = _block_diag_tiles(rg_wa[l], 256)
        wx4 = _block_diag_tiles(rg_wx[l], 256)
        ret_parts, rnn_parts = [], []
        for gi, (xg, r0, h0, buf0, pos0) in enumerate(groups):
            b, t = shapes[gi]
            row0 = row0s[gi]
            o, r_new = retention_path(p, row0, b, t, pos0, r0[l], ret_gn_g[l])
            xr_tm = _to_time_major(p, row0, b, t, 3)
            g_tm = _to_time_major(p, row0, b, t, 4)
            buf_tm = buf0[l].transpose(1, 0, 2).reshape((CONV_W - 1) * b, D_RNN)
            hg, h_last, nb = rglru_path(xr_tm, g_tm, buf_tm, h0[l], b, t, conv_w[l], conv_b[l], wa4, rg_ba[l],
                                        wx4, rg_bx[l], rg_lambda[l])
            ret_parts.append(o)
            rnn_parts.append(hg.reshape(t, b, D_RNN).transpose(1, 0, 2).reshape(b * t, D_RNN))
            states[gi][0].append(r_new)
            states[gi][1].append(h_last)
            states[gi][2].append(nb.reshape(CONV_W - 1, b, D_RNN).transpose(1, 0, 2))
        xt = merge_proj(x, ret_parts, rnn_parts, p, w_ret_out[l].astype(BF16), w_rnn_out[l].astype(BF16),
                        w_o[l].astype(BF16))
        x = peer_layer(xt, norm2_g[l], peer_wq[l], peer_keys[l], peer_u[l], peer_v[l])
    outs = []
    for gi, (b, t) in enumerate(shapes):
        y = final_norm(x, row0s[gi], b * t, normf_g).reshape(b, t, D_MODEL)
        outs.append((y, jnp.stack(states[gi][0]), jnp.stack(states[gi][1]), jnp.stack(states[gi][2])))
    return outs


def kernel(x_prompt, x_sample, state_ret, state_rnn, state_conv, norm1_g, norm2_g, normf_g, w_in, ret_gn_g,
           w_ret_out, conv_w, conv_b, rg_wa, rg_ba, rg_wx, rg_bx, rg_lambda, w_rnn_out, w_o, peer_wq, peer_keys,
           peer_u, peer_v):
    bp = x_prompt.shape[0]
    dt = x_prompt.dtype
    zr = jnp.zeros((DEPTH, bp, RET_HEADS, RET_DK, RET_DV), dt)
    zh = jnp.zeros((DEPTH, bp, D_RNN), dt)
    zc = jnp.zeros((DEPTH, bp, CONV_W - 1, D_RNN), dt)
    groups = [(x_prompt, zr, zh, zc, 0.0), (x_sample, state_ret, state_rnn, state_conv, float(PAST_LEN))]
    (yp, rp, hp, cp), (ys, rs, hs, cs) = _trunk(
        groups, norm1_g, norm2_g, normf_g, w_in, ret_gn_g, w_ret_out, conv_w, conv_b, rg_wa, rg_ba, rg_wx, rg_bx,
        rg_lambda, w_rnn_out, w_o, peer_wq, peer_keys, peer_u, peer_v)
    return (yp, ys, rp, hp, cp, rs, hs, cs)
```

```python
import functools
import math

import jax
import jax.numpy as jnp
import numpy as np
from jax import lax
from jax.experimental import pallas as pl
from jax.experimental.pallas import tpu as pltpu

D_MODEL = 1024
DEPTH = 2
PAST_LEN = 16384
RET_HEADS = 8
RET_DK = 64
RET_DV = 128
RET_QK = RET_HEADS * RET_DK
RET_V = RET_HEADS * RET_DV
RET_CHUNK = 128
ROPE_BASE = 10000.0
D_RNN = 1024
RNN_BLOCKS = 16
RNN_BS = D_RNN // RNN_BLOCKS
CONV_W = 4
RG_C = 8.0
IN_SIZES = (RET_QK, RET_QK, RET_V, RET_V, D_RNN, D_RNN, D_MODEL, D_MODEL)
N_IN = sum(IN_SIZES)
PEER_HEADS = 8
N_KEYS = 128
N_EXPERTS = N_KEYS * N_KEYS
PEER_DKEY = 256
PEER_HALF = PEER_DKEY // 2
PEER_TOPK = 16
EPS = 1e-6

SUBLANES = 8
LANES = 128
VMEM_LIMIT = 56 * 1024 * 1024

F32 = jnp.float32
BF16 = jnp.bfloat16


def _params(*sem, flags=None):
    return pltpu.CompilerParams(dimension_semantics=sem, vmem_limit_bytes=VMEM_LIMIT, flags=flags)


def _pick(n, prefs):
    for p in prefs:
        if n % p == 0:
            return p
    return n


def _norm_matmul_kernel(x_ref, g_ref, w_ref, y_ref, xn_scr):
    @pl.when(pl.program_id(1) == 0)
    def _():
        x = x_ref[...]
        ms = jnp.mean(x * x, axis=-1, keepdims=True)
        xn_scr[...] = ((x * lax.rsqrt(ms + EPS)) * g_ref[...]).astype(BF16)

    y_ref[...] = jnp.dot(xn_scr[...], w_ref[...], preferred_element_type=F32)


def norm_matmul(x, g, w_bf16):
    n, d = x.shape
    m = w_bf16.shape[1]
    tm = _pick(n, (1024, 512, 256, 128, 64, 32, 16, 8))
    tn = _pick(m, (1024, 512, 256, 128))
    return pl.pallas_call(
        _norm_matmul_kernel,
        grid=(n // tm, m // tn),
        in_specs=[
            pl.BlockSpec((tm, d), lambda i, j: (i, 0)),
            pl.BlockSpec((1, d), lambda i, j: (0, 0)),
            pl.BlockSpec((d, tn), lambda i, j: (0, j)),
        ],
        out_specs=pl.BlockSpec((tm, tn), lambda i, j: (i, j)),
        out_shape=jax.ShapeDtypeStruct((n, m), F32),
        scratch_shapes=[pltpu.VMEM((tm, d), BF16)],
        compiler_params=_params("parallel", "arbitrary"),
        name="norm_in_proj",
    )(x, g.reshape(1, d), w_bf16)


def _rot_half(x):
    n = x.shape[-1]
    half = RET_DK // 2
    fwd = pltpu.roll(x, half, axis=1)
    bwd = pltpu.roll(x, n - half, axis=1)
    lane = lax.broadcasted_iota(jnp.int32, x.shape, 1)
    return jnp.where((lane % RET_DK) < half, bwd, fwd)


def _retention_kernel(q_ref, k_ref, v_ref, g_ref, r0_ref, cos_ref, sin_ref, mask_ref, qw_ref, kw_ref,
                      gc_ref, gn_ref, o_ref, r_out_ref, r_scr):
    c = pl.program_id(1)

    @pl.when(c == 0)
    def _():
        r_scr[...] = r0_ref[0]

    cos = cos_ref[...]
    sin = sin_ref[...]
    q = q_ref[...]
    k = k_ref[...]
    qr = q * cos + _rot_half(q) * sin
    kr = (k * cos + _rot_half(k) * sin) * (RET_DK ** -0.5)
    qd = (qr * qw_ref[...]).astype(BF16)
    kd = (kr * kw_ref[...]).astype(BF16)
    qb = qr.astype(BF16)
    kb = kr.astype(BF16)
    v = v_ref[...].astype(BF16)
    g = g_ref[...]
    for h in range(RET_HEADS):
        ks = slice(h * RET_DK, (h + 1) * RET_DK)
        vs = slice(h * RET_DV, (h + 1) * RET_DV)
        vh = v[:, vs]
        s = lax.dot_general(qb[:, ks], kb[:, ks], (((1,), (1,)), ((), ())),
                            preferred_element_type=F32) * mask_ref[h]
        r_h = r_scr[h]
        o = jnp.dot(s.astype(BF16), vh, preferred_element_type=F32)
        o = o + jnp.dot(qd[:, ks], r_h.astype(BF16), preferred_element_type=F32)
        kv = lax.dot_general(kd[:, ks], vh, (((0,), (0,)), ((), ())), preferred_element_type=F32)
        r_scr[h] = gc_ref[h] * r_h + kv
        mu = jnp.mean(o, axis=-1, keepdims=True)
        var = jnp.mean(jnp.square(o - mu), axis=-1, keepdims=True)
        on = ((o - mu) * lax.rsqrt(var + EPS)) * gn_ref[:, vs]
        gh = g[:, vs]
        o_ref[:, vs] = ((gh * jax.nn.sigmoid(gh)) * on).astype(BF16)

    @pl.when(c == pl.num_programs(1) - 1)
    def _():
        r_out_ref[0] = r_scr[...]


def retention_path(p, row0, b, t, pos0, r0, gn_g):
    c = RET_CHUNK if t % RET_CHUNK == 0 else t
    nc = t // c
    blk0 = row0 // c
    assert row0 % c == 0
    log_g = jnp.log1p(-(2.0 ** (-5.0 - jnp.arange(RET_HEADS, dtype=F32))))
    idx = jnp.arange(c, dtype=F32)
    diff = idx[:, None] - idx[None, :]
    mask = jnp.where(diff[None] >= 0, jnp.exp(jnp.maximum(diff, 0.0)[None] * log_g[:, None, None]), 0.0)
    k_w = jnp.exp((c - 1 - idx)[:, None] * log_g[None, :])
    q_w = jnp.exp((idx + 1.0)[:, None] * log_g[None, :])
    g_c = jnp.exp(c * log_g)
    qw_tab = jnp.repeat(q_w, RET_DK, axis=1)
    kw_tab = jnp.repeat(k_w, RET_DK, axis=1)
    gc_tab = jnp.broadcast_to(g_c[:, None, None], (RET_HEADS, 1, RET_DV))
    half = RET_DK // 2
    pos = pos0 + jnp.arange(t, dtype=F32)
    freq = ROPE_BASE ** (-jnp.arange(half, dtype=F32) / half)
    ang = pos[:, None] * freq[None, :]
    cos_h = jnp.concatenate([jnp.cos(ang), jnp.cos(ang)], axis=1)
    sin_h = jnp.concatenate([-jnp.sin(ang), jnp.sin(ang)], axis=1)
    cos_tab = jnp.tile(cos_h, (1, RET_HEADS))
    sin_tab = jnp.tile(sin_h, (1, RET_HEADS))

    rows = lambda bi, ci: blk0 + bi * nc + ci
    o, r_new = pl.pallas_call(
        _retention_kernel,
        grid=(b, nc),
        in_specs=[
            pl.BlockSpec((c, RET_QK), lambda bi, ci: (rows(bi, ci), 0)),
            pl.BlockSpec((c, RET_QK), lambda bi, ci: (rows(bi, ci), 1)),
            pl.BlockSpec((c, RET_V), lambda bi, ci: (rows(bi, ci), 1)),
            pl.BlockSpec((c, RET_V), lambda bi, ci: (rows(bi, ci), 2)),
            pl.BlockSpec((1, RET_HEADS, RET_DK, RET_DV), lambda bi, ci: (bi, 0, 0, 0)),
            pl.BlockSpec((c, RET_QK), lambda bi, ci: (ci, 0)),
            pl.BlockSpec((c, RET_QK), lambda bi, ci: (ci, 0)),
            pl.BlockSpec((RET_HEADS, c, c), lambda bi, ci: (0, 0, 0)),
            pl.BlockSpec((c, RET_QK), lambda bi, ci: (0, 0)),
            pl.BlockSpec((c, RET_QK), lambda bi, ci: (0, 0)),
            pl.BlockSpec((RET_HEADS, 1, RET_DV), lambda bi, ci: (0, 0, 0)),
            pl.BlockSpec((1, RET_V), lambda bi, ci: (0, 0)),
        ],
        out_specs=[
            pl.BlockSpec((c, RET_V), lambda bi, ci: (bi * nc + ci, 0)),
            pl.BlockSpec((1, RET_HEADS, RET_DK, RET_DV), lambda bi, ci: (bi, 0, 0, 0)),
        ],
        out_shape=[
            jax.ShapeDtypeStruct((b * t, RET_V), BF16),
            jax.ShapeDtypeStruct((b, RET_HEADS, RET_DK, RET_DV), F32),
        ],
        scratch_shapes=[pltpu.VMEM((RET_HEADS, RET_DK, RET_DV), F32)],
        compiler_params=_params("parallel", "arbitrary"),
        name="retention",
    )(p, p, p, p, r0, cos_tab, sin_tab, mask, qw_tab, kw_tab, gc_tab, gn_g.reshape(1, RET_V))
    return o, r_new


GELU_C0 = math.sqrt(2.0 / math.pi)
GELU_C1 = GELU_C0 * 0.044715


def _gelu_tanh(x):
    return x * (0.5 * (1.0 + jnp.tanh(math.sqrt(2.0 / math.pi) * (x + 0.044715 * (x * x * x)))))


def _rglru_kernel(bsz, tc, xr_ref, g_ref, buf_ref, h0_ref, cw_ref, cb_ref, wa_ref, ba_ref, wx_ref, bx_ref,
                  lam_ref, o_ref, hl_ref, nb_ref, xcat_scr, a_scr, u_scr, hs_scr):
    rows = tc * bsz
    hist = (CONV_W - 1) * bsz
    step = pl.program_id(0)

    @pl.when(step == 0)
    def _():
        xcat_scr[pl.ds(rows, hist), :] = buf_ref[...]
        hs_scr[pl.ds(rows, bsz), :] = h0_ref[...]

    xcat_scr[pl.ds(0, hist), :] = xcat_scr[pl.ds(rows, hist), :]
    hs_scr[pl.ds(0, bsz), :] = hs_scr[pl.ds(rows, bsz), :]
    xcat_scr[pl.ds(hist, rows), :] = xr_ref[...]

    xc = cb_ref[...] + xcat_scr[pl.ds(0, rows), :] * cw_ref[0:1, :]
    for w in range(1, CONV_W):
        xc = xc + xcat_scr[pl.ds(w * bsz, rows), :] * cw_ref[w:w + 1, :]
    xcb = xc.astype(BF16)
    nblk = wa_ref.shape[0]
    wdt = wa_ref.shape[1]
    ra = jnp.concatenate(
        [jnp.dot(xcb[:, j * wdt:(j + 1) * wdt], wa_ref[j], preferred_element_type=F32) for j in range(nblk)],
        axis=1)
    ri = jnp.concatenate(
        [jnp.dot(xcb[:, j * wdt:(j + 1) * wdt], wx_ref[j], preferred_element_type=F32) for j in range(nblk)],
        axis=1)
    r = jax.nn.sigmoid(ra + ba_ref[...])
    i = jax.nn.sigmoid(ri + bx_ref[...])
    z = -lam_ref[...]
    softplus = jnp.maximum(z, 0.0) + jnp.log1p(jnp.exp(-jnp.abs(z)))
    log_a = (-RG_C * r) * softplus
    a = jnp.exp(log_a)
    one_m_a2 = -jnp.tanh(log_a) * (a * a + 1.0)
    a_scr[...] = a
    u_scr[...] = jnp.sqrt(one_m_a2) * (i * xc)

    def scan_step(t, carry):
        prev = hs_scr[pl.ds(pl.multiple_of(t * bsz, bsz), bsz), :]
        cur = pl.ds(pl.multiple_of(t * bsz, bsz), bsz)
        h = a_scr[cur, :] * prev + u_scr[cur, :]
        hs_scr[pl.ds(pl.multiple_of((t + 1) * bsz, bsz), bsz), :] = h
        return carry

    lax.fori_loop(0, tc, scan_step, 0)
    hs = hs_scr[pl.ds(bsz, rows), :]
    o_ref[...] = (hs * _gelu_tanh(g_ref[...])).astype(BF16)

    @pl.when(step == pl.num_programs(0) - 1)
    def _():
        hl_ref[...] = hs_scr[pl.ds(rows, bsz), :]
        nb_ref[...] = xcat_scr[pl.ds(rows, hist), :]


def rglru_path(xr_tm, g_tm, buf_tm, h0, b, t, conv_w, conv_b, wa4, ba, wx4, bx, lam):
    assert t >= CONV_W - 1
    rows_target = 256
    tc = max(1, min(t, rows_target // b))
    while t % tc:
        tc -= 1
    rows = tc * b
    hist = (CONV_W - 1) * b
    d = D_RNN
    nblk, wdt = wa4.shape[0], wa4.shape[1]
    const2 = lambda s: (0, 0)
    out, h_last, new_buf = pl.pallas_call(
        functools.partial(_rglru_kernel, b, tc),
        grid=(t // tc,),
        in_specs=[
            pl.BlockSpec((rows, d), lambda s: (s, 0)),
            pl.BlockSpec((rows, d), lambda s: (s, 0)),
            pl.BlockSpec((hist, d), const2),
            pl.BlockSpec((b, d), const2),
            pl.BlockSpec((CONV_W, d), const2),
            pl.BlockSpec((1, d), const2),
            pl.BlockSpec((nblk, wdt, wdt), lambda s: (0, 0, 0)),
            pl.BlockSpec((1, d), const2),
            pl.BlockSpec((nblk, wdt, wdt), lambda s: (0, 0, 0)),
            pl.BlockSpec((1, d), const2),
            pl.BlockSpec((1, d), const2),
        ],
        out_specs=[
            pl.BlockSpec((rows, d), lambda s: (s, 0)),
            pl.BlockSpec((b, d), const2),
            pl.BlockSpec((hist, d), const2),
        ],
        out_shape=[
            jax.ShapeDtypeStruct((t * b, d), BF16),
            jax.ShapeDtypeStruct((b, d), F32),
            jax.ShapeDtypeStruct((hist, d), F32),
        ],
        scratch_shapes=[
            pltpu.VMEM((rows + hist, d), F32),
            pltpu.VMEM((rows, d), F32),
            pltpu.VMEM((rows, d), F32),
            pltpu.VMEM((rows + b, d), F32),
        ],
        compiler_params=_params("arbitrary"),
        name="rglru",
    )(xr_tm, g_tm, buf_tm, h0, conv_w, conv_b.reshape(1, d), wa4, ba.reshape(1, d), wx4, bx.reshape(1, d),
      lam.reshape(1, d))
    return out, h_last, new_buf


def _block_diag_tiles(w, tile):
    nb, bs, _ = w.shape
    per = tile // bs
    w = w.reshape(nb // per, per, bs, bs)
    eye = jnp.eye(per, dtype=w.dtype)
    dense = jnp.einsum("gpcd,pq->gpcqd", w, eye).reshape(nb // per, tile, tile)
    return dense.astype(BF16)


def _merge_kernel(blocks_a, x_ref, roa_ref, rob_ref, rna_ref, rnb_ref, ga_ref, gb_ref, wr_ref, wn_ref, wo_ref,
                  yt_ref):
    first = pl.program_id(0) < blocks_a
    ro = jnp.where(first, roa_ref[...], rob_ref[...])
    rn = jnp.where(first, rna_ref[...], rnb_ref[...])
    ret_out = jnp.dot(ro, wr_ref[...], preferred_element_type=F32)
    rnn_out = jnp.dot(rn, wn_ref[...], preferred_element_type=F32)
    merged = jax.nn.sigmoid(ga_ref[...]) * ret_out + jax.nn.sigmoid(gb_ref[...]) * rnn_out
    y = x_ref[...] + jnp.dot(merged.astype(BF16), wo_ref[...], preferred_element_type=F32)
    yt_ref[...] = y.T


def merge_proj(x, ret_parts, rnn_parts, p, w_ret_out, w_rnn_out, w_o):
    n, d = x.shape
    (ret_a, ret_b), (rnn_a, rnn_b) = ret_parts, rnn_parts
    rows_a, rows_b = ret_a.shape[0], ret_b.shape[0]
    tm = _pick(math.gcd(rows_a, rows_b), (512, 256, 128))
    blocks_a = rows_a // tm
    row = lambda i: (i, 0)
    row_a = lambda i: (jnp.minimum(i, blocks_a - 1), 0)
    row_b = lambda i: (jnp.maximum(i - blocks_a, 0), 0)
    const = lambda i: (0, 0)
    return pl.pallas_call(
        functools.partial(_merge_kernel, blocks_a),
        grid=(n // tm,),
        in_specs=[
            pl.BlockSpec((tm, d), row),
            pl.BlockSpec((tm, d), row_a),
            pl.BlockSpec((tm, d), row_b),
            pl.BlockSpec((tm, d), row_a),
            pl.BlockSpec((tm, d), row_b),
            pl.BlockSpec((tm, d), lambda i: (i, 5)),
            pl.BlockSpec((tm, d), lambda i: (i, 6)),
            pl.BlockSpec((d, d), const),
            pl.BlockSpec((d, d), const),
            pl.BlockSpec((d, d), const),
        ],
        out_specs=pl.BlockSpec((d, tm), lambda i: (0, i)),
        out_shape=jax.ShapeDtypeStruct((d, n), F32),
        compiler_params=_params("parallel"),
        name="merge_out_proj",
    )(x, ret_a, ret_b, rnn_a, rnn_b, p, p, w_ret_out, w_rnn_out, w_o)


def _peer_query_kernel(xt_ref, g_ref, wq_ref, xn_ref, q_ref):
    x = xt_ref[...]
    ms = jnp.mean(x * x, axis=0, keepdims=True)
    xn = ((x * lax.rsqrt(ms + EPS)) * g_ref[...]).astype(BF16)
    xn_ref[...] = xn
    q_ref[...] = jnp.dot(wq_ref[...], xn, preferred_element_type=F32)


def peer_query(xt, g, wq_t):
    d, n = xt.shape
    m = wq_t.shape[0]
    tn = _pick(n, (512, 256, 128))
    return pl.pallas_call(
        _peer_query_kernel,
        grid=(n // tn,),
        in_specs=[
            pl.BlockSpec((d, tn), lambda i: (0, i)),
            pl.BlockSpec((d, 1), lambda i: (0, 0)),
            pl.BlockSpec((m, d), lambda i: (0, 0)),
        ],
        out_specs=[pl.BlockSpec((d, tn), lambda i: (0, i)), pl.BlockSpec((m, tn), lambda i: (0, i))],
        out_shape=[jax.ShapeDtypeStruct((d, n), BF16), jax.ShapeDtypeStruct((m, n), F32)],
        compiler_params=_params("parallel"),
        name="peer_query",
    )(xt, g.reshape(d, 1), wq_t)


def _sort_pairs(n):
    def merge(lo, hi, r):
        step = r * 2
        if step < hi - lo:
            yield from merge(lo, hi, step)
            yield from merge(lo + r, hi, step)
            yield from [(i, i + r) for i in range(lo + r, hi - r, step)]
        else:
            yield (lo, lo + r)

    def sort(lo, hi):
        if hi - lo >= 1:
            mid = lo + (hi - lo) // 2
            yield from sort(lo, mid)
            yield from sort(mid + 1, hi)
            yield from merge(lo, hi, 1)

    return list(sort(0, n - 1))


_SORT16 = _sort_pairs(PEER_TOPK)


def _cmpx(vals, i, j):
    a, b = vals[i], vals[j]
    if b is None:
        return
    if a is None:
        vals[i], vals[j] = b, None
        return
    vals[i], vals[j] = jnp.maximum(a, b), jnp.minimum(a, b)


def _sort_desc(vals):
    vals = list(vals)
    for i, j in _SORT16:
        _cmpx(vals, i, j)
    return vals


def _merge_top(a, b):
    k = PEER_TOPK
    a = list(a) + [None] * (k - len(a))
    b = list(b) + [None] * (k - len(b))
    out = []
    for r in range(k):
        x, y = a[r], b[k - 1 - r]
        out.append(y if x is None else (x if y is None else jnp.maximum(x, y)))
    d = k // 2
    while d >= 1:
        for i in range(k):
            if not i & d:
                _cmpx(out, i, i + d)
        d //= 2
    return out


def _top_sorted(ref):
    groups = []
    for g0 in range(0, N_KEYS, PEER_TOPK):
        vals = [ref[pl.ds((g0 + j) * PEER_HEADS, PEER_HEADS), :] for j in range(PEER_TOPK)]
        groups.append(_sort_desc(vals))
    while len(groups) > 1:
        groups = [_merge_top(groups[i], groups[i + 1]) for i in range(0, len(groups), 2)]
    return groups[0]


def _peer_select_kernel(q_ref, k1_ref, k2_ref, c1_ref, e1_ref, r2_ref, e2_ref, s1_scr, s2_scr, r2_scr, e2_scr):
    nh = PEER_HEADS
    half_rows = q_ref.shape[0] // 2
    tn = q_ref.shape[1]
    s1 = jnp.dot(k1_ref[...], q_ref[pl.ds(0, half_rows), :].astype(BF16), preferred_element_type=F32)
    s2 = jnp.dot(k2_ref[...], q_ref[pl.ds(half_rows, half_rows), :].astype(BF16), preferred_element_type=F32)
    for lt in range(tn // LANES):
        s1_scr[lt] = s1[:, lt * LANES:(lt + 1) * LANES]
        s2_scr[lt] = s2[:, lt * LANES:(lt + 1) * LANES]
    for lt in range(tn // LANES):
        lanes = pl.ds(lt * LANES, LANES)
        s1_t, s2_t, r2_t, e2_t = s1_scr.at[lt], s2_scr.at[lt], r2_scr.at[lt], e2_scr.at[lt]
        a = _top_sorted(s1_t)
        b = _top_sorted(s2_t)
        k = PEER_TOPK
        lists = []
        for j in range(1, k + 1):
            col = [a[r - 1] + b[j - 1] for r in range(j, k // j + 1)]
            row = [a[j - 1] + b[s - 1] for s in range(j + 1, k // j + 1)]
            if col:
                lists.append(col)
            if row:
                lists.append(row)
        top = lists[0]
        for other in lists[1:]:
            top = _merge_top(top, other)
        tau = top[k - 1]
        z = jnp.ones_like(tau)
        for r in range(1, k):
            z = z + jnp.exp(top[r] - top[0])
        zinv = 1.0 / z
        inf = jnp.full((nh, LANES), jnp.inf, F32)
        phi = []
        for s in range(1, k + 1):
            p = inf
            for r in range(1, k // s + 1):
                p = jnp.where(a[r - 1] + b[s - 1] >= tau, a[r - 1], p)
            phi.append(p)

        def per_key(kk, carry):
            rows = pl.ds(pl.multiple_of(kk * nh, nh), nh)
            s1k = s1_t[rows, :]
            s2k = s2_t[rows, :]
            cnt = jnp.ones((nh, LANES), F32)
            for s in range(k):
                cnt = jnp.where(s1k >= phi[s], float(s + 2), cnt)
            rank = jnp.full((nh, LANES), float(k + 1), F32)
            for s in range(k - 1, -1, -1):
                rank = jnp.where(s2k >= b[s], float(s + 1), rank)
            c1_ref[rows, lanes] = cnt
            e1_ref[rows, lanes] = jnp.exp(s1k - a[0])
            r2_t[rows, :] = rank
            e2_t[rows, :] = jnp.exp(s2k - b[0]) * zinv
            return carry

        lax.fori_loop(0, N_KEYS, per_key, 0, unroll=2)
        pack = 2 * SUBLANES
        for h in range(nh):
            for kt in range(N_KEYS // pack):
                lo = pl.ds(kt * pack * nh + h, SUBLANES, stride=nh)
                hi = pl.ds((kt * pack + SUBLANES) * nh + h, SUBLANES, stride=nh)
                dst = pl.ds(h * N_KEYS + kt * pack, pack)
                r2_ref[dst, lanes] = jnp.concatenate([r2_t[lo, :], r2_t[hi, :]], axis=0).astype(BF16)
                e2_ref[dst, lanes] = jnp.concatenate([e2_t[lo, :], e2_t[hi, :]], axis=0).astype(BF16)


def peer_select(q_t, k1, k2):
    m, n = q_t.shape
    rows = PEER_HEADS * N_KEYS
    tn = _pick(n, (256, 128))
    tok = lambda i: (0, i)
    const = lambda i: (0, 0)
    return pl.pallas_call(
        _peer_select_kernel,
        grid=(n // tn,),
        in_specs=[pl.BlockSpec((m, tn), tok), pl.BlockSpec(k1.shape, const), pl.BlockSpec(k2.shape, const)],
        out_specs=[pl.BlockSpec((rows, tn), tok)] * 4,
        out_shape=[jax.ShapeDtypeStruct((rows, n), F32), jax.ShapeDtypeStruct((rows, n), F32),
                   jax.ShapeDtypeStruct((rows, n), BF16), jax.ShapeDtypeStruct((rows, n), BF16)],
        scratch_shapes=[pltpu.VMEM((tn // LANES, rows, LANES), F32)] * 4,
        compiler_params=_params("parallel"),
        name="peer_select",
    )(q_t, k1, k2)


PEER_EXPERT_BLOCK = 1024


def _peer_dense_kernel(xt_ref, xn_ref, u_ref, vt_ref, c1_ref, e1_ref, r2_ref, e2_ref, y_ref,
                       acc_scr, act_scr):
    j = pl.program_id(1)
    nh = PEER_HEADS
    tn = xn_ref.shape[1]
    pack = 2 * SUBLANES
    i1_per_blk = u_ref.shape[0] // N_KEYS

    @pl.when(j == 0)
    def _():
        acc_scr[...] = jnp.zeros_like(acc_scr)

    wide = 2 * LANES

    def gate_block(il, nc):
        erows = pl.ds(il * N_KEYS, N_KEYS)
        hrows = pl.ds(il * nh, nh)
        hblk = jnp.dot(u_ref[erows, :], xn_ref[:, pl.ds(nc * wide, wide)], preferred_element_type=F32)
        for lw in range(wide // LANES):
            lc = nc * (wide // LANES) + lw
            lanes = pl.ds(lc * LANES, LANES)
            c1b = [jnp.broadcast_to(c1_ref[pl.ds(il * nh + h, 1), lanes], (pack, LANES)).astype(BF16)
                   for h in range(nh)]
            e1b = [jnp.broadcast_to(e1_ref[pl.ds(il * nh + h, 1), lanes], (pack, LANES)).astype(BF16)
                   for h in range(nh)]
            zero = jnp.zeros((pack, LANES), BF16)
            half = jnp.full((), 0.5, BF16)
            for it in range(N_KEYS // pack):
                rows = pl.ds(il * N_KEYS + it * pack, pack)
                gate = None
                for h in range(nh):
                    krows = pl.ds(h * N_KEYS + it * pack, pack)
                    sel = jnp.minimum(jnp.maximum(c1b[h] - r2_ref[krows, lanes], zero), e2_ref[krows, lanes])
                    term = sel * e1b[h]
                    gate = term if gate is None else gate + term
                x = hblk[it * pack:(it + 1) * pack, lw * LANES:(lw + 1) * LANES]
                t = jnp.tanh(x * (GELU_C0 + GELU_C1 * (x * x)))
                act_scr[rows, lanes] = (x.astype(BF16) * (half + half * t.astype(BF16))) * gate

    for nc in range(tn // wide):
        for il in range(i1_per_blk):
            gate_block(il, nc)
        cols = pl.ds(nc * wide, wide)
        acc_scr[:, cols] += jnp.dot(vt_ref[...], act_scr[:, cols], preferred_element_type=F32)

    @pl.when(j == pl.num_programs(1) - 1)
    def _():
        y_ref[...] = (xt_ref[...] + acc_scr[...]).T


def peer_dense(xt, xn_t, u_bf16, vt_bf16, c1, e1, r2, e2):
    d, n = xt.shape
    ne = u_bf16.shape[0]
    tn = _pick(n, (512, 256, 128))
    te = PEER_EXPERT_BLOCK
    i1_per_blk = te // N_KEYS
    tok = lambda i, j: (0, i)
    return pl.pallas_call(
        _peer_dense_kernel,
        grid=(n // tn, ne // te),
        in_specs=[
            pl.BlockSpec((d, tn), tok),
            pl.BlockSpec((d, tn), tok),
            pl.BlockSpec((te, d), lambda i, j: (j, 0)),
            pl.BlockSpec((d, te), lambda i, j: (0, j)),
            pl.BlockSpec((i1_per_blk * PEER_HEADS, tn), lambda i, j: (j, i)),
            pl.BlockSpec((i1_per_blk * PEER_HEADS, tn), lambda i, j: (j, i)),
            pl.BlockSpec((PEER_HEADS * N_KEYS, tn), tok),
            pl.BlockSpec((PEER_HEADS * N_KEYS, tn), tok),
        ],
        out_specs=pl.BlockSpec((tn, d), lambda i, j: (i, 0)),
        out_shape=jax.ShapeDtypeStruct((n, d), F32),
        scratch_shapes=[pltpu.VMEM((d, tn), F32), pltpu.VMEM((te, tn), BF16)],
        compiler_params=_params("parallel", "arbitrary"),
        name="peer_dense",
    )(xt, xn_t, u_bf16, vt_bf16, c1, e1, r2, e2)


def _interleaved_keys(keys_p):
    nh, nk, half = keys_p.shape
    eye = jnp.eye(nh, dtype=keys_p.dtype)
    return jnp.einsum("hkd,hg->khgd", keys_p, eye).reshape(nk * nh, nh * half).astype(BF16)


def peer_layer(xt, norm_g, wq, keys, u_tab, v_tab):
    d, n = xt.shape
    wq_t = wq.reshape(d, PEER_HEADS, 2, PEER_HALF).transpose(2, 1, 3, 0).reshape(2 * PEER_HEADS * PEER_HALF, d)
    xn_t, q_t = peer_query(xt, norm_g, wq_t.astype(BF16))
    k1 = _interleaved_keys(keys[:, 0])
    k2 = _interleaved_keys(keys[:, 1])
    c1, e1, r2, e2 = peer_select(q_t, k1, k2)
    return peer_dense(xt, xn_t, u_tab.astype(BF16), v_tab.T.astype(BF16), c1, e1, r2, e2)


def _rmsnorm_kernel(x_ref, g_ref, y_ref):
    x = x_ref[...]
    ms = jnp.mean(x * x, axis=-1, keepdims=True)
    y_ref[...] = (x * lax.rsqrt(ms + EPS)) * g_ref[...]


def final_norm(x, row0, rows, g):
    d = x.shape[1]
    tm = _pick(math.gcd(rows, row0) if row0 else rows, (512, 256, 128, 64, 32, 16, 8))
    blk0 = row0 // tm
    return pl.pallas_call(
        _rmsnorm_kernel,
        grid=(rows // tm,),
        in_specs=[pl.BlockSpec((tm, d), lambda i: (blk0 + i, 0)), pl.BlockSpec((1, d), lambda i: (0, 0))],
        out_specs=pl.BlockSpec((tm, d), lambda i: (i, 0)),
        out_shape=jax.ShapeDtypeStruct((rows, d), F32),
        compiler_params=_params("parallel"),
        name="final_norm",
    )(x, g.reshape(1, d))


def _to_time_major(p, row0, b, t, col):
    blk = lax.slice(p, (row0, col * D_RNN), (row0 + b * t, (col + 1) * D_RNN))
    return blk.reshape(b, t, D_RNN).transpose(1, 0, 2).reshape(t * b, D_RNN)


def _trunk(groups, norm1_g, norm2_g, normf_g, w_in, ret_gn_g, w_ret_out, conv_w, conv_b, rg_wa, rg_ba, rg_wx,
           rg_bx, rg_lambda, w_rnn_out, w_o, peer_wq, peer_keys, peer_u, peer_v):
    shapes = [(g[0].shape[0], g[0].shape[1]) for g in groups]
    x = jnp.concatenate([g[0].reshape(-1, D_MODEL) for g in groups], axis=0)
    row0s = np.cumsum([0] + [b * t for b, t in shapes]).tolist()
    states = [([], [], []) for _ in groups]
    for l in range(DEPTH):
        p = norm_matmul(x, norm1_g[l], w_in[l].astype(BF16))
        wa4 = _block_diag_tiles(rg_wa[l], 256)
        wx4 = _block_diag_tiles(rg_wx[l], 256)
        ret_parts, rnn_parts = [], []
        for gi, (xg, r0, h0, buf0, pos0) in enumerate(groups):
            b, t = shapes[gi]
            row0 = row0s[gi]
            o, r_new = retention_path(p, row0, b, t, pos0, r0[l], ret_gn_g[l])
            xr_tm = _to_time_major(p, row0, b, t, 3)
            g_tm = _to_time_major(p, row0, b, t, 4)
            buf_tm = buf0[l].transpose(1, 0, 2).reshape((CONV_W - 1) * b, D_RNN)
            hg, h_last, nb = rglru_path(xr_tm, g_tm, buf_tm, h0[l], b, t, conv_w[l], conv_b[l], wa4, rg_ba[l],
                                        wx4, rg_bx[l], rg_lambda[l])
            ret_parts.append(o)
            rnn_parts.append(hg.reshape(t, b, D_RNN).transpose(1, 0, 2).reshape(b * t, D_RNN))
            states[gi][0].append(r_new)
            states[gi][1].append(h_last)
            states[gi][2].append(nb.reshape(CONV_W - 1, b, D_RNN).transpose(1, 0, 2))
        xt = merge_proj(x, ret_parts, rnn_parts, p, w_ret_out[l].astype(BF16), w_rnn_out[l].astype(BF16),
                        w_o[l].astype(BF16))
        x = peer_layer(xt, norm2_g[l], peer_wq[l], peer_keys[l], peer_u[l], peer_v[l])
    outs = []
    for gi, (b, t) in enumerate(shapes):
        y = final_norm(x, row0s[gi], b * t, normf_g).reshape(b, t, D_MODEL)
        outs.append((y, jnp.stack(states[gi][0]), jnp.stack(states[gi][1]), jnp.stack(states[gi][2])))
    return outs


def kernel(x_prompt, x_sample, state_ret, state_rnn, state_conv, norm1_g, norm2_g, normf_g, w_in, ret_gn_g,
           w_ret_out, conv_w, conv_b, rg_wa, rg_ba, rg_wx, rg_bx, rg_lambda, w_rnn_out, w_o, peer_wq, peer_keys,
           peer_u, peer_v):
    bp = x_prompt.shape[0]
    dt = x_prompt.dtype
    zr = jnp.zeros((DEPTH, bp, RET_HEADS, RET_DK, RET_DV), dt)
    zh = jnp.zeros((DEPTH, bp, D_RNN), dt)
    zc = jnp.zeros((DEPTH, bp, CONV_W - 1, D_RNN), dt)
    groups = [(x_prompt, zr, zh, zc, 0.0), (x_sample, state_ret, state_rnn, state_conv, float(PAST_LEN))]
    (yp, rp, hp, cp), (ys, rs, hs, cs) = _trunk(
        groups, norm1_g, norm2_g, normf_g, w_in, ret_gn_g, w_ret_out, conv_w, conv_b, rg_wa, rg_ba, rg_wx, rg_bx,
        rg_lambda, w_rnn_out, w_o, peer_wq, peer_keys, peer_u, peer_v)
    return (yp, ys, rp, hp, cp, rs, hs, cs)
```

```python
import functools
import math

import jax
import jax.numpy as jnp
import numpy as np
from jax import lax
from jax.experimental import pallas as pl
from jax.experimental.pallas import tpu as pltpu

D_MODEL = 1024
DEPTH = 2
PAST_LEN = 16384
RET_HEADS = 8
RET_DK = 64
RET_DV = 128
RET_QK = RET_HEADS * RET_DK
RET_V = RET_HEADS * RET_DV
RET_CHUNK = 128
ROPE_BASE = 10000.0
D_RNN = 1024
RNN_BLOCKS = 16
RNN_BS = D_RNN // RNN_BLOCKS
CONV_W = 4
RG_C = 8.0
IN_SIZES = (RET_QK, RET_QK, RET_V, RET_V, D_RNN, D_RNN, D_MODEL, D_MODEL)
N_IN = sum(IN_SIZES)
PEER_HEADS = 8
N_KEYS = 128
N_EXPERTS = N_KEYS * N_KEYS
PEER_DKEY = 256
PEER_HALF = PEER_DKEY // 2
PEER_TOPK = 16
EPS = 1e-6

SUBLANES = 8
LANES = 128
VMEM_LIMIT = 56 * 1024 * 1024

F32 = jnp.float32
BF16 = jnp.bfloat16


def _params(*sem, flags=None):
    return pltpu.CompilerParams(dimension_semantics=sem, vmem_limit_bytes=VMEM_LIMIT, flags=flags)


def _pick(n, prefs):
    for p in prefs:
        if n % p == 0:
            return p
    return n


def _norm_matmul_kernel(x_ref, g_ref, w_ref, y_ref, xn_scr):
    @pl.when(pl.program_id(1) == 0)
    def _():
        x = x_ref[...]
        ms = jnp.mean(x * x, axis=-1, keepdims=True)
        xn_scr[...] = ((x * lax.rsqrt(ms + EPS)) * g_ref[...]).astype(BF16)

    y_ref[...] = jnp.dot(xn_scr[...], w_ref[...], preferred_element_type=F32)


def norm_matmul(x, g, w_bf16):
    n, d = x.shape
    m = w_bf16.shape[1]
    tm = _pick(n, (1024, 512, 256, 128, 64, 32, 16, 8))
    tn = _pick(m, (1024, 512, 256, 128))
    return pl.pallas_call(
        _norm_matmul_kernel,
        grid=(n // tm, m // tn),
        in_specs=[
            pl.BlockSpec((tm, d), lambda i, j: (i, 0)),
            pl.BlockSpec((1, d), lambda i, j: (0, 0)),
            pl.BlockSpec((d, tn), lambda i, j: (0, j)),
        ],
        out_specs=pl.BlockSpec((tm, tn), lambda i, j: (i, j)),
        out_shape=jax.ShapeDtypeStruct((n, m), F32),
        scratch_shapes=[pltpu.VMEM((tm, d), BF16)],
        compiler_params=_params("parallel", "arbitrary"),
        name="norm_in_proj",
    )(x, g.reshape(1, d), w_bf16)


def _rot_half(x):
    n = x.shape[-1]
    half = RET_DK // 2
    fwd = pltpu.roll(x, half, axis=1)
    bwd = pltpu.roll(x, n - half, axis=1)
    lane = lax.broadcasted_iota(jnp.int32, x.shape, 1)
    return jnp.where((lane % RET_DK) < half, bwd, fwd)


def _retention_kernel(q_ref, k_ref, v_ref, g_ref, r0_ref, cos_ref, sin_ref, mask_ref, qw_ref, kw_ref,
                      gc_ref, gn_ref, o_ref, r_out_ref, r_scr):
    c = pl.program_id(1)

    @pl.when(c == 0)
    def _():
        r_scr[...] = r0_ref[0]

    cos = cos_ref[...]
    sin = sin_ref[...]
    q = q_ref[...]
    k = k_ref[...]
    qr = q * cos + _rot_half(q) * sin
    kr = (k * cos + _rot_half(k) * sin) * (RET_DK ** -0.5)
    qd = (qr * qw_ref[...]).astype(BF16)
    kd = (kr * kw_ref[...]).astype(BF16)
    qb = qr.astype(BF16)
    kb = kr.astype(BF16)
    v = v_ref[...].astype(BF16)
    g = g_ref[...]
    for h in range(RET_HEADS):
        ks = slice(h * RET_DK, (h + 1) * RET_DK)
        vs = slice(h * RET_DV, (h + 1) * RET_DV)
        vh = v[:, vs]
        s = lax.dot_general(qb[:, ks], kb[:, ks], (((1,), (1,)), ((), ())),
                            preferred_element_type=F32) * mask_ref[h]
        r_h = r_scr[h]
        o = jnp.dot(s.astype(BF16), vh, preferred_element_type=F32)
        o = o + jnp.dot(qd[:, ks], r_h.astype(BF16), preferred_element_type=F32)
        kv = lax.dot_general(kd[:, ks], vh, (((0,), (0,)), ((), ())), preferred_element_type=F32)
        r_scr[h] = gc_ref[h] * r_h + kv
        mu = jnp.mean(o, axis=-1, keepdims=True)
        var = jnp.mean(jnp.square(o - mu), axis=-1, keepdims=True)
        on = ((o - mu) * lax.rsqrt(var + EPS)) * gn_ref[:, vs]
        gh = g[:, vs]
        o_ref[:, vs] = ((gh * jax.nn.sigmoid(gh)) * on).astype(BF16)

    @pl.when(c == pl.num_programs(1) - 1)
    def _():
        r_out_ref[0] = r_scr[...]


def retention_path(p, row0, b, t, pos0, r0, gn_g):
    c = RET_CHUNK if t % RET_CHUNK == 0 else t
    nc = t // c
    blk0 = row0 // c
    assert row0 % c == 0
    log_g = jnp.log1p(-(2.0 ** (-5.0 - jnp.arange(RET_HEADS, dtype=F32))))
    idx = jnp.arange(c, dtype=F32)
    diff = idx[:, None] - idx[None, :]
    mask = jnp.where(diff[None] >= 0, jnp.exp(jnp.maximum(diff, 0.0)[None] * log_g[:, None, None]), 0.0)
    k_w = jnp.exp((c - 1 - idx)[:, None] * log_g[None, :])
    q_w = jnp.exp((idx + 1.0)[:, None] * log_g[None, :])
    g_c = jnp.exp(c * log_g)
    qw_tab = jnp.repeat(q_w, RET_DK, axis=1)
    kw_tab = jnp.repeat(k_w, RET_DK, axis=1)
    gc_tab = jnp.broadcast_to(g_c[:, None, None], (RET_HEADS, 1, RET_DV))
    half = RET_DK // 2
    pos = pos0 + jnp.arange(t, dtype=F32)
    freq = ROPE_BASE ** (-jnp.arange(half, dtype=F32) / half)
    ang = pos[:, None] * freq[None, :]
    cos_h = jnp.concatenate([jnp.cos(ang), jnp.cos(ang)], axis=1)
    sin_h = jnp.concatenate([-jnp.sin(ang), jnp.sin(ang)], axis=1)
    cos_tab = jnp.tile(cos_h, (1, RET_HEADS))
    sin_tab = jnp.tile(sin_h, (1, RET_HEADS))

    rows = lambda bi, ci: blk0 + bi * nc + ci
    o, r_new = pl.pallas_call(
        _retention_kernel,
        grid=(b, nc),
        in_specs=[
            pl.BlockSpec((c, RET_QK), lambda bi, ci: (rows(bi, ci), 0)),
            pl.BlockSpec((c, RET_QK), lambda bi, ci: (rows(bi, ci), 1)),
            pl.BlockSpec((c, RET_V), lambda bi, ci: (rows(bi, ci), 1)),
            pl.BlockSpec((c, RET_V), lambda bi, ci: (rows(bi, ci), 2)),
            pl.BlockSpec((1, RET_HEADS, RET_DK, RET_DV), lambda bi, ci: (bi, 0, 0, 0)),
            pl.BlockSpec((c, RET_QK), lambda bi, ci: (ci, 0)),
            pl.BlockSpec((c, RET_QK), lambda bi, ci: (ci, 0)),
            pl.BlockSpec((RET_HEADS, c, c), lambda bi, ci: (0, 0, 0)),
            pl.BlockSpec((c, RET_QK), lambda bi, ci: (0, 0)),
            pl.BlockSpec((c, RET_QK), lambda bi, ci: (0, 0)),
            pl.BlockSpec((RET_HEADS, 1, RET_DV), lambda bi, ci: (0, 0, 0)),
            pl.BlockSpec((1, RET_V), lambda bi, ci: (0, 0)),
        ],
        out_specs=[
            pl.BlockSpec((c, RET_V), lambda bi, ci: (bi * nc + ci, 0)),
            pl.BlockSpec((1, RET_HEADS, RET_DK, RET_DV), lambda bi, ci: (bi, 0, 0, 0)),
        ],
        out_shape=[
            jax.ShapeDtypeStruct((b * t, RET_V), BF16),
            jax.ShapeDtypeStruct((b, RET_HEADS, RET_DK, RET_DV), F32),
        ],
        scratch_shapes=[pltpu.VMEM((RET_HEADS, RET_DK, RET_DV), F32)],
        compiler_params=_params("parallel", "arbitrary"),
        name="retention",
    )(p, p, p, p, r0, cos_tab, sin_tab, mask, qw_tab, kw_tab, gc_tab, gn_g.reshape(1, RET_V))
    return o, r_new


GELU_C0 = math.sqrt(2.0 / math.pi)
GELU_C1 = GELU_C0 * 0.044715


def _gelu_tanh(x):
    return x * (0.5 * (1.0 + jnp.tanh(math.sqrt(2.0 / math.pi) * (x + 0.044715 * (x * x * x)))))


def _rglru_kernel(bsz, tc, xr_ref, g_ref, buf_ref, h0_ref, cw_ref, cb_ref, wa_ref, ba_ref, wx_ref, bx_ref,
                  lam_ref, o_ref, hl_ref, nb_ref, xcat_scr, a_scr, u_scr, hs_scr):
    rows = tc * bsz
    hist = (CONV_W - 1) * bsz
    step = pl.program_id(0)

    @pl.when(step == 0)
    def _():
        xcat_scr[pl.ds(rows, hist), :] = buf_ref[...]
        hs_scr[pl.ds(rows, bsz), :] = h0_ref[...]

    xcat_scr[pl.ds(0, hist), :] = xcat_scr[pl.ds(rows, hist), :]
    hs_scr[pl.ds(0, bsz), :] = hs_scr[pl.ds(rows, bsz), :]
    xcat_scr[pl.ds(hist, rows), :] = xr_ref[...]

    xc = cb_ref[...] + xcat_scr[pl.ds(0, rows), :] * cw_ref[0:1, :]
    for w in range(1, CONV_W):
        xc = xc + xcat_scr[pl.ds(w * bsz, rows), :] * cw_ref[w:w + 1, :]
    xcb = xc.astype(BF16)
    nblk = wa_ref.shape[0]
    wdt = wa_ref.shape[1]
    ra = jnp.concatenate(
        [jnp.dot(xcb[:, j * wdt:(j + 1) * wdt], wa_ref[j], preferred_element_type=F32) for j in range(nblk)],
        axis=1)
    ri = jnp.concatenate(
        [jnp.dot(xcb[:, j * wdt:(j + 1) * wdt], wx_ref[j], preferred_element_type=F32) for j in range(nblk)],
        axis=1)
    r = jax.nn.sigmoid(ra + ba_ref[...])
    i = jax.nn.sigmoid(ri + bx_ref[...])
    z = -lam_ref[...]
    softplus = jnp.maximum(z, 0.0) + jnp.log1p(jnp.exp(-jnp.abs(z)))
    log_a = (-RG_C * r) * softplus
    a = jnp.exp(log_a)
    one_m_a2 = -jnp.tanh(log_a) * (a * a + 1.0)
    a_scr[...] = a
    u_scr[...] = jnp.sqrt(one_m_a2) * (i * xc)

    def scan_step(t, carry):
        prev = hs_scr[pl.ds(pl.multiple_of(t * bsz, bsz), bsz), :]
        cur = pl.ds(pl.multiple_of(t * bsz, bsz), bsz)
        h = a_scr[cur, :] * prev + u_scr[cur, :]
        hs_scr[pl.ds(pl.multiple_of((t + 1) * bsz, bsz), bsz), :] = h
        return carry

    lax.fori_loop(0, tc, scan_step, 0)
    hs = hs_scr[pl.ds(bsz, rows), :]
    o_ref[...] = (hs * _gelu_tanh(g_ref[...])).astype(BF16)

    @pl.when(step == pl.num_programs(0) - 1)
    def _():
        hl_ref[...] = hs_scr[pl.ds(rows, bsz), :]
        nb_ref[...] = xcat_scr[pl.ds(rows, hist), :]


def rglru_path(xr_tm, g_tm, buf_tm, h0, b, t, conv_w, conv_b, wa4, ba, wx4, bx, lam):
    assert t >= CONV_W - 1
    rows_target = 256
    tc = max(1, min(t, rows_target // b))
    while t % tc:
        tc -= 1
    rows = tc * b
    hist = (CONV_W - 1) * b
    d = D_RNN
    nblk, wdt = wa4.shape[0], wa4.shape[1]
    const2 = lambda s: (0, 0)
    out, h_last, new_buf = pl.pallas_call(
        functools.partial(_rglru_kernel, b, tc),
        grid=(t // tc,),
        in_specs=[
            pl.BlockSpec((rows, d), lambda s: (s, 0)),
            pl.BlockSpec((rows, d), lambda s: (s, 0)),
            pl.BlockSpec((hist, d), const2),
            pl.BlockSpec((b, d), const2),
            pl.BlockSpec((CONV_W, d), const2),
            pl.BlockSpec((1, d), const2),
            pl.BlockSpec((nblk, wdt, wdt), lambda s: (0, 0, 0)),
            pl.BlockSpec((1, d), const2),
            pl.BlockSpec((nblk, wdt, wdt), lambda s: (0, 0, 0)),
            pl.BlockSpec((1, d), const2),
            pl.BlockSpec((1, d), const2),
        ],
        out_specs=[
            pl.BlockSpec((rows, d), lambda s: (s, 0)),
            pl.BlockSpec((b, d), const2),
            pl.BlockSpec((hist, d), const2),
        ],
        out_shape=[
            jax.ShapeDtypeStruct((t * b, d), BF16),
            jax.ShapeDtypeStruct((b, d), F32),
            jax.ShapeDtypeStruct((hist, d), F32),
        ],
        scratch_shapes=[
            pltpu.VMEM((rows + hist, d), F32),
            pltpu.VMEM((rows, d), F32),
            pltpu.VMEM((rows, d), F32),
            pltpu.VMEM((rows + b, d), F32),
        ],
        compiler_params=_params("arbitrary"),
        name="rglru",
    )(xr_tm, g_tm, buf_tm, h0, conv_w, conv_b.reshape(1, d), wa4, ba.reshape(1, d), wx4, bx.reshape(1, d),
      lam.reshape(1, d))
    return out, h_last, new_buf


def _block_diag_tiles(w, tile):
    nb, bs, _ = w.shape
    per = tile // bs
    w = w.reshape(nb // per, per, bs, bs)
    eye = jnp.eye(per, dtype=w.dtype)
    dense = jnp.einsum("gpcd,pq->gpcqd", w, eye).reshape(nb // per, tile, tile)
    return dense.astype(BF16)


def _merge_kernel(blocks_a, x_ref, roa_ref, rob_ref, rna_ref, rnb_ref, ga_ref, gb_ref, wr_ref, wn_ref, wo_ref,
                  yt_ref):
    first = pl.program_id(0) < blocks_a
    ro = jnp.where(first, roa_ref[...], rob_ref[...])
    rn = jnp.where(first, rna_ref[...], rnb_ref[...])
    ret_out = jnp.dot(ro, wr_ref[...], preferred_element_type=F32)
    rnn_out = jnp.dot(rn, wn_ref[...], preferred_element_type=F32)
    merged = jax.nn.sigmoid(ga_ref[...]) * ret_out + jax.nn.sigmoid(gb_ref[...]) * rnn_out
    y = x_ref[...] + jnp.dot(merged.astype(BF16), wo_ref[...], preferred_element_type=F32)
    yt_ref[...] = y.T


def merge_proj(x, ret_parts, rnn_parts, p, w_ret_out, w_rnn_out, w_o):
    n, d = x.shape
    (ret_a, ret_b), (rnn_a, rnn_b) = ret_parts, rnn_parts
    rows_a, rows_b = ret_a.shape[0], ret_b.shape[0]
    tm = _pick(math.gcd(rows_a, rows_b), (512, 256, 128))
    blocks_a = rows_a // tm
    row = lambda i: (i, 0)
    row_a = lambda i: (jnp.minimum(i, blocks_a - 1), 0)
    row_b = lambda i: (jnp.maximum(i - blocks_a, 0), 0)
    const = lambda i: (0, 0)
    return pl.pallas_call(
        functools.partial(_merge_kernel, blocks_a),
        grid=(n // tm,),
        in_specs=[
            pl.BlockSpec((tm, d), row),
            pl.BlockSpec((tm, d), row_a),
            pl.BlockSpec((tm, d), row_b),
            pl.BlockSpec((tm, d), row_a),
            pl.BlockSpec((tm, d), row_b),
            pl.BlockSpec((tm, d), lambda i: (i, 5)),
            pl.BlockSpec((tm, d), lambda i: (i, 6)),
            pl.BlockSpec((d, d), const),
            pl.BlockSpec((d, d), const),
            pl.BlockSpec((d, d), const),
        ],
        out_specs=pl.BlockSpec((d, tm), lambda i: (0, i)),
        out_shape=jax.ShapeDtypeStruct((d, n), F32),
        compiler_params=_params("parallel"),
        name="merge_out_proj",
    )(x, ret_a, ret_b, rnn_a, rnn_b, p, p, w_ret_out, w_rnn_out, w_o)


def _peer_query_kernel(xt_ref, g_ref, wq_ref, xn_ref, q_ref):
    x = xt_ref[...]
    ms = jnp.mean(x * x, axis=0, keepdims=True)
    xn = ((x * lax.rsqrt(ms + EPS)) * g_ref[...]).astype(BF16)
    xn_ref[...] = xn
    q_ref[...] = jnp.dot(wq_ref[...], xn, preferred_element_type=F32)


def peer_query(xt, g, wq_t):
    d, n = xt.shape
    m = wq_t.shape[0]
    tn = _pick(n, (512, 256, 128))
    return pl.pallas_call(
        _peer_query_kernel,
        grid=(n // tn,),
        in_specs=[
            pl.BlockSpec((d, tn), lambda i: (0, i)),
            pl.BlockSpec((d, 1), lambda i: (0, 0)),
            pl.BlockSpec((m, d), lambda i: (0, 0)),
        ],
        out_specs=[pl.BlockSpec((d, tn), lambda i: (0, i)), pl.BlockSpec((m, tn), lambda i: (0, i))],
        out_shape=[jax.ShapeDtypeStruct((d, n), BF16), jax.ShapeDtypeStruct((m, n), F32)],
        compiler_params=_params("parallel"),
        name="peer_query",
    )(xt, g.reshape(d, 1), wq_t)


def _sort_pairs(n):
    def merge(lo, hi, r):
        step = r * 2
        if step < hi - lo:
            yield from merge(lo, hi, step)
            yield from merge(lo + r, hi, step)
            yield from [(i, i + r) for i in range(lo + r, hi - r, step)]
        else:
            yield (lo, lo + r)

    def sort(lo, hi):
        if hi - lo >= 1:
            mid = lo + (hi - lo) // 2
            yield from sort(lo, mid)
            yield from sort(mid + 1, hi)
            yield from merge(lo, hi, 1)

    return list(sort(0, n - 1))


_SORT16 = _sort_pairs(PEER_TOPK)


def _cmpx(vals, i, j):
    a, b = vals[i], vals[j]
    if b is None:
        return
    if a is None:
        vals[i], vals[j] = b, None
        return
    vals[i], vals[j] = jnp.maximum(a, b), jnp.minimum(a, b)


def _sort_desc(vals):
    vals = list(vals)
    for i, j in _SORT16:
        _cmpx(vals, i, j)
    return vals


def _merge_top(a, b):
    k = PEER_TOPK
    a = list(a) + [None] * (k - len(a))
    b = list(b) + [None] * (k - len(b))
    out = []
    for r in range(k):
        x, y = a[r], b[k - 1 - r]
        out.append(y if x is None else (x if y is None else jnp.maximum(x, y)))
    d = k // 2
    while d >= 1:
        for i in range(k):
            if not i & d:
                _cmpx(out, i, i + d)
        d //= 2
    return out


def _top_sorted(ref):
    groups = []
    for g0 in range(0, N_KEYS, PEER_TOPK):
        vals = [ref[pl.ds((g0 + j) * PEER_HEADS, PEER_HEADS), :] for j in range(PEER_TOPK)]
        groups.append(_sort_desc(vals))
    while len(groups) > 1:
        groups = [_merge_top(groups[i], groups[i + 1]) for i in range(0, len(groups), 2)]
    return groups[0]


def _peer_select_kernel(q_ref, k1_ref, k2_ref, c1_ref, e1_ref, r2_ref, e2_ref, s1_scr, s2_scr, r2_scr, e2_scr):
    nh = PEER_HEADS
    half_rows = q_ref.shape[0] // 2
    tn = q_ref.shape[1]
    s1 = jnp.dot(k1_ref[...], q_ref[pl.ds(0, half_rows), :].astype(BF16), preferred_element_type=F32)
    s2 = jnp.dot(k2_ref[...], q_ref[pl.ds(half_rows, half_rows), :].astype(BF16), preferred_element_type=F32)
    for lt in range(tn // LANES):
        s1_scr[lt] = s1[:, lt * LANES:(lt + 1) * LANES]
        s2_scr[lt] = s2[:, lt * LANES:(lt + 1) * LANES]
    for lt in range(tn // LANES):
        lanes = pl.ds(lt * LANES, LANES)
        s1_t, s2_t, r2_t, e2_t = s1_scr.at[lt], s2_scr.at[lt], r2_scr.at[lt], e2_scr.at[lt]
        a = _top_sorted(s1_t)
        b = _top_sorted(s2_t)
        k = PEER_TOPK
        lists = []
        for j in range(1, k + 1):
            col = [a[r - 1] + b[j - 1] for r in range(j, k // j + 1)]
            row = [a[j - 1] + b[s - 1] for s in range(j + 1, k // j + 1)]
            if col:
                lists.append(col)
            if row:
                lists.append(row)
        top = lists[0]
        for other in lists[1:]:
            top = _merge_top(top, other)
        tau = top[k - 1]
        z = jnp.ones_like(tau)
        for r in range(1, k):
            z = z + jnp.exp(top[r] - top[0])
        zinv = 1.0 / z
        inf = jnp.full((nh, LANES), jnp.inf, F32)
        phi = []
        for s in range(1, k + 1):
            p = inf
            for r in range(1, k // s + 1):
                p = jnp.where(a[r - 1] + b[s - 1] >= tau, a[r - 1], p)
            phi.append(p)

        def per_key(kk, carry):
            rows = pl.ds(pl.multiple_of(kk * nh, nh), nh)
            s1k = s1_t[rows, :]
            s2k = s2_t[rows, :]
            cnt = jnp.ones((nh, LANES), F32)
            for s in range(k):
                cnt = jnp.where(s1k >= phi[s], float(s + 2), cnt)
            rank = jnp.full((nh, LANES), float(k + 1), F32)
            for s in range(k - 1, -1, -1):
                rank = jnp.where(s2k >= b[s], float(s + 1), rank)
            c1_ref[rows, lanes] = cnt
            e1_ref[rows, lanes] = jnp.exp(s1k - a[0])
            r2_t[rows, :] = rank
            e2_t[rows, :] = jnp.exp(s2k - b[0]) * zinv
            return carry

        lax.fori_loop(0, N_KEYS, per_key, 0, unroll=2)
        pack = 2 * SUBLANES
        for h in range(nh):
            for kt in range(N_KEYS // pack):
                lo = pl.ds(kt * pack * nh + h, SUBLANES, stride=nh)
                hi = pl.ds((kt * pack + SUBLANES) * nh + h, SUBLANES, stride=nh)
                dst = pl.ds(h * N_KEYS + kt * pack, pack)
                r2_ref[dst, lanes] = jnp.concatenate([r2_t[lo, :], r2_t[hi, :]], axis=0).astype(BF16)
                e2_ref[dst, lanes] = jnp.concatenate([e2_t[lo, :], e2_t[hi, :]], axis=0).astype(BF16)


def peer_select(q_t, k1, k2):
    m, n = q_t.shape
    rows = PEER_HEADS * N_KEYS
    tn = _pick(n, (256, 128))
    tok = lambda i: (0, i)
    const = lambda i: (0, 0)
    return pl.pallas_call(
        _peer_select_kernel,
        grid=(n // tn,),
        in_specs=[pl.BlockSpec((m, tn), tok), pl.BlockSpec(k1.shape, const), pl.BlockSpec(k2.shape, const)],
        out_specs=[pl.BlockSpec((rows, tn), tok)] * 4,
        out_shape=[jax.ShapeDtypeStruct((rows, n), F32), jax.ShapeDtypeStruct((rows, n), F32),
                   jax.ShapeDtypeStruct((rows, n), BF16), jax.ShapeDtypeStruct((rows, n), BF16)],
        scratch_shapes=[pltpu.VMEM((tn // LANES, rows, LANES), F32)] * 4,
        compiler_params=_params("parallel"),
        name="peer_select",
    )(q_t, k1, k2)


PEER_EXPERT_BLOCK = 1024


PEER_PIPE_DEPTH = 2


def _peer_dense_kernel(nblk, xt_ref, xn_ref, u_ref, vt_ref, c1_ref, e1_ref, r2_ref, e2_ref, y_ref,
                       acc_scr, h0_scr, h1_scr, act0_scr, act1_scr):
    j = pl.program_id(0)
    nh = PEER_HEADS
    tn = xn_ref.shape[1]
    pack = 2 * SUBLANES
    i1_per_blk = u_ref.shape[0] // N_KEYS
    halves = 2
    half_lanes = tn // halves

    @pl.when(j == 0)
    def _():
        acc_scr[...] = jnp.zeros_like(acc_scr)
        h1_scr[...] = jnp.zeros_like(h1_scr)
        act0_scr[...] = jnp.zeros_like(act0_scr)

    def gate_block(il, nc, h_cur, act_scr):
        for lw in range(half_lanes // LANES):
            lc = nc * (half_lanes // LANES) + lw
            lanes = pl.ds(lc * LANES, LANES)
            c1b = [jnp.broadcast_to(c1_ref[pl.ds(il * nh + h, 1), lanes], (pack, LANES)).astype(BF16)
                   for h in range(nh)]
            e1b = [jnp.broadcast_to(e1_ref[pl.ds(il * nh + h, 1), lanes], (pack, LANES)).astype(BF16)
                   for h in range(nh)]
            zero = jnp.zeros((pack, LANES), BF16)
            half = jnp.full((), 0.5, BF16)
            for it in range(N_KEYS // pack):
                rows = pl.ds(il * N_KEYS + it * pack, pack)
                gate = None
                for h in range(nh):
                    krows = pl.ds(h * N_KEYS + it * pack, pack)
                    sel = jnp.minimum(jnp.maximum(c1b[h] - r2_ref[krows, lanes], zero), e2_ref[krows, lanes])
                    term = sel * e1b[h]
                    gate = term if gate is None else gate + term
                x = h_cur[rows, lanes]
                t = jnp.tanh(x * (GELU_C0 + GELU_C1 * (x * x)))
                act_scr[rows, lanes] = (x.astype(BF16) * (half + half * t.astype(BF16))) * gate

    def body(h_next, h_cur, act_cur, act_prev):
        for il in range(i1_per_blk):
            erows = pl.ds(il * N_KEYS, N_KEYS)
            h_next[erows, :] = jnp.dot(u_ref[erows, :], xn_ref[...], preferred_element_type=F32)
            gate_block(il, 0, h_cur, act_cur)
        for il in range(i1_per_blk):
            orows = pl.ds(il * N_KEYS, N_KEYS)
            acc_scr[orows, :] += jnp.dot(vt_ref[orows, :], act_prev[...], preferred_element_type=F32)
            gate_block(il, 1, h_cur, act_cur)

    @pl.when(j % 2 == 0)
    def _():
        body(h0_scr, h1_scr, act1_scr, act0_scr)

    @pl.when(j % 2 == 1)
    def _():
        body(h1_scr, h0_scr, act0_scr, act1_scr)

    done = j - PEER_PIPE_DEPTH

    @pl.when(jnp.logical_and(done >= 0, done % nblk == nblk - 1))
    def _():
        y_ref[...] = (xt_ref[...] + acc_scr[...]).T
        acc_scr[...] = jnp.zeros_like(acc_scr)


def peer_dense(xt, xn_t, u_bf16, vt_bf16, c1, e1, r2, e2):
    d, n = xt.shape
    ne = u_bf16.shape[0]
    tn = _pick(n, (512, 256))
    assert n % tn == 0 and tn % (2 * LANES) == 0
    te = PEER_EXPERT_BLOCK
    assert te == d
    nblk = ne // te
    items = (n // tn) * nblk
    i1_per_blk = te // N_KEYS
    first = lambda j: jnp.minimum(j, items - 1)
    gated = lambda j: jnp.clip(j - 1, 0, items - 1)
    second = lambda j: jnp.maximum(j - PEER_PIPE_DEPTH, 0)
    tb = lambda item: item // nblk
    eb = lambda item: item % nblk
    return pl.pallas_call(
        functools.partial(_peer_dense_kernel, nblk),
        grid=(items + PEER_PIPE_DEPTH,),
        in_specs=[
            pl.BlockSpec((d, tn), lambda j: (0, tb(second(j)))),
            pl.BlockSpec((d, tn), lambda j: (0, tb(first(j)))),
            pl.BlockSpec((te, d), lambda j: (eb(first(j)), 0)),
            pl.BlockSpec((d, te), lambda j: (0, eb(second(j)))),
            pl.BlockSpec((i1_per_blk * PEER_HEADS, tn), lambda j: (eb(gated(j)), tb(gated(j)))),
            pl.BlockSpec((i1_per_blk * PEER_HEADS, tn), lambda j: (eb(gated(j)), tb(gated(j)))),
            pl.BlockSpec((PEER_HEADS * N_KEYS, tn), lambda j: (0, tb(gated(j)))),
            pl.BlockSpec((PEER_HEADS * N_KEYS, tn), lambda j: (0, tb(gated(j)))),
        ],
        out_specs=pl.BlockSpec((tn, d), lambda j: (tb(second(j)), 0)),
        out_shape=jax.ShapeDtypeStruct((n, d), F32),
        scratch_shapes=[pltpu.VMEM((d, tn), F32), pltpu.VMEM((te, tn), F32), pltpu.VMEM((te, tn), F32),
                        pltpu.VMEM((te, tn), BF16), pltpu.VMEM((te, tn), BF16)],
        compiler_params=_params("arbitrary"),
        name="peer_dense",
    )(xt, xn_t, u_bf16, vt_bf16, c1, e1, r2, e2)


def _interleaved_keys(keys_p):
    nh, nk, half = keys_p.shape
    eye = jnp.eye(nh, dtype=keys_p.dtype)
    return jnp.einsum("hkd,hg->khgd", keys_p, eye).reshape(nk * nh, nh * half).astype(BF16)


def peer_layer(xt, norm_g, wq, keys, u_tab, v_tab):
    d, n = xt.shape
    wq_t = wq.reshape(d, PEER_HEADS, 2, PEER_HALF).transpose(2, 1, 3, 0).reshape(2 * PEER_HEADS * PEER_HALF, d)
    xn_t, q_t = peer_query(xt, norm_g, wq_t.astype(BF16))
    k1 = _interleaved_keys(keys[:, 0])
    k2 = _interleaved_keys(keys[:, 1])
    c1, e1, r2, e2 = peer_select(q_t, k1, k2)
    return peer_dense(xt, xn_t, u_tab.astype(BF16), v_tab.T.astype(BF16), c1, e1, r2, e2)


def _rmsnorm_kernel(x_ref, g_ref, y_ref):
    x = x_ref[...]
    ms = jnp.mean(x * x, axis=-1, keepdims=True)
    y_ref[...] = (x * lax.rsqrt(ms + EPS)) * g_ref[...]


def final_norm(x, row0, rows, g):
    d = x.shape[1]
    tm = _pick(math.gcd(rows, row0) if row0 else rows, (512, 256, 128, 64, 32, 16, 8))
    blk0 = row0 // tm
    return pl.pallas_call(
        _rmsnorm_kernel,
        grid=(rows // tm,),
        in_specs=[pl.BlockSpec((tm, d), lambda i: (blk0 + i, 0)), pl.BlockSpec((1, d), lambda i: (0, 0))],
        out_specs=pl.BlockSpec((tm, d), lambda i: (i, 0)),
        out_shape=jax.ShapeDtypeStruct((rows, d), F32),
        compiler_params=_params("parallel"),
        name="final_norm",
    )(x, g.reshape(1, d))


def _to_time_major(p, row0, b, t, col):
    blk = lax.slice(p, (row0, col * D_RNN), (row0 + b * t, (col + 1) * D_RNN))
    return blk.reshape(b, t, D_RNN).transpose(1, 0, 2).reshape(t * b, D_RNN)


def _trunk(groups, norm1_g, norm2_g, normf_g, w_in, ret_gn_g, w_ret_out, conv_w, conv_b, rg_wa, rg_ba, rg_wx,
           rg_bx, rg_lambda, w_rnn_out, w_o, peer_wq, peer_keys, peer_u, peer_v):
    shapes = [(g[0].shape[0], g[0].shape[1]) for g in groups]
    x = jnp.concatenate([g[0].reshape(-1, D_MODEL) for g in groups], axis=0)
    row0s = np.cumsum([0] + [b * t for b, t in shapes]).tolist()
    states = [([], [], []) for _ in groups]
    for l in range(DEPTH):
        p = norm_matmul(x, norm1_g[l], w_in[l].astype(BF16))
        wa4 = _block_diag_tiles(rg_wa[l], 256)
        wx4 = _block_diag_tiles(rg_wx[l], 256)
        ret_parts, rnn_parts = [], []
        for gi, (xg, r0, h0, buf0, pos0) in enumerate(groups):
            b, t = shapes[gi]
            row0 = row0s[gi]
            o, r_new = retention_path(p, row0, b, t, pos0, r0[l], ret_gn_g[l])
            xr_tm = _to_time_major(p, row0, b, t, 3)
            g_tm = _to_time_major(p, row0, b, t, 4)
            buf_tm = buf0[l].transpose(1, 0, 2).reshape((CONV_W - 1) * b, D_RNN)
            hg, h_last, nb = rglru_path(xr_tm, g_tm, buf_tm, h0[l], b, t, conv_w[l], conv_b[l], wa4, rg_ba[l],
                                        wx4, rg_bx[l], rg_lambda[l])
            ret_parts.append(o)
            rnn_parts.append(hg.reshape(t, b, D_RNN).transpose(1, 0, 2).reshape(b * t, D_RNN))
            states[gi][0].append(r_new)
            states[gi][1].append(h_last)
            states[gi][2].append(nb.reshape(CONV_W - 1, b, D_RNN).transpose(1, 0, 2))
        xt = merge_proj(x, ret_parts, rnn_parts, p, w_ret_out[l].astype(BF16), w_rnn_out[l].astype(BF16),
                        w_o[l].astype(BF16))
        x = peer_layer(xt, norm2_g[l], peer_wq[l], peer_keys[l], peer_u[l], peer_v[l])
    outs = []
    for gi, (b, t) in enumerate(shapes):
        y = final_norm(x, row0s[gi], b * t, normf_g).reshape(b, t, D_MODEL)
        outs.append((y, jnp.stack(states[gi][0]), jnp.stack(states[gi][1]), jnp.stack(states[gi][2])))
    return outs


def kernel(x_prompt, x_sample, state_ret, state_rnn, state_conv, norm1_g, norm2_g, normf_g, w_in, ret_gn_g,
           w_ret_out, conv_w, conv_b, rg_wa, rg_ba, rg_wx, rg_bx, rg_lambda, w_rnn_out, w_o, peer_wq, peer_keys,
           peer_u, peer_v):
    bp = x_prompt.shape[0]
    dt = x_prompt.dtype
    zr = jnp.zeros((DEPTH, bp, RET_HEADS, RET_DK, RET_DV), dt)
    zh = jnp.zeros((DEPTH, bp, D_RNN), dt)
    zc = jnp.zeros((DEPTH, bp, CONV_W - 1, D_RNN), dt)
    groups = [(x_prompt, zr, zh, zc, 0.0), (x_sample, state_ret, state_rnn, state_conv, float(PAST_LEN))]
    (yp, rp, hp, cp), (ys, rs, hs, cs) = _trunk(
        groups, norm1_g, norm2_g, normf_g, w_in, ret_gn_g, w_ret_out, conv_w, conv_b, rg_wa, rg_ba, rg_wx, rg_bx,
        rg_lambda, w_rnn_out, w_o, peer_wq, peer_keys, peer_u, peer_v)
    return (yp, ys, rp, hp, cp, rs, hs, cs)
```

```python
import functools
import math

import jax
import jax.numpy as jnp
import numpy as np
from jax import lax
from jax.experimental import pallas as pl
from jax.experimental.pallas import tpu as pltpu

D_MODEL = 1024
DEPTH = 2
PAST_LEN = 16384
RET_HEADS = 8
RET_DK = 64
RET_DV = 128
RET_QK = RET_HEADS * RET_DK
RET_V = RET_HEADS * RET_DV
RET_CHUNK = 128
ROPE_BASE = 10000.0
D_RNN = 1024
RNN_BLOCKS = 16
RNN_BS = D_RNN // RNN_BLOCKS
CONV_W = 4
RG_C = 8.0
IN_SIZES = (RET_QK, RET_QK, RET_V, RET_V, D_RNN, D_RNN, D_MODEL, D_MODEL)
N_IN = sum(IN_SIZES)
PEER_HEADS = 8
N_KEYS = 128
N_EXPERTS = N_KEYS * N_KEYS
PEER_DKEY = 256
PEER_HALF = PEER_DKEY // 2
PEER_TOPK = 16
EPS = 1e-6

SUBLANES = 8
LANES = 128
VMEM_LIMIT = 56 * 1024 * 1024

F32 = jnp.float32
BF16 = jnp.bfloat16


def _params(*sem, flags=None):
    return pltpu.CompilerParams(dimension_semantics=sem, vmem_limit_bytes=VMEM_LIMIT, flags=flags)


def _pick(n, prefs):
    for p in prefs:
        if n % p == 0:
            return p
    return n


def _norm_matmul_kernel(x_ref, g_ref, w_ref, y_ref, xn_scr):
    @pl.when(pl.program_id(1) == 0)
    def _():
        x = x_ref[...]
        ms = jnp.mean(x * x, axis=-1, keepdims=True)
        xn_scr[...] = ((x * lax.rsqrt(ms + EPS)) * g_ref[...]).astype(BF16)

    y_ref[...] = jnp.dot(xn_scr[...], w_ref[...], preferred_element_type=F32)


def norm_matmul(x, g, w_bf16):
    n, d = x.shape
    m = w_bf16.shape[1]
    tm = _pick(n, (1024, 512, 256, 128, 64, 32, 16, 8))
    tn = _pick(m, (1024, 512, 256, 128))
    return pl.pallas_call(
        _norm_matmul_kernel,
        grid=(n // tm, m // tn),
        in_specs=[
            pl.BlockSpec((tm, d), lambda i, j: (i, 0)),
            pl.BlockSpec((1, d), lambda i, j: (0, 0)),
            pl.BlockSpec((d, tn), lambda i, j: (0, j)),
        ],
        out_specs=pl.BlockSpec((tm, tn), lambda i, j: (i, j)),
        out_shape=jax.ShapeDtypeStruct((n, m), F32),
        scratch_shapes=[pltpu.VMEM((tm, d), BF16)],
        compiler_params=_params("parallel", "arbitrary"),
        name="norm_in_proj",
    )(x, g.reshape(1, d), w_bf16)


def _rot_half(x):
    n = x.shape[-1]
    half = RET_DK // 2
    fwd = pltpu.roll(x, half, axis=1)
    bwd = pltpu.roll(x, n - half, axis=1)
    lane = lax.broadcasted_iota(jnp.int32, x.shape, 1)
    return jnp.where((lane % RET_DK) < half, bwd, fwd)


def _retention_kernel(q_ref, k_ref, v_ref, g_ref, r0_ref, cos_ref, sin_ref, mask_ref, qw_ref, kw_ref,
                      gc_ref, gn_ref, o_ref, r_out_ref, r_scr):
    c = pl.program_id(1)

    @pl.when(c == 0)
    def _():
        r_scr[...] = r0_ref[0]

    cos = cos_ref[...]
    sin = sin_ref[...]
    q = q_ref[...]
    k = k_ref[...]
    qr = q * cos + _rot_half(q) * sin
    kr = (k * cos + _rot_half(k) * sin) * (RET_DK ** -0.5)
    qd = (qr * qw_ref[...]).astype(BF16)
    kd = (kr * kw_ref[...]).astype(BF16)
    qb = qr.astype(BF16)
    kb = kr.astype(BF16)
    v = v_ref[...].astype(BF16)
    g = g_ref[...]
    for h in range(RET_HEADS):
        ks = slice(h * RET_DK, (h + 1) * RET_DK)
        vs = slice(h * RET_DV, (h + 1) * RET_DV)
        vh = v[:, vs]
        s = lax.dot_general(qb[:, ks], kb[:, ks], (((1,), (1,)), ((), ())),
                            preferred_element_type=F32) * mask_ref[h]
        r_h = r_scr[h]
        o = jnp.dot(s.astype(BF16), vh, preferred_element_type=F32)
        o = o + jnp.dot(qd[:, ks], r_h.astype(BF16), preferred_element_type=F32)
        kv = lax.dot_general(kd[:, ks], vh, (((0,), (0,)), ((), ())), preferred_element_type=F32)
        r_scr[h] = gc_ref[h] * r_h + kv
        mu = jnp.mean(o, axis=-1, keepdims=True)
        var = jnp.mean(jnp.square(o - mu), axis=-1, keepdims=True)
        on = ((o - mu) * lax.rsqrt(var + EPS)) * gn_ref[:, vs]
        gh = g[:, vs]
        o_ref[:, vs] = ((gh * jax.nn.sigmoid(gh)) * on).astype(BF16)

    @pl.when(c == pl.num_programs(1) - 1)
    def _():
        r_out_ref[0] = r_scr[...]


def retention_path(p, row0, b, t, pos0, r0, gn_g):
    c = RET_CHUNK if t % RET_CHUNK == 0 else t
    nc = t // c
    blk0 = row0 // c
    assert row0 % c == 0
    log_g = jnp.log1p(-(2.0 ** (-5.0 - jnp.arange(RET_HEADS, dtype=F32))))
    idx = jnp.arange(c, dtype=F32)
    diff = idx[:, None] - idx[None, :]
    mask = jnp.where(diff[None] >= 0, jnp.exp(jnp.maximum(diff, 0.0)[None] * log_g[:, None, None]), 0.0)
    k_w = jnp.exp((c - 1 - idx)[:, None] * log_g[None, :])
    q_w = jnp.exp((idx + 1.0)[:, None] * log_g[None, :])
    g_c = jnp.exp(c * log_g)
    qw_tab = jnp.repeat(q_w, RET_DK, axis=1)
    kw_tab = jnp.repeat(k_w, RET_DK, axis=1)
    gc_tab = jnp.broadcast_to(g_c[:, None, None], (RET_HEADS, 1, RET_DV))
    half = RET_DK // 2
    pos = pos0 + jnp.arange(t, dtype=F32)
    freq = ROPE_BASE ** (-jnp.arange(half, dtype=F32) / half)
    ang = pos[:, None] * freq[None, :]
    cos_h = jnp.concatenate([jnp.cos(ang), jnp.cos(ang)], axis=1)
    sin_h = jnp.concatenate([-jnp.sin(ang), jnp.sin(ang)], axis=1)
    cos_tab = jnp.tile(cos_h, (1, RET_HEADS))
    sin_tab = jnp.tile(sin_h, (1, RET_HEADS))

    rows = lambda bi, ci: blk0 + bi * nc + ci
    o, r_new = pl.pallas_call(
        _retention_kernel,
        grid=(b, nc),
        in_specs=[
            pl.BlockSpec((c, RET_QK), lambda bi, ci: (rows(bi, ci), 0)),
            pl.BlockSpec((c, RET_QK), lambda bi, ci: (rows(bi, ci), 1)),
            pl.BlockSpec((c, RET_V), lambda bi, ci: (rows(bi, ci), 1)),
            pl.BlockSpec((c, RET_V), lambda bi, ci: (rows(bi, ci), 2)),
            pl.BlockSpec((1, RET_HEADS, RET_DK, RET_DV), lambda bi, ci: (bi, 0, 0, 0)),
            pl.BlockSpec((c, RET_QK), lambda bi, ci: (ci, 0)),
            pl.BlockSpec((c, RET_QK), lambda bi, ci: (ci, 0)),
            pl.BlockSpec((RET_HEADS, c, c), lambda bi, ci: (0, 0, 0)),
            pl.BlockSpec((c, RET_QK), lambda bi, ci: (0, 0)),
            pl.BlockSpec((c, RET_QK), lambda bi, ci: (0, 0)),
            pl.BlockSpec((RET_HEADS, 1, RET_DV), lambda bi, ci: (0, 0, 0)),
            pl.BlockSpec((1, RET_V), lambda bi, ci: (0, 0)),
        ],
        out_specs=[
            pl.BlockSpec((c, RET_V), lambda bi, ci: (bi * nc + ci, 0)),
            pl.BlockSpec((1, RET_HEADS, RET_DK, RET_DV), lambda bi, ci: (bi, 0, 0, 0)),
        ],
        out_shape=[
            jax.ShapeDtypeStruct((b * t, RET_V), BF16),
            jax.ShapeDtypeStruct((b, RET_HEADS, RET_DK, RET_DV), F32),
        ],
        scratch_shapes=[pltpu.VMEM((RET_HEADS, RET_DK, RET_DV), F32)],
        compiler_params=_params("parallel", "arbitrary"),
        name="retention",
    )(p, p, p, p, r0, cos_tab, sin_tab, mask, qw_tab, kw_tab, gc_tab, gn_g.reshape(1, RET_V))
    return o, r_new


GELU_C0 = math.sqrt(2.0 / math.pi)
GELU_C1 = GELU_C0 * 0.044715


def _gelu_tanh(x):
    return x * (0.5 * (1.0 + jnp.tanh(math.sqrt(2.0 / math.pi) * (x + 0.044715 * (x * x * x)))))


def _rglru_kernel(bsz, tc, xr_ref, g_ref, buf_ref, h0_ref, cw_ref, cb_ref, wa_ref, ba_ref, wx_ref, bx_ref,
                  lam_ref, o_ref, hl_ref, nb_ref, xcat_scr, a_scr, u_scr, hs_scr):
    rows = tc * bsz
    hist = (CONV_W - 1) * bsz
    step = pl.program_id(0)

    @pl.when(step == 0)
    def _():
        xcat_scr[pl.ds(rows, hist), :] = buf_ref[...]
        hs_scr[pl.ds(rows, bsz), :] = h0_ref[...]

    xcat_scr[pl.ds(0, hist), :] = xcat_scr[pl.ds(rows, hist), :]
    hs_scr[pl.ds(0, bsz), :] = hs_scr[pl.ds(rows, bsz), :]
    xcat_scr[pl.ds(hist, rows), :] = xr_ref[...]

    xc = cb_ref[...] + xcat_scr[pl.ds(0, rows), :] * cw_ref[0:1, :]
    for w in range(1, CONV_W):
        xc = xc + xcat_scr[pl.ds(w * bsz, rows), :] * cw_ref[w:w + 1, :]
    xcb = xc.astype(BF16)
    nblk = wa_ref.shape[0]
    wdt = wa_ref.shape[1]
    ra = jnp.concatenate(
        [jnp.dot(xcb[:, j * wdt:(j + 1) * wdt], wa_ref[j], preferred_element_type=F32) for j in range(nblk)],
        axis=1)
    ri = jnp.concatenate(
        [jnp.dot(xcb[:, j * wdt:(j + 1) * wdt], wx_ref[j], preferred_element_type=F32) for j in range(nblk)],
        axis=1)
    r = jax.nn.sigmoid(ra + ba_ref[...])
    i = jax.nn.sigmoid(ri + bx_ref[...])
    z = -lam_ref[...]
    softplus = jnp.maximum(z, 0.0) + jnp.log1p(jnp.exp(-jnp.abs(z)))
    log_a = (-RG_C * r) * softplus
    a = jnp.exp(log_a)
    one_m_a2 = -jnp.tanh(log_a) * (a * a + 1.0)
    a_scr[...] = a
    u_scr[...] = jnp.sqrt(one_m_a2) * (i * xc)

    def scan_step(t, carry):
        prev = hs_scr[pl.ds(pl.multiple_of(t * bsz, bsz), bsz), :]
        cur = pl.ds(pl.multiple_of(t * bsz, bsz), bsz)
        h = a_scr[cur, :] * prev + u_scr[cur, :]
        hs_scr[pl.ds(pl.multiple_of((t + 1) * bsz, bsz), bsz), :] = h
        return carry

    lax.fori_loop(0, tc, scan_step, 0)
    hs = hs_scr[pl.ds(bsz, rows), :]
    o_ref[...] = (hs * _gelu_tanh(g_ref[...])).astype(BF16)

    @pl.when(step == pl.num_programs(0) - 1)
    def _():
        hl_ref[...] = hs_scr[pl.ds(rows, bsz), :]
        nb_ref[...] = xcat_scr[pl.ds(rows, hist), :]


def rglru_path(xr_tm, g_tm, buf_tm, h0, b, t, conv_w, conv_b, wa4, ba, wx4, bx, lam):
    assert t >= CONV_W - 1
    rows_target = 256
    tc = max(1, min(t, rows_target // b))
    while t % tc:
        tc -= 1
    rows = tc * b
    hist = (CONV_W - 1) * b
    d = D_RNN
    nblk, wdt = wa4.shape[0], wa4.shape[1]
    const2 = lambda s: (0, 0)
    out, h_last, new_buf = pl.pallas_call(
        functools.partial(_rglru_kernel, b, tc),
        grid=(t // tc,),
        in_specs=[
            pl.BlockSpec((rows, d), lambda s: (s, 0)),
            pl.BlockSpec((rows, d), lambda s: (s, 0)),
            pl.BlockSpec((hist, d), const2),
            pl.BlockSpec((b, d), const2),
            pl.BlockSpec((CONV_W, d), const2),
            pl.BlockSpec((1, d), const2),
            pl.BlockSpec((nblk, wdt, wdt), lambda s: (0, 0, 0)),
            pl.BlockSpec((1, d), const2),
            pl.BlockSpec((nblk, wdt, wdt), lambda s: (0, 0, 0)),
            pl.BlockSpec((1, d), const2),
            pl.BlockSpec((1, d), const2),
        ],
        out_specs=[
            pl.BlockSpec((rows, d), lambda s: (s, 0)),
            pl.BlockSpec((b, d), const2),
            pl.BlockSpec((hist, d), const2),
        ],
        out_shape=[
            jax.ShapeDtypeStruct((t * b, d), BF16),
            jax.ShapeDtypeStruct((b, d), F32),
            jax.ShapeDtypeStruct((hist, d), F32),
        ],
        scratch_shapes=[
            pltpu.VMEM((rows + hist, d), F32),
            pltpu.VMEM((rows, d), F32),
            pltpu.VMEM((rows, d), F32),
            pltpu.VMEM((rows + b, d), F32),
        ],
        compiler_params=_params("arbitrary"),
        name="rglru",
    )(xr_tm, g_tm, buf_tm, h0, conv_w, conv_b.reshape(1, d), wa4, ba.reshape(1, d), wx4, bx.reshape(1, d),
      lam.reshape(1, d))
    return out, h_last, new_buf


def _block_diag_tiles(w, tile):
    nb, bs, _ = w.shape
    per = tile // bs
    w = w.reshape(nb // per, per, bs, bs)
    eye = jnp.eye(per, dtype=w.dtype)
    dense = jnp.einsum("gpcd,pq->gpcqd", w, eye).reshape(nb // per, tile, tile)
    return dense.astype(BF16)


def _merge_kernel(blocks_a, x_ref, roa_ref, rob_ref, rna_ref, rnb_ref, ga_ref, gb_ref, wr_ref, wn_ref, wo_ref,
                  yt_ref):
    first = pl.program_id(0) < blocks_a
    ro = jnp.where(first, roa_ref[...], rob_ref[...])
    rn = jnp.where(first, rna_ref[...], rnb_ref[...])
    ret_out = jnp.dot(ro, wr_ref[...], preferred_element_type=F32)
    rnn_out = jnp.dot(rn, wn_ref[...], preferred_element_type=F32)
    merged = jax.nn.sigmoid(ga_ref[...]) * ret_out + jax.nn.sigmoid(gb_ref[...]) * rnn_out
    y = x_ref[...] + jnp.dot(merged.astype(BF16), wo_ref[...], preferred_element_type=F32)
    yt_ref[...] = y.T


def merge_proj(x, ret_parts, rnn_parts, p, w_ret_out, w_rnn_out, w_o):
    n, d = x.shape
    (ret_a, ret_b), (rnn_a, rnn_b) = ret_parts, rnn_parts
    rows_a, rows_b = ret_a.shape[0], ret_b.shape[0]
    tm = _pick(math.gcd(rows_a, rows_b), (512, 256, 128))
    blocks_a = rows_a // tm
    row = lambda i: (i, 0)
    row_a = lambda i: (jnp.minimum(i, blocks_a - 1), 0)
    row_b = lambda i: (jnp.maximum(i - blocks_a, 0), 0)
    const = lambda i: (0, 0)
    return pl.pallas_call(
        functools.partial(_merge_kernel, blocks_a),
        grid=(n // tm,),
        in_specs=[
            pl.BlockSpec((tm, d), row),
            pl.BlockSpec((tm, d), row_a),
            pl.BlockSpec((tm, d), row_b),
            pl.BlockSpec((tm, d), row_a),
            pl.BlockSpec((tm, d), row_b),
            pl.BlockSpec((tm, d), lambda i: (i, 5)),
            pl.BlockSpec((tm, d), lambda i: (i, 6)),
            pl.BlockSpec((d, d), const),
            pl.BlockSpec((d, d), const),
            pl.BlockSpec((d, d), const),
        ],
        out_specs=pl.BlockSpec((d, tm), lambda i: (0, i)),
        out_shape=jax.ShapeDtypeStruct((d, n), F32),
        compiler_params=_params("parallel"),
        name="merge_out_proj",
    )(x, ret_a, ret_b, rnn_a, rnn_b, p, p, w_ret_out, w_rnn_out, w_o)


def _peer_query_kernel(xt_ref, g_ref, wq_ref, xn_ref, q_ref):
    x = xt_ref[...]
    ms = jnp.mean(x * x, axis=0, keepdims=True)
    xn = ((x * lax.rsqrt(ms + EPS)) * g_ref[...]).astype(BF16)
    xn_ref[...] = xn
    q_ref[...] = jnp.dot(wq_ref[...], xn, preferred_element_type=F32)


def peer_query(xt, g, wq_t):
    d, n = xt.shape
    m = wq_t.shape[0]
    tn = _pick(n, (512, 256, 128))
    return pl.pallas_call(
        _peer_query_kernel,
        grid=(n // tn,),
        in_specs=[
            pl.BlockSpec((d, tn), lambda i: (0, i)),
            pl.BlockSpec((d, 1), lambda i: (0, 0)),
            pl.BlockSpec((m, d), lambda i: (0, 0)),
        ],
        out_specs=[pl.BlockSpec((d, tn), lambda i: (0, i)), pl.BlockSpec((m, tn), lambda i: (0, i))],
        out_shape=[jax.ShapeDtypeStruct((d, n), BF16), jax.ShapeDtypeStruct((m, n), F32)],
        compiler_params=_params("parallel"),
        name="peer_query",
    )(xt, g.reshape(d, 1), wq_t)


def _sort_pairs(n):
    def merge(lo, hi, r):
        step = r * 2
        if step < hi - lo:
            yield from merge(lo, hi, step)
            yield from merge(lo + r, hi, step)
            yield from [(i, i + r) for i in range(lo + r, hi - r, step)]
        else:
            yield (lo, lo + r)

    def sort(lo, hi):
        if hi - lo >= 1:
            mid = lo + (hi - lo) // 2
            yield from sort(lo, mid)
            yield from sort(mid + 1, hi)
            yield from merge(lo, hi, 1)

    return list(sort(0, n - 1))


_SORT16 = _sort_pairs(PEER_TOPK)


def _cmpx(vals, i, j):
    a, b = vals[i], vals[j]
    if b is None:
        return
    if a is None:
        vals[i], vals[j] = b, None
        return
    vals[i], vals[j] = jnp.maximum(a, b), jnp.minimum(a, b)


def _sort_desc(vals):
    vals = list(vals)
    for i, j in _SORT16:
        _cmpx(vals, i, j)
    return vals


def _merge_top(a, b):
    k = PEER_TOPK
    a = list(a) + [None] * (k - len(a))
    b = list(b) + [None] * (k - len(b))
    out = []
    for r in range(k):
        x, y = a[r], b[k - 1 - r]
        out.append(y if x is None else (x if y is None else jnp.maximum(x, y)))
    d = k // 2
    while d >= 1:
        for i in range(k):
            if not i & d:
                _cmpx(out, i, i + d)
        d //= 2
    return out


def _top_sorted(ref):
    groups = []
    for g0 in range(0, N_KEYS, PEER_TOPK):
        vals = [ref[pl.ds((g0 + j) * PEER_HEADS, PEER_HEADS), :] for j in range(PEER_TOPK)]
        groups.append(_sort_desc(vals))
    while len(groups) > 1:
        groups = [_merge_top(groups[i], groups[i + 1]) for i in range(0, len(groups), 2)]
    return groups[0]


def _peer_select_kernel(q_ref, k1_ref, k2_ref, c1_ref, e1_ref, r2_ref, e2_ref, s1_scr, s2_scr, r2_scr, e2_scr):
    nh = PEER_HEADS
    half_rows = q_ref.shape[0] // 2
    tn = q_ref.shape[1]
    s1 = jnp.dot(k1_ref[...], q_ref[pl.ds(0, half_rows), :].astype(BF16), preferred_element_type=F32)
    s2 = jnp.dot(k2_ref[...], q_ref[pl.ds(half_rows, half_rows), :].astype(BF16), preferred_element_type=F32)
    for lt in range(tn // LANES):
        s1_scr[lt] = s1[:, lt * LANES:(lt + 1) * LANES]
        s2_scr[lt] = s2[:, lt * LANES:(lt + 1) * LANES]
    for lt in range(tn // LANES):
        lanes = pl.ds(lt * LANES, LANES)
        s1_t, s2_t, r2_t, e2_t = s1_scr.at[lt], s2_scr.at[lt], r2_scr.at[lt], e2_scr.at[lt]
        a = _top_sorted(s1_t)
        b = _top_sorted(s2_t)
        k = PEER_TOPK
        lists = []
        for j in range(1, k + 1):
            col = [a[r - 1] + b[j - 1] for r in range(j, k // j + 1)]
            row = [a[j - 1] + b[s - 1] for s in range(j + 1, k // j + 1)]
            if col:
                lists.append(col)
            if row:
                lists.append(row)
        top = lists[0]
        for other in lists[1:]:
            top = _merge_top(top, other)
        tau = top[k - 1]
        z = jnp.ones_like(tau)
        for r in range(1, k):
            z = z + jnp.exp(top[r] - top[0])
        zinv = 1.0 / z
        inf = jnp.full((nh, LANES), jnp.inf, F32)
        phi = []
        for s in range(1, k + 1):
            p = inf
            for r in range(1, k // s + 1):
                p = jnp.where(a[r - 1] + b[s - 1] >= tau, a[r - 1], p)
            phi.append(p)

        def per_key(kk, carry):
            rows = pl.ds(pl.multiple_of(kk * nh, nh), nh)
            s1k = s1_t[rows, :]
            s2k = s2_t[rows, :]
            cnt = jnp.ones((nh, LANES), F32)
            for s in range(k):
                cnt = jnp.where(s1k >= phi[s], float(s + 2), cnt)
            rank = jnp.full((nh, LANES), float(k + 1), F32)
            for s in range(k - 1, -1, -1):
                rank = jnp.where(s2k >= b[s], float(s + 1), rank)
            c1_ref[rows, lanes] = cnt
            e1_ref[rows, lanes] = jnp.exp(s1k - a[0])
            r2_t[rows, :] = rank
            e2_t[rows, :] = jnp.exp(s2k - b[0]) * zinv
            return carry

        lax.fori_loop(0, N_KEYS, per_key, 0, unroll=2)
        pack = 2 * SUBLANES
        for h in range(nh):
            for kt in range(N_KEYS // pack):
                lo = pl.ds(kt * pack * nh + h, SUBLANES, stride=nh)
                hi = pl.ds((kt * pack + SUBLANES) * nh + h, SUBLANES, stride=nh)
                dst = pl.ds(h * N_KEYS + kt * pack, pack)
                r2_ref[dst, lanes] = jnp.concatenate([r2_t[lo, :], r2_t[hi, :]], axis=0).astype(BF16)
                e2_ref[dst, lanes] = jnp.concatenate([e2_t[lo, :], e2_t[hi, :]], axis=0).astype(BF16)


def peer_select(q_t, k1, k2):
    m, n = q_t.shape
    rows = PEER_HEADS * N_KEYS
    tn = _pick(n, (256, 128))
    tok = lambda i: (0, i)
    const = lambda i: (0, 0)
    return pl.pallas_call(
        _peer_select_kernel,
        grid=(n // tn,),
        in_specs=[pl.BlockSpec((m, tn), tok), pl.BlockSpec(k1.shape, const), pl.BlockSpec(k2.shape, const)],
        out_specs=[pl.BlockSpec((rows, tn), tok)] * 4,
        out_shape=[jax.ShapeDtypeStruct((rows, n), F32), jax.ShapeDtypeStruct((rows, n), F32),
                   jax.ShapeDtypeStruct((rows, n), BF16), jax.ShapeDtypeStruct((rows, n), BF16)],
        scratch_shapes=[pltpu.VMEM((tn // LANES, rows, LANES), F32)] * 4,
        compiler_params=_params("parallel"),
        name="peer_select",
    )(q_t, k1, k2)


PEER_EXPERT_BLOCK = 1024


PEER_PIPE_DEPTH = 2
PEER_PIECE_ROWS = 256


def _peer_dense_kernel(nblk, xt_ref, xn_ref, u_ref, vt_ref, c1_ref, e1_ref, r2_ref, e2_ref, y_ref,
                       acc_scr, h0_scr, h1_scr, act0_scr, act1_scr):
    j = pl.program_id(0)
    nh = PEER_HEADS
    tn = xn_ref.shape[1]
    pack = 2 * SUBLANES
    i1_per_blk = u_ref.shape[0] // N_KEYS
    halves = 2
    half_lanes = tn // halves

    @pl.when(j == 0)
    def _():
        acc_scr[...] = jnp.zeros_like(acc_scr)
        h1_scr[...] = jnp.zeros_like(h1_scr)
        act0_scr[...] = jnp.zeros_like(act0_scr)

    def gate_block(il, nc, h_cur, act_scr):
        for lw in range(half_lanes // LANES):
            lc = nc * (half_lanes // LANES) + lw
            lanes = pl.ds(lc * LANES, LANES)
            c1b = [jnp.broadcast_to(c1_ref[pl.ds(il * nh + h, 1), lanes], (pack, LANES)).astype(BF16)
                   for h in range(nh)]
            e1b = [jnp.broadcast_to(e1_ref[pl.ds(il * nh + h, 1), lanes], (pack, LANES)).astype(BF16)
                   for h in range(nh)]
            zero = jnp.zeros((pack, LANES), BF16)
            half = jnp.full((), 0.5, BF16)
            for it in range(N_KEYS // pack):
                rows = pl.ds(il * N_KEYS + it * pack, pack)
                gate = None
                for h in range(nh):
                    krows = pl.ds(h * N_KEYS + it * pack, pack)
                    sel = jnp.minimum(jnp.maximum(c1b[h] - r2_ref[krows, lanes], zero), e2_ref[krows, lanes])
                    term = sel * e1b[h]
                    gate = term if gate is None else gate + term
                x = h_cur[rows, lanes]
                t = jnp.tanh(x * (GELU_C0 + GELU_C1 * (x * x)))
                act_scr[rows, lanes] = (x.astype(BF16) * (half + half * t.astype(BF16))) * gate

    def body(h_next, h_cur, act_cur, act_prev):
        per_piece = PEER_PIECE_ROWS // N_KEYS
        for ip in range(i1_per_blk // per_piece):
            erows = pl.ds(ip * PEER_PIECE_ROWS, PEER_PIECE_ROWS)
            h_next[erows, :] = jnp.dot(u_ref[erows, :], xn_ref[...], preferred_element_type=F32)
            for il in range(ip * per_piece, (ip + 1) * per_piece):
                gate_block(il, 0, h_cur, act_cur)
        for ip in range(i1_per_blk // per_piece):
            orows = pl.ds(ip * PEER_PIECE_ROWS, PEER_PIECE_ROWS)
            acc_scr[orows, :] += jnp.dot(vt_ref[orows, :], act_prev[...], preferred_element_type=F32)
            for il in range(ip * per_piece, (ip + 1) * per_piece):
                gate_block(il, 1, h_cur, act_cur)

    @pl.when(j % 2 == 0)
    def _():
        body(h0_scr, h1_scr, act1_scr, act0_scr)

    @pl.when(j % 2 == 1)
    def _():
        body(h1_scr, h0_scr, act0_scr, act1_scr)

    done = j - PEER_PIPE_DEPTH

    @pl.when(jnp.logical_and(done >= 0, done % nblk == nblk - 1))
    def _():
        y_ref[...] = (xt_ref[...] + acc_scr[...]).T
        acc_scr[...] = jnp.zeros_like(acc_scr)


def peer_dense(xt, xn_t, u_bf16, vt_bf16, c1, e1, r2, e2):
    d, n = xt.shape
    ne = u_bf16.shape[0]
    tn = _pick(n, (512, 256))
    assert n % tn == 0 and tn % (2 * LANES) == 0
    te = PEER_EXPERT_BLOCK
    assert te == d
    nblk = ne // te
    items = (n // tn) * nblk
    i1_per_blk = te // N_KEYS
    first = lambda j: jnp.minimum(j, items - 1)
    gated = lambda j: jnp.clip(j - 1, 0, items - 1)
    second = lambda j: jnp.maximum(j - PEER_PIPE_DEPTH, 0)
    tb = lambda item: item // nblk
    eb = lambda item: item % nblk
    return pl.pallas_call(
        functools.partial(_peer_dense_kernel, nblk),
        grid=(items + PEER_PIPE_DEPTH,),
        in_specs=[
            pl.BlockSpec((d, tn), lambda j: (0, tb(second(j)))),
            pl.BlockSpec((d, tn), lambda j: (0, tb(first(j)))),
            pl.BlockSpec((te, d), lambda j: (eb(first(j)), 0)),
            pl.BlockSpec((d, te), lambda j: (0, eb(second(j)))),
            pl.BlockSpec((i1_per_blk * PEER_HEADS, tn), lambda j: (eb(gated(j)), tb(gated(j)))),
            pl.BlockSpec((i1_per_blk * PEER_HEADS, tn), lambda j: (eb(gated(j)), tb(gated(j)))),
            pl.BlockSpec((PEER_HEADS * N_KEYS, tn), lambda j: (0, tb(gated(j)))),
            pl.BlockSpec((PEER_HEADS * N_KEYS, tn), lambda j: (0, tb(gated(j)))),
        ],
        out_specs=pl.BlockSpec((tn, d), lambda j: (tb(second(j)), 0)),
        out_shape=jax.ShapeDtypeStruct((n, d), F32),
        scratch_shapes=[pltpu.VMEM((d, tn), F32), pltpu.VMEM((te, tn), F32), pltpu.VMEM((te, tn), F32),
                        pltpu.VMEM((te, tn), BF16), pltpu.VMEM((te, tn), BF16)],
        compiler_params=_params("arbitrary"),
        name="peer_dense",
    )(xt, xn_t, u_bf16, vt_bf16, c1, e1, r2, e2)


def _interleaved_keys(keys_p):
    nh, nk, half = keys_p.shape
    eye = jnp.eye(nh, dtype=keys_p.dtype)
    return jnp.einsum("hkd,hg->khgd", keys_p, eye).reshape(nk * nh, nh * half).astype(BF16)


def peer_layer(xt, norm_g, wq, keys, u_tab, v_tab):
    d, n = xt.shape
    wq_t = wq.reshape(d, PEER_HEADS, 2, PEER_HALF).transpose(2, 1, 3, 0).reshape(2 * PEER_HEADS * PEER_HALF, d)
    xn_t, q_t = peer_query(xt, norm_g, wq_t.astype(BF16))
    k1 = _interleaved_keys(keys[:, 0])
    k2 = _interleaved_keys(keys[:, 1])
    c1, e1, r2, e2 = peer_select(q_t, k1, k2)
    return peer_dense(xt, xn_t, u_tab.astype(BF16), v_tab.T.astype(BF16), c1, e1, r2, e2)


def _rmsnorm_kernel(x_ref, g_ref, y_ref):
    x = x_ref[...]
    ms = jnp.mean(x * x, axis=-1, keepdims=True)
    y_ref[...] = (x * lax.rsqrt(ms + EPS)) * g_ref[...]


def final_norm(x, row0, rows, g):
    d = x.shape[1]
    tm = _pick(math.gcd(rows, row0) if row0 else rows, (512, 256, 128, 64, 32, 16, 8))
    blk0 = row0 // tm
    return pl.pallas_call(
        _rmsnorm_kernel,
        grid=(rows // tm,),
        in_specs=[pl.BlockSpec((tm, d), lambda i: (blk0 + i, 0)), pl.BlockSpec((1, d), lambda i: (0, 0))],
        out_specs=pl.BlockSpec((tm, d), lambda i: (i, 0)),
        out_shape=jax.ShapeDtypeStruct((rows, d), F32),
        compiler_params=_params("parallel"),
        name="final_norm",
    )(x, g.reshape(1, d))


def _to_time_major(p, row0, b, t, col):
    blk = lax.slice(p, (row0, col * D_RNN), (row0 + b * t, (col + 1) * D_RNN))
    return blk.reshape(b, t, D_RNN).transpose(1, 0, 2).reshape(t * b, D_RNN)


def _trunk(groups, norm1_g, norm2_g, normf_g, w_in, ret_gn_g, w_ret_out, conv_w, conv_b, rg_wa, rg_ba, rg_wx,
           rg_bx, rg_lambda, w_rnn_out, w_o, peer_wq, peer_keys, peer_u, peer_v):
    shapes = [(g[0].shape[0], g[0].shape[1]) for g in groups]
    x = jnp.concatenate([g[0].reshape(-1, D_MODEL) for g in groups], axis=0)
    row0s = np.cumsum([0] + [b * t for b, t in shapes]).tolist()
    states = [([], [], []) for _ in groups]
    for l in range(DEPTH):
        p = norm_matmul(x, norm1_g[l], w_in[l].astype(BF16))
        wa4 = _block_diag_tiles(rg_wa[l], 256)
        wx4 = _block_diag_tiles(rg_wx[l], 256)
        ret_parts, rnn_parts = [], []
        for gi, (xg, r0, h0, buf0, pos0) in enumerate(groups):
            b, t = shapes[gi]
            row0 = row0s[gi]
            o, r_new = retention_path(p, row0, b, t, pos0, r0[l], ret_gn_g[l])
            xr_tm = _to_time_major(p, row0, b, t, 3)
            g_tm = _to_time_major(p, row0, b, t, 4)
            buf_tm = buf0[l].transpose(1, 0, 2).reshape((CONV_W - 1) * b, D_RNN)
            hg, h_last, nb = rglru_path(xr_tm, g_tm, buf_tm, h0[l], b, t, conv_w[l], conv_b[l], wa4, rg_ba[l],
                                        wx4, rg_bx[l], rg_lambda[l])
            ret_parts.append(o)
            rnn_parts.append(hg.reshape(t, b, D_RNN).transpose(1, 0, 2).reshape(b * t, D_RNN))
            states[gi][0].append(r_new)
            states[gi][1].append(h_last)
            states[gi][2].append(nb.reshape(CONV_W - 1, b, D_RNN).transpose(1, 0, 2))
        xt = merge_proj(x, ret_parts, rnn_parts, p, w_ret_out[l].astype(BF16), w_rnn_out[l].astype(BF16),
                        w_o[l].astype(BF16))
        x = peer_layer(xt, norm2_g[l], peer_wq[l], peer_keys[l], peer_u[l], peer_v[l])
    outs = []
    for gi, (b, t) in enumerate(shapes):
        y = final_norm(x, row0s[gi], b * t, normf_g).reshape(b, t, D_MODEL)
        outs.append((y, jnp.stack(states[gi][0]), jnp.stack(states[gi][1]), jnp.stack(states[gi][2])))
    return outs


def kernel(x_prompt, x_sample, state_ret, state_rnn, state_conv, norm1_g, norm2_g, normf_g, w_in, ret_gn_g,
           w_ret_out, conv_w, conv_b, rg_wa, rg_ba, rg_wx, rg_bx, rg_lambda, w_rnn_out, w_o, peer_wq, peer_keys,
           peer_u, peer_v):
    bp = x_prompt.shape[0]
    dt = x_prompt.dtype
    zr = jnp.zeros((DEPTH, bp, RET_HEADS, RET_DK, RET_DV), dt)
    zh = jnp.zeros((DEPTH, bp, D_RNN), dt)
    zc = jnp.zeros((DEPTH, bp, CONV_W - 1, D_RNN), dt)
    groups = [(x_prompt, zr, zh, zc, 0.0), (x_sample, state_ret, state_rnn, state_conv, float(PAST_LEN))]
    (yp, rp, hp, cp), (ys, rs, hs, cs) = _trunk(
        groups, norm1_g, norm2_g, normf_g, w_in, ret_gn_g, w_ret_out, conv_w, conv_b, rg_wa, rg_ba, rg_wx, rg_bx,
        rg_lambda, w_rnn_out, w_o, peer_wq, peer_keys, peer_u, peer_v)
    return (yp, ys, rp, hp, cp, rs, hs, cs)
```

```python
import functools
import math

import jax
import jax.numpy as jnp
import numpy as np
from jax import lax
from jax.experimental import pallas as pl
from jax.experimental.pallas import tpu as pltpu

D_MODEL = 1024
DEPTH = 2
PAST_LEN = 16384
RET_HEADS = 8
RET_DK = 64
RET_DV = 128
RET_QK = RET_HEADS * RET_DK
RET_V = RET_HEADS * RET_DV
RET_CHUNK = 128
ROPE_BASE = 10000.0
D_RNN = 1024
RNN_BLOCKS = 16
RNN_BS = D_RNN // RNN_BLOCKS
CONV_W = 4
RG_C = 8.0
IN_SIZES = (RET_QK, RET_QK, RET_V, RET_V, D_RNN, D_RNN, D_MODEL, D_MODEL)
N_IN = sum(IN_SIZES)
PEER_HEADS = 8
N_KEYS = 128
N_EXPERTS = N_KEYS * N_KEYS
PEER_DKEY = 256
PEER_HALF = PEER_DKEY // 2
PEER_TOPK = 16
EPS = 1e-6

SUBLANES = 8
LANES = 128
VMEM_LIMIT = 56 * 1024 * 1024

F32 = jnp.float32
BF16 = jnp.bfloat16


def _params(*sem, flags=None):
    return pltpu.CompilerParams(dimension_semantics=sem, vmem_limit_bytes=VMEM_LIMIT, flags=flags)


def _pick(n, prefs):
    for p in prefs:
        if n % p == 0:
            return p
    return n


def _row_part_maps(parts, tm):
    assert len(parts) in (1, 2) and all(p.shape[0] % tm == 0 for p in parts)
    blocks_a = parts[0].shape[0] // tm
    maps = [lambda i, *_: (jnp.minimum(i, blocks_a - 1), 0)]
    if len(parts) == 2:
        maps.append(lambda i, *_: (jnp.maximum(i - blocks_a, 0), 0))
    return blocks_a, maps


def _pick_rows(refs, blocks_a):
    x = refs[0][...]
    if len(refs) == 2:
        x = jnp.where(pl.program_id(0) < blocks_a, x, refs[1][...])
    return x


def _norm_matmul_kernel(blocks_a, nparts, *refs):
    x_refs = refs[:nparts]
    g_ref, w_ref, y_ref, xn_scr = refs[nparts:]

    @pl.when(pl.program_id(1) == 0)
    def _():
        x = _pick_rows(x_refs, blocks_a)
        ms = jnp.mean(x * x, axis=-1, keepdims=True)
        xn_scr[...] = ((x * lax.rsqrt(ms + EPS)) * g_ref[...]).astype(BF16)

    y_ref[...] = jnp.dot(xn_scr[...], w_ref[...], preferred_element_type=F32)


def norm_matmul(x_parts, g, w_bf16):
    n = sum(p.shape[0] for p in x_parts)
    d = x_parts[0].shape[1]
    m = w_bf16.shape[1]
    tm = _pick(math.gcd(*[p.shape[0] for p in x_parts]), (1024, 512, 256, 128))
    tn = _pick(m, (1792, 1024, 512, 256, 128))
    blocks_a, row_maps = _row_part_maps(x_parts, tm)
    return pl.pallas_call(
        functools.partial(_norm_matmul_kernel, blocks_a, len(x_parts)),
        grid=(n // tm, m // tn),
        in_specs=[pl.BlockSpec((tm, d), rm) for rm in row_maps] + [
            pl.BlockSpec((1, d), lambda i, j: (0, 0)),
            pl.BlockSpec((d, tn), lambda i, j: (0, j)),
        ],
        out_specs=pl.BlockSpec((tm, tn), lambda i, j: (i, j)),
        out_shape=jax.ShapeDtypeStruct((n, m), F32),
        scratch_shapes=[pltpu.VMEM((tm, d), BF16)],
        compiler_params=_params("parallel", "arbitrary"),
        name="norm_in_proj",
    )(*x_parts, g.reshape(1, d), w_bf16)


def _rot_half(x):
    n = x.shape[-1]
    half = RET_DK // 2
    fwd = pltpu.roll(x, half, axis=1)
    bwd = pltpu.roll(x, n - half, axis=1)
    lane = lax.broadcasted_iota(jnp.int32, x.shape, 1)
    return jnp.where((lane % RET_DK) < half, bwd, fwd)


def _retention_kernel(q_ref, k_ref, v_ref, g_ref, r0_ref, cos_ref, sin_ref, mask_ref, qw_ref, kw_ref,
                      gc_ref, gn_ref, o_ref, r_out_ref, r_scr):
    c = pl.program_id(1)

    @pl.when(c == 0)
    def _():
        r_scr[...] = r0_ref[0]

    cos = cos_ref[...]
    sin = sin_ref[...]
    q = q_ref[...]
    k = k_ref[...]
    qr = q * cos + _rot_half(q) * sin
    kr = (k * cos + _rot_half(k) * sin) * (RET_DK ** -0.5)
    qd = (qr * qw_ref[...]).astype(BF16)
    kd = (kr * kw_ref[...]).astype(BF16)
    qb = qr.astype(BF16)
    kb = kr.astype(BF16)
    v = v_ref[...].astype(BF16)
    g = g_ref[...]
    for h in range(RET_HEADS):
        ks = slice(h * RET_DK, (h + 1) * RET_DK)
        vs = slice(h * RET_DV, (h + 1) * RET_DV)
        vh = v[:, vs]
        s = lax.dot_general(qb[:, ks], kb[:, ks], (((1,), (1,)), ((), ())),
                            preferred_element_type=F32) * mask_ref[h]
        r_h = r_scr[h]
        o = jnp.dot(s.astype(BF16), vh, preferred_element_type=F32)
        o = o + jnp.dot(qd[:, ks], r_h.astype(BF16), preferred_element_type=F32)
        kv = lax.dot_general(kd[:, ks], vh, (((0,), (0,)), ((), ())), preferred_element_type=F32)
        r_scr[h] = gc_ref[h] * r_h + kv
        mu = jnp.mean(o, axis=-1, keepdims=True)
        var = jnp.mean(jnp.square(o - mu), axis=-1, keepdims=True)
        on = ((o - mu) * lax.rsqrt(var + EPS)) * gn_ref[:, vs]
        gh = g[:, vs]
        o_ref[:, vs] = ((gh * jax.nn.sigmoid(gh)) * on).astype(BF16)

    @pl.when(c == pl.num_programs(1) - 1)
    def _():
        r_out_ref[0] = r_scr[...]


def retention_path(p, row0, b, t, pos0, r0, gn_g):
    c = RET_CHUNK if t % RET_CHUNK == 0 else t
    nc = t // c
    blk0 = row0 // c
    assert row0 % c == 0
    log_g = jnp.log1p(-(2.0 ** (-5.0 - jnp.arange(RET_HEADS, dtype=F32))))
    idx = jnp.arange(c, dtype=F32)
    diff = idx[:, None] - idx[None, :]
    mask = jnp.where(diff[None] >= 0, jnp.exp(jnp.maximum(diff, 0.0)[None] * log_g[:, None, None]), 0.0)
    k_w = jnp.exp((c - 1 - idx)[:, None] * log_g[None, :])
    q_w = jnp.exp((idx + 1.0)[:, None] * log_g[None, :])
    g_c = jnp.exp(c * log_g)
    qw_tab = jnp.repeat(q_w, RET_DK, axis=1)
    kw_tab = jnp.repeat(k_w, RET_DK, axis=1)
    gc_tab = jnp.broadcast_to(g_c[:, None, None], (RET_HEADS, 1, RET_DV))
    half = RET_DK // 2
    pos = pos0 + jnp.arange(t, dtype=F32)
    freq = ROPE_BASE ** (-jnp.arange(half, dtype=F32) / half)
    ang = pos[:, None] * freq[None, :]
    cos_h = jnp.concatenate([jnp.cos(ang), jnp.cos(ang)], axis=1)
    sin_h = jnp.concatenate([-jnp.sin(ang), jnp.sin(ang)], axis=1)
    cos_tab = jnp.tile(cos_h, (1, RET_HEADS))
    sin_tab = jnp.tile(sin_h, (1, RET_HEADS))

    rows = lambda bi, ci: blk0 + bi * nc + ci
    o, r_new = pl.pallas_call(
        _retention_kernel,
        grid=(b, nc),
        in_specs=[
            pl.BlockSpec((c, RET_QK), lambda bi, ci: (rows(bi, ci), 0)),
            pl.BlockSpec((c, RET_QK), lambda bi, ci: (rows(bi, ci), 1)),
            pl.BlockSpec((c, RET_V), lambda bi, ci: (rows(bi, ci), 1)),
            pl.BlockSpec((c, RET_V), lambda bi, ci: (rows(bi, ci), 2)),
            pl.BlockSpec((1, RET_HEADS, RET_DK, RET_DV), lambda bi, ci: (bi, 0, 0, 0)),
            pl.BlockSpec((c, RET_QK), lambda bi, ci: (ci, 0)),
            pl.BlockSpec((c, RET_QK), lambda bi, ci: (ci, 0)),
            pl.BlockSpec((RET_HEADS, c, c), lambda bi, ci: (0, 0, 0)),
            pl.BlockSpec((c, RET_QK), lambda bi, ci: (0, 0)),
            pl.BlockSpec((c, RET_QK), lambda bi, ci: (0, 0)),
            pl.BlockSpec((RET_HEADS, 1, RET_DV), lambda bi, ci: (0, 0, 0)),
            pl.BlockSpec((1, RET_V), lambda bi, ci: (0, 0)),
        ],
        out_specs=[
            pl.BlockSpec((c, RET_V), lambda bi, ci: (bi * nc + ci, 0)),
            pl.BlockSpec((1, RET_HEADS, RET_DK, RET_DV), lambda bi, ci: (bi, 0, 0, 0)),
        ],
        out_shape=[
            jax.ShapeDtypeStruct((b * t, RET_V), BF16),
            jax.ShapeDtypeStruct((b, RET_HEADS, RET_DK, RET_DV), F32),
        ],
        scratch_shapes=[pltpu.VMEM((RET_HEADS, RET_DK, RET_DV), F32)],
        compiler_params=_params("parallel", "arbitrary"),
        name="retention",
    )(p, p, p, p, r0, cos_tab, sin_tab, mask, qw_tab, kw_tab, gc_tab, gn_g.reshape(1, RET_V))
    return o, r_new


GELU_C0 = math.sqrt(2.0 / math.pi)
GELU_C1 = GELU_C0 * 0.044715


def _gelu_tanh(x):
    return x * (0.5 * (1.0 + jnp.tanh(math.sqrt(2.0 / math.pi) * (x + 0.044715 * (x * x * x)))))


def _rglru_kernel(bsz, tc, xr_ref, g_ref, buf_ref, h0_ref, cw_ref, cb_ref, wa_ref, ba_ref, wx_ref, bx_ref,
                  lam_ref, o_ref, hl_ref, nb_ref, xcat_scr, a_scr, u_scr, hs_scr):
    rows = tc * bsz
    hist = (CONV_W - 1) * bsz
    step = pl.program_id(0)

    @pl.when(step == 0)
    def _():
        xcat_scr[pl.ds(rows, hist), :] = buf_ref[...]
        hs_scr[pl.ds(rows, bsz), :] = h0_ref[...]

    xcat_scr[pl.ds(0, hist), :] = xcat_scr[pl.ds(rows, hist), :]
    hs_scr[pl.ds(0, bsz), :] = hs_scr[pl.ds(rows, bsz), :]
    xcat_scr[pl.ds(hist, rows), :] = xr_ref[...]

    xc = cb_ref[...] + xcat_scr[pl.ds(0, rows), :] * cw_ref[0:1, :]
    for w in range(1, CONV_W):
        xc = xc + xcat_scr[pl.ds(w * bsz, rows), :] * cw_ref[w:w + 1, :]
    xcb = xc.astype(BF16)
    nblk = wa_ref.shape[0]
    wdt = wa_ref.shape[1]
    ra = jnp.concatenate(
        [jnp.dot(xcb[:, j * wdt:(j + 1) * wdt], wa_ref[j], preferred_element_type=F32) for j in range(nblk)],
        axis=1)
    ri = jnp.concatenate(
        [jnp.dot(xcb[:, j * wdt:(j + 1) * wdt], wx_ref[j], preferred_element_type=F32) for j in range(nblk)],
        axis=1)
    r = jax.nn.sigmoid(ra + ba_ref[...])
    i = jax.nn.sigmoid(ri + bx_ref[...])
    z = -lam_ref[...]
    softplus = jnp.maximum(z, 0.0) + jnp.log1p(jnp.exp(-jnp.abs(z)))
    log_a = (-RG_C * r) * softplus
    a = jnp.exp(log_a)
    one_m_a2 = -jnp.tanh(log_a) * (a * a + 1.0)
    a_scr[...] = a
    u_scr[...] = jnp.sqrt(one_m_a2) * (i * xc)

    def scan_step(t, carry):
        prev = hs_scr[pl.ds(pl.multiple_of(t * bsz, bsz), bsz), :]
        cur = pl.ds(pl.multiple_of(t * bsz, bsz), bsz)
        h = a_scr[cur, :] * prev + u_scr[cur, :]
        hs_scr[pl.ds(pl.multiple_of((t + 1) * bsz, bsz), bsz), :] = h
        return carry

    lax.fori_loop(0, tc, scan_step, 0)
    hs = hs_scr[pl.ds(bsz, rows), :]
    o_ref[...] = (hs * _gelu_tanh(g_ref[...])).astype(BF16)

    @pl.when(step == pl.num_programs(0) - 1)
    def _():
        hl_ref[...] = hs_scr[pl.ds(rows, bsz), :]
        nb_ref[...] = xcat_scr[pl.ds(rows, hist), :]


def rglru_path(xr_tm, g_tm, buf_tm, h0, b, t, conv_w, conv_b, wa4, ba, wx4, bx, lam):
    assert t >= CONV_W - 1
    rows_target = 256
    tc = max(1, min(t, rows_target // b))
    while t % tc:
        tc -= 1
    rows = tc * b
    hist = (CONV_W - 1) * b
    d = D_RNN
    nblk, wdt = wa4.shape[0], wa4.shape[1]
    const2 = lambda s: (0, 0)
    out, h_last, new_buf = pl.pallas_call(
        functools.partial(_rglru_kernel, b, tc),
        grid=(t // tc,),
        in_specs=[
            pl.BlockSpec((rows, d), lambda s: (s, 0)),
            pl.BlockSpec((rows, d), lambda s: (s, 0)),
            pl.BlockSpec((hist, d), const2),
            pl.BlockSpec((b, d), const2),
            pl.BlockSpec((CONV_W, d), const2),
            pl.BlockSpec((1, d), const2),
            pl.BlockSpec((nblk, wdt, wdt), lambda s: (0, 0, 0)),
            pl.BlockSpec((1, d), const2),
            pl.BlockSpec((nblk, wdt, wdt), lambda s: (0, 0, 0)),
            pl.BlockSpec((1, d), const2),
            pl.BlockSpec((1, d), const2),
        ],
        out_specs=[
            pl.BlockSpec((rows, d), lambda s: (s, 0)),
            pl.BlockSpec((b, d), const2),
            pl.BlockSpec((hist, d), const2),
        ],
        out_shape=[
            jax.ShapeDtypeStruct((t * b, d), BF16),
            jax.ShapeDtypeStruct((b, d), F32),
            jax.ShapeDtypeStruct((hist, d), F32),
        ],
        scratch_shapes=[
            pltpu.VMEM((rows + hist, d), F32),
            pltpu.VMEM((rows, d), F32),
            pltpu.VMEM((rows, d), F32),
            pltpu.VMEM((rows + b, d), F32),
        ],
        compiler_params=_params("arbitrary"),
        name="rglru",
    )(xr_tm, g_tm, buf_tm, h0, conv_w, conv_b.reshape(1, d), wa4, ba.reshape(1, d), wx4, bx.reshape(1, d),
      lam.reshape(1, d))
    return out, h_last, new_buf


def _block_diag_tiles(w, tile):
    nb, bs, _ = w.shape
    per = tile // bs
    w = w.reshape(nb // per, per, bs, bs)
    eye = jnp.eye(per, dtype=w.dtype)
    dense = jnp.einsum("gpcd,pq->gpcqd", w, eye).reshape(nb // per, tile, tile)
    return dense.astype(BF16)


def _merge_kernel(blocks_a, nx, *refs):
    x_refs, refs = refs[:nx], refs[nx:]
    ro_refs, rn_refs = refs[0:2], refs[2:4]
    ga_ref, gb_ref, wr_ref, wn_ref, wo_ref, yt_ref = refs[4:]
    ret_out = jnp.dot(_pick_rows(ro_refs, blocks_a), wr_ref[...], preferred_element_type=F32)
    rnn_out = jnp.dot(_pick_rows(rn_refs, blocks_a), wn_ref[...], preferred_element_type=F32)
    merged = jax.nn.sigmoid(ga_ref[...]) * ret_out + jax.nn.sigmoid(gb_ref[...]) * rnn_out
    y = _pick_rows(x_refs, blocks_a) + jnp.dot(merged.astype(BF16), wo_ref[...], preferred_element_type=F32)
    yt_ref[...] = y.T


def merge_proj(x_parts, ret_parts, rnn_parts, p, w_ret_out, w_rnn_out, w_o):
    n, d = p.shape[0], x_parts[0].shape[1]
    tm = _pick(math.gcd(*[r.shape[0] for r in ret_parts]), (512, 256, 128))
    blocks_a, part_maps = _row_part_maps(ret_parts, tm)
    assert len(x_parts) == 1 or x_parts[0].shape[0] == ret_parts[0].shape[0]
    _, x_maps = _row_part_maps(x_parts, tm) if len(x_parts) == 2 else (None, [lambda i: (i, 0)])
    const = lambda i: (0, 0)
    return pl.pallas_call(
        functools.partial(_merge_kernel, blocks_a, len(x_parts)),
        grid=(n // tm,),
        in_specs=[pl.BlockSpec((tm, d), m) for m in x_maps + part_maps + part_maps] + [
            pl.BlockSpec((tm, d), lambda i: (i, 5)),
            pl.BlockSpec((tm, d), lambda i: (i, 6)),
            pl.BlockSpec((d, d), const),
            pl.BlockSpec((d, d), const),
            pl.BlockSpec((d, d), const),
        ],
        out_specs=pl.BlockSpec((d, tm), lambda i: (0, i)),
        out_shape=jax.ShapeDtypeStruct((d, n), F32),
        compiler_params=_params("parallel"),
        name="merge_out_proj",
    )(*x_parts, *ret_parts, *rnn_parts, p, p, w_ret_out, w_rnn_out, w_o)


def _peer_query_kernel(xt_ref, g_ref, wq_ref, xn_ref, q_ref):
    x = xt_ref[...]
    ms = jnp.mean(x * x, axis=0, keepdims=True)
    xn = ((x * lax.rsqrt(ms + EPS)) * g_ref[...]).astype(BF16)
    xn_ref[...] = xn
    q_ref[...] = jnp.dot(wq_ref[...], xn, preferred_element_type=F32)


def peer_query(xt, g, wq_t):
    d, n = xt.shape
    m = wq_t.shape[0]
    tn = _pick(n, (512, 256, 128))
    return pl.pallas_call(
        _peer_query_kernel,
        grid=(n // tn,),
        in_specs=[
            pl.BlockSpec((d, tn), lambda i: (0, i)),
            pl.BlockSpec((d, 1), lambda i: (0, 0)),
            pl.BlockSpec((m, d), lambda i: (0, 0)),
        ],
        out_specs=[pl.BlockSpec((d, tn), lambda i: (0, i)), pl.BlockSpec((m, tn), lambda i: (0, i))],
        out_shape=[jax.ShapeDtypeStruct((d, n), BF16), jax.ShapeDtypeStruct((m, n), F32)],
        compiler_params=_params("parallel"),
        name="peer_query",
    )(xt, g.reshape(d, 1), wq_t)


def _sort_pairs(n):
    def merge(lo, hi, r):
        step = r * 2
        if step < hi - lo:
            yield from merge(lo, hi, step)
            yield from merge(lo + r, hi, step)
            yield from [(i, i + r) for i in range(lo + r, hi - r, step)]
        else:
            yield (lo, lo + r)

    def sort(lo, hi):
        if hi - lo >= 1:
            mid = lo + (hi - lo) // 2
            yield from sort(lo, mid)
            yield from sort(mid + 1, hi)
            yield from merge(lo, hi, 1)

    return list(sort(0, n - 1))


_SORT16 = _sort_pairs(PEER_TOPK)


def _cmpx(vals, i, j):
    a, b = vals[i], vals[j]
    if b is None:
        return
    if a is None:
        vals[i], vals[j] = b, None
        return
    vals[i], vals[j] = jnp.maximum(a, b), jnp.minimum(a, b)


def _sort_desc(vals):
    vals = list(vals)
    for i, j in _SORT16:
        _cmpx(vals, i, j)
    return vals


def _merge_top(a, b):
    k = PEER_TOPK
    a = list(a) + [None] * (k - len(a))
    b = list(b) + [None] * (k - len(b))
    out = []
    for r in range(k):
        x, y = a[r], b[k - 1 - r]
        out.append(y if x is None else (x if y is None else jnp.maximum(x, y)))
    d = k // 2
    while d >= 1:
        for i in range(k):
            if not i & d:
                _cmpx(out, i, i + d)
        d //= 2
    return out


def _top_sorted(ref):
    groups = []
    for g0 in range(0, N_KEYS, PEER_TOPK):
        vals = [ref[pl.ds((g0 + j) * PEER_HEADS, PEER_HEADS), :] for j in range(PEER_TOPK)]
        groups.append(_sort_desc(vals))
    while len(groups) > 1:
        groups = [_merge_top(groups[i], groups[i + 1]) for i in range(0, len(groups), 2)]
    return groups[0]


def _peer_select_kernel(q_ref, k1_ref, k2_ref, c1_ref, e1_ref, r2_ref, e2_ref, s1_scr, s2_scr, r2_scr, e2_scr):
    nh = PEER_HEADS
    half_rows = q_ref.shape[0] // 2
    tn = q_ref.shape[1]
    s1 = jnp.dot(k1_ref[...], q_ref[pl.ds(0, half_rows), :].astype(BF16), preferred_element_type=F32)
    s2 = jnp.dot(k2_ref[...], q_ref[pl.ds(half_rows, half_rows), :].astype(BF16), preferred_element_type=F32)
    for lt in range(tn // LANES):
        s1_scr[lt] = s1[:, lt * LANES:(lt + 1) * LANES]
        s2_scr[lt] = s2[:, lt * LANES:(lt + 1) * LANES]
    for lt in range(tn // LANES):
        lanes = pl.ds(lt * LANES, LANES)
        s1_t, s2_t, r2_t, e2_t = s1_scr.at[lt], s2_scr.at[lt], r2_scr.at[lt], e2_scr.at[lt]
        a = _top_sorted(s1_t)
        b = _top_sorted(s2_t)
        k = PEER_TOPK
        lists = []
        for j in range(1, k + 1):
            col = [a[r - 1] + b[j - 1] for r in range(j, k // j + 1)]
            row = [a[j - 1] + b[s - 1] for s in range(j + 1, k // j + 1)]
            if col:
                lists.append(col)
            if row:
                lists.append(row)
        top = lists[0]
        for other in lists[1:]:
            top = _merge_top(top, other)
        tau = top[k - 1]
        z = jnp.ones_like(tau)
        for r in range(1, k):
            z = z + jnp.exp(top[r] - top[0])
        zinv = 1.0 / z
        inf = jnp.full((nh, LANES), jnp.inf, F32)
        phi = []
        for s in range(1, k + 1):
            p = inf
            for r in range(1, k // s + 1):
                p = jnp.where(a[r - 1] + b[s - 1] >= tau, a[r - 1], p)
            phi.append(p)

        def per_key(kk, carry):
            rows = pl.ds(pl.multiple_of(kk * nh, nh), nh)
            s1k = s1_t[rows, :]
            s2k = s2_t[rows, :]
            cnt = jnp.ones((nh, LANES), F32)
            for s in range(k):
                cnt = jnp.where(s1k >= phi[s], float(s + 2), cnt)
            rank = jnp.full((nh, LANES), float(k + 1), F32)
            for s in range(k - 1, -1, -1):
                rank = jnp.where(s2k >= b[s], float(s + 1), rank)
            c1_ref[rows, lanes] = cnt
            e1_ref[rows, lanes] = jnp.exp(s1k - a[0])
            r2_t[rows, :] = rank
            e2_t[rows, :] = jnp.exp(s2k - b[0]) * zinv
            return carry

        lax.fori_loop(0, N_KEYS, per_key, 0, unroll=2)
        pack = 2 * SUBLANES
        for h in range(nh):
            for kt in range(N_KEYS // pack):
                lo = pl.ds(kt * pack * nh + h, SUBLANES, stride=nh)
                hi = pl.ds((kt * pack + SUBLANES) * nh + h, SUBLANES, stride=nh)
                dst = pl.ds(h * N_KEYS + kt * pack, pack)
                r2_ref[dst, lanes] = jnp.concatenate([r2_t[lo, :], r2_t[hi, :]], axis=0).astype(BF16)
                e2_ref[dst, lanes] = jnp.concatenate([e2_t[lo, :], e2_t[hi, :]], axis=0).astype(BF16)


def peer_select(q_t, k1, k2):
    m, n = q_t.shape
    rows = PEER_HEADS * N_KEYS
    tn = _pick(n, (256, 128))
    tok = lambda i: (0, i)
    const = lambda i: (0, 0)
    return pl.pallas_call(
        _peer_select_kernel,
        grid=(n // tn,),
        in_specs=[pl.BlockSpec((m, tn), tok), pl.BlockSpec(k1.shape, const), pl.BlockSpec(k2.shape, const)],
        out_specs=[pl.BlockSpec((rows, tn), tok)] * 4,
        out_shape=[jax.ShapeDtypeStruct((rows, n), F32), jax.ShapeDtypeStruct((rows, n), F32),
                   jax.ShapeDtypeStruct((rows, n), BF16), jax.ShapeDtypeStruct((rows, n), BF16)],
        scratch_shapes=[pltpu.VMEM((tn // LANES, rows, LANES), F32)] * 4,
        compiler_params=_params("parallel"),
        name="peer_select",
    )(q_t, k1, k2)


PEER_EXPERT_BLOCK = 1024


def _peer_dense_kernel(xt_ref, xn_ref, u_ref, vt_ref, c1_ref, e1_ref, r2_ref, e2_ref, y_ref,
                       acc_scr, act_scr):
    j = pl.program_id(1)
    nh = PEER_HEADS
    tn = xn_ref.shape[1]
    pack = 2 * SUBLANES
    i1_per_blk = u_ref.shape[0] // N_KEYS

    @pl.when(j == 0)
    def _():
        acc_scr[...] = jnp.zeros_like(acc_scr)

    wide = 2 * LANES

    def gate_block(il, nc):
        erows = pl.ds(il * N_KEYS, N_KEYS)
        hrows = pl.ds(il * nh, nh)
        hblk = jnp.dot(u_ref[erows, :], xn_ref[:, pl.ds(nc * wide, wide)], preferred_element_type=F32)
        for lw in range(wide // LANES):
            lc = nc * (wide // LANES) + lw
            lanes = pl.ds(lc * LANES, LANES)
            c1b = [jnp.broadcast_to(c1_ref[pl.ds(il * nh + h, 1), lanes], (pack, LANES)).astype(BF16)
                   for h in range(nh)]
            e1b = [jnp.broadcast_to(e1_ref[pl.ds(il * nh + h, 1), lanes], (pack, LANES)).astype(BF16)
                   for h in range(nh)]
            zero = jnp.zeros((pack, LANES), BF16)
            half = jnp.full((), 0.5, BF16)
            gelu_c0 = jnp.full((), GELU_C0, BF16)
            gelu_c1 = jnp.full((), GELU_C1, BF16)
            for it in range(N_KEYS // pack):
                rows = pl.ds(il * N_KEYS + it * pack, pack)
                gate = None
                for h in range(nh):
                    krows = pl.ds(h * N_KEYS + it * pack, pack)
                    sel = jnp.minimum(jnp.maximum(c1b[h] - r2_ref[krows, lanes], zero), e2_ref[krows, lanes])
                    term = sel * e1b[h]
                    gate = term if gate is None else gate + term
                x = hblk[it * pack:(it + 1) * pack, lw * LANES:(lw + 1) * LANES].astype(BF16)
                t = jnp.tanh(x * (gelu_c0 + gelu_c1 * (x * x)))
                act_scr[rows, lanes] = (x * (half + half * t)) * gate

    for nc in range(tn // wide):
        for il in range(i1_per_blk):
            gate_block(il, nc)
        cols = pl.ds(nc * wide, wide)
        acc_scr[:, cols] += jnp.dot(vt_ref[...], act_scr[:, cols], preferred_element_type=F32)

    @pl.when(j == pl.num_programs(1) - 1)
    def _():
        y_ref[...] = (xt_ref[...] + acc_scr[...]).T


def peer_dense(xt, xn_t, u_bf16, vt_bf16, c1, e1, r2, e2):
    d, n = xt.shape
    ne = u_bf16.shape[0]
    tn = _pick(n, (512, 256, 128))
    te = PEER_EXPERT_BLOCK
    i1_per_blk = te // N_KEYS
    tok = lambda i, j: (0, i)
    return pl.pallas_call(
        _peer_dense_kernel,
        grid=(n // tn, ne // te),
        in_specs=[
            pl.BlockSpec((d, tn), tok),
            pl.BlockSpec((d, tn), tok),
            pl.BlockSpec((te, d), lambda i, j: (j, 0)),
            pl.BlockSpec((d, te), lambda i, j: (0, j)),
            pl.BlockSpec((i1_per_blk * PEER_HEADS, tn), lambda i, j: (j, i)),
            pl.BlockSpec((i1_per_blk * PEER_HEADS, tn), lambda i, j: (j, i)),
            pl.BlockSpec((PEER_HEADS * N_KEYS, tn), tok),
            pl.BlockSpec((PEER_HEADS * N_KEYS, tn), tok),
        ],
        out_specs=pl.BlockSpec((tn, d), lambda i, j: (i, 0)),
        out_shape=jax.ShapeDtypeStruct((n, d), F32),
        scratch_shapes=[pltpu.VMEM((d, tn), F32), pltpu.VMEM((te, tn), BF16)],
        compiler_params=_params("parallel", "arbitrary"),
        name="peer_dense",
    )(xt, xn_t, u_bf16, vt_bf16, c1, e1, r2, e2)


def _interleaved_keys(keys_p):
    nh, nk, half = keys_p.shape
    eye = jnp.eye(nh, dtype=keys_p.dtype)
    return jnp.einsum("hkd,hg->khgd", keys_p, eye).reshape(nk * nh, nh * half).astype(BF16)


def peer_layer(xt, norm_g, wq, keys, u_tab, v_tab):
    d, n = xt.shape
    wq_t = wq.reshape(d, PEER_HEADS, 2, PEER_HALF).transpose(2, 1, 3, 0).reshape(2 * PEER_HEADS * PEER_HALF, d)
    xn_t, q_t = peer_query(xt, norm_g, wq_t.astype(BF16))
    k1 = _interleaved_keys(keys[:, 0])
    k2 = _interleaved_keys(keys[:, 1])
    c1, e1, r2, e2 = peer_select(q_t, k1, k2)
    return peer_dense(xt, xn_t, u_tab.astype(BF16), v_tab.T.astype(BF16), c1, e1, r2, e2)


def _rmsnorm_kernel(x_ref, g_ref, y_ref):
    x = x_ref[...]
    ms = jnp.mean(x * x, axis=-1, keepdims=True)
    y_ref[...] = (x * lax.rsqrt(ms + EPS)) * g_ref[...]


def final_norm(x, row0, rows, g):
    d = x.shape[1]
    tm = _pick(math.gcd(rows, row0) if row0 else rows, (512, 256, 128, 64, 32, 16, 8))
    blk0 = row0 // tm
    return pl.pallas_call(
        _rmsnorm_kernel,
        grid=(rows // tm,),
        in_specs=[pl.BlockSpec((tm, d), lambda i: (blk0 + i, 0)), pl.BlockSpec((1, d), lambda i: (0, 0))],
        out_specs=pl.BlockSpec((tm, d), lambda i: (i, 0)),
        out_shape=jax.ShapeDtypeStruct((rows, d), F32),
        compiler_params=_params("parallel"),
        name="final_norm",
    )(x, g.reshape(1, d))


def _to_time_major(p, row0, b, t, col):
    blk = lax.slice(p, (row0, col * D_RNN), (row0 + b * t, (col + 1) * D_RNN))
    return blk.reshape(b, t, D_RNN).transpose(1, 0, 2).reshape(t * b, D_RNN)


def _trunk(groups, norm1_g, norm2_g, normf_g, w_in, ret_gn_g, w_ret_out, conv_w, conv_b, rg_wa, rg_ba, rg_wx,
           rg_bx, rg_lambda, w_rnn_out, w_o, peer_wq, peer_keys, peer_u, peer_v):
    shapes = [(g[0].shape[0], g[0].shape[1]) for g in groups]
    x_parts = [g[0].reshape(-1, D_MODEL) for g in groups]
    row0s = np.cumsum([0] + [b * t for b, t in shapes]).tolist()
    states = [([], [], []) for _ in groups]
    for l in range(DEPTH):
        p = norm_matmul(x_parts, norm1_g[l], w_in[l].astype(BF16))
        wa4 = _block_diag_tiles(rg_wa[l], 256)
        wx4 = _block_diag_tiles(rg_wx[l], 256)
        ret_parts, rnn_parts = [], []
        for gi, (xg, r0, h0, buf0, pos0) in enumerate(groups):
            b, t = shapes[gi]
            row0 = row0s[gi]
            o, r_new = retention_path(p, row0, b, t, pos0, r0[l], ret_gn_g[l])
            xr_tm = _to_time_major(p, row0, b, t, 3)
            g_tm = _to_time_major(p, row0, b, t, 4)
            buf_tm = buf0[l].transpose(1, 0, 2).reshape((CONV_W - 1) * b, D_RNN)
            hg, h_last, nb = rglru_path(xr_tm, g_tm, buf_tm, h0[l], b, t, conv_w[l], conv_b[l], wa4, rg_ba[l],
                                        wx4, rg_bx[l], rg_lambda[l])
            ret_parts.append(o)
            rnn_parts.append(hg.reshape(t, b, D_RNN).transpose(1, 0, 2).reshape(b * t, D_RNN))
            states[gi][0].append(r_new)
            states[gi][1].append(h_last)
            states[gi][2].append(nb.reshape(CONV_W - 1, b, D_RNN).transpose(1, 0, 2))
        xt = merge_proj(x_parts, ret_parts, rnn_parts, p, w_ret_out[l].astype(BF16), w_rnn_out[l].astype(BF16),
                        w_o[l].astype(BF16))
        x = peer_layer(xt, norm2_g[l], peer_wq[l], peer_keys[l], peer_u[l], peer_v[l])
        x_parts = [x]
    outs = []
    for gi, (b, t) in enumerate(shapes):
        y = final_norm(x, row0s[gi], b * t, normf_g).reshape(b, t, D_MODEL)
        outs.append((y, jnp.stack(states[gi][0]), jnp.stack(states[gi][1]), jnp.stack(states[gi][2])))
    return outs


def kernel(x_prompt, x_sample, state_ret, state_rnn, state_conv, norm1_g, norm2_g, normf_g, w_in, ret_gn_g,
           w_ret_out, conv_w, conv_b, rg_wa, rg_ba, rg_wx, rg_bx, rg_lambda, w_rnn_out, w_o, peer_wq, peer_keys,
           peer_u, peer_v):
    bp = x_prompt.shape[0]
    dt = x_prompt.dtype
    zr = jnp.zeros((DEPTH, bp, RET_HEADS, RET_DK, RET_DV), dt)
    zh = jnp.zeros((DEPTH, bp, D_RNN), dt)
    zc = jnp.zeros((DEPTH, bp, CONV_W - 1, D_RNN), dt)
    groups = [(x_prompt, zr, zh, zc, 0.0), (x_sample, state_ret, state_rnn, state_conv, float(PAST_LEN))]
    (yp, rp, hp, cp), (ys, rs, hs, cs) = _trunk(
        groups, norm1_g, norm2_g, normf_g, w_in, ret_gn_g, w_ret_out, conv_w, conv_b, rg_wa, rg_ba, rg_wx, rg_bx,
        rg_lambda, w_rnn_out, w_o, peer_wq, peer_keys, peer_u, peer_v)
    return (yp, ys, rp, hp, cp, rs, hs, cs)
```

```python
import functools
import math

import jax
import jax.numpy as jnp
import numpy as np
from jax import lax
from jax.experimental import pallas as pl
from jax.experimental.pallas import tpu as pltpu

D_MODEL = 1024
DEPTH = 2
PAST_LEN = 16384
RET_HEADS = 8
RET_DK = 64
RET_DV = 128
RET_QK = RET_HEADS * RET_DK
RET_V = RET_HEADS * RET_DV
RET_CHUNK = 128
ROPE_BASE = 10000.0
D_RNN = 1024
RNN_BLOCKS = 16
RNN_BS = D_RNN // RNN_BLOCKS
CONV_W = 4
RG_C = 8.0
IN_SIZES = (RET_QK, RET_QK, RET_V, RET_V, D_RNN, D_RNN, D_MODEL, D_MODEL)
N_IN = sum(IN_SIZES)
PEER_HEADS = 8
N_KEYS = 128
N_EXPERTS = N_KEYS * N_KEYS
PEER_DKEY = 256
PEER_HALF = PEER_DKEY // 2
PEER_TOPK = 16
EPS = 1e-6

SUBLANES = 8
LANES = 128
VMEM_LIMIT = 56 * 1024 * 1024

F32 = jnp.float32
BF16 = jnp.bfloat16


def _params(*sem, flags=None):
    return pltpu.CompilerParams(dimension_semantics=sem, vmem_limit_bytes=VMEM_LIMIT, flags=flags)


def _pick(n, prefs):
    for p in prefs:
        if n % p == 0:
            return p
    return n


def _row_part_maps(parts, tm):
    assert len(parts) in (1, 2) and all(p.shape[0] % tm == 0 for p in parts)
    blocks_a = parts[0].shape[0] // tm
    maps = [lambda i, *_: (jnp.minimum(i, blocks_a - 1), 0)]
    if len(parts) == 2:
        maps.append(lambda i, *_: (jnp.maximum(i - blocks_a, 0), 0))
    return blocks_a, maps


def _pick_rows(refs, blocks_a):
    x = refs[0][...]
    if len(refs) == 2:
        x = jnp.where(pl.program_id(0) < blocks_a, x, refs[1][...])
    return x


def _norm_matmul_kernel(blocks_a, nparts, *refs):
    x_refs = refs[:nparts]
    g_ref, w_ref, y_ref, xn_scr = refs[nparts:]

    @pl.when(pl.program_id(1) == 0)
    def _():
        x = _pick_rows(x_refs, blocks_a)
        ms = jnp.mean(x * x, axis=-1, keepdims=True)
        xn_scr[...] = ((x * lax.rsqrt(ms + EPS)) * g_ref[...]).astype(BF16)

    y_ref[...] = jnp.dot(xn_scr[...], w_ref[...], preferred_element_type=F32)


def norm_matmul(x_parts, g, w_bf16):
    n = sum(p.shape[0] for p in x_parts)
    d = x_parts[0].shape[1]
    m = w_bf16.shape[1]
    tm = _pick(math.gcd(*[p.shape[0] for p in x_parts]), (1024, 512, 256, 128))
    tn = _pick(m, (1792, 1024, 512, 256, 128))
    blocks_a, row_maps = _row_part_maps(x_parts, tm)
    return pl.pallas_call(
        functools.partial(_norm_matmul_kernel, blocks_a, len(x_parts)),
        grid=(n // tm, m // tn),
        in_specs=[pl.BlockSpec((tm, d), rm) for rm in row_maps] + [
            pl.BlockSpec((1, d), lambda i, j: (0, 0)),
            pl.BlockSpec((d, tn), lambda i, j: (0, j)),
        ],
        out_specs=pl.BlockSpec((tm, tn), lambda i, j: (i, j)),
        out_shape=jax.ShapeDtypeStruct((n, m), F32),
        scratch_shapes=[pltpu.VMEM((tm, d), BF16)],
        compiler_params=_params("parallel", "arbitrary"),
        name="norm_in_proj",
    )(*x_parts, g.reshape(1, d), w_bf16)


def _rot_half(x):
    n = x.shape[-1]
    half = RET_DK // 2
    fwd = pltpu.roll(x, half, axis=1)
    bwd = pltpu.roll(x, n - half, axis=1)
    lane = lax.broadcasted_iota(jnp.int32, x.shape, 1)
    return jnp.where((lane % RET_DK) < half, bwd, fwd)


def _retention_kernel(q_ref, k_ref, v_ref, g_ref, r0_ref, cos_ref, sin_ref, mask_ref, qw_ref, kw_ref,
                      gc_ref, gn_ref, o_ref, r_out_ref, r_scr):
    c = pl.program_id(1)

    @pl.when(c == 0)
    def _():
        r_scr[...] = r0_ref[0]

    cos = cos_ref[...]
    sin = sin_ref[...]
    q = q_ref[...]
    k = k_ref[...]
    qr = q * cos + _rot_half(q) * sin
    kr = (k * cos + _rot_half(k) * sin) * (RET_DK ** -0.5)
    qd = (qr * qw_ref[...]).astype(BF16)
    kd = (kr * kw_ref[...]).astype(BF16)
    qb = qr.astype(BF16)
    kb = kr.astype(BF16)
    v = v_ref[...].astype(BF16)
    g = g_ref[...]
    for h in range(RET_HEADS):
        ks = slice(h * RET_DK, (h + 1) * RET_DK)
        vs = slice(h * RET_DV, (h + 1) * RET_DV)
        vh = v[:, vs]
        s = lax.dot_general(qb[:, ks], kb[:, ks], (((1,), (1,)), ((), ())),
                            preferred_element_type=F32) * mask_ref[h]
        r_h = r_scr[h]
        o = jnp.dot(s.astype(BF16), vh, preferred_element_type=F32)
        o = o + jnp.dot(qd[:, ks], r_h.astype(BF16), preferred_element_type=F32)
        kv = lax.dot_general(kd[:, ks], vh, (((0,), (0,)), ((), ())), preferred_element_type=F32)
        r_scr[h] = gc_ref[h] * r_h + kv
        mu = jnp.mean(o, axis=-1, keepdims=True)
        var = jnp.mean(jnp.square(o - mu), axis=-1, keepdims=True)
        on = ((o - mu) * lax.rsqrt(var + EPS)) * gn_ref[:, vs]
        gh = g[:, vs]
        o_ref[:, vs] = ((gh * jax.nn.sigmoid(gh)) * on).astype(BF16)

    @pl.when(c == pl.num_programs(1) - 1)
    def _():
        r_out_ref[0] = r_scr[...]


def retention_path(p, row0, b, t, pos0, r0, gn_g):
    c = RET_CHUNK if t % RET_CHUNK == 0 else t
    nc = t // c
    blk0 = row0 // c
    assert row0 % c == 0
    log_g = jnp.log1p(-(2.0 ** (-5.0 - jnp.arange(RET_HEADS, dtype=F32))))
    idx = jnp.arange(c, dtype=F32)
    diff = idx[:, None] - idx[None, :]
    mask = jnp.where(diff[None] >= 0, jnp.exp(jnp.maximum(diff, 0.0)[None] * log_g[:, None, None]), 0.0)
    k_w = jnp.exp((c - 1 - idx)[:, None] * log_g[None, :])
    q_w = jnp.exp((idx + 1.0)[:, None] * log_g[None, :])
    g_c = jnp.exp(c * log_g)
    qw_tab = jnp.repeat(q_w, RET_DK, axis=1)
    kw_tab = jnp.repeat(k_w, RET_DK, axis=1)
    gc_tab = jnp.broadcast_to(g_c[:, None, None], (RET_HEADS, 1, RET_DV))
    half = RET_DK // 2
    pos = pos0 + jnp.arange(t, dtype=F32)
    freq = ROPE_BASE ** (-jnp.arange(half, dtype=F32) / half)
    ang = pos[:, None] * freq[None, :]
    cos_h = jnp.concatenate([jnp.cos(ang), jnp.cos(ang)], axis=1)
    sin_h = jnp.concatenate([-jnp.sin(ang), jnp.sin(ang)], axis=1)
    cos_tab = jnp.tile(cos_h, (1, RET_HEADS))
    sin_tab = jnp.tile(sin_h, (1, RET_HEADS))

    rows = lambda bi, ci: blk0 + bi * nc + ci
    o, r_new = pl.pallas_call(
        _retention_kernel,
        grid=(b, nc),
        in_specs=[
            pl.BlockSpec((c, RET_QK), lambda bi, ci: (rows(bi, ci), 0)),
            pl.BlockSpec((c, RET_QK), lambda bi, ci: (rows(bi, ci), 1)),
            pl.BlockSpec((c, RET_V), lambda bi, ci: (rows(bi, ci), 1)),
            pl.BlockSpec((c, RET_V), lambda bi, ci: (rows(bi, ci), 2)),
            pl.BlockSpec((1, RET_HEADS, RET_DK, RET_DV), lambda bi, ci: (bi, 0, 0, 0)),
            pl.BlockSpec((c, RET_QK), lambda bi, ci: (ci, 0)),
            pl.BlockSpec((c, RET_QK), lambda bi, ci: (ci, 0)),
            pl.BlockSpec((RET_HEADS, c, c), lambda bi, ci: (0, 0, 0)),
            pl.BlockSpec((c, RET_QK), lambda bi, ci: (0, 0)),
            pl.BlockSpec((c, RET_QK), lambda bi, ci: (0, 0)),
            pl.BlockSpec((RET_HEADS, 1, RET_DV), lambda bi, ci: (0, 0, 0)),
            pl.BlockSpec((1, RET_V), lambda bi, ci: (0, 0)),
        ],
        out_specs=[
            pl.BlockSpec((c, RET_V), lambda bi, ci: (bi * nc + ci, 0)),
            pl.BlockSpec((1, RET_HEADS, RET_DK, RET_DV), lambda bi, ci: (bi, 0, 0, 0)),
        ],
        out_shape=[
            jax.ShapeDtypeStruct((b * t, RET_V), BF16),
            jax.ShapeDtypeStruct((b, RET_HEADS, RET_DK, RET_DV), F32),
        ],
        scratch_shapes=[pltpu.VMEM((RET_HEADS, RET_DK, RET_DV), F32)],
        compiler_params=_params("parallel", "arbitrary"),
        name="retention",
    )(p, p, p, p, r0, cos_tab, sin_tab, mask, qw_tab, kw_tab, gc_tab, gn_g.reshape(1, RET_V))
    return o, r_new


GELU_C0 = math.sqrt(2.0 / math.pi)
GELU_C1 = GELU_C0 * 0.044715


def _gelu_tanh(x):
    return x * (0.5 * (1.0 + jnp.tanh(math.sqrt(2.0 / math.pi) * (x + 0.044715 * (x * x * x)))))


def _rglru_kernel(bsz, tc, xr_ref, g_ref, buf_ref, h0_ref, cw_ref, cb_ref, wa_ref, ba_ref, wx_ref, bx_ref,
                  lam_ref, o_ref, hl_ref, nb_ref, xcat_scr, a_scr, u_scr, hs_scr):
    rows = tc * bsz
    hist = (CONV_W - 1) * bsz
    step = pl.program_id(0)

    @pl.when(step == 0)
    def _():
        xcat_scr[pl.ds(rows, hist), :] = buf_ref[...]
        hs_scr[pl.ds(rows, bsz), :] = h0_ref[...]

    xcat_scr[pl.ds(0, hist), :] = xcat_scr[pl.ds(rows, hist), :]
    hs_scr[pl.ds(0, bsz), :] = hs_scr[pl.ds(rows, bsz), :]
    xcat_scr[pl.ds(hist, rows), :] = xr_ref[...]

    xc = cb_ref[...] + xcat_scr[pl.ds(0, rows), :] * cw_ref[0:1, :]
    for w in range(1, CONV_W):
        xc = xc + xcat_scr[pl.ds(w * bsz, rows), :] * cw_ref[w:w + 1, :]
    xcb = xc.astype(BF16)
    nblk = wa_ref.shape[0]
    wdt = wa_ref.shape[1]
    ra = jnp.concatenate(
        [jnp.dot(xcb[:, j * wdt:(j + 1) * wdt], wa_ref[j], preferred_element_type=F32) for j in range(nblk)],
        axis=1)
    ri = jnp.concatenate(
        [jnp.dot(xcb[:, j * wdt:(j + 1) * wdt], wx_ref[j], preferred_element_type=F32) for j in range(nblk)],
        axis=1)
    r = jax.nn.sigmoid(ra + ba_ref[...])
    i = jax.nn.sigmoid(ri + bx_ref[...])
    z = -lam_ref[...]
    softplus = jnp.maximum(z, 0.0) + jnp.log1p(jnp.exp(-jnp.abs(z)))
    log_a = (-RG_C * r) * softplus
    a = jnp.exp(log_a)
    one_m_a2 = -jnp.tanh(log_a) * (a * a + 1.0)
    a_scr[...] = a
    u_scr[...] = jnp.sqrt(one_m_a2) * (i * xc)

    def scan_step(t, carry):
        prev = hs_scr[pl.ds(pl.multiple_of(t * bsz, bsz), bsz), :]
        cur = pl.ds(pl.multiple_of(t * bsz, bsz), bsz)
        h = a_scr[cur, :] * prev + u_scr[cur, :]
        hs_scr[pl.ds(pl.multiple_of((t + 1) * bsz, bsz), bsz), :] = h
        return carry

    lax.fori_loop(0, tc, scan_step, 0)
    hs = hs_scr[pl.ds(bsz, rows), :]
    o_ref[...] = (hs * _gelu_tanh(g_ref[...])).astype(BF16)

    @pl.when(step == pl.num_programs(0) - 1)
    def _():
        hl_ref[...] = hs_scr[pl.ds(rows, bsz), :]
        nb_ref[...] = xcat_scr[pl.ds(rows, hist), :]


def rglru_path(xr_tm, g_tm, buf_tm, h0, b, t, conv_w, conv_b, wa4, ba, wx4, bx, lam):
    assert t >= CONV_W - 1
    rows_target = 256
    tc = max(1, min(t, rows_target // b))
    while t % tc:
        tc -= 1
    rows = tc * b
    hist = (CONV_W - 1) * b
    d = D_RNN
    nblk, wdt = wa4.shape[0], wa4.shape[1]
    const2 = lambda s: (0, 0)
    out, h_last, new_buf = pl.pallas_call(
        functools.partial(_rglru_kernel, b, tc),
        grid=(t // tc,),
        in_specs=[
            pl.BlockSpec((rows, d), lambda s: (s, 0)),
            pl.BlockSpec((rows, d), lambda s: (s, 0)),
            pl.BlockSpec((hist, d), const2),
            pl.BlockSpec((b, d), const2),
            pl.BlockSpec((CONV_W, d), const2),
            pl.BlockSpec((1, d), const2),
            pl.BlockSpec((nblk, wdt, wdt), lambda s: (0, 0, 0)),
            pl.BlockSpec((1, d), const2),
            pl.BlockSpec((nblk, wdt, wdt), lambda s: (0, 0, 0)),
            pl.BlockSpec((1, d), const2),
            pl.BlockSpec((1, d), const2),
        ],
        out_specs=[
            pl.BlockSpec((rows, d), lambda s: (s, 0)),
            pl.BlockSpec((b, d), const2),
            pl.BlockSpec((hist, d), const2),
        ],
        out_shape=[
            jax.ShapeDtypeStruct((t * b, d), BF16),
            jax.ShapeDtypeStruct((b, d), F32),
            jax.ShapeDtypeStruct((hist, d), F32),
        ],
        scratch_shapes=[
            pltpu.VMEM((rows + hist, d), F32),
            pltpu.VMEM((rows, d), F32),
            pltpu.VMEM((rows, d), F32),
            pltpu.VMEM((rows + b, d), F32),
        ],
        compiler_params=_params("arbitrary"),
        name="rglru",
    )(xr_tm, g_tm, buf_tm, h0, conv_w, conv_b.reshape(1, d), wa4, ba.reshape(1, d), wx4, bx.reshape(1, d),
      lam.reshape(1, d))
    return out, h_last, new_buf


def _block_diag_tiles(w, tile):
    nb, bs, _ = w.shape
    per = tile // bs
    w = w.reshape(nb // per, per, bs, bs)
    eye = jnp.eye(per, dtype=w.dtype)
    dense = jnp.einsum("gpcd,pq->gpcqd", w, eye).reshape(nb // per, tile, tile)
    return dense.astype(BF16)


def _merge_kernel(blocks_a, nx, *refs):
    x_refs, refs = refs[:nx], refs[nx:]
    ro_refs, rn_refs = refs[0:2], refs[2:4]
    ga_ref, gb_ref, wr_ref, wn_ref, wo_ref, yt_ref = refs[4:]
    ret_out = jnp.dot(_pick_rows(ro_refs, blocks_a), wr_ref[...], preferred_element_type=F32)
    rnn_out = jnp.dot(_pick_rows(rn_refs, blocks_a), wn_ref[...], preferred_element_type=F32)
    merged = jax.nn.sigmoid(ga_ref[...]) * ret_out + jax.nn.sigmoid(gb_ref[...]) * rnn_out
    y = _pick_rows(x_refs, blocks_a) + jnp.dot(merged.astype(BF16), wo_ref[...], preferred_element_type=F32)
    yt_ref[...] = y.T


def merge_proj(x_parts, ret_parts, rnn_parts, p, w_ret_out, w_rnn_out, w_o):
    n, d = p.shape[0], x_parts[0].shape[1]
    tm = _pick(math.gcd(*[r.shape[0] for r in ret_parts]), (512, 256, 128))
    blocks_a, part_maps = _row_part_maps(ret_parts, tm)
    assert len(x_parts) == 1 or x_parts[0].shape[0] == ret_parts[0].shape[0]
    _, x_maps = _row_part_maps(x_parts, tm) if len(x_parts) == 2 else (None, [lambda i: (i, 0)])
    const = lambda i: (0, 0)
    return pl.pallas_call(
        functools.partial(_merge_kernel, blocks_a, len(x_parts)),
        grid=(n // tm,),
        in_specs=[pl.BlockSpec((tm, d), m) for m in x_maps + part_maps + part_maps] + [
            pl.BlockSpec((tm, d), lambda i: (i, 5)),
            pl.BlockSpec((tm, d), lambda i: (i, 6)),
            pl.BlockSpec((d, d), const),
            pl.BlockSpec((d, d), const),
            pl.BlockSpec((d, d), const),
        ],
        out_specs=pl.BlockSpec((d, tm), lambda i: (0, i)),
        out_shape=jax.ShapeDtypeStruct((d, n), F32),
        compiler_params=_params("parallel"),
        name="merge_out_proj",
    )(*x_parts, *ret_parts, *rnn_parts, p, p, w_ret_out, w_rnn_out, w_o)


def _peer_query_kernel(xt_ref, g_ref, wq_ref, xn_ref, q_ref):
    x = xt_ref[...]
    ms = jnp.mean(x * x, axis=0, keepdims=True)
    xn = ((x * lax.rsqrt(ms + EPS)) * g_ref[...]).astype(BF16)
    xn_ref[...] = xn
    q_ref[...] = jnp.dot(wq_ref[...], xn, preferred_element_type=F32)


def peer_query(xt, g, wq_t):
    d, n = xt.shape
    m = wq_t.shape[0]
    tn = _pick(n, (512, 256, 128))
    return pl.pallas_call(
        _peer_query_kernel,
        grid=(n // tn,),
        in_specs=[
            pl.BlockSpec((d, tn), lambda i: (0, i)),
            pl.BlockSpec((d, 1), lambda i: (0, 0)),
            pl.BlockSpec((m, d), lambda i: (0, 0)),
        ],
        out_specs=[pl.BlockSpec((d, tn), lambda i: (0, i)), pl.BlockSpec((m, tn), lambda i: (0, i))],
        out_shape=[jax.ShapeDtypeStruct((d, n), BF16), jax.ShapeDtypeStruct((m, n), F32)],
        compiler_params=_params("parallel"),
        name="peer_query",
    )(xt, g.reshape(d, 1), wq_t)


def _sort_pairs(n):
    def merge(lo, hi, r):
        step = r * 2
        if step < hi - lo:
            yield from merge(lo, hi, step)
            yield from merge(lo + r, hi, step)
            yield from [(i, i + r) for i in range(lo + r, hi - r, step)]
        else:
            yield (lo, lo + r)

    def sort(lo, hi):
        if hi - lo >= 1:
            mid = lo + (hi - lo) // 2
            yield from sort(lo, mid)
            yield from sort(mid + 1, hi)
            yield from merge(lo, hi, 1)

    return list(sort(0, n - 1))


_SORT16 = _sort_pairs(PEER_TOPK)


def _cmpx(vals, i, j):
    a, b = vals[i], vals[j]
    if b is None:
        return
    if a is None:
        vals[i], vals[j] = b, None
        return
    vals[i], vals[j] = jnp.maximum(a, b), jnp.minimum(a, b)


def _sort_desc(vals):
    vals = list(vals)
    for i, j in _SORT16:
        _cmpx(vals, i, j)
    return vals


def _merge_top(a, b):
    k = PEER_TOPK
    a = list(a) + [None] * (k - len(a))
    b = list(b) + [None] * (k - len(b))
    out = []
    for r in range(k):
        x, y = a[r], b[k - 1 - r]
        out.append(y if x is None else (x if y is None else jnp.maximum(x, y)))
    d = k // 2
    while d >= 1:
        for i in range(k):
            if not i & d:
                _cmpx(out, i, i + d)
        d //= 2
    return out


def _top_sorted(ref):
    groups = []
    for g0 in range(0, N_KEYS, PEER_TOPK):
        vals = [ref[pl.ds((g0 + j) * PEER_HEADS, PEER_HEADS), :] for j in range(PEER_TOPK)]
        groups.append(_sort_desc(vals))
    while len(groups) > 1:
        groups = [_merge_top(groups[i], groups[i + 1]) for i in range(0, len(groups), 2)]
    return groups[0]


def _peer_select_kernel(q_ref, k1_ref, k2_ref, c1_ref, e1_ref, r2_ref, e2_ref, s1_scr, s2_scr, r2_scr, e2_scr):
    nh = PEER_HEADS
    half_rows = q_ref.shape[0] // 2
    tn = q_ref.shape[1]
    s1 = jnp.dot(k1_ref[...], q_ref[pl.ds(0, half_rows), :].astype(BF16), preferred_element_type=F32)
    s2 = jnp.dot(k2_ref[...], q_ref[pl.ds(half_rows, half_rows), :].astype(BF16), preferred_element_type=F32)
    for lt in range(tn // LANES):
        s1_scr[lt] = s1[:, lt * LANES:(lt + 1) * LANES]
        s2_scr[lt] = s2[:, lt * LANES:(lt + 1) * LANES]
    for lt in range(tn // LANES):
        lanes = pl.ds(lt * LANES, LANES)
        s1_t, s2_t, r2_t, e2_t = s1_scr.at[lt], s2_scr.at[lt], r2_scr.at[lt], e2_scr.at[lt]
        a = _top_sorted(s1_t)
        b = _top_sorted(s2_t)
        k = PEER_TOPK
        lists = []
        for j in range(1, k + 1):
            col = [a[r - 1] + b[j - 1] for r in range(j, k // j + 1)]
            row = [a[j - 1] + b[s - 1] for s in range(j + 1, k // j + 1)]
            if col:
                lists.append(col)
            if row:
                lists.append(row)
        top = lists[0]
        for other in lists[1:]:
            top = _merge_top(top, other)
        tau = top[k - 1]
        z = jnp.ones_like(tau)
        for r in range(1, k):
            z = z + jnp.exp(top[r] - top[0])
        zinv = 1.0 / z
        inf = jnp.full((nh, LANES), jnp.inf, F32)
        phi = []
        for s in range(1, k + 1):
            p = inf
            for r in range(1, k // s + 1):
                p = jnp.where(a[r - 1] + b[s - 1] >= tau, a[r - 1], p)
            phi.append(p)

        def per_key(kk, carry):
            rows = pl.ds(pl.multiple_of(kk * nh, nh), nh)
            s1k = s1_t[rows, :]
            s2k = s2_t[rows, :]
            cnt = jnp.ones((nh, LANES), F32)
            for s in range(k):
                cnt = jnp.where(s1k >= phi[s], float(s + 2), cnt)
            rank = jnp.full((nh, LANES), float(k + 1), F32)
            for s in range(k - 1, -1, -1):
                rank = jnp.where(s2k >= b[s], float(s + 1), rank)
            c1_ref[rows, lanes] = cnt
            e1_ref[rows, lanes] = jnp.exp(s1k - a[0])
            r2_t[rows, :] = rank
            e2_t[rows, :] = jnp.exp(s2k - b[0]) * zinv
            return carry

        lax.fori_loop(0, N_KEYS, per_key, 0, unroll=2)
        pack = 2 * SUBLANES
        for h in range(nh):
            for kt in range(N_KEYS // pack):
                lo = pl.ds(kt * pack * nh + h, SUBLANES, stride=nh)
                hi = pl.ds((kt * pack + SUBLANES) * nh + h, SUBLANES, stride=nh)
                dst = pl.ds(h * N_KEYS + kt * pack, pack)
                r2_ref[dst, lanes] = jnp.concatenate([r2_t[lo, :], r2_t[hi, :]], axis=0).astype(BF16)
                e2_ref[dst, lanes] = jnp.concatenate([e2_t[lo, :], e2_t[hi, :]], axis=0).astype(BF16)


def peer_select(q_t, k1, k2):
    m, n = q_t.shape
    rows = PEER_HEADS * N_KEYS
    tn = _pick(n, (256, 128))
    tok = lambda i: (0, i)
    const = lambda i: (0, 0)
    return pl.pallas_call(
        _peer_select_kernel,
        grid=(n // tn,),
        in_specs=[pl.BlockSpec((m, tn), tok), pl.BlockSpec(k1.shape, const), pl.BlockSpec(k2.shape, const)],
        out_specs=[pl.BlockSpec((rows, tn), tok)] * 4,
        out_shape=[jax.ShapeDtypeStruct((rows, n), F32), jax.ShapeDtypeStruct((rows, n), F32),
                   jax.ShapeDtypeStruct((rows, n), BF16), jax.ShapeDtypeStruct((rows, n), BF16)],
        scratch_shapes=[pltpu.VMEM((tn // LANES, rows, LANES), F32)] * 4,
        compiler_params=_params("parallel"),
        name="peer_select",
    )(q_t, k1, k2)


PEER_EXPERT_BLOCK = 1024


def _mxu_dot(lhs, rhs):
    return lax.dot_general(lhs, rhs, (((1,), (0,)), ((), ())), preferred_element_type=F32)


def _peer_dense_kernel(xt_ref, xn_ref, u_ref, vt_ref, c1_ref, e1_ref, r2_ref, e2_ref, y_ref,
                       acc_scr, act_scr):
    j = pl.program_id(1)
    nh = PEER_HEADS
    tn = xn_ref.shape[1]
    pack = 2 * SUBLANES
    i1_per_blk = u_ref.shape[0] // N_KEYS

    @pl.when(j == 0)
    def _():
        acc_scr[...] = jnp.zeros_like(acc_scr)

    wide = 2 * LANES

    def gate_block(il, nc):
        erows = pl.ds(il * N_KEYS, N_KEYS)
        hrows = pl.ds(il * nh, nh)
        hblk = _mxu_dot(u_ref[erows, :], xn_ref[:, pl.ds(nc * wide, wide)])
        for lw in range(wide // LANES):
            lc = nc * (wide // LANES) + lw
            lanes = pl.ds(lc * LANES, LANES)
            c1b = [jnp.broadcast_to(c1_ref[pl.ds(il * nh + h, 1), lanes], (pack, LANES)).astype(BF16)
                   for h in range(nh)]
            e1b = [jnp.broadcast_to(e1_ref[pl.ds(il * nh + h, 1), lanes], (pack, LANES)).astype(BF16)
                   for h in range(nh)]
            zero = jnp.zeros((pack, LANES), BF16)
            half = jnp.full((), 0.5, BF16)
            gelu_c0 = jnp.full((), GELU_C0, BF16)
            gelu_c1 = jnp.full((), GELU_C1, BF16)
            for it in range(N_KEYS // pack):
                rows = pl.ds(il * N_KEYS + it * pack, pack)
                gate = None
                for h in range(nh):
                    krows = pl.ds(h * N_KEYS + it * pack, pack)
                    sel = jnp.minimum(jnp.maximum(c1b[h] - r2_ref[krows, lanes], zero), e2_ref[krows, lanes])
                    term = sel * e1b[h]
                    gate = term if gate is None else gate + term
                x = hblk[it * pack:(it + 1) * pack, lw * LANES:(lw + 1) * LANES].astype(BF16)
                t = jnp.tanh(x * (gelu_c0 + gelu_c1 * (x * x)))
                act_scr[rows, lanes] = (x * (half + half * t)) * gate

    for nc in range(tn // wide):
        for il in range(i1_per_blk):
            gate_block(il, nc)
        cols = pl.ds(nc * wide, wide)
        acc_scr[:, cols] += _mxu_dot(vt_ref[...], act_scr[:, cols])

    @pl.when(j == pl.num_programs(1) - 1)
    def _():
        y_ref[...] = (xt_ref[...] + acc_scr[...]).T


def peer_dense(xt, xn_t, u_bf16, vt_bf16, c1, e1, r2, e2):
    d, n = xt.shape
    ne = u_bf16.shape[0]
    tn = _pick(n, (512, 256, 128))
    te = PEER_EXPERT_BLOCK
    i1_per_blk = te // N_KEYS
    tok = lambda i, j: (0, i)
    return pl.pallas_call(
        _peer_dense_kernel,
        grid=(n // tn, ne // te),
        in_specs=[
            pl.BlockSpec((d, tn), tok),
            pl.BlockSpec((d, tn), tok),
            pl.BlockSpec((te, d), lambda i, j: (j, 0)),
            pl.BlockSpec((d, te), lambda i, j: (0, j)),
            pl.BlockSpec((i1_per_blk * PEER_HEADS, tn), lambda i, j: (j, i)),
            pl.BlockSpec((i1_per_blk * PEER_HEADS, tn), lambda i, j: (j, i)),
            pl.BlockSpec((PEER_HEADS * N_KEYS, tn), tok),
            pl.BlockSpec((PEER_HEADS * N_KEYS, tn), tok),
        ],
        out_specs=pl.BlockSpec((tn, d), lambda i, j: (i, 0)),
        out_shape=jax.ShapeDtypeStruct((n, d), F32),
        scratch_shapes=[pltpu.VMEM((d, tn), F32), pltpu.VMEM((te, tn), BF16)],
        compiler_params=_params("parallel", "arbitrary"),
        name="peer_dense",
    )(xt, xn_t, u_bf16, vt_bf16, c1, e1, r2, e2)


def _interleaved_keys(keys_p):
    nh, nk, half = keys_p.shape
    eye = jnp.eye(nh, dtype=keys_p.dtype)
    return jnp.einsum("hkd,hg->khgd", keys_p, eye).reshape(nk * nh, nh * half).astype(BF16)


def peer_layer(xt, norm_g, wq, keys, u_tab, v_tab):
    d, n = xt.shape
    wq_t = wq.reshape(d, PEER_HEADS, 2, PEER_HALF).transpose(2, 1, 3, 0).reshape(2 * PEER_HEADS * PEER_HALF, d)
    xn_t, q_t = peer_query(xt, norm_g, wq_t.astype(BF16))
    k1 = _interleaved_keys(keys[:, 0])
    k2 = _interleaved_keys(keys[:, 1])
    c1, e1, r2, e2 = peer_select(q_t, k1, k2)
    return peer_dense(xt, xn_t, u_tab, v_tab.T, c1, e1, r2, e2)


def _rmsnorm_kernel(x_ref, g_ref, y_ref):
    x = x_ref[...]
    ms = jnp.mean(x * x, axis=-1, keepdims=True)
    y_ref[...] = (x * lax.rsqrt(ms + EPS)) * g_ref[...]


def final_norm(x, row0, rows, g):
    d = x.shape[1]
    tm = _pick(math.gcd(rows, row0) if row0 else rows, (512, 256, 128, 64, 32, 16, 8))
    blk0 = row0 // tm
    return pl.pallas_call(
        _rmsnorm_kernel,
        grid=(rows // tm,),
        in_specs=[pl.BlockSpec((tm, d), lambda i: (blk0 + i, 0)), pl.BlockSpec((1, d), lambda i: (0, 0))],
        out_specs=pl.BlockSpec((tm, d), lambda i: (i, 0)),
        out_shape=jax.ShapeDtypeStruct((rows, d), F32),
        compiler_params=_params("parallel"),
        name="final_norm",
    )(x, g.reshape(1, d))


def _to_time_major(p, row0, b, t, col):
    blk = lax.slice(p, (row0, col * D_RNN), (row0 + b * t, (col + 1) * D_RNN))
    return blk.reshape(b, t, D_RNN).transpose(1, 0, 2).reshape(t * b, D_RNN)


def _trunk(groups, norm1_g, norm2_g, normf_g, w_in, ret_gn_g, w_ret_out, conv_w, conv_b, rg_wa, rg_ba, rg_wx,
           rg_bx, rg_lambda, w_rnn_out, w_o, peer_wq, peer_keys, peer_u, peer_v):
    shapes = [(g[0].shape[0], g[0].shape[1]) for g in groups]
    x_parts = [g[0].reshape(-1, D_MODEL) for g in groups]
    row0s = np.cumsum([0] + [b * t for b, t in shapes]).tolist()
    states = [([], [], []) for _ in groups]
    for l in range(DEPTH):
        p = norm_matmul(x_parts, norm1_g[l], w_in[l].astype(BF16))
        wa4 = _block_diag_tiles(rg_wa[l], 256)
        wx4 = _block_diag_tiles(rg_wx[l], 256)
        ret_parts, rnn_parts = [], []
        for gi, (xg, r0, h0, buf0, pos0) in enumerate(groups):
            b, t = shapes[gi]
            row0 = row0s[gi]
            o, r_new = retention_path(p, row0, b, t, pos0, r0[l], ret_gn_g[l])
            xr_tm = _to_time_major(p, row0, b, t, 3)
            g_tm = _to_time_major(p, row0, b, t, 4)
            buf_tm = buf0[l].transpose(1, 0, 2).reshape((CONV_W - 1) * b, D_RNN)
            hg, h_last, nb = rglru_path(xr_tm, g_tm, buf_tm, h0[l], b, t, conv_w[l], conv_b[l], wa4, rg_ba[l],
                                        wx4, rg_bx[l], rg_lambda[l])
            ret_parts.append(o)
            rnn_parts.append(hg.reshape(t, b, D_RNN).transpose(1, 0, 2).reshape(b * t, D_RNN))
            states[gi][0].append(r_new)
            states[gi][1].append(h_last)
            states[gi][2].append(nb.reshape(CONV_W - 1, b, D_RNN).transpose(1, 0, 2))
        xt = merge_proj(x_parts, ret_parts, rnn_parts, p, w_ret_out[l].astype(BF16), w_rnn_out[l].astype(BF16),
                        w_o[l].astype(BF16))
        x = peer_layer(xt, norm2_g[l], peer_wq[l], peer_keys[l], peer_u[l], peer_v[l])
        x_parts = [x]
    outs = []
    for gi, (b, t) in enumerate(shapes):
        y = final_norm(x, row0s[gi], b * t, normf_g).reshape(b, t, D_MODEL)
        outs.append((y, jnp.stack(states[gi][0]), jnp.stack(states[gi][1]), jnp.stack(states[gi][2])))
    return outs


def kernel(x_prompt, x_sample, state_ret, state_rnn, state_conv, norm1_g, norm2_g, normf_g, w_in, ret_gn_g,
           w_ret_out, conv_w, conv_b, rg_wa, rg_ba, rg_wx, rg_bx, rg_lambda, w_rnn_out, w_o, peer_wq, peer_keys,
           peer_u, peer_v):
    bp = x_prompt.shape[0]
    dt = x_prompt.dtype
    zr = jnp.zeros((DEPTH, bp, RET_HEADS, RET_DK, RET_DV), dt)
    zh = jnp.zeros((DEPTH, bp, D_RNN), dt)
    zc = jnp.zeros((DEPTH, bp, CONV_W - 1, D_RNN), dt)
    groups = [(x_prompt, zr, zh, zc, 0.0), (x_sample, state_ret, state_rnn, state_conv, float(PAST_LEN))]
    (yp, rp, hp, cp), (ys, rs, hs, cs) = _trunk(
        groups, norm1_g, norm2_g, normf_g, w_in, ret_gn_g, w_ret_out, conv_w, conv_b, rg_wa, rg_ba, rg_wx, rg_bx,
        rg_lambda, w_rnn_out, w_o, peer_wq, peer_keys, peer_u, peer_v)
    return (yp, ys, rp, hp, cp, rs, hs, cs)
```

```python
import functools
import math

import jax
import jax.numpy as jnp
import numpy as np
from jax import lax
from jax.experimental import pallas as pl
from jax.experimental.pallas import tpu as pltpu

D_MODEL = 1024
DEPTH = 2
PAST_LEN = 16384
RET_HEADS = 8
RET_DK = 64
RET_DV = 128
RET_QK = RET_HEADS * RET_DK
RET_V = RET_HEADS * RET_DV
RET_CHUNK = 128
ROPE_BASE = 10000.0
D_RNN = 1024
RNN_BLOCKS = 16
RNN_BS = D_RNN // RNN_BLOCKS
CONV_W = 4
RG_C = 8.0
IN_SIZES = (RET_QK, RET_QK, RET_V, RET_V, D_RNN, D_RNN, D_MODEL, D_MODEL)
N_IN = sum(IN_SIZES)
PEER_HEADS = 8
N_KEYS = 128
N_EXPERTS = N_KEYS * N_KEYS
PEER_DKEY = 256
PEER_HALF = PEER_DKEY // 2
PEER_TOPK = 16
EPS = 1e-6

SUBLANES = 8
LANES = 128
VMEM_LIMIT = 56 * 1024 * 1024

F32 = jnp.float32
BF16 = jnp.bfloat16


def _params(*sem, flags=None):
    return pltpu.CompilerParams(dimension_semantics=sem, vmem_limit_bytes=VMEM_LIMIT, flags=flags)


def _pick(n, prefs):
    for p in prefs:
        if n % p == 0:
            return p
    return n


def _row_part_maps(parts, tm):
    assert len(parts) in (1, 2) and all(p.shape[0] % tm == 0 for p in parts)
    blocks_a = parts[0].shape[0] // tm
    maps = [lambda i, *_: (jnp.minimum(i, blocks_a - 1), 0)]
    if len(parts) == 2:
        maps.append(lambda i, *_: (jnp.maximum(i - blocks_a, 0), 0))
    return blocks_a, maps


def _pick_rows(refs, blocks_a):
    x = refs[0][...]
    if len(refs) == 2:
        x = jnp.where(pl.program_id(0) < blocks_a, x, refs[1][...])
    return x


def _norm_matmul_kernel(blocks_a, nparts, *refs):
    x_refs = refs[:nparts]
    g_ref, w_ref, y_ref, xn_scr = refs[nparts:]

    @pl.when(pl.program_id(1) == 0)
    def _():
        x = _pick_rows(x_refs, blocks_a)
        ms = jnp.mean(x * x, axis=-1, keepdims=True)
        xn_scr[...] = ((x * lax.rsqrt(ms + EPS)) * g_ref[...]).astype(BF16)

    y_ref[...] = jnp.dot(xn_scr[...], w_ref[...], preferred_element_type=F32)


def norm_matmul(x_parts, g, w_bf16):
    n = sum(p.shape[0] for p in x_parts)
    d = x_parts[0].shape[1]
    m = w_bf16.shape[1]
    tm = _pick(math.gcd(*[p.shape[0] for p in x_parts]), (1024, 512, 256, 128))
    tn = _pick(m, (1792, 1024, 512, 256, 128))
    blocks_a, row_maps = _row_part_maps(x_parts, tm)
    return pl.pallas_call(
        functools.partial(_norm_matmul_kernel, blocks_a, len(x_parts)),
        grid=(n // tm, m // tn),
        in_specs=[pl.BlockSpec((tm, d), rm) for rm in row_maps] + [
            pl.BlockSpec((1, d), lambda i, j: (0, 0)),
            pl.BlockSpec((d, tn), lambda i, j: (0, j)),
        ],
        out_specs=pl.BlockSpec((tm, tn), lambda i, j: (i, j)),
        out_shape=jax.ShapeDtypeStruct((n, m), F32),
        scratch_shapes=[pltpu.VMEM((tm, d), BF16)],
        compiler_params=_params("parallel", "arbitrary"),
        name="norm_in_proj",
    )(*x_parts, g.reshape(1, d), w_bf16)


def _rot_half(x):
    n = x.shape[-1]
    half = RET_DK // 2
    fwd = pltpu.roll(x, half, axis=1)
    bwd = pltpu.roll(x, n - half, axis=1)
    lane = lax.broadcasted_iota(jnp.int32, x.shape, 1)
    return jnp.where((lane % RET_DK) < half, bwd, fwd)


def _retention_kernel(q_ref, k_ref, v_ref, g_ref, r0_ref, cos_ref, sin_ref, mask_ref, qw_ref, kw_ref,
                      gc_ref, gn_ref, o_ref, r_out_ref, r_scr):
    c = pl.program_id(1)

    @pl.when(c == 0)
    def _():
        r_scr[...] = r0_ref[0]

    cos = cos_ref[...]
    sin = sin_ref[...]
    q = q_ref[...]
    k = k_ref[...]
    qr = q * cos + _rot_half(q) * sin
    kr = (k * cos + _rot_half(k) * sin) * (RET_DK ** -0.5)
    qd = (qr * qw_ref[...]).astype(BF16)
    kd = (kr * kw_ref[...]).astype(BF16)
    qb = qr.astype(BF16)
    kb = kr.astype(BF16)
    v = v_ref[...].astype(BF16)
    g = g_ref[...]
    for h in range(RET_HEADS):
        ks = slice(h * RET_DK, (h + 1) * RET_DK)
        vs = slice(h * RET_DV, (h + 1) * RET_DV)
        vh = v[:, vs]
        s = lax.dot_general(qb[:, ks], kb[:, ks], (((1,), (1,)), ((), ())),
                            preferred_element_type=F32) * mask_ref[h]
        r_h = r_scr[h]
        o = jnp.dot(s.astype(BF16), vh, preferred_element_type=F32)
        o = o + jnp.dot(qd[:, ks], r_h.astype(BF16), preferred_element_type=F32)
        kv = lax.dot_general(kd[:, ks], vh, (((0,), (0,)), ((), ())), preferred_element_type=F32)
        r_scr[h] = gc_ref[h] * r_h + kv
        mu = jnp.mean(o, axis=-1, keepdims=True)
        var = jnp.mean(jnp.square(o - mu), axis=-1, keepdims=True)
        on = ((o - mu) * lax.rsqrt(var + EPS)) * gn_ref[:, vs]
        gh = g[:, vs]
        o_ref[:, vs] = ((gh * jax.nn.sigmoid(gh)) * on).astype(BF16)

    @pl.when(c == pl.num_programs(1) - 1)
    def _():
        r_out_ref[0] = r_scr[...]


def retention_path(p, row0, b, t, pos0, r0, gn_g):
    c = RET_CHUNK if t % RET_CHUNK == 0 else t
    nc = t // c
    blk0 = row0 // c
    assert row0 % c == 0
    log_g = jnp.log1p(-(2.0 ** (-5.0 - jnp.arange(RET_HEADS, dtype=F32))))
    idx = jnp.arange(c, dtype=F32)
    diff = idx[:, None] - idx[None, :]
    mask = jnp.where(diff[None] >= 0, jnp.exp(jnp.maximum(diff, 0.0)[None] * log_g[:, None, None]), 0.0)
    k_w = jnp.exp((c - 1 - idx)[:, None] * log_g[None, :])
    q_w = jnp.exp((idx + 1.0)[:, None] * log_g[None, :])
    g_c = jnp.exp(c * log_g)
    qw_tab = jnp.repeat(q_w, RET_DK, axis=1)
    kw_tab = jnp.repeat(k_w, RET_DK, axis=1)
    gc_tab = jnp.broadcast_to(g_c[:, None, None], (RET_HEADS, 1, RET_DV))
    half = RET_DK // 2
    pos = pos0 + jnp.arange(t, dtype=F32)
    freq = ROPE_BASE ** (-jnp.arange(half, dtype=F32) / half)
    ang = pos[:, None] * freq[None, :]
    cos_h = jnp.concatenate([jnp.cos(ang), jnp.cos(ang)], axis=1)
    sin_h = jnp.concatenate([-jnp.sin(ang), jnp.sin(ang)], axis=1)
    cos_tab = jnp.tile(cos_h, (1, RET_HEADS))
    sin_tab = jnp.tile(sin_h, (1, RET_HEADS))

    rows = lambda bi, ci: blk0 + bi * nc + ci
    o, r_new = pl.pallas_call(
        _retention_kernel,
        grid=(b, nc),
        in_specs=[
            pl.BlockSpec((c, RET_QK), lambda bi, ci: (rows(bi, ci), 0)),
            pl.BlockSpec((c, RET_QK), lambda bi, ci: (rows(bi, ci), 1)),
            pl.BlockSpec((c, RET_V), lambda bi, ci: (rows(bi, ci), 1)),
            pl.BlockSpec((c, RET_V), lambda bi, ci: (rows(bi, ci), 2)),
            pl.BlockSpec((1, RET_HEADS, RET_DK, RET_DV), lambda bi, ci: (bi, 0, 0, 0)),
            pl.BlockSpec((c, RET_QK), lambda bi, ci: (ci, 0)),
            pl.BlockSpec((c, RET_QK), lambda bi, ci: (ci, 0)),
            pl.BlockSpec((RET_HEADS, c, c), lambda bi, ci: (0, 0, 0)),
            pl.BlockSpec((c, RET_QK), lambda bi, ci: (0, 0)),
            pl.BlockSpec((c, RET_QK), lambda bi, ci: (0, 0)),
            pl.BlockSpec((RET_HEADS, 1, RET_DV), lambda bi, ci: (0, 0, 0)),
            pl.BlockSpec((1, RET_V), lambda bi, ci: (0, 0)),
        ],
        out_specs=[
            pl.BlockSpec((c, RET_V), lambda bi, ci: (bi * nc + ci, 0)),
            pl.BlockSpec((1, RET_HEADS, RET_DK, RET_DV), lambda bi, ci: (bi, 0, 0, 0)),
        ],
        out_shape=[
            jax.ShapeDtypeStruct((b * t, RET_V), BF16),
            jax.ShapeDtypeStruct((b, RET_HEADS, RET_DK, RET_DV), F32),
        ],
        scratch_shapes=[pltpu.VMEM((RET_HEADS, RET_DK, RET_DV), F32)],
        compiler_params=_params("parallel", "arbitrary"),
        name="retention",
    )(p, p, p, p, r0, cos_tab, sin_tab, mask, qw_tab, kw_tab, gc_tab, gn_g.reshape(1, RET_V))
    return o, r_new


GELU_C0 = math.sqrt(2.0 / math.pi)
GELU_C1 = GELU_C0 * 0.044715


def _gelu_tanh(x):
    return x * (0.5 * (1.0 + jnp.tanh(math.sqrt(2.0 / math.pi) * (x + 0.044715 * (x * x * x)))))


def _rglru_gates(xc, wa_ref, ba_ref, wx_ref, bx_ref, lam_ref):
    xcb = xc.astype(BF16)
    nblk = wa_ref.shape[0]
    wdt = wa_ref.shape[1]
    ra = jnp.concatenate(
        [jnp.dot(xcb[:, j * wdt:(j + 1) * wdt], wa_ref[j], preferred_element_type=F32) for j in range(nblk)],
        axis=1)
    ri = jnp.concatenate(
        [jnp.dot(xcb[:, j * wdt:(j + 1) * wdt], wx_ref[j], preferred_element_type=F32) for j in range(nblk)],
        axis=1)
    r = jax.nn.sigmoid(ra + ba_ref[...])
    i = jax.nn.sigmoid(ri + bx_ref[...])
    z = -lam_ref[...]
    softplus = jnp.maximum(z, 0.0) + jnp.log1p(jnp.exp(-jnp.abs(z)))
    log_a = (-RG_C * r) * softplus
    a = jnp.exp(log_a)
    one_m_a2 = -jnp.tanh(log_a) * (a * a + 1.0)
    return a, jnp.sqrt(one_m_a2) * (i * xc)


def _rglru_rows_kernel(bsz, tc, *refs):
    xr_refs, g_refs = refs[:bsz], refs[bsz:2 * bsz]
    (buf_ref, h0_ref, cw_ref, cb_ref, wa_ref, ba_ref, wx_ref, bx_ref, lam_ref, o_ref, hl_ref, nb_ref,
     xcat_scr, a_scr, u_scr, hs_scr, h_scr) = refs[2 * bsz:]
    step = pl.program_id(0)
    rows = tc * bsz
    hist = (CONV_W - 1) * bsz

    def time_major(x):
        return jnp.swapaxes(x, 0, 1).reshape(x.shape[0] * x.shape[1], D_RNN)

    @pl.when(step == 0)
    def _():
        xcat_scr[pl.ds(rows, hist), :] = time_major(buf_ref[...])
        h_scr[...] = h0_ref[...]

    xcat_scr[pl.ds(0, hist), :] = xcat_scr[pl.ds(rows, hist), :]
    xcat_scr[pl.ds(hist, rows), :] = time_major(jnp.stack([xr_refs[b][...] for b in range(bsz)]))
    xc = cb_ref[...] + xcat_scr[pl.ds(0, rows), :] * cw_ref[0:1, :]
    for w in range(1, CONV_W):
        xc = xc + xcat_scr[pl.ds(w * bsz, rows), :] * cw_ref[w:w + 1, :]
    a, u = _rglru_gates(xc, wa_ref, ba_ref, wx_ref, bx_ref, lam_ref)
    a_scr[...] = a.reshape(tc, bsz, D_RNN)
    u_scr[...] = u.reshape(tc, bsz, D_RNN)

    def scan_step(t, h):
        h = a_scr[t] * h + u_scr[t]
        hs_scr[t] = h
        return h

    h_scr[...] = lax.fori_loop(0, tc, scan_step, h_scr[...])
    hs = jnp.swapaxes(hs_scr[...], 0, 1)
    for b in range(bsz):
        o_ref[b] = (hs[b] * _gelu_tanh(g_refs[b][...])).astype(BF16)

    @pl.when(step == pl.num_programs(0) - 1)
    def _():
        hl_ref[...] = h_scr[...]
        nb_ref[...] = jnp.swapaxes(xcat_scr[pl.ds(rows, hist), :].reshape(CONV_W - 1, bsz, D_RNN), 0, 1)


def rglru_rows_path(p, row0, b, t, buf0, h0, conv_w, conv_b, wa4, ba, wx4, bx, lam):
    assert b == SUBLANES and t >= CONV_W - 1
    tc = _pick(t, (32, 16))
    assert t % tc == 0 and row0 % tc == 0
    d = D_RNN
    hist = CONV_W - 1
    nblk, wdt = wa4.shape[0], wa4.shape[1]
    blk0, per_batch = row0 // tc, t // tc
    col_xr, col_g = 3, 4
    row_map = lambda bi, col: (lambda s: (blk0 + bi * per_batch + s, col))
    const2 = lambda s: (0, 0)
    const3 = lambda s: (0, 0, 0)
    out, h_last, new_buf = pl.pallas_call(
        functools.partial(_rglru_rows_kernel, b, tc),
        grid=(per_batch,),
        in_specs=[pl.BlockSpec((tc, d), row_map(bi, col_xr)) for bi in range(b)]
        + [pl.BlockSpec((tc, d), row_map(bi, col_g)) for bi in range(b)]
        + [
            pl.BlockSpec((b, hist, d), const3),
            pl.BlockSpec((b, d), const2),
            pl.BlockSpec((CONV_W, d), const2),
            pl.BlockSpec((1, d), const2),
            pl.BlockSpec((nblk, wdt, wdt), const3),
            pl.BlockSpec((1, d), const2),
            pl.BlockSpec((nblk, wdt, wdt), const3),
            pl.BlockSpec((1, d), const2),
            pl.BlockSpec((1, d), const2),
        ],
        out_specs=[
            pl.BlockSpec((b, tc, d), lambda s: (0, s, 0)),
            pl.BlockSpec((b, d), const2),
            pl.BlockSpec((b, hist, d), const3),
        ],
        out_shape=[
            jax.ShapeDtypeStruct((b, t, d), BF16),
            jax.ShapeDtypeStruct((b, d), F32),
            jax.ShapeDtypeStruct((b, hist, d), F32),
        ],
        scratch_shapes=[
            pltpu.VMEM(((CONV_W - 1 + tc) * b, d), F32),
            pltpu.VMEM((tc, b, d), F32),
            pltpu.VMEM((tc, b, d), F32),
            pltpu.VMEM((tc, b, d), F32),
            pltpu.VMEM((b, d), F32),
        ],
        compiler_params=_params("arbitrary"),
        name="rglru_rows",
    )(*([p] * (2 * b)), buf0, h0, conv_w, conv_b.reshape(1, d), wa4, ba.reshape(1, d), wx4, bx.reshape(1, d),
      lam.reshape(1, d))
    return out.reshape(b * t, d), h_last, new_buf


def _rglru_kernel(bsz, tc, xr_ref, g_ref, buf_ref, h0_ref, cw_ref, cb_ref, wa_ref, ba_ref, wx_ref, bx_ref,
                  lam_ref, o_ref, hl_ref, nb_ref, xcat_scr, a_scr, u_scr, hs_scr):
    rows = tc * bsz
    hist = (CONV_W - 1) * bsz
    step = pl.program_id(0)

    @pl.when(step == 0)
    def _():
        xcat_scr[pl.ds(rows, hist), :] = buf_ref[...]
        hs_scr[pl.ds(rows, bsz), :] = h0_ref[...]

    xcat_scr[pl.ds(0, hist), :] = xcat_scr[pl.ds(rows, hist), :]
    hs_scr[pl.ds(0, bsz), :] = hs_scr[pl.ds(rows, bsz), :]
    xcat_scr[pl.ds(hist, rows), :] = xr_ref[...]

    xc = cb_ref[...] + xcat_scr[pl.ds(0, rows), :] * cw_ref[0:1, :]
    for w in range(1, CONV_W):
        xc = xc + xcat_scr[pl.ds(w * bsz, rows), :] * cw_ref[w:w + 1, :]
    a, u = _rglru_gates(xc, wa_ref, ba_ref, wx_ref, bx_ref, lam_ref)
    a_scr[...] = a
    u_scr[...] = u

    def scan_step(t, carry):
        prev = hs_scr[pl.ds(pl.multiple_of(t * bsz, bsz), bsz), :]
        cur = pl.ds(pl.multiple_of(t * bsz, bsz), bsz)
        h = a_scr[cur, :] * prev + u_scr[cur, :]
        hs_scr[pl.ds(pl.multiple_of((t + 1) * bsz, bsz), bsz), :] = h
        return carry

    lax.fori_loop(0, tc, scan_step, 0)
    hs = hs_scr[pl.ds(bsz, rows), :]
    o_ref[...] = (hs * _gelu_tanh(g_ref[...])).astype(BF16)

    @pl.when(step == pl.num_programs(0) - 1)
    def _():
        hl_ref[...] = hs_scr[pl.ds(rows, bsz), :]
        nb_ref[...] = xcat_scr[pl.ds(rows, hist), :]


def rglru_path(xr_tm, g_tm, buf_tm, h0, b, t, conv_w, conv_b, wa4, ba, wx4, bx, lam):
    assert t >= CONV_W - 1
    rows_target = 256
    tc = max(1, min(t, rows_target // b))
    while t % tc:
        tc -= 1
    rows = tc * b
    hist = (CONV_W - 1) * b
    d = D_RNN
    nblk, wdt = wa4.shape[0], wa4.shape[1]
    const2 = lambda s: (0, 0)
    out, h_last, new_buf = pl.pallas_call(
        functools.partial(_rglru_kernel, b, tc),
        grid=(t // tc,),
        in_specs=[
            pl.BlockSpec((rows, d), lambda s: (s, 0)),
            pl.BlockSpec((rows, d), lambda s: (s, 0)),
            pl.BlockSpec((hist, d), const2),
            pl.BlockSpec((b, d), const2),
            pl.BlockSpec((CONV_W, d), const2),
            pl.BlockSpec((1, d), const2),
            pl.BlockSpec((nblk, wdt, wdt), lambda s: (0, 0, 0)),
            pl.BlockSpec((1, d), const2),
            pl.BlockSpec((nblk, wdt, wdt), lambda s: (0, 0, 0)),
            pl.BlockSpec((1, d), const2),
            pl.BlockSpec((1, d), const2),
        ],
        out_specs=[
            pl.BlockSpec((rows, d), lambda s: (s, 0)),
            pl.BlockSpec((b, d), const2),
            pl.BlockSpec((hist, d), const2),
        ],
        out_shape=[
            jax.ShapeDtypeStruct((t * b, d), BF16),
            jax.ShapeDtypeStruct((b, d), F32),
            jax.ShapeDtypeStruct((hist, d), F32),
        ],
        scratch_shapes=[
            pltpu.VMEM((rows + hist, d), F32),
            pltpu.VMEM((rows, d), F32),
            pltpu.VMEM((rows, d), F32),
            pltpu.VMEM((rows + b, d), F32),
        ],
        compiler_params=_params("arbitrary"),
        name="rglru",
    )(xr_tm, g_tm, buf_tm, h0, conv_w, conv_b.reshape(1, d), wa4, ba.reshape(1, d), wx4, bx.reshape(1, d),
      lam.reshape(1, d))
    return out, h_last, new_buf


def _block_diag_tiles(w, tile):
    nb, bs, _ = w.shape
    per = tile // bs
    w = w.reshape(nb // per, per, bs, bs)
    eye = jnp.eye(per, dtype=w.dtype)
    dense = jnp.einsum("gpcd,pq->gpcqd", w, eye).reshape(nb // per, tile, tile)
    return dense.astype(BF16)


def _merge_kernel(blocks_a, nx, *refs):
    x_refs, refs = refs[:nx], refs[nx:]
    ro_refs, rn_refs = refs[0:2], refs[2:4]
    ga_ref, gb_ref, wr_ref, wn_ref, wo_ref, yt_ref = refs[4:]
    ret_out = jnp.dot(_pick_rows(ro_refs, blocks_a), wr_ref[...], preferred_element_type=F32)
    rnn_out = jnp.dot(_pick_rows(rn_refs, blocks_a), wn_ref[...], preferred_element_type=F32)
    merged = jax.nn.sigmoid(ga_ref[...]) * ret_out + jax.nn.sigmoid(gb_ref[...]) * rnn_out
    y = _pick_rows(x_refs, blocks_a) + jnp.dot(merged.astype(BF16), wo_ref[...], preferred_element_type=F32)
    yt_ref[...] = y.T


def merge_proj(x_parts, ret_parts, rnn_parts, p, w_ret_out, w_rnn_out, w_o):
    n, d = p.shape[0], x_parts[0].shape[1]
    tm = _pick(math.gcd(*[r.shape[0] for r in ret_parts]), (512, 256, 128))
    blocks_a, part_maps = _row_part_maps(ret_parts, tm)
    assert len(x_parts) == 1 or x_parts[0].shape[0] == ret_parts[0].shape[0]
    _, x_maps = _row_part_maps(x_parts, tm) if len(x_parts) == 2 else (None, [lambda i: (i, 0)])
    const = lambda i: (0, 0)
    return pl.pallas_call(
        functools.partial(_merge_kernel, blocks_a, len(x_parts)),
        grid=(n // tm,),
        in_specs=[pl.BlockSpec((tm, d), m) for m in x_maps + part_maps + part_maps] + [
            pl.BlockSpec((tm, d), lambda i: (i, 5)),
            pl.BlockSpec((tm, d), lambda i: (i, 6)),
            pl.BlockSpec((d, d), const),
            pl.BlockSpec((d, d), const),
            pl.BlockSpec((d, d), const),
        ],
        out_specs=pl.BlockSpec((d, tm), lambda i: (0, i)),
        out_shape=jax.ShapeDtypeStruct((d, n), F32),
        compiler_params=_params("parallel"),
        name="merge_out_proj",
    )(*x_parts, *ret_parts, *rnn_parts, p, p, w_ret_out, w_rnn_out, w_o)


def _peer_query_kernel(xt_ref, g_ref, wq_ref, xn_ref, q_ref):
    x = xt_ref[...]
    ms = jnp.mean(x * x, axis=0, keepdims=True)
    xn = ((x * lax.rsqrt(ms + EPS)) * g_ref[...]).astype(BF16)
    xn_ref[...] = xn
    q_ref[...] = jnp.dot(wq_ref[...], xn, preferred_element_type=F32)


def peer_query(xt, g, wq_t):
    d, n = xt.shape
    m = wq_t.shape[0]
    tn = _pick(n, (512, 256, 128))
    return pl.pallas_call(
        _peer_query_kernel,
        grid=(n // tn,),
        in_specs=[
            pl.BlockSpec((d, tn), lambda i: (0, i)),
            pl.BlockSpec((d, 1), lambda i: (0, 0)),
            pl.BlockSpec((m, d), lambda i: (0, 0)),
        ],
        out_specs=[pl.BlockSpec((d, tn), lambda i: (0, i)), pl.BlockSpec((m, tn), lambda i: (0, i))],
        out_shape=[jax.ShapeDtypeStruct((d, n), BF16), jax.ShapeDtypeStruct((m, n), F32)],
        compiler_params=_params("parallel"),
        name="peer_query",
    )(xt, g.reshape(d, 1), wq_t)


def _sort_pairs(n):
    def merge(lo, hi, r):
        step = r * 2
        if step < hi - lo:
            yield from merge(lo, hi, step)
            yield from merge(lo + r, hi, step)
            yield from [(i, i + r) for i in range(lo + r, hi - r, step)]
        else:
            yield (lo, lo + r)

    def sort(lo, hi):
        if hi - lo >= 1:
            mid = lo + (hi - lo) // 2
            yield from sort(lo, mid)
            yield from sort(mid + 1, hi)
            yield from merge(lo, hi, 1)

    return list(sort(0, n - 1))


_SORT16 = _sort_pairs(PEER_TOPK)


def _cmpx(vals, i, j):
    a, b = vals[i], vals[j]
    if b is None:
        return
    if a is None:
        vals[i], vals[j] = b, None
        return
    vals[i], vals[j] = jnp.maximum(a, b), jnp.minimum(a, b)


def _sort_desc(vals):
    vals = list(vals)
    for i, j in _SORT16:
        _cmpx(vals, i, j)
    return vals


def _merge_top(a, b):
    k = PEER_TOPK
    a = list(a) + [None] * (k - len(a))
    b = list(b) + [None] * (k - len(b))
    out = []
    for r in range(k):
        x, y = a[r], b[k - 1 - r]
        out.append(y if x is None else (x if y is None else jnp.maximum(x, y)))
    d = k // 2
    while d >= 1:
        for i in range(k):
            if not i & d:
                _cmpx(out, i, i + d)
        d //= 2
    return out


def _top_sorted(ref):
    groups = []
    for g0 in range(0, N_KEYS, PEER_TOPK):
        vals = [ref[pl.ds((g0 + j) * PEER_HEADS, PEER_HEADS), :] for j in range(PEER_TOPK)]
        groups.append(_sort_desc(vals))
    while len(groups) > 1:
        groups = [_merge_top(groups[i], groups[i + 1]) for i in range(0, len(groups), 2)]
    return groups[0]


def _peer_select_kernel(q_ref, k1_ref, k2_ref, c1_ref, e1_ref, r2_ref, e2_ref, s1_scr, s2_scr, r2_scr, e2_scr):
    nh = PEER_HEADS
    half_rows = q_ref.shape[0] // 2
    tn = q_ref.shape[1]
    s1 = jnp.dot(k1_ref[...], q_ref[pl.ds(0, half_rows), :].astype(BF16), preferred_element_type=F32)
    s2 = jnp.dot(k2_ref[...], q_ref[pl.ds(half_rows, half_rows), :].astype(BF16), preferred_element_type=F32)
    for lt in range(tn // LANES):
        s1_scr[lt] = s1[:, lt * LANES:(lt + 1) * LANES]
        s2_scr[lt] = s2[:, lt * LANES:(lt + 1) * LANES]
    for lt in range(tn // LANES):
        lanes = pl.ds(lt * LANES, LANES)
        s1_t, s2_t, r2_t, e2_t = s1_scr.at[lt], s2_scr.at[lt], r2_scr.at[lt], e2_scr.at[lt]
        a = _top_sorted(s1_t)
        b = _top_sorted(s2_t)
        k = PEER_TOPK
        lists = []
        for j in range(1, k + 1):
            col = [a[r - 1] + b[j - 1] for r in range(j, k // j + 1)]
            row = [a[j - 1] + b[s - 1] for s in range(j + 1, k // j + 1)]
            if col:
                lists.append(col)
            if row:
                lists.append(row)
        top = lists[0]
        for other in lists[1:]:
            top = _merge_top(top, other)
        tau = top[k - 1]
        z = jnp.ones_like(tau)
        for r in range(1, k):
            z = z + jnp.exp(top[r] - top[0])
        zinv = 1.0 / z
        inf = jnp.full((nh, LANES), jnp.inf, F32)
        phi = []
        for s in range(1, k + 1):
            p = inf
            for r in range(1, k // s + 1):
                p = jnp.where(a[r - 1] + b[s - 1] >= tau, a[r - 1], p)
            phi.append(p)

        def per_key(kk, carry):
            rows = pl.ds(pl.multiple_of(kk * nh, nh), nh)
            s1k = s1_t[rows, :]
            s2k = s2_t[rows, :]
            cnt = jnp.ones((nh, LANES), F32)
            for s in range(k):
                cnt = jnp.where(s1k >= phi[s], float(s + 2), cnt)
            rank = jnp.full((nh, LANES), float(k + 1), F32)
            for s in range(k - 1, -1, -1):
                rank = jnp.where(s2k >= b[s], float(s + 1), rank)
            c1_ref[rows, lanes] = cnt
            e1_ref[rows, lanes] = jnp.exp(s1k - a[0])
            r2_t[rows, :] = rank
            e2_t[rows, :] = jnp.exp(s2k - b[0]) * zinv
            return carry

        lax.fori_loop(0, N_KEYS, per_key, 0, unroll=2)
        pack = 2 * SUBLANES
        for h in range(nh):
            for kt in range(N_KEYS // pack):
                lo = pl.ds(kt * pack * nh + h, SUBLANES, stride=nh)
                hi = pl.ds((kt * pack + SUBLANES) * nh + h, SUBLANES, stride=nh)
                dst = pl.ds(h * N_KEYS + kt * pack, pack)
                r2_ref[dst, lanes] = jnp.concatenate([r2_t[lo, :], r2_t[hi, :]], axis=0).astype(BF16)
                e2_ref[dst, lanes] = jnp.concatenate([e2_t[lo, :], e2_t[hi, :]], axis=0).astype(BF16)


def peer_select(q_t, k1, k2):
    m, n = q_t.shape
    rows = PEER_HEADS * N_KEYS
    tn = _pick(n, (256, 128))
    tok = lambda i: (0, i)
    const = lambda i: (0, 0)
    return pl.pallas_call(
        _peer_select_kernel,
        grid=(n // tn,),
        in_specs=[pl.BlockSpec((m, tn), tok), pl.BlockSpec(k1.shape, const), pl.BlockSpec(k2.shape, const)],
        out_specs=[pl.BlockSpec((rows, tn), tok)] * 4,
        out_shape=[jax.ShapeDtypeStruct((rows, n), F32), jax.ShapeDtypeStruct((rows, n), F32),
                   jax.ShapeDtypeStruct((rows, n), BF16), jax.ShapeDtypeStruct((rows, n), BF16)],
        scratch_shapes=[pltpu.VMEM((tn // LANES, rows, LANES), F32)] * 4,
        compiler_params=_params("parallel"),
        name="peer_select",
    )(q_t, k1, k2)


PEER_EXPERT_BLOCK = 1024


def _mxu_dot(lhs, rhs):
    return lax.dot_general(lhs, rhs, (((1,), (0,)), ((), ())), preferred_element_type=F32)


def _peer_dense_kernel(xt_ref, xn_ref, u_ref, vt_ref, c1_ref, e1_ref, r2_ref, e2_ref, y_ref,
                       acc_scr, act_scr):
    j = pl.program_id(1)
    nh = PEER_HEADS
    tn = xn_ref.shape[1]
    pack = 2 * SUBLANES
    i1_per_blk = u_ref.shape[0] // N_KEYS

    @pl.when(j == 0)
    def _():
        acc_scr[...] = jnp.zeros_like(acc_scr)

    wide = 2 * LANES

    def gate_block(il, nc):
        erows = pl.ds(il * N_KEYS, N_KEYS)
        hrows = pl.ds(il * nh, nh)
        hblk = _mxu_dot(u_ref[erows, :], xn_ref[:, pl.ds(nc * wide, wide)])
        for lw in range(wide // LANES):
            lc = nc * (wide // LANES) + lw
            lanes = pl.ds(lc * LANES, LANES)
            c1b = [jnp.broadcast_to(c1_ref[pl.ds(il * nh + h, 1), lanes], (pack, LANES)).astype(BF16)
                   for h in range(nh)]
            e1b = [jnp.broadcast_to(e1_ref[pl.ds(il * nh + h, 1), lanes], (pack, LANES)).astype(BF16)
                   for h in range(nh)]
            zero = jnp.zeros((pack, LANES), BF16)
            half = jnp.full((), 0.5, BF16)
            gelu_c0 = jnp.full((), GELU_C0, BF16)
            gelu_c1 = jnp.full((), GELU_C1, BF16)
            for it in range(N_KEYS // pack):
                rows = pl.ds(il * N_KEYS + it * pack, pack)
                gate = None
                for h in range(nh):
                    krows = pl.ds(h * N_KEYS + it * pack, pack)
                    sel = jnp.minimum(jnp.maximum(c1b[h] - r2_ref[krows, lanes], zero), e2_ref[krows, lanes])
                    term = sel * e1b[h]
                    gate = term if gate is None else gate + term
                x = hblk[it * pack:(it + 1) * pack, lw * LANES:(lw + 1) * LANES].astype(BF16)
                t = jnp.tanh(x * (gelu_c0 + gelu_c1 * (x * x)))
                act_scr[rows, lanes] = (x * (half + half * t)) * gate

    for nc in range(tn // wide):
        for il in range(i1_per_blk):
            gate_block(il, nc)
        cols = pl.ds(nc * wide, wide)
        acc_scr[:, cols] += _mxu_dot(vt_ref[...], act_scr[:, cols])

    @pl.when(j == pl.num_programs(1) - 1)
    def _():
        y_ref[...] = (xt_ref[...] + acc_scr[...]).T


def peer_dense(xt, xn_t, u_bf16, vt_bf16, c1, e1, r2, e2):
    d, n = xt.shape
    ne = u_bf16.shape[0]
    tn = _pick(n, (512, 256, 128))
    te = PEER_EXPERT_BLOCK
    i1_per_blk = te // N_KEYS
    tok = lambda i, j: (0, i)
    return pl.pallas_call(
        _peer_dense_kernel,
        grid=(n // tn, ne // te),
        in_specs=[
            pl.BlockSpec((d, tn), tok),
            pl.BlockSpec((d, tn), tok),
            pl.BlockSpec((te, d), lambda i, j: (j, 0)),
            pl.BlockSpec((d, te), lambda i, j: (0, j)),
            pl.BlockSpec((i1_per_blk * PEER_HEADS, tn), lambda i, j: (j, i)),
            pl.BlockSpec((i1_per_blk * PEER_HEADS, tn), lambda i, j: (j, i)),
            pl.BlockSpec((PEER_HEADS * N_KEYS, tn), tok),
            pl.BlockSpec((PEER_HEADS * N_KEYS, tn), tok),
        ],
        out_specs=pl.BlockSpec((tn, d), lambda i, j: (i, 0)),
        out_shape=jax.ShapeDtypeStruct((n, d), F32),
        scratch_shapes=[pltpu.VMEM((d, tn), F32), pltpu.VMEM((te, tn), BF16)],
        compiler_params=_params("parallel", "arbitrary"),
        name="peer_dense",
    )(xt, xn_t, u_bf16, vt_bf16, c1, e1, r2, e2)


def _interleaved_keys(keys_p):
    nh, nk, half = keys_p.shape
    eye = jnp.eye(nh, dtype=keys_p.dtype)
    return jnp.einsum("hkd,hg->khgd", keys_p, eye).reshape(nk * nh, nh * half).astype(BF16)


def peer_layer(xt, norm_g, wq, keys, u_tab, v_tab):
    d, n = xt.shape
    wq_t = wq.reshape(d, PEER_HEADS, 2, PEER_HALF).transpose(2, 1, 3, 0).reshape(2 * PEER_HEADS * PEER_HALF, d)
    xn_t, q_t = peer_query(xt, norm_g, wq_t.astype(BF16))
    k1 = _interleaved_keys(keys[:, 0])
    k2 = _interleaved_keys(keys[:, 1])
    c1, e1, r2, e2 = peer_select(q_t, k1, k2)
    return peer_dense(xt, xn_t, u_tab, v_tab.T, c1, e1, r2, e2)


def _rmsnorm_kernel(x_ref, g_ref, y_ref):
    x = x_ref[...]
    ms = jnp.mean(x * x, axis=-1, keepdims=True)
    y_ref[...] = (x * lax.rsqrt(ms + EPS)) * g_ref[...]


def final_norm(x, row0, rows, g):
    d = x.shape[1]
    tm = _pick(math.gcd(rows, row0) if row0 else rows, (512, 256, 128, 64, 32, 16, 8))
    blk0 = row0 // tm
    return pl.pallas_call(
        _rmsnorm_kernel,
        grid=(rows // tm,),
        in_specs=[pl.BlockSpec((tm, d), lambda i: (blk0 + i, 0)), pl.BlockSpec((1, d), lambda i: (0, 0))],
        out_specs=pl.BlockSpec((tm, d), lambda i: (i, 0)),
        out_shape=jax.ShapeDtypeStruct((rows, d), F32),
        compiler_params=_params("parallel"),
        name="final_norm",
    )(x, g.reshape(1, d))


def _to_time_major(p, row0, b, t, col):
    blk = lax.slice(p, (row0, col * D_RNN), (row0 + b * t, (col + 1) * D_RNN))
    return blk.reshape(b, t, D_RNN).transpose(1, 0, 2).reshape(t * b, D_RNN)


def _trunk(groups, norm1_g, norm2_g, normf_g, w_in, ret_gn_g, w_ret_out, conv_w, conv_b, rg_wa, rg_ba, rg_wx,
           rg_bx, rg_lambda, w_rnn_out, w_o, peer_wq, peer_keys, peer_u, peer_v):
    shapes = [(g[0].shape[0], g[0].shape[1]) for g in groups]
    x_parts = [g[0].reshape(-1, D_MODEL) for g in groups]
    row0s = np.cumsum([0] + [b * t for b, t in shapes]).tolist()
    states = [([], [], []) for _ in groups]
    for l in range(DEPTH):
        p = norm_matmul(x_parts, norm1_g[l], w_in[l].astype(BF16))
        wa4 = _block_diag_tiles(rg_wa[l], 256)
        wx4 = _block_diag_tiles(rg_wx[l], 256)
        ret_parts, rnn_parts = [], []
        for gi, (xg, r0, h0, buf0, pos0) in enumerate(groups):
            b, t = shapes[gi]
            row0 = row0s[gi]
            o, r_new = retention_path(p, row0, b, t, pos0, r0[l], ret_gn_g[l])
            rg_args = (conv_w[l], conv_b[l], wa4, rg_ba[l], wx4, rg_bx[l], rg_lambda[l])
            if b == SUBLANES and t % 16 == 0:
                hg, h_last, nb = rglru_rows_path(p, row0, b, t, buf0[l], h0[l], *rg_args)
            else:
                xr_tm = _to_time_major(p, row0, b, t, 3)
                g_tm = _to_time_major(p, row0, b, t, 4)
                buf_tm = buf0[l].transpose(1, 0, 2).reshape((CONV_W - 1) * b, D_RNN)
                hg, h_last, nb = rglru_path(xr_tm, g_tm, buf_tm, h0[l], b, t, *rg_args)
                hg = hg.reshape(t, b, D_RNN).transpose(1, 0, 2).reshape(b * t, D_RNN)
                nb = nb.reshape(CONV_W - 1, b, D_RNN).transpose(1, 0, 2)
            ret_parts.append(o)
            rnn_parts.append(hg)
            states[gi][0].append(r_new)
            states[gi][1].append(h_last)
            states[gi][2].append(nb)
        xt = merge_proj(x_parts, ret_parts, rnn_parts, p, w_ret_out[l].astype(BF16), w_rnn_out[l].astype(BF16),
                        w_o[l].astype(BF16))
        x = peer_layer(xt, norm2_g[l], peer_wq[l], peer_keys[l], peer_u[l], peer_v[l])
        x_parts = [x]
    outs = []
    for gi, (b, t) in enumerate(shapes):
        y = final_norm(x, row0s[gi], b * t, normf_g).reshape(b, t, D_MODEL)
        outs.append((y, jnp.stack(states[gi][0]), jnp.stack(states[gi][1]), jnp.stack(states[gi][2])))
    return outs


def kernel(x_prompt, x_sample, state_ret, state_rnn, state_conv, norm1_g, norm2_g, normf_g, w_in, ret_gn_g,
           w_ret_out, conv_w, conv_b, rg_wa, rg_ba, rg_wx, rg_bx, rg_lambda, w_rnn_out, w_o, peer_wq, peer_keys,
           peer_u, peer_v):
    bp = x_prompt.shape[0]
    dt = x_prompt.dtype
    zr = jnp.zeros((DEPTH, bp, RET_HEADS, RET_DK, RET_DV), dt)
    zh = jnp.zeros((DEPTH, bp, D_RNN), dt)
    zc = jnp.zeros((DEPTH, bp, CONV_W - 1, D_RNN), dt)
    groups = [(x_prompt, zr, zh, zc, 0.0), (x_sample, state_ret, state_rnn, state_conv, float(PAST_LEN))]
    (yp, rp, hp, cp), (ys, rs, hs, cs) = _trunk(
        groups, norm1_g, norm2_g, normf_g, w_in, ret_gn_g, w_ret_out, conv_w, conv_b, rg_wa, rg_ba, rg_wx, rg_bx,
        rg_lambda, w_rnn_out, w_o, peer_wq, peer_keys, peer_u, peer_v)
    return (yp, ys, rp, hp, cp, rs, hs, cs)
```

```python
import functools
import math

import jax
import jax.numpy as jnp
import numpy as np
from jax import lax
from jax.experimental import pallas as pl
from jax.experimental.pallas import tpu as pltpu

D_MODEL = 1024
DEPTH = 2
PAST_LEN = 16384
RET_HEADS = 8
RET_DK = 64
RET_DV = 128
RET_QK = RET_HEADS * RET_DK
RET_V = RET_HEADS * RET_DV
RET_CHUNK = 128
ROPE_BASE = 10000.0
D_RNN = 1024
RNN_BLOCKS = 16
RNN_BS = D_RNN // RNN_BLOCKS
CONV_W = 4
RG_C = 8.0
IN_SIZES = (RET_QK, RET_QK, RET_V, RET_V, D_RNN, D_RNN, D_MODEL, D_MODEL)
N_IN = sum(IN_SIZES)
PEER_HEADS = 8
N_KEYS = 128
N_EXPERTS = N_KEYS * N_KEYS
PEER_DKEY = 256
PEER_HALF = PEER_DKEY // 2
PEER_TOPK = 16
EPS = 1e-6

SUBLANES = 8
LANES = 128
VMEM_LIMIT = 56 * 1024 * 1024

F32 = jnp.float32
BF16 = jnp.bfloat16


def _params(*sem, flags=None):
    return pltpu.CompilerParams(dimension_semantics=sem, vmem_limit_bytes=VMEM_LIMIT, flags=flags)


def _pick(n, prefs):
    for p in prefs:
        if n % p == 0:
            return p
    return n


def _row_part_maps(parts, tm):
    assert len(parts) in (1, 2) and all(p.shape[0] % tm == 0 for p in parts)
    blocks_a = parts[0].shape[0] // tm
    maps = [lambda i, *_: (jnp.minimum(i, blocks_a - 1), 0)]
    if len(parts) == 2:
        maps.append(lambda i, *_: (jnp.maximum(i - blocks_a, 0), 0))
    return blocks_a, maps


def _pick_rows(refs, blocks_a):
    x = refs[0][...]
    if len(refs) == 2:
        x = jnp.where(pl.program_id(0) < blocks_a, x, refs[1][...])
    return x


def _norm_matmul_kernel(blocks_a, nparts, *refs):
    x_refs = refs[:nparts]
    g_ref, w_ref, y_ref, xn_scr = refs[nparts:]

    @pl.when(pl.program_id(1) == 0)
    def _():
        x = _pick_rows(x_refs, blocks_a)
        ms = jnp.mean(x * x, axis=-1, keepdims=True)
        xn_scr[...] = ((x * lax.rsqrt(ms + EPS)) * g_ref[...]).astype(BF16)

    y_ref[...] = jnp.dot(xn_scr[...], w_ref[...], preferred_element_type=F32)


def norm_matmul(x_parts, g, w_bf16):
    n = sum(p.shape[0] for p in x_parts)
    d = x_parts[0].shape[1]
    m = w_bf16.shape[1]
    tm = _pick(math.gcd(*[p.shape[0] for p in x_parts]), (1024, 512, 256, 128))
    tn = _pick(m, (1792, 1024, 512, 256, 128))
    blocks_a, row_maps = _row_part_maps(x_parts, tm)
    return pl.pallas_call(
        functools.partial(_norm_matmul_kernel, blocks_a, len(x_parts)),
        grid=(n // tm, m // tn),
        in_specs=[pl.BlockSpec((tm, d), rm) for rm in row_maps] + [
            pl.BlockSpec((1, d), lambda i, j: (0, 0)),
            pl.BlockSpec((d, tn), lambda i, j: (0, j)),
        ],
        out_specs=pl.BlockSpec((tm, tn), lambda i, j: (i, j)),
        out_shape=jax.ShapeDtypeStruct((n, m), F32),
        scratch_shapes=[pltpu.VMEM((tm, d), BF16)],
        compiler_params=_params("parallel", "arbitrary"),
        name="norm_in_proj",
    )(*x_parts, g.reshape(1, d), w_bf16)


def _rot_half(x):
    n = x.shape[-1]
    half = RET_DK // 2
    fwd = pltpu.roll(x, half, axis=1)
    bwd = pltpu.roll(x, n - half, axis=1)
    lane = lax.broadcasted_iota(jnp.int32, x.shape, 1)
    return jnp.where((lane % RET_DK) < half, bwd, fwd)


def _retention_kernel(q_ref, k_ref, v_ref, g_ref, r0_ref, cos_ref, sin_ref, mask_ref, qw_ref, kw_ref,
                      gc_ref, gn_ref, o_ref, r_out_ref, r_scr):
    c = pl.program_id(1)
    rows_c = mask_ref.shape[1]
    nb = q_ref.shape[0] // rows_c

    @pl.when(c == 0)
    def _():
        r_scr[...] = r0_ref[...]

    cos = cos_ref[...]
    sin = sin_ref[...]
    q = q_ref[...]
    k = k_ref[...]
    qr = q * cos + _rot_half(q) * sin
    kr = (k * cos + _rot_half(k) * sin) * (RET_DK ** -0.5)
    qd = (qr * qw_ref[...]).astype(BF16)
    kd = (kr * kw_ref[...]).astype(BF16)
    qb = qr.astype(BF16)
    kb = kr.astype(BF16)
    v = v_ref[...].astype(BF16)
    g = g_ref[...]
    for bb in range(nb):
        rs = slice(bb * rows_c, (bb + 1) * rows_c)
        for h in range(RET_HEADS):
            ks = slice(h * RET_DK, (h + 1) * RET_DK)
            vs = slice(h * RET_DV, (h + 1) * RET_DV)
            vh = v[rs, vs]
            s = lax.dot_general(qb[rs, ks], kb[rs, ks], (((1,), (1,)), ((), ())),
                                preferred_element_type=F32) * mask_ref[h]
            r_h = r_scr[bb, h]
            o = jnp.dot(s.astype(BF16), vh, preferred_element_type=F32)
            o = o + jnp.dot(qd[rs, ks], r_h.astype(BF16), preferred_element_type=F32)
            kv = lax.dot_general(kd[rs, ks], vh, (((0,), (0,)), ((), ())), preferred_element_type=F32)
            r_scr[bb, h] = gc_ref[h] * r_h + kv
            mu = jnp.mean(o, axis=-1, keepdims=True)
            var = jnp.mean(jnp.square(o - mu), axis=-1, keepdims=True)
            on = ((o - mu) * lax.rsqrt(var + EPS)) * gn_ref[:, vs]
            gh = g[rs, vs]
            o_ref[rs, vs] = ((gh * jax.nn.sigmoid(gh)) * on).astype(BF16)

    @pl.when(c == pl.num_programs(1) - 1)
    def _():
        r_out_ref[...] = r_scr[...]


def retention_path(p, row0, b, t, pos0, r0, gn_g):
    c = RET_CHUNK if t % RET_CHUNK == 0 else t
    nc = t // c
    blk0 = row0 // c
    assert row0 % c == 0
    log_g = jnp.log1p(-(2.0 ** (-5.0 - jnp.arange(RET_HEADS, dtype=F32))))
    idx = jnp.arange(c, dtype=F32)
    diff = idx[:, None] - idx[None, :]
    mask = jnp.where(diff[None] >= 0, jnp.exp(jnp.maximum(diff, 0.0)[None] * log_g[:, None, None]), 0.0)
    k_w = jnp.exp((c - 1 - idx)[:, None] * log_g[None, :])
    q_w = jnp.exp((idx + 1.0)[:, None] * log_g[None, :])
    g_c = jnp.exp(c * log_g)
    qw_tab = jnp.repeat(q_w, RET_DK, axis=1)
    kw_tab = jnp.repeat(k_w, RET_DK, axis=1)
    gc_tab = jnp.broadcast_to(g_c[:, None, None], (RET_HEADS, 1, RET_DV))
    half = RET_DK // 2
    pos = pos0 + jnp.arange(t, dtype=F32)
    freq = ROPE_BASE ** (-jnp.arange(half, dtype=F32) / half)
    ang = pos[:, None] * freq[None, :]
    cos_h = jnp.concatenate([jnp.cos(ang), jnp.cos(ang)], axis=1)
    sin_h = jnp.concatenate([-jnp.sin(ang), jnp.sin(ang)], axis=1)
    cos_tab = jnp.tile(cos_h, (1, RET_HEADS))
    sin_tab = jnp.tile(sin_h, (1, RET_HEADS))
    nb = _pick(b, (8, 4, 2, 1)) if nc == 1 and c * 8 <= RET_CHUNK else 1
    if nb > 1:
        cos_tab, sin_tab, qw_tab, kw_tab = [jnp.tile(tab, (nb, 1)) for tab in (cos_tab, sin_tab, qw_tab, kw_tab)]
    rb = nb * c
    blk0 = row0 // rb
    assert row0 % rb == 0

    rows = lambda bi, ci: blk0 + bi * nc + ci
    state_spec = pl.BlockSpec((nb, RET_HEADS, RET_DK, RET_DV), lambda bi, ci: (bi, 0, 0, 0))
    o, r_new = pl.pallas_call(
        _retention_kernel,
        grid=(b // nb, nc),
        in_specs=[
            pl.BlockSpec((rb, RET_QK), lambda bi, ci: (rows(bi, ci), 0)),
            pl.BlockSpec((rb, RET_QK), lambda bi, ci: (rows(bi, ci), 1)),
            pl.BlockSpec((rb, RET_V), lambda bi, ci: (rows(bi, ci), 1)),
            pl.BlockSpec((rb, RET_V), lambda bi, ci: (rows(bi, ci), 2)),
            state_spec,
            pl.BlockSpec((rb, RET_QK), lambda bi, ci: (ci, 0)),
            pl.BlockSpec((rb, RET_QK), lambda bi, ci: (ci, 0)),
            pl.BlockSpec((RET_HEADS, c, c), lambda bi, ci: (0, 0, 0)),
            pl.BlockSpec((rb, RET_QK), lambda bi, ci: (0, 0)),
            pl.BlockSpec((rb, RET_QK), lambda bi, ci: (0, 0)),
            pl.BlockSpec((RET_HEADS, 1, RET_DV), lambda bi, ci: (0, 0, 0)),
            pl.BlockSpec((1, RET_V), lambda bi, ci: (0, 0)),
        ],
        out_specs=[
            pl.BlockSpec((rb, RET_V), lambda bi, ci: (bi * nc + ci, 0)),
            state_spec,
        ],
        out_shape=[
            jax.ShapeDtypeStruct((b * t, RET_V), BF16),
            jax.ShapeDtypeStruct((b, RET_HEADS, RET_DK, RET_DV), F32),
        ],
        scratch_shapes=[pltpu.VMEM((nb, RET_HEADS, RET_DK, RET_DV), F32)],
        compiler_params=_params("parallel", "arbitrary"),
        name="retention",
    )(p, p, p, p, r0, cos_tab, sin_tab, mask, qw_tab, kw_tab, gc_tab, gn_g.reshape(1, RET_V))
    return o, r_new


GELU_C0 = math.sqrt(2.0 / math.pi)
GELU_C1 = GELU_C0 * 0.044715


def _gelu_tanh(x):
    return x * (0.5 * (1.0 + jnp.tanh(math.sqrt(2.0 / math.pi) * (x + 0.044715 * (x * x * x)))))


def _rglru_gates(xc, wa_ref, ba_ref, wx_ref, bx_ref, lam_ref):
    xcb = xc.astype(BF16)
    nblk = wa_ref.shape[0]
    wdt = wa_ref.shape[1]
    ra = jnp.concatenate(
        [jnp.dot(xcb[:, j * wdt:(j + 1) * wdt], wa_ref[j], preferred_element_type=F32) for j in range(nblk)],
        axis=1)
    ri = jnp.concatenate(
        [jnp.dot(xcb[:, j * wdt:(j + 1) * wdt], wx_ref[j], preferred_element_type=F32) for j in range(nblk)],
        axis=1)
    r = jax.nn.sigmoid(ra + ba_ref[...])
    i = jax.nn.sigmoid(ri + bx_ref[...])
    z = -lam_ref[...]
    softplus = jnp.maximum(z, 0.0) + jnp.log1p(jnp.exp(-jnp.abs(z)))
    log_a = (-RG_C * r) * softplus
    a = jnp.exp(log_a)
    one_m_a2 = -jnp.tanh(log_a) * (a * a + 1.0)
    return a, jnp.sqrt(one_m_a2) * (i * xc)


def _rglru_rows_kernel(bsz, tc, *refs):
    xr_refs, g_refs = refs[:bsz], refs[bsz:2 * bsz]
    (buf_ref, h0_ref, cw_ref, cb_ref, wa_ref, ba_ref, wx_ref, bx_ref, lam_ref, o_ref, hl_ref, nb_ref,
     xcat_scr, a_scr, u_scr, hs_scr, h_scr) = refs[2 * bsz:]
    step = pl.program_id(0)
    rows = tc * bsz
    hist = (CONV_W - 1) * bsz

    def time_major(x):
        return jnp.swapaxes(x, 0, 1).reshape(x.shape[0] * x.shape[1], D_RNN)

    @pl.when(step == 0)
    def _():
        xcat_scr[pl.ds(rows, hist), :] = time_major(buf_ref[...])
        h_scr[...] = h0_ref[...]

    xcat_scr[pl.ds(0, hist), :] = xcat_scr[pl.ds(rows, hist), :]
    xcat_scr[pl.ds(hist, rows), :] = time_major(jnp.stack([xr_refs[b][...] for b in range(bsz)]))
    xc = cb_ref[...] + xcat_scr[pl.ds(0, rows), :] * cw_ref[0:1, :]
    for w in range(1, CONV_W):
        xc = xc + xcat_scr[pl.ds(w * bsz, rows), :] * cw_ref[w:w + 1, :]
    a, u = _rglru_gates(xc, wa_ref, ba_ref, wx_ref, bx_ref, lam_ref)
    a_scr[...] = a.reshape(tc, bsz, D_RNN)
    u_scr[...] = u.reshape(tc, bsz, D_RNN)

    def scan_step(t, h):
        h = a_scr[t] * h + u_scr[t]
        hs_scr[t] = h
        return h

    h_scr[...] = lax.fori_loop(0, tc, scan_step, h_scr[...])
    hs = jnp.swapaxes(hs_scr[...], 0, 1)
    for b in range(bsz):
        o_ref[b] = (hs[b] * _gelu_tanh(g_refs[b][...])).astype(BF16)

    @pl.when(step == pl.num_programs(0) - 1)
    def _():
        hl_ref[...] = h_scr[...]
        nb_ref[...] = jnp.swapaxes(xcat_scr[pl.ds(rows, hist), :].reshape(CONV_W - 1, bsz, D_RNN), 0, 1)


def rglru_rows_path(p, row0, b, t, buf0, h0, conv_w, conv_b, wa4, ba, wx4, bx, lam):
    assert b == SUBLANES and t >= CONV_W - 1
    tc = _pick(t, (32, 16))
    assert t % tc == 0 and row0 % tc == 0
    d = D_RNN
    hist = CONV_W - 1
    nblk, wdt = wa4.shape[0], wa4.shape[1]
    blk0, per_batch = row0 // tc, t // tc
    col_xr, col_g = 3, 4
    row_map = lambda bi, col: (lambda s: (blk0 + bi * per_batch + s, col))
    const2 = lambda s: (0, 0)
    const3 = lambda s: (0, 0, 0)
    out, h_last, new_buf = pl.pallas_call(
        functools.partial(_rglru_rows_kernel, b, tc),
        grid=(per_batch,),
        in_specs=[pl.BlockSpec((tc, d), row_map(bi, col_xr)) for bi in range(b)]
        + [pl.BlockSpec((tc, d), row_map(bi, col_g)) for bi in range(b)]
        + [
            pl.BlockSpec((b, hist, d), const3),
            pl.BlockSpec((b, d), const2),
            pl.BlockSpec((CONV_W, d), const2),
            pl.BlockSpec((1, d), const2),
            pl.BlockSpec((nblk, wdt, wdt), const3),
            pl.BlockSpec((1, d), const2),
            pl.BlockSpec((nblk, wdt, wdt), const3),
            pl.BlockSpec((1, d), const2),
            pl.BlockSpec((1, d), const2),
        ],
        out_specs=[
            pl.BlockSpec((b, tc, d), lambda s: (0, s, 0)),
            pl.BlockSpec((b, d), const2),
            pl.BlockSpec((b, hist, d), const3),
        ],
        out_shape=[
            jax.ShapeDtypeStruct((b, t, d), BF16),
            jax.ShapeDtypeStruct((b, d), F32),
            jax.ShapeDtypeStruct((b, hist, d), F32),
        ],
        scratch_shapes=[
            pltpu.VMEM(((CONV_W - 1 + tc) * b, d), F32),
            pltpu.VMEM((tc, b, d), F32),
            pltpu.VMEM((tc, b, d), F32),
            pltpu.VMEM((tc, b, d), F32),
            pltpu.VMEM((b, d), F32),
        ],
        compiler_params=_params("arbitrary"),
        name="rglru_rows",
    )(*([p] * (2 * b)), buf0, h0, conv_w, conv_b.reshape(1, d), wa4, ba.reshape(1, d), wx4, bx.reshape(1, d),
      lam.reshape(1, d))
    return out.reshape(b * t, d), h_last, new_buf


def _rglru_kernel(bsz, tc, xr_ref, g_ref, buf_ref, h0_ref, cw_ref, cb_ref, wa_ref, ba_ref, wx_ref, bx_ref,
                  lam_ref, o_ref, hl_ref, nb_ref, xcat_scr, a_scr, u_scr, hs_scr):
    rows = tc * bsz
    hist = (CONV_W - 1) * bsz
    step = pl.program_id(0)

    @pl.when(step == 0)
    def _():
        xcat_scr[pl.ds(rows, hist), :] = buf_ref[...]
        hs_scr[pl.ds(rows, bsz), :] = h0_ref[...]

    xcat_scr[pl.ds(0, hist), :] = xcat_scr[pl.ds(rows, hist), :]
    hs_scr[pl.ds(0, bsz), :] = hs_scr[pl.ds(rows, bsz), :]
    xcat_scr[pl.ds(hist, rows), :] = xr_ref[...]

    xc = cb_ref[...] + xcat_scr[pl.ds(0, rows), :] * cw_ref[0:1, :]
    for w in range(1, CONV_W):
        xc = xc + xcat_scr[pl.ds(w * bsz, rows), :] * cw_ref[w:w + 1, :]
    a, u = _rglru_gates(xc, wa_ref, ba_ref, wx_ref, bx_ref, lam_ref)
    a_scr[...] = a
    u_scr[...] = u

    def scan_step(t, carry):
        prev = hs_scr[pl.ds(pl.multiple_of(t * bsz, bsz), bsz), :]
        cur = pl.ds(pl.multiple_of(t * bsz, bsz), bsz)
        h = a_scr[cur, :] * prev + u_scr[cur, :]
        hs_scr[pl.ds(pl.multiple_of((t + 1) * bsz, bsz), bsz), :] = h
        return carry

    lax.fori_loop(0, tc, scan_step, 0)
    hs = hs_scr[pl.ds(bsz, rows), :]
    o_ref[...] = (hs * _gelu_tanh(g_ref[...])).astype(BF16)

    @pl.when(step == pl.num_programs(0) - 1)
    def _():
        hl_ref[...] = hs_scr[pl.ds(rows, bsz), :]
        nb_ref[...] = xcat_scr[pl.ds(rows, hist), :]


def rglru_path(xr_tm, g_tm, buf_tm, h0, b, t, conv_w, conv_b, wa4, ba, wx4, bx, lam):
    assert t >= CONV_W - 1
    rows_target = 256
    tc = max(1, min(t, rows_target // b))
    while t % tc:
        tc -= 1
    rows = tc * b
    hist = (CONV_W - 1) * b
    d = D_RNN
    nblk, wdt = wa4.shape[0], wa4.shape[1]
    const2 = lambda s: (0, 0)
    out, h_last, new_buf = pl.pallas_call(
        functools.partial(_rglru_kernel, b, tc),
        grid=(t // tc,),
        in_specs=[
            pl.BlockSpec((rows, d), lambda s: (s, 0)),
            pl.BlockSpec((rows, d), lambda s: (s, 0)),
            pl.BlockSpec((hist, d), const2),
            pl.BlockSpec((b, d), const2),
            pl.BlockSpec((CONV_W, d), const2),
            pl.BlockSpec((1, d), const2),
            pl.BlockSpec((nblk, wdt, wdt), lambda s: (0, 0, 0)),
            pl.BlockSpec((1, d), const2),
            pl.BlockSpec((nblk, wdt, wdt), lambda s: (0, 0, 0)),
            pl.BlockSpec((1, d), const2),
            pl.BlockSpec((1, d), const2),
        ],
        out_specs=[
            pl.BlockSpec((rows, d), lambda s: (s, 0)),
            pl.BlockSpec((b, d), const2),
            pl.BlockSpec((hist, d), const2),
        ],
        out_shape=[
            jax.ShapeDtypeStruct((t * b, d), BF16),
            jax.ShapeDtypeStruct((b, d), F32),
            jax.ShapeDtypeStruct((hist, d), F32),
        ],
        scratch_shapes=[
            pltpu.VMEM((rows + hist, d), F32),
            pltpu.VMEM((rows, d), F32),
            pltpu.VMEM((rows, d), F32),
            pltpu.VMEM((rows + b, d), F32),
        ],
        compiler_params=_params("arbitrary"),
        name="rglru",
    )(xr_tm, g_tm, buf_tm, h0, conv_w, conv_b.reshape(1, d), wa4, ba.reshape(1, d), wx4, bx.reshape(1, d),
      lam.reshape(1, d))
    return out, h_last, new_buf


def _block_diag_tiles(w, tile):
    nb, bs, _ = w.shape
    per = tile // bs
    w = w.reshape(nb // per, per, bs, bs)
    eye = jnp.eye(per, dtype=w.dtype)
    dense = jnp.einsum("gpcd,pq->gpcqd", w, eye).reshape(nb // per, tile, tile)
    return dense.astype(BF16)


def _merge_kernel(blocks_a, nx, *refs):
    x_refs, refs = refs[:nx], refs[nx:]
    ro_refs, rn_refs = refs[0:2], refs[2:4]
    ga_ref, gb_ref, wr_ref, wn_ref, wo_ref, yt_ref = refs[4:]
    ret_out = jnp.dot(_pick_rows(ro_refs, blocks_a), wr_ref[...], preferred_element_type=F32)
    rnn_out = jnp.dot(_pick_rows(rn_refs, blocks_a), wn_ref[...], preferred_element_type=F32)
    merged = jax.nn.sigmoid(ga_ref[...]) * ret_out + jax.nn.sigmoid(gb_ref[...]) * rnn_out
    y = _pick_rows(x_refs, blocks_a) + jnp.dot(merged.astype(BF16), wo_ref[...], preferred_element_type=F32)
    yt_ref[...] = y.T


def merge_proj(x_parts, ret_parts, rnn_parts, p, w_ret_out, w_rnn_out, w_o):
    n, d = p.shape[0], x_parts[0].shape[1]
    tm = _pick(math.gcd(*[r.shape[0] for r in ret_parts]), (512, 256, 128))
    blocks_a, part_maps = _row_part_maps(ret_parts, tm)
    assert len(x_parts) == 1 or x_parts[0].shape[0] == ret_parts[0].shape[0]
    _, x_maps = _row_part_maps(x_parts, tm) if len(x_parts) == 2 else (None, [lambda i: (i, 0)])
    const = lambda i: (0, 0)
    return pl.pallas_call(
        functools.partial(_merge_kernel, blocks_a, len(x_parts)),
        grid=(n // tm,),
        in_specs=[pl.BlockSpec((tm, d), m) for m in x_maps + part_maps + part_maps] + [
            pl.BlockSpec((tm, d), lambda i: (i, 5)),
            pl.BlockSpec((tm, d), lambda i: (i, 6)),
            pl.BlockSpec((d, d), const),
            pl.BlockSpec((d, d), const),
            pl.BlockSpec((d, d), const),
        ],
        out_specs=pl.BlockSpec((d, tm), lambda i: (0, i)),
        out_shape=jax.ShapeDtypeStruct((d, n), F32),
        compiler_params=_params("parallel"),
        name="merge_out_proj",
    )(*x_parts, *ret_parts, *rnn_parts, p, p, w_ret_out, w_rnn_out, w_o)


def _peer_query_kernel(xt_ref, g_ref, wq_ref, xn_ref, q_ref):
    x = xt_ref[...]
    ms = jnp.mean(x * x, axis=0, keepdims=True)
    xn = ((x * lax.rsqrt(ms + EPS)) * g_ref[...]).astype(BF16)
    xn_ref[...] = xn
    q_ref[...] = jnp.dot(wq_ref[...], xn, preferred_element_type=F32)


def peer_query(xt, g, wq_t):
    d, n = xt.shape
    m = wq_t.shape[0]
    tn = _pick(n, (512, 256, 128))
    return pl.pallas_call(
        _peer_query_kernel,
        grid=(n // tn,),
        in_specs=[
            pl.BlockSpec((d, tn), lambda i: (0, i)),
            pl.BlockSpec((d, 1), lambda i: (0, 0)),
            pl.BlockSpec((m, d), lambda i: (0, 0)),
        ],
        out_specs=[pl.BlockSpec((d, tn), lambda i: (0, i)), pl.BlockSpec((m, tn), lambda i: (0, i))],
        out_shape=[jax.ShapeDtypeStruct((d, n), BF16), jax.ShapeDtypeStruct((m, n), F32)],
        compiler_params=_params("parallel"),
        name="peer_query",
    )(xt, g.reshape(d, 1), wq_t)


def _sort_pairs(n):
    def merge(lo, hi, r):
        step = r * 2
        if step < hi - lo:
            yield from merge(lo, hi, step)
            yield from merge(lo + r, hi, step)
            yield from [(i, i + r) for i in range(lo + r, hi - r, step)]
        else:
            yield (lo, lo + r)

    def sort(lo, hi):
        if hi - lo >= 1:
            mid = lo + (hi - lo) // 2
            yield from sort(lo, mid)
            yield from sort(mid + 1, hi)
            yield from merge(lo, hi, 1)

    return list(sort(0, n - 1))


_SORT16 = _sort_pairs(PEER_TOPK)


def _cmpx(vals, i, j):
    a, b = vals[i], vals[j]
    if b is None:
        return
    if a is None:
        vals[i], vals[j] = b, None
        return
    vals[i], vals[j] = jnp.maximum(a, b), jnp.minimum(a, b)


def _sort_desc(vals):
    vals = list(vals)
    for i, j in _SORT16:
        _cmpx(vals, i, j)
    return vals


def _merge_top(a, b):
    k = PEER_TOPK
    a = list(a) + [None] * (k - len(a))
    b = list(b) + [None] * (k - len(b))
    out = []
    for r in range(k):
        x, y = a[r], b[k - 1 - r]
        out.append(y if x is None else (x if y is None else jnp.maximum(x, y)))
    d = k // 2
    while d >= 1:
        for i in range(k):
            if not i & d:
                _cmpx(out, i, i + d)
        d //= 2
    return out


def _top_sorted(ref):
    groups = []
    for g0 in range(0, N_KEYS, PEER_TOPK):
        vals = [ref[pl.ds((g0 + j) * PEER_HEADS, PEER_HEADS), :] for j in range(PEER_TOPK)]
        groups.append(_sort_desc(vals))
    while len(groups) > 1:
        groups = [_merge_top(groups[i], groups[i + 1]) for i in range(0, len(groups), 2)]
    return groups[0]


def _peer_select_kernel(q_ref, k1_ref, k2_ref, c1_ref, e1_ref, r2_ref, e2_ref, s1_scr, s2_scr, r2_scr, e2_scr):
    nh = PEER_HEADS
    half_rows = q_ref.shape[0] // 2
    tn = q_ref.shape[1]
    s1 = jnp.dot(k1_ref[...], q_ref[pl.ds(0, half_rows), :].astype(BF16), preferred_element_type=F32)
    s2 = jnp.dot(k2_ref[...], q_ref[pl.ds(half_rows, half_rows), :].astype(BF16), preferred_element_type=F32)
    for lt in range(tn // LANES):
        s1_scr[lt] = s1[:, lt * LANES:(lt + 1) * LANES]
        s2_scr[lt] = s2[:, lt * LANES:(lt + 1) * LANES]
    for lt in range(tn // LANES):
        lanes = pl.ds(lt * LANES, LANES)
        s1_t, s2_t, r2_t, e2_t = s1_scr.at[lt], s2_scr.at[lt], r2_scr.at[lt], e2_scr.at[lt]
        a = _top_sorted(s1_t)
        b = _top_sorted(s2_t)
        k = PEER_TOPK
        lists = []
        for j in range(1, k + 1):
            col = [a[r - 1] + b[j - 1] for r in range(j, k // j + 1)]
            row = [a[j - 1] + b[s - 1] for s in range(j + 1, k // j + 1)]
            if col:
                lists.append(col)
            if row:
                lists.append(row)
        top = lists[0]
        for other in lists[1:]:
            top = _merge_top(top, other)
        tau = top[k - 1]
        z = jnp.ones_like(tau)
        for r in range(1, k):
            z = z + jnp.exp(top[r] - top[0])
        zinv = 1.0 / z
        inf = jnp.full((nh, LANES), jnp.inf, F32)
        phi = []
        for s in range(1, k + 1):
            p = inf
            for r in range(1, k // s + 1):
                p = jnp.where(a[r - 1] + b[s - 1] >= tau, a[r - 1], p)
            phi.append(p)

        def per_key(kk, carry):
            rows = pl.ds(pl.multiple_of(kk * nh, nh), nh)
            s1k = s1_t[rows, :]
            s2k = s2_t[rows, :]
            cnt = jnp.ones((nh, LANES), F32)
            for s in range(k):
                cnt = jnp.where(s1k >= phi[s], float(s + 2), cnt)
            rank = jnp.full((nh, LANES), float(k + 1), F32)
            for s in range(k - 1, -1, -1):
                rank = jnp.where(s2k >= b[s], float(s + 1), rank)
            c1_ref[rows, lanes] = cnt
            e1_ref[rows, lanes] = jnp.exp(s1k - a[0])
            r2_t[rows, :] = rank
            e2_t[rows, :] = jnp.exp(s2k - b[0]) * zinv
            return carry

        lax.fori_loop(0, N_KEYS, per_key, 0, unroll=2)
        pack = 2 * SUBLANES
        for h in range(nh):
            for kt in range(N_KEYS // pack):
                lo = pl.ds(kt * pack * nh + h, SUBLANES, stride=nh)
                hi = pl.ds((kt * pack + SUBLANES) * nh + h, SUBLANES, stride=nh)
                dst = pl.ds(h * N_KEYS + kt * pack, pack)
                r2_ref[dst, lanes] = jnp.concatenate([r2_t[lo, :], r2_t[hi, :]], axis=0).astype(BF16)
                e2_ref[dst, lanes] = jnp.concatenate([e2_t[lo, :], e2_t[hi, :]], axis=0).astype(BF16)


def peer_select(q_t, k1, k2):
    m, n = q_t.shape
    rows = PEER_HEADS * N_KEYS
    tn = _pick(n, (256, 128))
    tok = lambda i: (0, i)
    const = lambda i: (0, 0)
    return pl.pallas_call(
        _peer_select_kernel,
        grid=(n // tn,),
        in_specs=[pl.BlockSpec((m, tn), tok), pl.BlockSpec(k1.shape, const), pl.BlockSpec(k2.shape, const)],
        out_specs=[pl.BlockSpec((rows, tn), tok)] * 4,
        out_shape=[jax.ShapeDtypeStruct((rows, n), F32), jax.ShapeDtypeStruct((rows, n), F32),
                   jax.ShapeDtypeStruct((rows, n), BF16), jax.ShapeDtypeStruct((rows, n), BF16)],
        scratch_shapes=[pltpu.VMEM((tn // LANES, rows, LANES), F32)] * 4,
        compiler_params=_params("parallel"),
        name="peer_select",
    )(q_t, k1, k2)


PEER_EXPERT_BLOCK = 1024


def _mxu_dot(lhs, rhs):
    return lax.dot_general(lhs, rhs, (((1,), (0,)), ((), ())), preferred_element_type=F32)


def _peer_dense_kernel(xt_ref, xn_ref, u_ref, vt_ref, c1_ref, e1_ref, r2_ref, e2_ref, y_ref,
                       acc_scr, act_scr):
    j = pl.program_id(1)
    nh = PEER_HEADS
    tn = xn_ref.shape[1]
    pack = 2 * SUBLANES
    i1_per_blk = u_ref.shape[0] // N_KEYS

    @pl.when(j == 0)
    def _():
        acc_scr[...] = jnp.zeros_like(acc_scr)

    wide = 2 * LANES

    def gate_block(il, nc):
        erows = pl.ds(il * N_KEYS, N_KEYS)
        hrows = pl.ds(il * nh, nh)
        hblk = _mxu_dot(u_ref[erows, :], xn_ref[:, pl.ds(nc * wide, wide)])
        for lw in range(wide // LANES):
            lc = nc * (wide // LANES) + lw
            lanes = pl.ds(lc * LANES, LANES)
            c1b = [jnp.broadcast_to(c1_ref[pl.ds(il * nh + h, 1), lanes], (pack, LANES)).astype(BF16)
                   for h in range(nh)]
            e1b = [jnp.broadcast_to(e1_ref[pl.ds(il * nh + h, 1), lanes], (pack, LANES)).astype(BF16)
                   for h in range(nh)]
            zero = jnp.zeros((pack, LANES), BF16)
            half = jnp.full((), 0.5, BF16)
            gelu_c0 = jnp.full((), GELU_C0, BF16)
            gelu_c1 = jnp.full((), GELU_C1, BF16)
            for it in range(N_KEYS // pack):
                rows = pl.ds(il * N_KEYS + it * pack, pack)
                gate = None
                for h in range(nh):
                    krows = pl.ds(h * N_KEYS + it * pack, pack)
                    sel = jnp.minimum(jnp.maximum(c1b[h] - r2_ref[krows, lanes], zero), e2_ref[krows, lanes])
                    term = sel * e1b[h]
                    gate = term if gate is None else gate + term
                x = hblk[it * pack:(it + 1) * pack, lw * LANES:(lw + 1) * LANES].astype(BF16)
                t = jnp.tanh(x * (gelu_c0 + gelu_c1 * (x * x)))
                act_scr[rows, lanes] = (x * (half + half * t)) * gate

    for nc in range(tn // wide):
        for il in range(i1_per_blk):
            gate_block(il, nc)
        cols = pl.ds(nc * wide, wide)
        acc_scr[:, cols] += _mxu_dot(vt_ref[...], act_scr[:, cols])

    @pl.when(j == pl.num_programs(1) - 1)
    def _():
        y_ref[...] = (xt_ref[...] + acc_scr[...]).T


def peer_dense(xt, xn_t, u_bf16, vt_bf16, c1, e1, r2, e2):
    d, n = xt.shape
    ne = u_bf16.shape[0]
    tn = _pick(n, (512, 256, 128))
    te = PEER_EXPERT_BLOCK
    i1_per_blk = te // N_KEYS
    tok = lambda i, j: (0, i)
    return pl.pallas_call(
        _peer_dense_kernel,
        grid=(n // tn, ne // te),
        in_specs=[
            pl.BlockSpec((d, tn), tok),
            pl.BlockSpec((d, tn), tok),
            pl.BlockSpec((te, d), lambda i, j: (j, 0)),
            pl.BlockSpec((d, te), lambda i, j: (0, j)),
            pl.BlockSpec((i1_per_blk * PEER_HEADS, tn), lambda i, j: (j, i)),
            pl.BlockSpec((i1_per_blk * PEER_HEADS, tn), lambda i, j: (j, i)),
            pl.BlockSpec((PEER_HEADS * N_KEYS, tn), tok),
            pl.BlockSpec((PEER_HEADS * N_KEYS, tn), tok),
        ],
        out_specs=pl.BlockSpec((tn, d), lambda i, j: (i, 0)),
        out_shape=jax.ShapeDtypeStruct((n, d), F32),
        scratch_shapes=[pltpu.VMEM((d, tn), F32), pltpu.VMEM((te, tn), BF16)],
        compiler_params=_params("parallel", "arbitrary"),
        name="peer_dense",
    )(xt, xn_t, u_bf16, vt_bf16, c1, e1, r2, e2)


def _interleaved_keys(keys_p):
    nh, nk, half = keys_p.shape
    eye = jnp.eye(nh, dtype=keys_p.dtype)
    return jnp.einsum("hkd,hg->khgd", keys_p, eye).reshape(nk * nh, nh * half).astype(BF16)


def peer_layer(xt, norm_g, wq, keys, u_tab, v_tab):
    d, n = xt.shape
    wq_t = wq.reshape(d, PEER_HEADS, 2, PEER_HALF).transpose(2, 1, 3, 0).reshape(2 * PEER_HEADS * PEER_HALF, d)
    xn_t, q_t = peer_query(xt, norm_g, wq_t.astype(BF16))
    k1 = _interleaved_keys(keys[:, 0])
    k2 = _interleaved_keys(keys[:, 1])
    c1, e1, r2, e2 = peer_select(q_t, k1, k2)
    return peer_dense(xt, xn_t, u_tab, v_tab.T, c1, e1, r2, e2)


def _rmsnorm_kernel(x_ref, g_ref, y_ref):
    x = x_ref[...]
    ms = jnp.mean(x * x, axis=-1, keepdims=True)
    y_ref[...] = (x * lax.rsqrt(ms + EPS)) * g_ref[...]


def final_norm(x, row0, rows, g):
    d = x.shape[1]
    tm = _pick(math.gcd(rows, row0) if row0 else rows, (512, 256, 128, 64, 32, 16, 8))
    blk0 = row0 // tm
    return pl.pallas_call(
        _rmsnorm_kernel,
        grid=(rows // tm,),
        in_specs=[pl.BlockSpec((tm, d), lambda i: (blk0 + i, 0)), pl.BlockSpec((1, d), lambda i: (0, 0))],
        out_specs=pl.BlockSpec((tm, d), lambda i: (i, 0)),
        out_shape=jax.ShapeDtypeStruct((rows, d), F32),
        compiler_params=_params("parallel"),
        name="final_norm",
    )(x, g.reshape(1, d))


def _to_time_major(p, row0, b, t, col):
    blk = lax.slice(p, (row0, col * D_RNN), (row0 + b * t, (col + 1) * D_RNN))
    return blk.reshape(b, t, D_RNN).transpose(1, 0, 2).reshape(t * b, D_RNN)


def _trunk(groups, norm1_g, norm2_g, normf_g, w_in, ret_gn_g, w_ret_out, conv_w, conv_b, rg_wa, rg_ba, rg_wx,
           rg_bx, rg_lambda, w_rnn_out, w_o, peer_wq, peer_keys, peer_u, peer_v):
    shapes = [(g[0].shape[0], g[0].shape[1]) for g in groups]
    x_parts = [g[0].reshape(-1, D_MODEL) for g in groups]
    row0s = np.cumsum([0] + [b * t for b, t in shapes]).tolist()
    states = [([], [], []) for _ in groups]
    for l in range(DEPTH):
        p = norm_matmul(x_parts, norm1_g[l], w_in[l].astype(BF16))
        wa4 = _block_diag_tiles(rg_wa[l], 256)
        wx4 = _block_diag_tiles(rg_wx[l], 256)
        ret_parts, rnn_parts = [], []
        for gi, (xg, r0, h0, buf0, pos0) in enumerate(groups):
            b, t = shapes[gi]
            row0 = row0s[gi]
            o, r_new = retention_path(p, row0, b, t, pos0, r0[l], ret_gn_g[l])
            rg_args = (conv_w[l], conv_b[l], wa4, rg_ba[l], wx4, rg_bx[l], rg_lambda[l])
            if b == SUBLANES and t % 16 == 0:
                hg, h_last, nb = rglru_rows_path(p, row0, b, t, buf0[l], h0[l], *rg_args)
            else:
                xr_tm = _to_time_major(p, row0, b, t, 3)
                g_tm = _to_time_major(p, row0, b, t, 4)
                buf_tm = buf0[l].transpose(1, 0, 2).reshape((CONV_W - 1) * b, D_RNN)
                hg, h_last, nb = rglru_path(xr_tm, g_tm, buf_tm, h0[l], b, t, *rg_args)
                hg = hg.reshape(t, b, D_RNN).transpose(1, 0, 2).reshape(b * t, D_RNN)
                nb = nb.reshape(CONV_W - 1, b, D_RNN).transpose(1, 0, 2)
            ret_parts.append(o)
            rnn_parts.append(hg)
            states[gi][0].append(r_new)
            states[gi][1].append(h_last)
            states[gi][2].append(nb)
        xt = merge_proj(x_parts, ret_parts, rnn_parts, p, w_ret_out[l].astype(BF16), w_rnn_out[l].astype(BF16),
                        w_o[l].astype(BF16))
        x = peer_layer(xt, norm2_g[l], peer_wq[l], peer_keys[l], peer_u[l], peer_v[l])
        x_parts = [x]
    outs = []
    for gi, (b, t) in enumerate(shapes):
        y = final_norm(x, row0s[gi], b * t, normf_g).reshape(b, t, D_MODEL)
        outs.append((y, jnp.stack(states[gi][0]), jnp.stack(states[gi][1]), jnp.stack(states[gi][2])))
    return outs


def kernel(x_prompt, x_sample, state_ret, state_rnn, state_conv, norm1_g, norm2_g, normf_g, w_in, ret_gn_g,
           w_ret_out, conv_w, conv_b, rg_wa, rg_ba, rg_wx, rg_bx, rg_lambda, w_rnn_out, w_o, peer_wq, peer_keys,
           peer_u, peer_v):
    bp = x_prompt.shape[0]
    dt = x_prompt.dtype
    zr = jnp.zeros((DEPTH, bp, RET_HEADS, RET_DK, RET_DV), dt)
    zh = jnp.zeros((DEPTH, bp, D_RNN), dt)
    zc = jnp.zeros((DEPTH, bp, CONV_W - 1, D_RNN), dt)
    groups = [(x_prompt, zr, zh, zc, 0.0), (x_sample, state_ret, state_rnn, state_conv, float(PAST_LEN))]
    (yp, rp, hp, cp), (ys, rs, hs, cs) = _trunk(
        groups, norm1_g, norm2_g, normf_g, w_in, ret_gn_g, w_ret_out, conv_w, conv_b, rg_wa, rg_ba, rg_wx, rg_bx,
        rg_lambda, w_rnn_out, w_o, peer_wq, peer_keys, peer_u, peer_v)
    return (yp, ys, rp, hp, cp, rs, hs, cs)
```

```python
import functools
import math

import jax
import jax.numpy as jnp
import numpy as np
from jax import lax
from jax.experimental import pallas as pl
from jax.experimental.pallas import tpu as pltpu

D_MODEL = 1024
DEPTH = 2
PAST_LEN = 16384
RET_HEADS = 8
RET_DK = 64
RET_DV = 128
RET_QK = RET_HEADS * RET_DK
RET_V = RET_HEADS * RET_DV
RET_CHUNK = 128
ROPE_BASE = 10000.0
D_RNN = 1024
CONV_W = 4
RG_C = 8.0
COL_V, COL_GRET, COL_XR, COL_GRNN, COL_GA, COL_GB = 1, 2, 3, 4, 5, 6
PEER_HEADS = 8
N_KEYS = 128
PEER_DKEY = 256
PEER_HALF = PEER_DKEY // 2
PEER_TOPK = 16
EPS = 1e-6

SUBLANES = 8
LANES = 128
VMEM_LIMIT = 56 * 1024 * 1024

F32 = jnp.float32
BF16 = jnp.bfloat16


def _params(*sem):
    return pltpu.CompilerParams(dimension_semantics=sem, vmem_limit_bytes=VMEM_LIMIT)


def _pick(n, prefs):
    for p in prefs:
        if n % p == 0:
            return p
    return n


def _row_part_maps(parts, tm):
    assert len(parts) in (1, 2) and all(p.shape[0] % tm == 0 for p in parts)
    blocks_a = parts[0].shape[0] // tm
    maps = [lambda i, *_: (jnp.minimum(i, blocks_a - 1), 0)]
    if len(parts) == 2:
        maps.append(lambda i, *_: (jnp.maximum(i - blocks_a, 0), 0))
    return blocks_a, maps


def _pick_rows(refs, blocks_a):
    x = refs[0][...]
    if len(refs) == 2:
        x = jnp.where(pl.program_id(0) < blocks_a, x, refs[1][...])
    return x


def _norm_matmul_kernel(blocks_a, nparts, *refs):
    x_refs = refs[:nparts]
    g_ref, w_ref, y_ref, xn_scr = refs[nparts:]

    @pl.when(pl.program_id(1) == 0)
    def _():
        x = _pick_rows(x_refs, blocks_a)
        ms = jnp.mean(x * x, axis=-1, keepdims=True)
        xn_scr[...] = ((x * lax.rsqrt(ms + EPS)) * g_ref[...]).astype(BF16)

    y_ref[...] = jnp.dot(xn_scr[...], w_ref[...], preferred_element_type=F32)


def norm_matmul(x_parts, g, w_bf16):
    n = sum(p.shape[0] for p in x_parts)
    d = x_parts[0].shape[1]
    m = w_bf16.shape[1]
    tm = _pick(math.gcd(*[p.shape[0] for p in x_parts]), (1024, 512, 256, 128))
    tn = _pick(m, (1792, 1024, 512, 256, 128))
    blocks_a, row_maps = _row_part_maps(x_parts, tm)
    return pl.pallas_call(
        functools.partial(_norm_matmul_kernel, blocks_a, len(x_parts)),
        grid=(n // tm, m // tn),
        in_specs=[pl.BlockSpec((tm, d), rm) for rm in row_maps] + [
            pl.BlockSpec((1, d), lambda i, j: (0, 0)),
            pl.BlockSpec((d, tn), lambda i, j: (0, j)),
        ],
        out_specs=pl.BlockSpec((tm, tn), lambda i, j: (i, j)),
        out_shape=jax.ShapeDtypeStruct((n, m), F32),
        scratch_shapes=[pltpu.VMEM((tm, d), BF16)],
        compiler_params=_params("parallel", "arbitrary"),
        name="norm_in_proj",
    )(*x_parts, g.reshape(1, d), w_bf16)


def _rot_half(x):
    n = x.shape[-1]
    half = RET_DK // 2
    fwd = pltpu.roll(x, half, axis=1)
    bwd = pltpu.roll(x, n - half, axis=1)
    lane = lax.broadcasted_iota(jnp.int32, x.shape, 1)
    return jnp.where((lane % RET_DK) < half, bwd, fwd)


def _retention_kernel(q_ref, k_ref, v_ref, g_ref, r0_ref, cos_ref, sin_ref, mask_ref, qw_ref, kw_ref,
                      gc_ref, gn_ref, o_ref, r_out_ref, r_scr):
    c = pl.program_id(1)
    rows_c = mask_ref.shape[1]
    nb = q_ref.shape[0] // rows_c

    @pl.when(c == 0)
    def _():
        r_scr[...] = r0_ref[...]

    cos = cos_ref[...]
    sin = sin_ref[...]
    q = q_ref[...]
    k = k_ref[...]
    qr = q * cos + _rot_half(q) * sin
    kr = (k * cos + _rot_half(k) * sin) * (RET_DK ** -0.5)
    qd = (qr * qw_ref[...]).astype(BF16)
    kd = (kr * kw_ref[...]).astype(BF16)
    qb = qr.astype(BF16)
    kb = kr.astype(BF16)
    v = v_ref[...].astype(BF16)
    g = g_ref[...]
    for bb in range(nb):
        rs = slice(bb * rows_c, (bb + 1) * rows_c)
        for h in range(RET_HEADS):
            ks = slice(h * RET_DK, (h + 1) * RET_DK)
            vs = slice(h * RET_DV, (h + 1) * RET_DV)
            vh = v[rs, vs]
            s = lax.dot_general(qb[rs, ks], kb[rs, ks], (((1,), (1,)), ((), ())),
                                preferred_element_type=F32) * mask_ref[h]
            r_h = r_scr[bb, h]
            o = jnp.dot(s.astype(BF16), vh, preferred_element_type=F32)
            o = o + jnp.dot(qd[rs, ks], r_h.astype(BF16), preferred_element_type=F32)
            kv = lax.dot_general(kd[rs, ks], vh, (((0,), (0,)), ((), ())), preferred_element_type=F32)
            r_scr[bb, h] = gc_ref[h] * r_h + kv
            mu = jnp.mean(o, axis=-1, keepdims=True)
            var = jnp.mean(jnp.square(o - mu), axis=-1, keepdims=True)
            on = ((o - mu) * lax.rsqrt(var + EPS)) * gn_ref[:, vs]
            gh = g[rs, vs]
            o_ref[rs, vs] = ((gh * jax.nn.sigmoid(gh)) * on).astype(BF16)

    @pl.when(c == pl.num_programs(1) - 1)
    def _():
        r_out_ref[...] = r_scr[...]


def retention_path(p, row0, b, t, pos0, r0, gn_g):
    c = RET_CHUNK if t % RET_CHUNK == 0 else t
    nc = t // c
    blk0 = row0 // c
    assert row0 % c == 0
    log_g = jnp.log1p(-(2.0 ** (-5.0 - jnp.arange(RET_HEADS, dtype=F32))))
    idx = jnp.arange(c, dtype=F32)
    diff = idx[:, None] - idx[None, :]
    mask = jnp.where(diff[None] >= 0, jnp.exp(jnp.maximum(diff, 0.0)[None] * log_g[:, None, None]), 0.0)
    k_w = jnp.exp((c - 1 - idx)[:, None] * log_g[None, :])
    q_w = jnp.exp((idx + 1.0)[:, None] * log_g[None, :])
    g_c = jnp.exp(c * log_g)
    qw_tab = jnp.repeat(q_w, RET_DK, axis=1)
    kw_tab = jnp.repeat(k_w, RET_DK, axis=1)
    gc_tab = jnp.broadcast_to(g_c[:, None, None], (RET_HEADS, 1, RET_DV))
    half = RET_DK // 2
    pos = pos0 + jnp.arange(t, dtype=F32)
    freq = ROPE_BASE ** (-jnp.arange(half, dtype=F32) / half)
    ang = pos[:, None] * freq[None, :]
    cos_h = jnp.concatenate([jnp.cos(ang), jnp.cos(ang)], axis=1)
    sin_h = jnp.concatenate([-jnp.sin(ang), jnp.sin(ang)], axis=1)
    cos_tab = jnp.tile(cos_h, (1, RET_HEADS))
    sin_tab = jnp.tile(sin_h, (1, RET_HEADS))
    nb = _pick(b, (8, 4, 2, 1)) if nc == 1 and c * 8 <= RET_CHUNK else 1
    if nb > 1:
        cos_tab, sin_tab, qw_tab, kw_tab = [jnp.tile(tab, (nb, 1)) for tab in (cos_tab, sin_tab, qw_tab, kw_tab)]
    rb = nb * c
    blk0 = row0 // rb
    assert row0 % rb == 0

    rows = lambda bi, ci: blk0 + bi * nc + ci
    state_spec = pl.BlockSpec((nb, RET_HEADS, RET_DK, RET_DV), lambda bi, ci: (bi, 0, 0, 0))
    o, r_new = pl.pallas_call(
        _retention_kernel,
        grid=(b // nb, nc),
        in_specs=[
            pl.BlockSpec((rb, RET_QK), lambda bi, ci: (rows(bi, ci), 0)),
            pl.BlockSpec((rb, RET_QK), lambda bi, ci: (rows(bi, ci), 1)),
            pl.BlockSpec((rb, RET_V), lambda bi, ci: (rows(bi, ci), COL_V)),
            pl.BlockSpec((rb, RET_V), lambda bi, ci: (rows(bi, ci), COL_GRET)),
            state_spec,
            pl.BlockSpec((rb, RET_QK), lambda bi, ci: (ci, 0)),
            pl.BlockSpec((rb, RET_QK), lambda bi, ci: (ci, 0)),
            pl.BlockSpec((RET_HEADS, c, c), lambda bi, ci: (0, 0, 0)),
            pl.BlockSpec((rb, RET_QK), lambda bi, ci: (0, 0)),
            pl.BlockSpec((rb, RET_QK), lambda bi, ci: (0, 0)),
            pl.BlockSpec((RET_HEADS, 1, RET_DV), lambda bi, ci: (0, 0, 0)),
            pl.BlockSpec((1, RET_V), lambda bi, ci: (0, 0)),
        ],
        out_specs=[
            pl.BlockSpec((rb, RET_V), lambda bi, ci: (bi * nc + ci, 0)),
            state_spec,
        ],
        out_shape=[
            jax.ShapeDtypeStruct((b * t, RET_V), BF16),
            jax.ShapeDtypeStruct((b, RET_HEADS, RET_DK, RET_DV), F32),
        ],
        scratch_shapes=[pltpu.VMEM((nb, RET_HEADS, RET_DK, RET_DV), F32)],
        compiler_params=_params("parallel", "arbitrary"),
        name="retention",
    )(p, p, p, p, r0, cos_tab, sin_tab, mask, qw_tab, kw_tab, gc_tab, gn_g.reshape(1, RET_V))
    return o, r_new


GELU_C0 = math.sqrt(2.0 / math.pi)
GELU_C1 = GELU_C0 * 0.044715


def _gelu_tanh(x):
    return x * (0.5 * (1.0 + jnp.tanh(math.sqrt(2.0 / math.pi) * (x + 0.044715 * (x * x * x)))))


def _rglru_gates(xc, wa_ref, ba_ref, wx_ref, bx_ref, lam_ref):
    xcb = xc.astype(BF16)
    nblk = wa_ref.shape[0]
    wdt = wa_ref.shape[1]
    ra = jnp.concatenate(
        [jnp.dot(xcb[:, j * wdt:(j + 1) * wdt], wa_ref[j], preferred_element_type=F32) for j in range(nblk)],
        axis=1)
    ri = jnp.concatenate(
        [jnp.dot(xcb[:, j * wdt:(j + 1) * wdt], wx_ref[j], preferred_element_type=F32) for j in range(nblk)],
        axis=1)
    r = jax.nn.sigmoid(ra + ba_ref[...])
    i = jax.nn.sigmoid(ri + bx_ref[...])
    z = -lam_ref[...]
    softplus = jnp.maximum(z, 0.0) + jnp.log1p(jnp.exp(-jnp.abs(z)))
    log_a = (-RG_C * r) * softplus
    a = jnp.exp(log_a)
    one_m_a2 = -jnp.tanh(log_a) * (a * a + 1.0)
    return a, jnp.sqrt(one_m_a2) * (i * xc)


def _rglru_rows_kernel(bsz, tc, *refs):
    xr_refs, g_refs = refs[:bsz], refs[bsz:2 * bsz]
    (buf_ref, h0_ref, cw_ref, cb_ref, wa_ref, ba_ref, wx_ref, bx_ref, lam_ref, o_ref, hl_ref, nb_ref,
     xcat_scr, a_scr, u_scr, hs_scr, h_scr) = refs[2 * bsz:]
    step = pl.program_id(0)
    rows = tc * bsz
    hist = (CONV_W - 1) * bsz

    def time_major(x):
        return jnp.swapaxes(x, 0, 1).reshape(x.shape[0] * x.shape[1], D_RNN)

    @pl.when(step == 0)
    def _():
        xcat_scr[pl.ds(rows, hist), :] = time_major(buf_ref[...])
        h_scr[...] = h0_ref[...]

    xcat_scr[pl.ds(0, hist), :] = xcat_scr[pl.ds(rows, hist), :]
    xcat_scr[pl.ds(hist, rows), :] = time_major(jnp.stack([xr_refs[b][...] for b in range(bsz)]))
    xc = cb_ref[...] + xcat_scr[pl.ds(0, rows), :] * cw_ref[0:1, :]
    for w in range(1, CONV_W):
        xc = xc + xcat_scr[pl.ds(w * bsz, rows), :] * cw_ref[w:w + 1, :]
    a, u = _rglru_gates(xc, wa_ref, ba_ref, wx_ref, bx_ref, lam_ref)
    a_scr[...] = a.reshape(tc, bsz, D_RNN)
    u_scr[...] = u.reshape(tc, bsz, D_RNN)

    def scan_step(t, h):
        h = a_scr[t] * h + u_scr[t]
        hs_scr[t] = h
        return h

    h_scr[...] = lax.fori_loop(0, tc, scan_step, h_scr[...])
    hs = jnp.swapaxes(hs_scr[...], 0, 1)
    for b in range(bsz):
        o_ref[b] = (hs[b] * _gelu_tanh(g_refs[b][...])).astype(BF16)

    @pl.when(step == pl.num_programs(0) - 1)
    def _():
        hl_ref[...] = h_scr[...]
        nb_ref[...] = jnp.swapaxes(xcat_scr[pl.ds(rows, hist), :].reshape(CONV_W - 1, bsz, D_RNN), 0, 1)


def rglru_rows_path(p, row0, b, t, buf0, h0, conv_w, conv_b, wa4, ba, wx4, bx, lam):
    assert b == SUBLANES and t >= CONV_W - 1
    tc = _pick(t, (32, 16))
    assert t % tc == 0 and row0 % tc == 0
    d = D_RNN
    hist = CONV_W - 1
    nblk, wdt = wa4.shape[0], wa4.shape[1]
    blk0, per_batch = row0 // tc, t // tc
    col_xr, col_g = COL_XR, COL_GRNN
    row_map = lambda bi, col: (lambda s: (blk0 + bi * per_batch + s, col))
    const2 = lambda s: (0, 0)
    const3 = lambda s: (0, 0, 0)
    out, h_last, new_buf = pl.pallas_call(
        functools.partial(_rglru_rows_kernel, b, tc),
        grid=(per_batch,),
        in_specs=[pl.BlockSpec((tc, d), row_map(bi, col_xr)) for bi in range(b)]
        + [pl.BlockSpec((tc, d), row_map(bi, col_g)) for bi in range(b)]
        + [
            pl.BlockSpec((b, hist, d), const3),
            pl.BlockSpec((b, d), const2),
            pl.BlockSpec((CONV_W, d), const2),
            pl.BlockSpec((1, d), const2),
            pl.BlockSpec((nblk, wdt, wdt), const3),
            pl.BlockSpec((1, d), const2),
            pl.BlockSpec((nblk, wdt, wdt), const3),
            pl.BlockSpec((1, d), const2),
            pl.BlockSpec((1, d), const2),
        ],
        out_specs=[
            pl.BlockSpec((b, tc, d), lambda s: (0, s, 0)),
            pl.BlockSpec((b, d), const2),
            pl.BlockSpec((b, hist, d), const3),
        ],
        out_shape=[
            jax.ShapeDtypeStruct((b, t, d), BF16),
            jax.ShapeDtypeStruct((b, d), F32),
            jax.ShapeDtypeStruct((b, hist, d), F32),
        ],
        scratch_shapes=[
            pltpu.VMEM(((CONV_W - 1 + tc) * b, d), F32),
            pltpu.VMEM((tc, b, d), F32),
            pltpu.VMEM((tc, b, d), F32),
            pltpu.VMEM((tc, b, d), F32),
            pltpu.VMEM((b, d), F32),
        ],
        compiler_params=_params("arbitrary"),
        name="rglru_rows",
    )(*([p] * (2 * b)), buf0, h0, conv_w, conv_b.reshape(1, d), wa4, ba.reshape(1, d), wx4, bx.reshape(1, d),
      lam.reshape(1, d))
    return out.reshape(b * t, d), h_last, new_buf


def _rglru_kernel(bsz, tc, xr_ref, g_ref, buf_ref, h0_ref, cw_ref, cb_ref, wa_ref, ba_ref, wx_ref, bx_ref,
                  lam_ref, o_ref, hl_ref, nb_ref, xcat_scr, a_scr, u_scr, hs_scr):
    rows = tc * bsz
    hist = (CONV_W - 1) * bsz
    step = pl.program_id(0)

    @pl.when(step == 0)
    def _():
        xcat_scr[pl.ds(rows, hist), :] = buf_ref[...]
        hs_scr[pl.ds(rows, bsz), :] = h0_ref[...]

    xcat_scr[pl.ds(0, hist), :] = xcat_scr[pl.ds(rows, hist), :]
    hs_scr[pl.ds(0, bsz), :] = hs_scr[pl.ds(rows, bsz), :]
    xcat_scr[pl.ds(hist, rows), :] = xr_ref[...]

    xc = cb_ref[...] + xcat_scr[pl.ds(0, rows), :] * cw_ref[0:1, :]
    for w in range(1, CONV_W):
        xc = xc + xcat_scr[pl.ds(w * bsz, rows), :] * cw_ref[w:w + 1, :]
    a, u = _rglru_gates(xc, wa_ref, ba_ref, wx_ref, bx_ref, lam_ref)
    a_scr[...] = a
    u_scr[...] = u

    def scan_step(t, carry):
        prev = hs_scr[pl.ds(pl.multiple_of(t * bsz, bsz), bsz), :]
        cur = pl.ds(pl.multiple_of(t * bsz, bsz), bsz)
        h = a_scr[cur, :] * prev + u_scr[cur, :]
        hs_scr[pl.ds(pl.multiple_of((t + 1) * bsz, bsz), bsz), :] = h
        return carry

    lax.fori_loop(0, tc, scan_step, 0)
    hs = hs_scr[pl.ds(bsz, rows), :]
    o_ref[...] = (hs * _gelu_tanh(g_ref[...])).astype(BF16)

    @pl.when(step == pl.num_programs(0) - 1)
    def _():
        hl_ref[...] = hs_scr[pl.ds(rows, bsz), :]
        nb_ref[...] = xcat_scr[pl.ds(rows, hist), :]


def rglru_path(xr_tm, g_tm, buf_tm, h0, b, t, conv_w, conv_b, wa4, ba, wx4, bx, lam):
    assert t >= CONV_W - 1
    rows_target = 256
    tc = max(1, min(t, rows_target // b))
    while t % tc:
        tc -= 1
    rows = tc * b
    hist = (CONV_W - 1) * b
    d = D_RNN
    nblk, wdt = wa4.shape[0], wa4.shape[1]
    const2 = lambda s: (0, 0)
    out, h_last, new_buf = pl.pallas_call(
        functools.partial(_rglru_kernel, b, tc),
        grid=(t // tc,),
        in_specs=[
            pl.BlockSpec((rows, d), lambda s: (s, 0)),
            pl.BlockSpec((rows, d), lambda s: (s, 0)),
            pl.BlockSpec((hist, d), const2),
            pl.BlockSpec((b, d), const2),
            pl.BlockSpec((CONV_W, d), const2),
            pl.BlockSpec((1, d), const2),
            pl.BlockSpec((nblk, wdt, wdt), lambda s: (0, 0, 0)),
            pl.BlockSpec((1, d), const2),
            pl.BlockSpec((nblk, wdt, wdt), lambda s: (0, 0, 0)),
            pl.BlockSpec((1, d), const2),
            pl.BlockSpec((1, d), const2),
        ],
        out_specs=[
            pl.BlockSpec((rows, d), lambda s: (s, 0)),
            pl.BlockSpec((b, d), const2),
            pl.BlockSpec((hist, d), const2),
        ],
        out_shape=[
            jax.ShapeDtypeStruct((t * b, d), BF16),
            jax.ShapeDtypeStruct((b, d), F32),
            jax.ShapeDtypeStruct((hist, d), F32),
        ],
        scratch_shapes=[
            pltpu.VMEM((rows + hist, d), F32),
            pltpu.VMEM((rows, d), F32),
            pltpu.VMEM((rows, d), F32),
            pltpu.VMEM((rows + b, d), F32),
        ],
        compiler_params=_params("arbitrary"),
        name="rglru",
    )(xr_tm, g_tm, buf_tm, h0, conv_w, conv_b.reshape(1, d), wa4, ba.reshape(1, d), wx4, bx.reshape(1, d),
      lam.reshape(1, d))
    return out, h_last, new_buf


def _block_diag_tiles(w, tile):
    nb, bs, _ = w.shape
    per = tile // bs
    w = w.reshape(nb // per, per, bs, bs)
    eye = jnp.eye(per, dtype=w.dtype)
    dense = jnp.einsum("gpcd,pq->gpcqd", w, eye).reshape(nb // per, tile, tile)
    return dense.astype(BF16)


def _merge_kernel(blocks_a, nx, *refs):
    x_refs, refs = refs[:nx], refs[nx:]
    ro_refs, rn_refs = refs[0:2], refs[2:4]
    ga_ref, gb_ref, wr_ref, wn_ref, wo_ref, yt_ref = refs[4:]
    ret_out = jnp.dot(_pick_rows(ro_refs, blocks_a), wr_ref[...], preferred_element_type=F32)
    rnn_out = jnp.dot(_pick_rows(rn_refs, blocks_a), wn_ref[...], preferred_element_type=F32)
    merged = jax.nn.sigmoid(ga_ref[...]) * ret_out + jax.nn.sigmoid(gb_ref[...]) * rnn_out
    y = _pick_rows(x_refs, blocks_a) + jnp.dot(merged.astype(BF16), wo_ref[...], preferred_element_type=F32)
    yt_ref[...] = y.T


def merge_proj(x_parts, ret_parts, rnn_parts, p, w_ret_out, w_rnn_out, w_o):
    n, d = p.shape[0], x_parts[0].shape[1]
    tm = _pick(math.gcd(*[r.shape[0] for r in ret_parts]), (512, 256, 128))
    blocks_a, part_maps = _row_part_maps(ret_parts, tm)
    assert len(x_parts) == 1 or x_parts[0].shape[0] == ret_parts[0].shape[0]
    _, x_maps = _row_part_maps(x_parts, tm) if len(x_parts) == 2 else (None, [lambda i: (i, 0)])
    const = lambda i: (0, 0)
    return pl.pallas_call(
        functools.partial(_merge_kernel, blocks_a, len(x_parts)),
        grid=(n // tm,),
        in_specs=[pl.BlockSpec((tm, d), m) for m in x_maps + part_maps + part_maps] + [
            pl.BlockSpec((tm, d), lambda i: (i, COL_GA)),
            pl.BlockSpec((tm, d), lambda i: (i, COL_GB)),
            pl.BlockSpec((d, d), const),
            pl.BlockSpec((d, d), const),
            pl.BlockSpec((d, d), const),
        ],
        out_specs=pl.BlockSpec((d, tm), lambda i: (0, i)),
        out_shape=jax.ShapeDtypeStruct((d, n), F32),
        compiler_params=_params("parallel"),
        name="merge_out_proj",
    )(*x_parts, *ret_parts, *rnn_parts, p, p, w_ret_out, w_rnn_out, w_o)


def _peer_query_kernel(xt_ref, g_ref, wq_ref, xn_ref, q_ref):
    x = xt_ref[...]
    ms = jnp.mean(x * x, axis=0, keepdims=True)
    xn = ((x * lax.rsqrt(ms + EPS)) * g_ref[...]).astype(BF16)
    xn_ref[...] = xn
    q_ref[...] = jnp.dot(wq_ref[...], xn, preferred_element_type=F32)


def peer_query(xt, g, wq_t):
    d, n = xt.shape
    m = wq_t.shape[0]
    tn = _pick(n, (512, 256, 128))
    return pl.pallas_call(
        _peer_query_kernel,
        grid=(n // tn,),
        in_specs=[
            pl.BlockSpec((d, tn), lambda i: (0, i)),
            pl.BlockSpec((d, 1), lambda i: (0, 0)),
            pl.BlockSpec((m, d), lambda i: (0, 0)),
        ],
        out_specs=[pl.BlockSpec((d, tn), lambda i: (0, i)), pl.BlockSpec((m, tn), lambda i: (0, i))],
        out_shape=[jax.ShapeDtypeStruct((d, n), BF16), jax.ShapeDtypeStruct((m, n), F32)],
        compiler_params=_params("parallel"),
        name="peer_query",
    )(xt, g.reshape(d, 1), wq_t)


def _sort_pairs(n):
    def merge(lo, hi, r):
        step = r * 2
        if step < hi - lo:
            yield from merge(lo, hi, step)
            yield from merge(lo + r, hi, step)
            yield from [(i, i + r) for i in range(lo + r, hi - r, step)]
        else:
            yield (lo, lo + r)

    def sort(lo, hi):
        if hi - lo >= 1:
            mid = lo + (hi - lo) // 2
            yield from sort(lo, mid)
            yield from sort(mid + 1, hi)
            yield from merge(lo, hi, 1)

    return list(sort(0, n - 1))


_SORT16 = _sort_pairs(PEER_TOPK)


def _cmpx(vals, i, j):
    a, b = vals[i], vals[j]
    if b is None:
        return
    if a is None:
        vals[i], vals[j] = b, None
        return
    vals[i], vals[j] = jnp.maximum(a, b), jnp.minimum(a, b)


def _sort_desc(vals):
    vals = list(vals)
    for i, j in _SORT16:
        _cmpx(vals, i, j)
    return vals


def _merge_top(a, b):
    k = PEER_TOPK
    a = list(a) + [None] * (k - len(a))
    b = list(b) + [None] * (k - len(b))
    out = []
    for r in range(k):
        x, y = a[r], b[k - 1 - r]
        out.append(y if x is None else (x if y is None else jnp.maximum(x, y)))
    d = k // 2
    while d >= 1:
        for i in range(k):
            if not i & d:
                _cmpx(out, i, i + d)
        d //= 2
    return out


def _top_sorted(ref):
    groups = []
    for g0 in range(0, N_KEYS, PEER_TOPK):
        vals = [ref[pl.ds((g0 + j) * PEER_HEADS, PEER_HEADS), :] for j in range(PEER_TOPK)]
        groups.append(_sort_desc(vals))
    while len(groups) > 1:
        groups = [_merge_top(groups[i], groups[i + 1]) for i in range(0, len(groups), 2)]
    return groups[0]


def _peer_select_kernel(q_ref, k1_ref, k2_ref, c1_ref, e1_ref, r2_ref, e2_ref, s1_scr, s2_scr, r2_scr, e2_scr):
    nh = PEER_HEADS
    half_rows = q_ref.shape[0] // 2
    tn = q_ref.shape[1]
    s1 = jnp.dot(k1_ref[...], q_ref[pl.ds(0, half_rows), :].astype(BF16), preferred_element_type=F32)
    s2 = jnp.dot(k2_ref[...], q_ref[pl.ds(half_rows, half_rows), :].astype(BF16), preferred_element_type=F32)
    for lt in range(tn // LANES):
        s1_scr[lt] = s1[:, lt * LANES:(lt + 1) * LANES]
        s2_scr[lt] = s2[:, lt * LANES:(lt + 1) * LANES]
    for lt in range(tn // LANES):
        lanes = pl.ds(lt * LANES, LANES)
        s1_t, s2_t, r2_t, e2_t = s1_scr.at[lt], s2_scr.at[lt], r2_scr.at[lt], e2_scr.at[lt]
        a = _top_sorted(s1_t)
        b = _top_sorted(s2_t)
        k = PEER_TOPK
        lists = []
        for j in range(1, k + 1):
            col = [a[r - 1] + b[j - 1] for r in range(j, k // j + 1)]
            row = [a[j - 1] + b[s - 1] for s in range(j + 1, k // j + 1)]
            if col:
                lists.append(col)
            if row:
                lists.append(row)
        top = lists[0]
        for other in lists[1:]:
            top = _merge_top(top, other)
        tau = top[k - 1]
        z = jnp.ones_like(tau)
        for r in range(1, k):
            z = z + jnp.exp(top[r] - top[0])
        zinv = 1.0 / z
        inf = jnp.full((nh, LANES), jnp.inf, F32)
        phi = []
        for s in range(1, k + 1):
            p = inf
            for r in range(1, k // s + 1):
                p = jnp.where(a[r - 1] + b[s - 1] >= tau, a[r - 1], p)
            phi.append(p)

        def per_key(kk, carry):
            rows = pl.ds(pl.multiple_of(kk * nh, nh), nh)
            s1k = s1_t[rows, :]
            s2k = s2_t[rows, :]
            cnt = jnp.ones((nh, LANES), F32)
            for s in range(k):
                cnt = jnp.where(s1k >= phi[s], float(s + 2), cnt)
            rank = jnp.full((nh, LANES), float(k + 1), F32)
            for s in range(k - 1, -1, -1):
                rank = jnp.where(s2k >= b[s], float(s + 1), rank)
            c1_ref[rows, lanes] = cnt
            e1_ref[rows, lanes] = jnp.exp(s1k - a[0])
            r2_t[rows, :] = rank
            e2_t[rows, :] = jnp.exp(s2k - b[0]) * zinv
            return carry

        lax.fori_loop(0, N_KEYS, per_key, 0, unroll=2)
        pack = 2 * SUBLANES
        for h in range(nh):
            for kt in range(N_KEYS // pack):
                lo = pl.ds(kt * pack * nh + h, SUBLANES, stride=nh)
                hi = pl.ds((kt * pack + SUBLANES) * nh + h, SUBLANES, stride=nh)
                dst = pl.ds(h * N_KEYS + kt * pack, pack)
                r2_ref[dst, lanes] = jnp.concatenate([r2_t[lo, :], r2_t[hi, :]], axis=0).astype(BF16)
                e2_ref[dst, lanes] = jnp.concatenate([e2_t[lo, :], e2_t[hi, :]], axis=0).astype(BF16)


def peer_select(q_t, k1, k2):
    m, n = q_t.shape
    rows = PEER_HEADS * N_KEYS
    tn = _pick(n, (256, 128))
    tok = lambda i: (0, i)
    const = lambda i: (0, 0)
    return pl.pallas_call(
        _peer_select_kernel,
        grid=(n // tn,),
        in_specs=[pl.BlockSpec((m, tn), tok), pl.BlockSpec(k1.shape, const), pl.BlockSpec(k2.shape, const)],
        out_specs=[pl.BlockSpec((rows, tn), tok)] * 4,
        out_shape=[jax.ShapeDtypeStruct((rows, n), F32), jax.ShapeDtypeStruct((rows, n), F32),
                   jax.ShapeDtypeStruct((rows, n), BF16), jax.ShapeDtypeStruct((rows, n), BF16)],
        scratch_shapes=[pltpu.VMEM((tn // LANES, rows, LANES), F32)] * 4,
        compiler_params=_params("parallel"),
        name="peer_select",
    )(q_t, k1, k2)


PEER_EXPERT_BLOCK = 1024


def _mxu_dot(lhs, rhs):
    return lax.dot_general(lhs, rhs, (((1,), (0,)), ((), ())), preferred_element_type=F32)


def _peer_dense_kernel(xt_ref, xn_ref, u_ref, vt_ref, c1_ref, e1_ref, r2_ref, e2_ref, y_ref,
                       acc_scr, act_scr):
    j = pl.program_id(1)
    nh = PEER_HEADS
    tn = xn_ref.shape[1]
    pack = 2 * SUBLANES
    i1_per_blk = u_ref.shape[0] // N_KEYS

    @pl.when(j == 0)
    def _():
        acc_scr[...] = jnp.zeros_like(acc_scr)

    wide = 2 * LANES

    def gate_block(il, nc):
        erows = pl.ds(il * N_KEYS, N_KEYS)
        hblk = _mxu_dot(u_ref[erows, :], xn_ref[:, pl.ds(nc * wide, wide)])
        for lw in range(wide // LANES):
            lc = nc * (wide // LANES) + lw
            lanes = pl.ds(lc * LANES, LANES)
            c1b = [jnp.broadcast_to(c1_ref[pl.ds(il * nh + h, 1), lanes], (pack, LANES)).astype(BF16)
                   for h in range(nh)]
            e1b = [jnp.broadcast_to(e1_ref[pl.ds(il * nh + h, 1), lanes], (pack, LANES)).astype(BF16)
                   for h in range(nh)]
            zero = jnp.zeros((pack, LANES), BF16)
            half = jnp.full((), 0.5, BF16)
            gelu_c0 = jnp.full((), GELU_C0, BF16)
            gelu_c1 = jnp.full((), GELU_C1, BF16)
            for it in range(N_KEYS // pack):
                rows = pl.ds(il * N_KEYS + it * pack, pack)
                gate = None
                for h in range(nh):
                    krows = pl.ds(h * N_KEYS + it * pack, pack)
                    sel = jnp.minimum(jnp.maximum(c1b[h] - r2_ref[krows, lanes], zero), e2_ref[krows, lanes])
                    term = sel * e1b[h]
                    gate = term if gate is None else gate + term
                x = hblk[it * pack:(it + 1) * pack, lw * LANES:(lw + 1) * LANES].astype(BF16)
                t = jnp.tanh(x * (gelu_c0 + gelu_c1 * (x * x)))
                act_scr[rows, lanes] = (x * (half + half * t)) * gate

    for nc in range(tn // wide):
        for il in range(i1_per_blk):
            gate_block(il, nc)
        cols = pl.ds(nc * wide, wide)
        acc_scr[:, cols] += _mxu_dot(vt_ref[...], act_scr[:, cols])

    @pl.when(j == pl.num_programs(1) - 1)
    def _():
        y_ref[...] = (xt_ref[...] + acc_scr[...]).T


def peer_dense(xt, xn_t, u_bf16, vt_bf16, c1, e1, r2, e2):
    d, n = xt.shape
    ne = u_bf16.shape[0]
    tn = _pick(n, (512, 256, 128))
    te = PEER_EXPERT_BLOCK
    i1_per_blk = te // N_KEYS
    tok = lambda i, j: (0, i)
    return pl.pallas_call(
        _peer_dense_kernel,
        grid=(n // tn, ne // te),
        in_specs=[
            pl.BlockSpec((d, tn), tok),
            pl.BlockSpec((d, tn), tok),
            pl.BlockSpec((te, d), lambda i, j: (j, 0)),
            pl.BlockSpec((d, te), lambda i, j: (0, j)),
            pl.BlockSpec((i1_per_blk * PEER_HEADS, tn), lambda i, j: (j, i)),
            pl.BlockSpec((i1_per_blk * PEER_HEADS, tn), lambda i, j: (j, i)),
            pl.BlockSpec((PEER_HEADS * N_KEYS, tn), tok),
            pl.BlockSpec((PEER_HEADS * N_KEYS, tn), tok),
        ],
        out_specs=pl.BlockSpec((tn, d), lambda i, j: (i, 0)),
        out_shape=jax.ShapeDtypeStruct((n, d), F32),
        scratch_shapes=[pltpu.VMEM((d, tn), F32), pltpu.VMEM((te, tn), BF16)],
        compiler_params=_params("parallel", "arbitrary"),
        name="peer_dense",
    )(xt, xn_t, u_bf16, vt_bf16, c1, e1, r2, e2)


def _interleaved_keys(keys_p):
    nh, nk, half = keys_p.shape
    eye = jnp.eye(nh, dtype=keys_p.dtype)
    return jnp.einsum("hkd,hg->khgd", keys_p, eye).reshape(nk * nh, nh * half).astype(BF16)


def peer_layer(xt, norm_g, wq, keys, u_tab, v_tab):
    d = xt.shape[0]
    wq_t = wq.reshape(d, PEER_HEADS, 2, PEER_HALF).transpose(2, 1, 3, 0).reshape(2 * PEER_HEADS * PEER_HALF, d)
    xn_t, q_t = peer_query(xt, norm_g, wq_t.astype(BF16))
    k1 = _interleaved_keys(keys[:, 0])
    k2 = _interleaved_keys(keys[:, 1])
    c1, e1, r2, e2 = peer_select(q_t, k1, k2)
    return peer_dense(xt, xn_t, u_tab, v_tab.T.astype(BF16), c1, e1, r2, e2)


def _rmsnorm_kernel(x_ref, g_ref, y_ref):
    x = x_ref[...]
    ms = jnp.mean(x * x, axis=-1, keepdims=True)
    y_ref[...] = (x * lax.rsqrt(ms + EPS)) * g_ref[...]


def final_norm(x, row0, rows, g):
    d = x.shape[1]
    tm = _pick(math.gcd(rows, row0) if row0 else rows, (512, 256, 128, 64, 32, 16, 8))
    blk0 = row0 // tm
    return pl.pallas_call(
        _rmsnorm_kernel,
        grid=(rows // tm,),
        in_specs=[pl.BlockSpec((tm, d), lambda i: (blk0 + i, 0)), pl.BlockSpec((1, d), lambda i: (0, 0))],
        out_specs=pl.BlockSpec((tm, d), lambda i: (i, 0)),
        out_shape=jax.ShapeDtypeStruct((rows, d), F32),
        compiler_params=_params("parallel"),
        name="final_norm",
    )(x, g.reshape(1, d))


def _to_time_major(p, row0, b, t, col):
    blk = lax.slice(p, (row0, col * D_RNN), (row0 + b * t, (col + 1) * D_RNN))
    return blk.reshape(b, t, D_RNN).transpose(1, 0, 2).reshape(t * b, D_RNN)


def _trunk(groups, norm1_g, norm2_g, normf_g, w_in, ret_gn_g, w_ret_out, conv_w, conv_b, rg_wa, rg_ba, rg_wx,
           rg_bx, rg_lambda, w_rnn_out, w_o, peer_wq, peer_keys, peer_u, peer_v):
    shapes = [(g[0].shape[0], g[0].shape[1]) for g in groups]
    x_parts = [g[0].reshape(-1, D_MODEL) for g in groups]
    row0s = np.cumsum([0] + [b * t for b, t in shapes]).tolist()
    states = [([], [], []) for _ in groups]
    for l in range(DEPTH):
        p = norm_matmul(x_parts, norm1_g[l], w_in[l].astype(BF16))
        wa4 = _block_diag_tiles(rg_wa[l], 256)
        wx4 = _block_diag_tiles(rg_wx[l], 256)
        ret_parts, rnn_parts = [], []
        for gi, (xg, r0, h0, buf0, pos0) in enumerate(groups):
            b, t = shapes[gi]
            row0 = row0s[gi]
            o, r_new = retention_path(p, row0, b, t, pos0, r0[l], ret_gn_g[l])
            rg_args = (conv_w[l], conv_b[l], wa4, rg_ba[l], wx4, rg_bx[l], rg_lambda[l])
            if b == SUBLANES and t % 16 == 0:
                hg, h_last, nb = rglru_rows_path(p, row0, b, t, buf0[l], h0[l], *rg_args)
            else:
                xr_tm = _to_time_major(p, row0, b, t, COL_XR)
                g_tm = _to_time_major(p, row0, b, t, COL_GRNN)
                buf_tm = buf0[l].transpose(1, 0, 2).reshape((CONV_W - 1) * b, D_RNN)
                hg, h_last, nb = rglru_path(xr_tm, g_tm, buf_tm, h0[l], b, t, *rg_args)
                hg = hg.reshape(t, b, D_RNN).transpose(1, 0, 2).reshape(b * t, D_RNN)
                nb = nb.reshape(CONV_W - 1, b, D_RNN).transpose(1, 0, 2)
            ret_parts.append(o)
            rnn_parts.append(hg)
            states[gi][0].append(r_new)
            states[gi][1].append(h_last)
            states[gi][2].append(nb)
        xt = merge_proj(x_parts, ret_parts, rnn_parts, p, w_ret_out[l].astype(BF16), w_rnn_out[l].astype(BF16),
                        w_o[l].astype(BF16))
        x = peer_layer(xt, norm2_g[l], peer_wq[l], peer_keys[l], peer_u[l], peer_v[l])
        x_parts = [x]
    outs = []
    for gi, (b, t) in enumerate(shapes):
        y = final_norm(x, row0s[gi], b * t, normf_g).reshape(b, t, D_MODEL)
        outs.append((y, jnp.stack(states[gi][0]), jnp.stack(states[gi][1]), jnp.stack(states[gi][2])))
    return outs


def kernel(x_prompt, x_sample, state_ret, state_rnn, state_conv, norm1_g, norm2_g, normf_g, w_in, ret_gn_g,
           w_ret_out, conv_w, conv_b, rg_wa, rg_ba, rg_wx, rg_bx, rg_lambda, w_rnn_out, w_o, peer_wq, peer_keys,
           peer_u, peer_v):
    bp = x_prompt.shape[0]
    dt = x_prompt.dtype
    zr = jnp.zeros((DEPTH, bp, RET_HEADS, RET_DK, RET_DV), dt)
    zh = jnp.zeros((DEPTH, bp, D_RNN), dt)
    zc = jnp.zeros((DEPTH, bp, CONV_W - 1, D_RNN), dt)
    groups = [(x_prompt, zr, zh, zc, 0.0), (x_sample, state_ret, state_rnn, state_conv, float(PAST_LEN))]
    (yp, rp, hp, cp), (ys, rs, hs, cs) = _trunk(
        groups, norm1_g, norm2_g, normf_g, w_in, ret_gn_g, w_ret_out, conv_w, conv_b, rg_wa, rg_ba, rg_wx, rg_bx,
        rg_lambda, w_rnn_out, w_o, peer_wq, peer_keys, peer_u, peer_v)
    return (yp, ys, rp, hp, cp, rs, hs, cs)
```

```python
import functools
import math

import jax
import jax.numpy as jnp
import numpy as np
from jax import lax
from jax.experimental import pallas as pl
from jax.experimental.pallas import tpu as pltpu

D_MODEL = 1024
DEPTH = 2
PAST_LEN = 16384
RET_HEADS = 8
RET_DK = 64
RET_DV = 128
RET_QK = RET_HEADS * RET_DK
RET_V = RET_HEADS * RET_DV
RET_CHUNK = 128
ROPE_BASE = 10000.0
D_RNN = 1024
CONV_W = 4
RG_C = 8.0
COL_V, COL_GRET, COL_XR, COL_GRNN, COL_GA, COL_GB = 1, 2, 3, 4, 5, 6
PEER_HEADS = 8
N_KEYS = 128
PEER_DKEY = 256
PEER_HALF = PEER_DKEY // 2
PEER_TOPK = 16
EPS = 1e-6

SUBLANES = 8
LANES = 128
VMEM_LIMIT = 56 * 1024 * 1024

F32 = jnp.float32
BF16 = jnp.bfloat16


def _params(*sem):
    return pltpu.CompilerParams(dimension_semantics=sem, vmem_limit_bytes=VMEM_LIMIT)


def _pick(n, prefs):
    for p in prefs:
        if n % p == 0:
            return p
    return n


def _row_part_maps(parts, tm):
    assert len(parts) in (1, 2) and all(p.shape[0] % tm == 0 for p in parts)
    blocks_a = parts[0].shape[0] // tm
    maps = [lambda i, *_: (jnp.minimum(i, blocks_a - 1), 0)]
    if len(parts) == 2:
        maps.append(lambda i, *_: (jnp.maximum(i - blocks_a, 0), 0))
    return blocks_a, maps


def _pick_rows(refs, blocks_a):
    x = refs[0][...]
    if len(refs) == 2:
        x = jnp.where(pl.program_id(0) < blocks_a, x, refs[1][...])
    return x


def _norm_matmul_kernel(blocks_a, nparts, *refs):
    x_refs = refs[:nparts]
    g_ref, w_ref, y_ref, xn_scr = refs[nparts:]

    @pl.when(pl.program_id(1) == 0)
    def _():
        x = _pick_rows(x_refs, blocks_a)
        ms = jnp.mean(x * x, axis=-1, keepdims=True)
        xn_scr[...] = ((x * lax.rsqrt(ms + EPS)) * g_ref[...]).astype(BF16)

    y_ref[...] = jnp.dot(xn_scr[...], w_ref[...], preferred_element_type=F32)


def norm_matmul(x_parts, g, w_bf16):
    n = sum(p.shape[0] for p in x_parts)
    d = x_parts[0].shape[1]
    m = w_bf16.shape[1]
    tm = _pick(math.gcd(*[p.shape[0] for p in x_parts]), (1024, 512, 256, 128))
    tn = _pick(m, (1792, 1024, 512, 256, 128))
    blocks_a, row_maps = _row_part_maps(x_parts, tm)
    return pl.pallas_call(
        functools.partial(_norm_matmul_kernel, blocks_a, len(x_parts)),
        grid=(n // tm, m // tn),
        in_specs=[pl.BlockSpec((tm, d), rm) for rm in row_maps] + [
            pl.BlockSpec((1, d), lambda i, j: (0, 0)),
            pl.BlockSpec((d, tn), lambda i, j: (0, j)),
        ],
        out_specs=pl.BlockSpec((tm, tn), lambda i, j: (i, j)),
        out_shape=jax.ShapeDtypeStruct((n, m), F32),
        scratch_shapes=[pltpu.VMEM((tm, d), BF16)],
        compiler_params=_params("parallel", "arbitrary"),
        name="norm_in_proj",
    )(*x_parts, g.reshape(1, d), w_bf16)


def _rot_half(x):
    n = x.shape[-1]
    half = RET_DK // 2
    fwd = pltpu.roll(x, half, axis=1)
    bwd = pltpu.roll(x, n - half, axis=1)
    lane = lax.broadcasted_iota(jnp.int32, x.shape, 1)
    return jnp.where((lane % RET_DK) < half, bwd, fwd)


def _retention_kernel(q_ref, k_ref, v_ref, g_ref, r0_ref, cos_ref, sin_ref, mask_ref, qw_ref, kw_ref,
                      gc_ref, gn_ref, o_ref, r_out_ref, r_scr):
    c = pl.program_id(1)
    rows_c = mask_ref.shape[1]
    nb = q_ref.shape[0] // rows_c

    @pl.when(c == 0)
    def _():
        r_scr[...] = r0_ref[...]

    cos = cos_ref[...]
    sin = sin_ref[...]
    q = q_ref[...]
    k = k_ref[...]
    qr = q * cos + _rot_half(q) * sin
    kr = (k * cos + _rot_half(k) * sin) * (RET_DK ** -0.5)
    qd = (qr * qw_ref[...]).astype(BF16)
    kd = (kr * kw_ref[...]).astype(BF16)
    qb = qr.astype(BF16)
    kb = kr.astype(BF16)
    v = v_ref[...].astype(BF16)
    g = g_ref[...]
    for bb in range(nb):
        rs = slice(bb * rows_c, (bb + 1) * rows_c)
        for h in range(RET_HEADS):
            ks = slice(h * RET_DK, (h + 1) * RET_DK)
            vs = slice(h * RET_DV, (h + 1) * RET_DV)
            vh = v[rs, vs]
            s = lax.dot_general(qb[rs, ks], kb[rs, ks], (((1,), (1,)), ((), ())),
                                preferred_element_type=F32) * mask_ref[h]
            r_h = r_scr[bb, h]
            o = jnp.dot(s.astype(BF16), vh, preferred_element_type=F32)
            o = o + jnp.dot(qd[rs, ks], r_h.astype(BF16), preferred_element_type=F32)
            kv = lax.dot_general(kd[rs, ks], vh, (((0,), (0,)), ((), ())), preferred_element_type=F32)
            r_scr[bb, h] = gc_ref[h] * r_h + kv
            mu = jnp.mean(o, axis=-1, keepdims=True)
            var = jnp.mean(jnp.square(o - mu), axis=-1, keepdims=True)
            on = ((o - mu) * lax.rsqrt(var + EPS)) * gn_ref[:, vs]
            gh = g[rs, vs]
            o_ref[rs, vs] = ((gh * jax.nn.sigmoid(gh)) * on).astype(BF16)

    @pl.when(c == pl.num_programs(1) - 1)
    def _():
        r_out_ref[...] = r_scr[...]


def retention_path(p, row0, b, t, pos0, r0, gn_g):
    c = RET_CHUNK if t % RET_CHUNK == 0 else t
    nc = t // c
    blk0 = row0 // c
    assert row0 % c == 0
    log_g = jnp.log1p(-(2.0 ** (-5.0 - jnp.arange(RET_HEADS, dtype=F32))))
    idx = jnp.arange(c, dtype=F32)
    diff = idx[:, None] - idx[None, :]
    mask = jnp.where(diff[None] >= 0, jnp.exp(jnp.maximum(diff, 0.0)[None] * log_g[:, None, None]), 0.0)
    k_w = jnp.exp((c - 1 - idx)[:, None] * log_g[None, :])
    q_w = jnp.exp((idx + 1.0)[:, None] * log_g[None, :])
    g_c = jnp.exp(c * log_g)
    qw_tab = jnp.repeat(q_w, RET_DK, axis=1)
    kw_tab = jnp.repeat(k_w, RET_DK, axis=1)
    gc_tab = jnp.broadcast_to(g_c[:, None, None], (RET_HEADS, 1, RET_DV))
    half = RET_DK // 2
    pos = pos0 + jnp.arange(t, dtype=F32)
    freq = ROPE_BASE ** (-jnp.arange(half, dtype=F32) / half)
    ang = pos[:, None] * freq[None, :]
    cos_h = jnp.concatenate([jnp.cos(ang), jnp.cos(ang)], axis=1)
    sin_h = jnp.concatenate([-jnp.sin(ang), jnp.sin(ang)], axis=1)
    cos_tab = jnp.tile(cos_h, (1, RET_HEADS))
    sin_tab = jnp.tile(sin_h, (1, RET_HEADS))
    nb = _pick(b, (8, 4, 2, 1)) if nc == 1 and c * 8 <= RET_CHUNK else 1
    if nb > 1:
        cos_tab, sin_tab, qw_tab, kw_tab = [jnp.tile(tab, (nb, 1)) for tab in (cos_tab, sin_tab, qw_tab, kw_tab)]
    rb = nb * c
    blk0 = row0 // rb
    assert row0 % rb == 0

    rows = lambda bi, ci: blk0 + bi * nc + ci
    state_spec = pl.BlockSpec((nb, RET_HEADS, RET_DK, RET_DV), lambda bi, ci: (bi, 0, 0, 0))
    o, r_new = pl.pallas_call(
        _retention_kernel,
        grid=(b // nb, nc),
        in_specs=[
            pl.BlockSpec((rb, RET_QK), lambda bi, ci: (rows(bi, ci), 0)),
            pl.BlockSpec((rb, RET_QK), lambda bi, ci: (rows(bi, ci), 1)),
            pl.BlockSpec((rb, RET_V), lambda bi, ci: (rows(bi, ci), COL_V)),
            pl.BlockSpec((rb, RET_V), lambda bi, ci: (rows(bi, ci), COL_GRET)),
            state_spec,
            pl.BlockSpec((rb, RET_QK), lambda bi, ci: (ci, 0)),
            pl.BlockSpec((rb, RET_QK), lambda bi, ci: (ci, 0)),
            pl.BlockSpec((RET_HEADS, c, c), lambda bi, ci: (0, 0, 0)),
            pl.BlockSpec((rb, RET_QK), lambda bi, ci: (0, 0)),
            pl.BlockSpec((rb, RET_QK), lambda bi, ci: (0, 0)),
            pl.BlockSpec((RET_HEADS, 1, RET_DV), lambda bi, ci: (0, 0, 0)),
            pl.BlockSpec((1, RET_V), lambda bi, ci: (0, 0)),
        ],
        out_specs=[
            pl.BlockSpec((rb, RET_V), lambda bi, ci: (bi * nc + ci, 0)),
            state_spec,
        ],
        out_shape=[
            jax.ShapeDtypeStruct((b * t, RET_V), BF16),
            jax.ShapeDtypeStruct((b, RET_HEADS, RET_DK, RET_DV), F32),
        ],
        scratch_shapes=[pltpu.VMEM((nb, RET_HEADS, RET_DK, RET_DV), F32)],
        compiler_params=_params("parallel", "arbitrary"),
        name="retention",
    )(p, p, p, p, r0, cos_tab, sin_tab, mask, qw_tab, kw_tab, gc_tab, gn_g.reshape(1, RET_V))
    return o, r_new


GELU_C0 = math.sqrt(2.0 / math.pi)
GELU_C1 = GELU_C0 * 0.044715


def _gelu_tanh(x):
    return x * (0.5 * (1.0 + jnp.tanh(math.sqrt(2.0 / math.pi) * (x + 0.044715 * (x * x * x)))))


def _rglru_gates(xc, wa_ref, ba_ref, wx_ref, bx_ref, lam_ref):
    xcb = xc.astype(BF16)
    nblk = wa_ref.shape[0]
    wdt = wa_ref.shape[1]
    ra = jnp.concatenate(
        [jnp.dot(xcb[:, j * wdt:(j + 1) * wdt], wa_ref[j], preferred_element_type=F32) for j in range(nblk)],
        axis=1)
    ri = jnp.concatenate(
        [jnp.dot(xcb[:, j * wdt:(j + 1) * wdt], wx_ref[j], preferred_element_type=F32) for j in range(nblk)],
        axis=1)
    r = jax.nn.sigmoid(ra + ba_ref[...])
    i = jax.nn.sigmoid(ri + bx_ref[...])
    z = -lam_ref[...]
    softplus = jnp.maximum(z, 0.0) + jnp.log1p(jnp.exp(-jnp.abs(z)))
    log_a = (-RG_C * r) * softplus
    a = jnp.exp(log_a)
    one_m_a2 = -jnp.tanh(log_a) * (a * a + 1.0)
    return a, jnp.sqrt(one_m_a2) * (i * xc)


def _rglru_rows_kernel(bsz, tc, nin, chained, *refs):
    xr_refs, g_refs = refs[:nin], refs[nin:2 * nin]
    (buf_ref, h0_ref, cw_ref, cb_ref, wa_ref, ba_ref, wx_ref, bx_ref, lam_ref, o_ref, hl_ref, nb_ref,
     xcat_scr, a_scr, u_scr, hs_scr, h_scr) = refs[2 * nin:]
    step = pl.program_id(0)
    rows = tc * bsz
    hist = (CONV_W - 1) * bsz
    bpi = bsz // nin

    def time_major(x):
        return jnp.swapaxes(x, 0, 1).reshape(x.shape[0] * x.shape[1], D_RNN)

    def batch_major(blocks):
        return jnp.concatenate([blk[...].reshape(bpi, tc, D_RNN) for blk in blocks], axis=0)

    def load_state():
        xcat_scr[pl.ds(rows, hist), :] = time_major(buf_ref[...])
        h_scr[...] = h0_ref[...]

    if chained:
        pl.when(step == 0)(load_state)
    else:
        load_state()

    xcat_scr[pl.ds(0, hist), :] = xcat_scr[pl.ds(rows, hist), :]
    xcat_scr[pl.ds(hist, rows), :] = time_major(batch_major(xr_refs))
    xc = cb_ref[...] + xcat_scr[pl.ds(0, rows), :] * cw_ref[0:1, :]
    for w in range(1, CONV_W):
        xc = xc + xcat_scr[pl.ds(w * bsz, rows), :] * cw_ref[w:w + 1, :]
    a, u = _rglru_gates(xc, wa_ref, ba_ref, wx_ref, bx_ref, lam_ref)
    a_scr[...] = a.reshape(tc, bsz, D_RNN)
    u_scr[...] = u.reshape(tc, bsz, D_RNN)

    def scan_step(t, h):
        h = a_scr[t] * h + u_scr[t]
        hs_scr[t] = h
        return h

    h_scr[...] = lax.fori_loop(0, tc, scan_step, h_scr[...])
    gate = _gelu_tanh(batch_major(g_refs))
    o_ref[...] = (jnp.swapaxes(hs_scr[...], 0, 1) * gate).astype(BF16)

    def store_state():
        hl_ref[...] = h_scr[...]
        nb_ref[...] = jnp.swapaxes(xcat_scr[pl.ds(rows, hist), :].reshape(CONV_W - 1, bsz, D_RNN), 0, 1)

    if chained:
        pl.when(step == pl.num_programs(0) - 1)(store_state)
    else:
        store_state()


RGLRU_STEP_ROWS = 256


def _rglru_rows_ok(b, t, row0):
    if b % SUBLANES or t < CONV_W - 1:
        return False
    if b == SUBLANES and b * t > RGLRU_STEP_ROWS:
        tc = RGLRU_STEP_ROWS // b
        return t % tc == 0 and row0 % tc == 0
    bg = max(SUBLANES, RGLRU_STEP_ROWS // t)
    return t % SUBLANES == 0 and b % bg == 0 and row0 % (bg * t) == 0


def rglru_rows_path(p, row0, b, t, buf0, h0, conv_w, conv_b, wa4, ba, wx4, bx, lam):
    assert t >= CONV_W - 1 and b % SUBLANES == 0 and RGLRU_STEP_ROWS % SUBLANES == 0
    d = D_RNN
    hist = CONV_W - 1
    nblk, wdt = wa4.shape[0], wa4.shape[1]
    chained = b * t > RGLRU_STEP_ROWS and b == SUBLANES
    if chained:
        bg, tc, nin = b, RGLRU_STEP_ROWS // b, b
        steps = t // tc
        assert t % tc == 0 and row0 % tc == 0
        blk0 = row0 // tc
        row_map = lambda bi, col: (lambda s: (blk0 + bi * steps + s, col))
        state2, state3 = (lambda s: (0, 0)), (lambda s: (0, 0, 0))
        out_map = lambda s: (0, s, 0)
    else:
        bg, tc, nin = max(SUBLANES, RGLRU_STEP_ROWS // t), t, 1
        steps = b // bg
        assert b % bg == 0 and row0 % (bg * t) == 0
        blk0 = row0 // (bg * t)
        row_map = lambda bi, col: (lambda s: (blk0 + s, col))
        state2, state3 = (lambda s: (s, 0)), (lambda s: (s, 0, 0))
        out_map = lambda s: (s, 0, 0)
    in_rows = (bg // nin) * tc
    const2 = lambda s: (0, 0)
    const3 = lambda s: (0, 0, 0)
    out, h_last, new_buf = pl.pallas_call(
        functools.partial(_rglru_rows_kernel, bg, tc, nin, chained),
        grid=(steps,),
        in_specs=[pl.BlockSpec((in_rows, d), row_map(bi, COL_XR)) for bi in range(nin)]
        + [pl.BlockSpec((in_rows, d), row_map(bi, COL_GRNN)) for bi in range(nin)]
        + [
            pl.BlockSpec((bg, hist, d), state3),
            pl.BlockSpec((bg, d), state2),
            pl.BlockSpec((CONV_W, d), const2),
            pl.BlockSpec((1, d), const2),
            pl.BlockSpec((nblk, wdt, wdt), const3),
            pl.BlockSpec((1, d), const2),
            pl.BlockSpec((nblk, wdt, wdt), const3),
            pl.BlockSpec((1, d), const2),
            pl.BlockSpec((1, d), const2),
        ],
        out_specs=[
            pl.BlockSpec((bg, tc, d), out_map),
            pl.BlockSpec((bg, d), state2),
            pl.BlockSpec((bg, hist, d), state3),
        ],
        out_shape=[
            jax.ShapeDtypeStruct((b, t, d), BF16),
            jax.ShapeDtypeStruct((b, d), F32),
            jax.ShapeDtypeStruct((b, hist, d), F32),
        ],
        scratch_shapes=[
            pltpu.VMEM(((CONV_W - 1 + tc) * bg, d), F32),
            pltpu.VMEM((tc, bg, d), F32),
            pltpu.VMEM((tc, bg, d), F32),
            pltpu.VMEM((tc, bg, d), F32),
            pltpu.VMEM((bg, d), F32),
        ],
        compiler_params=_params("arbitrary"),
        name="rglru_rows",
    )(*([p] * (2 * nin)), buf0, h0, conv_w, conv_b.reshape(1, d), wa4, ba.reshape(1, d), wx4, bx.reshape(1, d),
      lam.reshape(1, d))
    return out.reshape(b * t, d), h_last, new_buf


def _rglru_kernel(bsz, tc, xr_ref, g_ref, buf_ref, h0_ref, cw_ref, cb_ref, wa_ref, ba_ref, wx_ref, bx_ref,
                  lam_ref, o_ref, hl_ref, nb_ref, xcat_scr, a_scr, u_scr, hs_scr):
    rows = tc * bsz
    hist = (CONV_W - 1) * bsz
    step = pl.program_id(0)

    @pl.when(step == 0)
    def _():
        xcat_scr[pl.ds(rows, hist), :] = buf_ref[...]
        hs_scr[pl.ds(rows, bsz), :] = h0_ref[...]

    xcat_scr[pl.ds(0, hist), :] = xcat_scr[pl.ds(rows, hist), :]
    hs_scr[pl.ds(0, bsz), :] = hs_scr[pl.ds(rows, bsz), :]
    xcat_scr[pl.ds(hist, rows), :] = xr_ref[...]

    xc = cb_ref[...] + xcat_scr[pl.ds(0, rows), :] * cw_ref[0:1, :]
    for w in range(1, CONV_W):
        xc = xc + xcat_scr[pl.ds(w * bsz, rows), :] * cw_ref[w:w + 1, :]
    a, u = _rglru_gates(xc, wa_ref, ba_ref, wx_ref, bx_ref, lam_ref)
    a_scr[...] = a
    u_scr[...] = u

    def scan_step(t, carry):
        prev = hs_scr[pl.ds(pl.multiple_of(t * bsz, bsz), bsz), :]
        cur = pl.ds(pl.multiple_of(t * bsz, bsz), bsz)
        h = a_scr[cur, :] * prev + u_scr[cur, :]
        hs_scr[pl.ds(pl.multiple_of((t + 1) * bsz, bsz), bsz), :] = h
        return carry

    lax.fori_loop(0, tc, scan_step, 0)
    hs = hs_scr[pl.ds(bsz, rows), :]
    o_ref[...] = (hs * _gelu_tanh(g_ref[...])).astype(BF16)

    @pl.when(step == pl.num_programs(0) - 1)
    def _():
        hl_ref[...] = hs_scr[pl.ds(rows, bsz), :]
        nb_ref[...] = xcat_scr[pl.ds(rows, hist), :]


def rglru_path(xr_tm, g_tm, buf_tm, h0, b, t, conv_w, conv_b, wa4, ba, wx4, bx, lam):
    assert t >= CONV_W - 1
    rows_target = 256
    tc = max(1, min(t, rows_target // b))
    while t % tc:
        tc -= 1
    rows = tc * b
    hist = (CONV_W - 1) * b
    d = D_RNN
    nblk, wdt = wa4.shape[0], wa4.shape[1]
    const2 = lambda s: (0, 0)
    out, h_last, new_buf = pl.pallas_call(
        functools.partial(_rglru_kernel, b, tc),
        grid=(t // tc,),
        in_specs=[
            pl.BlockSpec((rows, d), lambda s: (s, 0)),
            pl.BlockSpec((rows, d), lambda s: (s, 0)),
            pl.BlockSpec((hist, d), const2),
            pl.BlockSpec((b, d), const2),
            pl.BlockSpec((CONV_W, d), const2),
            pl.BlockSpec((1, d), const2),
            pl.BlockSpec((nblk, wdt, wdt), lambda s: (0, 0, 0)),
            pl.BlockSpec((1, d), const2),
            pl.BlockSpec((nblk, wdt, wdt), lambda s: (0, 0, 0)),
            pl.BlockSpec((1, d), const2),
            pl.BlockSpec((1, d), const2),
        ],
        out_specs=[
            pl.BlockSpec((rows, d), lambda s: (s, 0)),
            pl.BlockSpec((b, d), const2),
            pl.BlockSpec((hist, d), const2),
        ],
        out_shape=[
            jax.ShapeDtypeStruct((t * b, d), BF16),
            jax.ShapeDtypeStruct((b, d), F32),
            jax.ShapeDtypeStruct((hist, d), F32),
        ],
        scratch_shapes=[
            pltpu.VMEM((rows + hist, d), F32),
            pltpu.VMEM((rows, d), F32),
            pltpu.VMEM((rows, d), F32),
            pltpu.VMEM((rows + b, d), F32),
        ],
        compiler_params=_params("arbitrary"),
        name="rglru",
    )(xr_tm, g_tm, buf_tm, h0, conv_w, conv_b.reshape(1, d), wa4, ba.reshape(1, d), wx4, bx.reshape(1, d),
      lam.reshape(1, d))
    return out, h_last, new_buf


def _block_diag_tiles(w, tile):
    nb, bs, _ = w.shape
    per = tile // bs
    w = w.reshape(nb // per, per, bs, bs)
    eye = jnp.eye(per, dtype=w.dtype)
    dense = jnp.einsum("gpcd,pq->gpcqd", w, eye).reshape(nb // per, tile, tile)
    return dense.astype(BF16)


def _merge_kernel(blocks_a, nx, *refs):
    x_refs, refs = refs[:nx], refs[nx:]
    ro_refs, rn_refs = refs[0:2], refs[2:4]
    ga_ref, gb_ref, wr_ref, wn_ref, wo_ref, yt_ref = refs[4:]
    ret_out = jnp.dot(_pick_rows(ro_refs, blocks_a), wr_ref[...], preferred_element_type=F32)
    rnn_out = jnp.dot(_pick_rows(rn_refs, blocks_a), wn_ref[...], preferred_element_type=F32)
    merged = jax.nn.sigmoid(ga_ref[...]) * ret_out + jax.nn.sigmoid(gb_ref[...]) * rnn_out
    y = _pick_rows(x_refs, blocks_a) + jnp.dot(merged.astype(BF16), wo_ref[...], preferred_element_type=F32)
    yt_ref[...] = y.T


def merge_proj(x_parts, ret_parts, rnn_parts, p, w_ret_out, w_rnn_out, w_o):
    n, d = p.shape[0], x_parts[0].shape[1]
    tm = _pick(math.gcd(*[r.shape[0] for r in ret_parts]), (512, 256, 128))
    blocks_a, part_maps = _row_part_maps(ret_parts, tm)
    assert len(x_parts) == 1 or x_parts[0].shape[0] == ret_parts[0].shape[0]
    _, x_maps = _row_part_maps(x_parts, tm) if len(x_parts) == 2 else (None, [lambda i: (i, 0)])
    const = lambda i: (0, 0)
    return pl.pallas_call(
        functools.partial(_merge_kernel, blocks_a, len(x_parts)),
        grid=(n // tm,),
        in_specs=[pl.BlockSpec((tm, d), m) for m in x_maps + part_maps + part_maps] + [
            pl.BlockSpec((tm, d), lambda i: (i, COL_GA)),
            pl.BlockSpec((tm, d), lambda i: (i, COL_GB)),
            pl.BlockSpec((d, d), const),
            pl.BlockSpec((d, d), const),
            pl.BlockSpec((d, d), const),
        ],
        out_specs=pl.BlockSpec((d, tm), lambda i: (0, i)),
        out_shape=jax.ShapeDtypeStruct((d, n), F32),
        compiler_params=_params("parallel"),
        name="merge_out_proj",
    )(*x_parts, *ret_parts, *rnn_parts, p, p, w_ret_out, w_rnn_out, w_o)


def _peer_query_kernel(xt_ref, g_ref, wq_ref, xn_ref, q_ref):
    x = xt_ref[...]
    ms = jnp.mean(x * x, axis=0, keepdims=True)
    xn = ((x * lax.rsqrt(ms + EPS)) * g_ref[...]).astype(BF16)
    xn_ref[...] = xn
    q_ref[...] = jnp.dot(wq_ref[...], xn, preferred_element_type=F32)


def peer_query(xt, g, wq_t):
    d, n = xt.shape
    m = wq_t.shape[0]
    tn = _pick(n, (512, 256, 128))
    return pl.pallas_call(
        _peer_query_kernel,
        grid=(n // tn,),
        in_specs=[
            pl.BlockSpec((d, tn), lambda i: (0, i)),
            pl.BlockSpec((d, 1), lambda i: (0, 0)),
            pl.BlockSpec((m, d), lambda i: (0, 0)),
        ],
        out_specs=[pl.BlockSpec((d, tn), lambda i: (0, i)), pl.BlockSpec((m, tn), lambda i: (0, i))],
        out_shape=[jax.ShapeDtypeStruct((d, n), BF16), jax.ShapeDtypeStruct((m, n), F32)],
        compiler_params=_params("parallel"),
        name="peer_query",
    )(xt, g.reshape(d, 1), wq_t)


def _sort_pairs(n):
    def merge(lo, hi, r):
        step = r * 2
        if step < hi - lo:
            yield from merge(lo, hi, step)
            yield from merge(lo + r, hi, step)
            yield from [(i, i + r) for i in range(lo + r, hi - r, step)]
        else:
            yield (lo, lo + r)

    def sort(lo, hi):
        if hi - lo >= 1:
            mid = lo + (hi - lo) // 2
            yield from sort(lo, mid)
            yield from sort(mid + 1, hi)
            yield from merge(lo, hi, 1)

    return list(sort(0, n - 1))


_SORT16 = _sort_pairs(PEER_TOPK)


def _cmpx(vals, i, j):
    a, b = vals[i], vals[j]
    if b is None:
        return
    if a is None:
        vals[i], vals[j] = b, None
        return
    vals[i], vals[j] = jnp.maximum(a, b), jnp.minimum(a, b)


def _sort_desc(vals):
    vals = list(vals)
    for i, j in _SORT16:
        _cmpx(vals, i, j)
    return vals


def _merge_top(a, b):
    k = PEER_TOPK
    a = list(a) + [None] * (k - len(a))
    b = list(b) + [None] * (k - len(b))
    out = []
    for r in range(k):
        x, y = a[r], b[k - 1 - r]
        out.append(y if x is None else (x if y is None else jnp.maximum(x, y)))
    d = k // 2
    while d >= 1:
        for i in range(k):
            if not i & d:
                _cmpx(out, i, i + d)
        d //= 2
    return out


def _top_sorted(ref):
    groups = []
    for g0 in range(0, N_KEYS, PEER_TOPK):
        vals = [ref[pl.ds((g0 + j) * PEER_HEADS, PEER_HEADS), :] for j in range(PEER_TOPK)]
        groups.append(_sort_desc(vals))
    while len(groups) > 1:
        groups = [_merge_top(groups[i], groups[i + 1]) for i in range(0, len(groups), 2)]
    return groups[0]


def _peer_select_kernel(q_ref, k1_ref, k2_ref, c1_ref, e1_ref, r2_ref, e2_ref, s1_scr, s2_scr, r2_scr, e2_scr):
    nh = PEER_HEADS
    half_rows = q_ref.shape[0] // 2
    tn = q_ref.shape[1]
    s1 = jnp.dot(k1_ref[...], q_ref[pl.ds(0, half_rows), :].astype(BF16), preferred_element_type=F32)
    s2 = jnp.dot(k2_ref[...], q_ref[pl.ds(half_rows, half_rows), :].astype(BF16), preferred_element_type=F32)
    for lt in range(tn // LANES):
        s1_scr[lt] = s1[:, lt * LANES:(lt + 1) * LANES]
        s2_scr[lt] = s2[:, lt * LANES:(lt + 1) * LANES]
    for lt in range(tn // LANES):
        lanes = pl.ds(lt * LANES, LANES)
        s1_t, s2_t, r2_t, e2_t = s1_scr.at[lt], s2_scr.at[lt], r2_scr.at[lt], e2_scr.at[lt]
        a = _top_sorted(s1_t)
        b = _top_sorted(s2_t)
        k = PEER_TOPK
        lists = []
        for j in range(1, k + 1):
            col = [a[r - 1] + b[j - 1] for r in range(j, k // j + 1)]
            row = [a[j - 1] + b[s - 1] for s in range(j + 1, k // j + 1)]
            if col:
                lists.append(col)
            if row:
                lists.append(row)
        top = lists[0]
        for other in lists[1:]:
            top = _merge_top(top, other)
        tau = top[k - 1]
        z = jnp.ones_like(tau)
        for r in range(1, k):
            z = z + jnp.exp(top[r] - top[0])
        zinv = 1.0 / z
        inf = jnp.full((nh, LANES), jnp.inf, F32)
        phi = []
        for s in range(1, k + 1):
            p = inf
            for r in range(1, k // s + 1):
                p = jnp.where(a[r - 1] + b[s - 1] >= tau, a[r - 1], p)
            phi.append(p)

        def per_key(kk, carry):
            rows = pl.ds(pl.multiple_of(kk * nh, nh), nh)
            s1k = s1_t[rows, :]
            s2k = s2_t[rows, :]
            cnt = jnp.ones((nh, LANES), F32)
            for s in range(k):
                cnt = jnp.where(s1k >= phi[s], float(s + 2), cnt)
            rank = jnp.full((nh, LANES), float(k + 1), F32)
            for s in range(k - 1, -1, -1):
                rank = jnp.where(s2k >= b[s], float(s + 1), rank)
            c1_ref[rows, lanes] = cnt
            e1_ref[rows, lanes] = jnp.exp(s1k - a[0])
            r2_t[rows, :] = rank
            e2_t[rows, :] = jnp.exp(s2k - b[0]) * zinv
            return carry

        lax.fori_loop(0, N_KEYS, per_key, 0, unroll=2)
        pack = 2 * SUBLANES
        for h in range(nh):
            for kt in range(N_KEYS // pack):
                lo = pl.ds(kt * pack * nh + h, SUBLANES, stride=nh)
                hi = pl.ds((kt * pack + SUBLANES) * nh + h, SUBLANES, stride=nh)
                dst = pl.ds(h * N_KEYS + kt * pack, pack)
                r2_ref[dst, lanes] = jnp.concatenate([r2_t[lo, :], r2_t[hi, :]], axis=0).astype(BF16)
                e2_ref[dst, lanes] = jnp.concatenate([e2_t[lo, :], e2_t[hi, :]], axis=0).astype(BF16)


def peer_select(q_t, k1, k2):
    m, n = q_t.shape
    rows = PEER_HEADS * N_KEYS
    tn = _pick(n, (256, 128))
    tok = lambda i: (0, i)
    const = lambda i: (0, 0)
    return pl.pallas_call(
        _peer_select_kernel,
        grid=(n // tn,),
        in_specs=[pl.BlockSpec((m, tn), tok), pl.BlockSpec(k1.shape, const), pl.BlockSpec(k2.shape, const)],
        out_specs=[pl.BlockSpec((rows, tn), tok)] * 4,
        out_shape=[jax.ShapeDtypeStruct((rows, n), F32), jax.ShapeDtypeStruct((rows, n), F32),
                   jax.ShapeDtypeStruct((rows, n), BF16), jax.ShapeDtypeStruct((rows, n), BF16)],
        scratch_shapes=[pltpu.VMEM((tn // LANES, rows, LANES), F32)] * 4,
        compiler_params=_params("parallel"),
        name="peer_select",
    )(q_t, k1, k2)


PEER_EXPERT_BLOCK = 1024


def _mxu_dot(lhs, rhs):
    return lax.dot_general(lhs, rhs, (((1,), (0,)), ((), ())), preferred_element_type=F32)


def _peer_dense_kernel(xt_ref, xn_ref, u_ref, vt_ref, c1_ref, e1_ref, r2_ref, e2_ref, y_ref,
                       acc_scr, act_scr):
    j = pl.program_id(1)
    nh = PEER_HEADS
    tn = xn_ref.shape[1]
    pack = 2 * SUBLANES
    i1_per_blk = u_ref.shape[0] // N_KEYS

    @pl.when(j == 0)
    def _():
        acc_scr[...] = jnp.zeros_like(acc_scr)

    wide = 2 * LANES

    def gate_block(il, nc):
        erows = pl.ds(il * N_KEYS, N_KEYS)
        hblk = _mxu_dot(u_ref[erows, :], xn_ref[:, pl.ds(nc * wide, wide)])
        for lw in range(wide // LANES):
            lc = nc * (wide // LANES) + lw
            lanes = pl.ds(lc * LANES, LANES)
            c1b = [jnp.broadcast_to(c1_ref[pl.ds(il * nh + h, 1), lanes], (pack, LANES)).astype(BF16)
                   for h in range(nh)]
            e1b = [jnp.broadcast_to(e1_ref[pl.ds(il * nh + h, 1), lanes], (pack, LANES)).astype(BF16)
                   for h in range(nh)]
            zero = jnp.zeros((pack, LANES), BF16)
            half = jnp.full((), 0.5, BF16)
            gelu_c0 = jnp.full((), GELU_C0, BF16)
            gelu_c1 = jnp.full((), GELU_C1, BF16)
            for it in range(N_KEYS // pack):
                rows = pl.ds(il * N_KEYS + it * pack, pack)
                gate = None
                for h in range(nh):
                    krows = pl.ds(h * N_KEYS + it * pack, pack)
                    sel = jnp.minimum(jnp.maximum(c1b[h] - r2_ref[krows, lanes], zero), e2_ref[krows, lanes])
                    term = sel * e1b[h]
                    gate = term if gate is None else gate + term
                x = hblk[it * pack:(it + 1) * pack, lw * LANES:(lw + 1) * LANES].astype(BF16)
                t = jnp.tanh(x * (gelu_c0 + gelu_c1 * (x * x)))
                act_scr[rows, lanes] = (x * (half + half * t)) * gate

    for nc in range(tn // wide):
        for il in range(i1_per_blk):
            gate_block(il, nc)
        cols = pl.ds(nc * wide, wide)
        acc_scr[:, cols] += _mxu_dot(vt_ref[...], act_scr[:, cols])

    @pl.when(j == pl.num_programs(1) - 1)
    def _():
        y_ref[...] = (xt_ref[...] + acc_scr[...]).T


def peer_dense(xt, xn_t, u_bf16, vt_bf16, c1, e1, r2, e2):
    d, n = xt.shape
    ne = u_bf16.shape[0]
    tn = _pick(n, (512, 256, 128))
    te = PEER_EXPERT_BLOCK
    i1_per_blk = te // N_KEYS
    tok = lambda i, j: (0, i)
    return pl.pallas_call(
        _peer_dense_kernel,
        grid=(n // tn, ne // te),
        in_specs=[
            pl.BlockSpec((d, tn), tok),
            pl.BlockSpec((d, tn), tok),
            pl.BlockSpec((te, d), lambda i, j: (j, 0)),
            pl.BlockSpec((d, te), lambda i, j: (0, j)),
            pl.BlockSpec((i1_per_blk * PEER_HEADS, tn), lambda i, j: (j, i)),
            pl.BlockSpec((i1_per_blk * PEER_HEADS, tn), lambda i, j: (j, i)),
            pl.BlockSpec((PEER_HEADS * N_KEYS, tn), tok),
            pl.BlockSpec((PEER_HEADS * N_KEYS, tn), tok),
        ],
        out_specs=pl.BlockSpec((tn, d), lambda i, j: (i, 0)),
        out_shape=jax.ShapeDtypeStruct((n, d), F32),
        scratch_shapes=[pltpu.VMEM((d, tn), F32), pltpu.VMEM((te, tn), BF16)],
        compiler_params=_params("parallel", "arbitrary"),
        name="peer_dense",
    )(xt, xn_t, u_bf16, vt_bf16, c1, e1, r2, e2)


def _interleaved_keys(keys_p):
    nh, nk, half = keys_p.shape
    eye = jnp.eye(nh, dtype=keys_p.dtype)
    return jnp.einsum("hkd,hg->khgd", keys_p, eye).reshape(nk * nh, nh * half).astype(BF16)


def peer_layer(xt, norm_g, wq, keys, u_tab, v_tab):
    d = xt.shape[0]
    wq_t = wq.reshape(d, PEER_HEADS, 2, PEER_HALF).transpose(2, 1, 3, 0).reshape(2 * PEER_HEADS * PEER_HALF, d)
    xn_t, q_t = peer_query(xt, norm_g, wq_t.astype(BF16))
    k1 = _interleaved_keys(keys[:, 0])
    k2 = _interleaved_keys(keys[:, 1])
    c1, e1, r2, e2 = peer_select(q_t, k1, k2)
    return peer_dense(xt, xn_t, u_tab, v_tab.T.astype(BF16), c1, e1, r2, e2)


def _rmsnorm_kernel(x_ref, g_ref, y_ref):
    x = x_ref[...]
    ms = jnp.mean(x * x, axis=-1, keepdims=True)
    y_ref[...] = (x * lax.rsqrt(ms + EPS)) * g_ref[...]


def final_norm(x, row0, rows, g):
    d = x.shape[1]
    tm = _pick(math.gcd(rows, row0) if row0 else rows, (512, 256, 128, 64, 32, 16, 8))
    blk0 = row0 // tm
    return pl.pallas_call(
        _rmsnorm_kernel,
        grid=(rows // tm,),
        in_specs=[pl.BlockSpec((tm, d), lambda i: (blk0 + i, 0)), pl.BlockSpec((1, d), lambda i: (0, 0))],
        out_specs=pl.BlockSpec((tm, d), lambda i: (i, 0)),
        out_shape=jax.ShapeDtypeStruct((rows, d), F32),
        compiler_params=_params("parallel"),
        name="final_norm",
    )(x, g.reshape(1, d))


def _to_time_major(p, row0, b, t, col):
    blk = lax.slice(p, (row0, col * D_RNN), (row0 + b * t, (col + 1) * D_RNN))
    return blk.reshape(b, t, D_RNN).transpose(1, 0, 2).reshape(t * b, D_RNN)


def _trunk(groups, norm1_g, norm2_g, normf_g, w_in, ret_gn_g, w_ret_out, conv_w, conv_b, rg_wa, rg_ba, rg_wx,
           rg_bx, rg_lambda, w_rnn_out, w_o, peer_wq, peer_keys, peer_u, peer_v):
    shapes = [(g[0].shape[0], g[0].shape[1]) for g in groups]
    x_parts = [g[0].reshape(-1, D_MODEL) for g in groups]
    row0s = np.cumsum([0] + [b * t for b, t in shapes]).tolist()
    states = [([], [], []) for _ in groups]
    for l in range(DEPTH):
        p = norm_matmul(x_parts, norm1_g[l], w_in[l].astype(BF16))
        wa4 = _block_diag_tiles(rg_wa[l], 256)
        wx4 = _block_diag_tiles(rg_wx[l], 256)
        ret_parts, rnn_parts = [], []
        for gi, (xg, r0, h0, buf0, pos0) in enumerate(groups):
            b, t = shapes[gi]
            row0 = row0s[gi]
            o, r_new = retention_path(p, row0, b, t, pos0, r0[l], ret_gn_g[l])
            rg_args = (conv_w[l], conv_b[l], wa4, rg_ba[l], wx4, rg_bx[l], rg_lambda[l])
            if _rglru_rows_ok(b, t, row0):
                hg, h_last, nb = rglru_rows_path(p, row0, b, t, buf0[l], h0[l], *rg_args)
            else:
                xr_tm = _to_time_major(p, row0, b, t, COL_XR)
                g_tm = _to_time_major(p, row0, b, t, COL_GRNN)
                buf_tm = buf0[l].transpose(1, 0, 2).reshape((CONV_W - 1) * b, D_RNN)
                hg, h_last, nb = rglru_path(xr_tm, g_tm, buf_tm, h0[l], b, t, *rg_args)
                hg = hg.reshape(t, b, D_RNN).transpose(1, 0, 2).reshape(b * t, D_RNN)
                nb = nb.reshape(CONV_W - 1, b, D_RNN).transpose(1, 0, 2)
            ret_parts.append(o)
            rnn_parts.append(hg)
            states[gi][0].append(r_new)
            states[gi][1].append(h_last)
            states[gi][2].append(nb)
        xt = merge_proj(x_parts, ret_parts, rnn_parts, p, w_ret_out[l].astype(BF16), w_rnn_out[l].astype(BF16),
                        w_o[l].astype(BF16))
        x = peer_layer(xt, norm2_g[l], peer_wq[l], peer_keys[l], peer_u[l], peer_v[l])
        x_parts = [x]
    outs = []
    for gi, (b, t) in enumerate(shapes):
        y = final_norm(x, row0s[gi], b * t, normf_g).reshape(b, t, D_MODEL)
        outs.append((y, jnp.stack(states[gi][0]), jnp.stack(states[gi][1]), jnp.stack(states[gi][2])))
    return outs


def kernel(x_prompt, x_sample, state_ret, state_rnn, state_conv, norm1_g, norm2_g, normf_g, w_in, ret_gn_g,
           w_ret_out, conv_w, conv_b, rg_wa, rg_ba, rg_wx, rg_bx, rg_lambda, w_rnn_out, w_o, peer_wq, peer_keys,
           peer_u, peer_v):
    bp = x_prompt.shape[0]
    dt = x_prompt.dtype
    zr = jnp.zeros((DEPTH, bp, RET_HEADS, RET_DK, RET_DV), dt)
    zh = jnp.zeros((DEPTH, bp, D_RNN), dt)
    zc = jnp.zeros((DEPTH, bp, CONV_W - 1, D_RNN), dt)
    groups = [(x_prompt, zr, zh, zc, 0.0), (x_sample, state_ret, state_rnn, state_conv, float(PAST_LEN))]
    (yp, rp, hp, cp), (ys, rs, hs, cs) = _trunk(
        groups, norm1_g, norm2_g, normf_g, w_in, ret_gn_g, w_ret_out, conv_w, conv_b, rg_wa, rg_ba, rg_wx, rg_bx,
        rg_lambda, w_rnn_out, w_o, peer_wq, peer_keys, peer_u, peer_v)
    return (yp, ys, rp, hp, cp, rs, hs, cs)
```

```python
import functools
import math

import jax
import jax.numpy as jnp
import numpy as np
from jax import lax
from jax.experimental import pallas as pl
from jax.experimental.pallas import tpu as pltpu

D_MODEL = 1024
DEPTH = 2
PAST_LEN = 16384
RET_HEADS = 8
RET_DK = 64
RET_DV = 128
RET_QK = RET_HEADS * RET_DK
RET_V = RET_HEADS * RET_DV
RET_CHUNK = 128
ROPE_BASE = 10000.0
D_RNN = 1024
CONV_W = 4
RG_C = 8.0
COL_V, COL_GRET, COL_XR, COL_GRNN, COL_GA, COL_GB = 1, 2, 3, 4, 5, 6
PEER_HEADS = 8
N_KEYS = 128
PEER_DKEY = 256
PEER_HALF = PEER_DKEY // 2
PEER_TOPK = 16
EPS = 1e-6

SUBLANES = 8
LANES = 128
VMEM_LIMIT = 56 * 1024 * 1024

F32 = jnp.float32
BF16 = jnp.bfloat16


def _params(*sem):
    return pltpu.CompilerParams(dimension_semantics=sem, vmem_limit_bytes=VMEM_LIMIT)


def _pick(n, prefs):
    for p in prefs:
        if n % p == 0:
            return p
    return n


def _row_part_maps(parts, tm):
    assert len(parts) in (1, 2) and all(p.shape[0] % tm == 0 for p in parts)
    blocks_a = parts[0].shape[0] // tm
    maps = [lambda i, *_: (jnp.minimum(i, blocks_a - 1), 0)]
    if len(parts) == 2:
        maps.append(lambda i, *_: (jnp.maximum(i - blocks_a, 0), 0))
    return blocks_a, maps


def _pick_rows(refs, blocks_a):
    x = refs[0][...]
    if len(refs) == 2:
        x = jnp.where(pl.program_id(0) < blocks_a, x, refs[1][...])
    return x


def _norm_matmul_kernel(blocks_a, nparts, *refs):
    x_refs = refs[:nparts]
    g_ref, w_ref, y_ref, xn_scr = refs[nparts:]

    @pl.when(pl.program_id(1) == 0)
    def _():
        x = _pick_rows(x_refs, blocks_a)
        ms = jnp.mean(x * x, axis=-1, keepdims=True)
        xn_scr[...] = ((x * lax.rsqrt(ms + EPS)) * g_ref[...]).astype(BF16)

    y_ref[...] = jnp.dot(xn_scr[...], w_ref[...], preferred_element_type=F32)


def norm_matmul(x_parts, g, w_bf16):
    n = sum(p.shape[0] for p in x_parts)
    d = x_parts[0].shape[1]
    m = w_bf16.shape[1]
    tm = _pick(math.gcd(*[p.shape[0] for p in x_parts]), (1024, 512, 256, 128))
    tn = _pick(m, (1792, 1024, 512, 256, 128))
    blocks_a, row_maps = _row_part_maps(x_parts, tm)
    return pl.pallas_call(
        functools.partial(_norm_matmul_kernel, blocks_a, len(x_parts)),
        grid=(n // tm, m // tn),
        in_specs=[pl.BlockSpec((tm, d), rm) for rm in row_maps] + [
            pl.BlockSpec((1, d), lambda i, j: (0, 0)),
            pl.BlockSpec((d, tn), lambda i, j: (0, j)),
        ],
        out_specs=pl.BlockSpec((tm, tn), lambda i, j: (i, j)),
        out_shape=jax.ShapeDtypeStruct((n, m), F32),
        scratch_shapes=[pltpu.VMEM((tm, d), BF16)],
        compiler_params=_params("parallel", "arbitrary"),
        name="norm_in_proj",
    )(*x_parts, g.reshape(1, d), w_bf16)


def _rot_half(x):
    n = x.shape[-1]
    half = RET_DK // 2
    fwd = pltpu.roll(x, half, axis=1)
    bwd = pltpu.roll(x, n - half, axis=1)
    lane = lax.broadcasted_iota(jnp.int32, x.shape, 1)
    return jnp.where((lane % RET_DK) < half, bwd, fwd)


def _retention_kernel(q_ref, k_ref, v_ref, g_ref, r0_ref, cos_ref, sin_ref, mask_ref, qw_ref, kw_ref,
                      gc_ref, gn_ref, o_ref, r_out_ref, r_scr):
    c = pl.program_id(1)
    rows_c = mask_ref.shape[1]
    nb = q_ref.shape[0] // rows_c

    @pl.when(c == 0)
    def _():
        r_scr[...] = r0_ref[...]

    cos = cos_ref[...]
    sin = sin_ref[...]
    q = q_ref[...]
    k = k_ref[...]
    qr = q * cos + _rot_half(q) * sin
    kr = (k * cos + _rot_half(k) * sin) * (RET_DK ** -0.5)
    qd = (qr * qw_ref[...]).astype(BF16)
    kd = (kr * kw_ref[...]).astype(BF16)
    qb = qr.astype(BF16)
    kb = kr.astype(BF16)
    v = v_ref[...].astype(BF16)
    g = g_ref[...]
    for bb in range(nb):
        rs = slice(bb * rows_c, (bb + 1) * rows_c)
        for h in range(RET_HEADS):
            ks = slice(h * RET_DK, (h + 1) * RET_DK)
            vs = slice(h * RET_DV, (h + 1) * RET_DV)
            vh = v[rs, vs]
            s = lax.dot_general(qb[rs, ks], kb[rs, ks], (((1,), (1,)), ((), ())),
                                preferred_element_type=F32) * mask_ref[h]
            r_h = r_scr[bb, h]
            o = jnp.dot(s.astype(BF16), vh, preferred_element_type=F32)
            o = o + jnp.dot(qd[rs, ks], r_h.astype(BF16), preferred_element_type=F32)
            kv = lax.dot_general(kd[rs, ks], vh, (((0,), (0,)), ((), ())), preferred_element_type=F32)
            r_scr[bb, h] = gc_ref[h] * r_h + kv
            mu = jnp.mean(o, axis=-1, keepdims=True)
            var = jnp.mean(jnp.square(o - mu), axis=-1, keepdims=True)
            on = ((o - mu) * lax.rsqrt(var + EPS)) * gn_ref[:, vs]
            gh = g[rs, vs]
            o_ref[rs, vs] = ((gh * jax.nn.sigmoid(gh)) * on).astype(BF16)

    @pl.when(c == pl.num_programs(1) - 1)
    def _():
        r_out_ref[...] = r_scr[...]


def retention_path(p, row0, b, t, pos0, r0_layers, layer, gn_g):
    c = RET_CHUNK if t % RET_CHUNK == 0 else t
    nc = t // c
    blk0 = row0 // c
    assert row0 % c == 0
    log_g = jnp.log1p(-(2.0 ** (-5.0 - jnp.arange(RET_HEADS, dtype=F32))))
    idx = jnp.arange(c, dtype=F32)
    diff = idx[:, None] - idx[None, :]
    mask = jnp.where(diff[None] >= 0, jnp.exp(jnp.maximum(diff, 0.0)[None] * log_g[:, None, None]), 0.0)
    k_w = jnp.exp((c - 1 - idx)[:, None] * log_g[None, :])
    q_w = jnp.exp((idx + 1.0)[:, None] * log_g[None, :])
    g_c = jnp.exp(c * log_g)
    qw_tab = jnp.repeat(q_w, RET_DK, axis=1)
    kw_tab = jnp.repeat(k_w, RET_DK, axis=1)
    gc_tab = jnp.broadcast_to(g_c[:, None, None], (RET_HEADS, 1, RET_DV))
    half = RET_DK // 2
    pos = pos0 + jnp.arange(t, dtype=F32)
    freq = ROPE_BASE ** (-jnp.arange(half, dtype=F32) / half)
    ang = pos[:, None] * freq[None, :]
    cos_h = jnp.concatenate([jnp.cos(ang), jnp.cos(ang)], axis=1)
    sin_h = jnp.concatenate([-jnp.sin(ang), jnp.sin(ang)], axis=1)
    cos_tab = jnp.tile(cos_h, (1, RET_HEADS))
    sin_tab = jnp.tile(sin_h, (1, RET_HEADS))
    nb = _pick(b, (8, 4, 2, 1)) if nc == 1 and c * 8 <= RET_CHUNK else 1
    if nb > 1:
        cos_tab, sin_tab, qw_tab, kw_tab = [jnp.tile(tab, (nb, 1)) for tab in (cos_tab, sin_tab, qw_tab, kw_tab)]
    rb = nb * c
    blk0 = row0 // rb
    assert row0 % rb == 0

    rows = lambda bi, ci: blk0 + bi * nc + ci
    state_spec = pl.BlockSpec((nb, RET_HEADS, RET_DK, RET_DV), lambda bi, ci: (bi, 0, 0, 0))
    o, r_new = pl.pallas_call(
        _retention_kernel,
        grid=(b // nb, nc),
        in_specs=[
            pl.BlockSpec((rb, RET_QK), lambda bi, ci: (rows(bi, ci), 0)),
            pl.BlockSpec((rb, RET_QK), lambda bi, ci: (rows(bi, ci), 1)),
            pl.BlockSpec((rb, RET_V), lambda bi, ci: (rows(bi, ci), COL_V)),
            pl.BlockSpec((rb, RET_V), lambda bi, ci: (rows(bi, ci), COL_GRET)),
            pl.BlockSpec((None, nb, RET_HEADS, RET_DK, RET_DV), lambda bi, ci: (layer, bi, 0, 0, 0)),
            pl.BlockSpec((rb, RET_QK), lambda bi, ci: (ci, 0)),
            pl.BlockSpec((rb, RET_QK), lambda bi, ci: (ci, 0)),
            pl.BlockSpec((RET_HEADS, c, c), lambda bi, ci: (0, 0, 0)),
            pl.BlockSpec((rb, RET_QK), lambda bi, ci: (0, 0)),
            pl.BlockSpec((rb, RET_QK), lambda bi, ci: (0, 0)),
            pl.BlockSpec((RET_HEADS, 1, RET_DV), lambda bi, ci: (0, 0, 0)),
            pl.BlockSpec((1, RET_V), lambda bi, ci: (0, 0)),
        ],
        out_specs=[
            pl.BlockSpec((rb, RET_V), lambda bi, ci: (bi * nc + ci, 0)),
            state_spec,
        ],
        out_shape=[
            jax.ShapeDtypeStruct((b * t, RET_V), BF16),
            jax.ShapeDtypeStruct((b, RET_HEADS, RET_DK, RET_DV), F32),
        ],
        scratch_shapes=[pltpu.VMEM((nb, RET_HEADS, RET_DK, RET_DV), F32)],
        compiler_params=_params("parallel", "arbitrary"),
        name="retention",
    )(p, p, p, p, r0_layers, cos_tab, sin_tab, mask, qw_tab, kw_tab, gc_tab, gn_g.reshape(1, RET_V))
    return o, r_new


GELU_C0 = math.sqrt(2.0 / math.pi)
GELU_C1 = GELU_C0 * 0.044715


def _gelu_tanh(x):
    return x * (0.5 * (1.0 + jnp.tanh(math.sqrt(2.0 / math.pi) * (x + 0.044715 * (x * x * x)))))


def _rglru_gates(xc, wa_ref, ba_ref, wx_ref, bx_ref, lam_ref):
    xcb = xc.astype(BF16)
    nblk = wa_ref.shape[0]
    wdt = wa_ref.shape[1]
    ra = jnp.concatenate(
        [jnp.dot(xcb[:, j * wdt:(j + 1) * wdt], wa_ref[j], preferred_element_type=F32) for j in range(nblk)],
        axis=1)
    ri = jnp.concatenate(
        [jnp.dot(xcb[:, j * wdt:(j + 1) * wdt], wx_ref[j], preferred_element_type=F32) for j in range(nblk)],
        axis=1)
    r = jax.nn.sigmoid(ra + ba_ref[...])
    i = jax.nn.sigmoid(ri + bx_ref[...])
    z = -lam_ref[...]
    softplus = jnp.maximum(z, 0.0) + jnp.log1p(jnp.exp(-jnp.abs(z)))
    log_a = (-RG_C * r) * softplus
    a = jnp.exp(log_a)
    one_m_a2 = -jnp.tanh(log_a) * (a * a + 1.0)
    return a, jnp.sqrt(one_m_a2) * (i * xc)


def _rglru_rows_kernel(bsz, tc, nin, chained, *refs):
    xr_refs, g_refs = refs[:nin], refs[nin:2 * nin]
    (buf_ref, h0_ref, cw_ref, cb_ref, wa_ref, ba_ref, wx_ref, bx_ref, lam_ref, o_ref, hl_ref, nb_ref,
     xcat_scr, a_scr, u_scr, hs_scr, h_scr) = refs[2 * nin:]
    step = pl.program_id(0)
    rows = tc * bsz
    hist = (CONV_W - 1) * bsz
    bpi = bsz // nin

    def time_major(x):
        return jnp.swapaxes(x, 0, 1).reshape(x.shape[0] * x.shape[1], D_RNN)

    def batch_major(blocks):
        return jnp.concatenate([blk[...].reshape(bpi, tc, D_RNN) for blk in blocks], axis=0)

    def load_state():
        xcat_scr[pl.ds(rows, hist), :] = time_major(buf_ref[...])
        h_scr[...] = h0_ref[...]

    if chained:
        pl.when(step == 0)(load_state)
    else:
        load_state()

    xcat_scr[pl.ds(0, hist), :] = xcat_scr[pl.ds(rows, hist), :]
    xcat_scr[pl.ds(hist, rows), :] = time_major(batch_major(xr_refs))
    xc = cb_ref[...] + xcat_scr[pl.ds(0, rows), :] * cw_ref[0:1, :]
    for w in range(1, CONV_W):
        xc = xc + xcat_scr[pl.ds(w * bsz, rows), :] * cw_ref[w:w + 1, :]
    a, u = _rglru_gates(xc, wa_ref, ba_ref, wx_ref, bx_ref, lam_ref)
    a_scr[...] = a.reshape(tc, bsz, D_RNN)
    u_scr[...] = u.reshape(tc, bsz, D_RNN)

    def scan_step(t, h):
        h = a_scr[t] * h + u_scr[t]
        hs_scr[t] = h
        return h

    h_scr[...] = lax.fori_loop(0, tc, scan_step, h_scr[...])
    gate = _gelu_tanh(batch_major(g_refs))
    o_ref[...] = (jnp.swapaxes(hs_scr[...], 0, 1) * gate).astype(BF16)

    def store_state():
        hl_ref[...] = h_scr[...]
        nb_ref[...] = jnp.swapaxes(xcat_scr[pl.ds(rows, hist), :].reshape(CONV_W - 1, bsz, D_RNN), 0, 1)

    if chained:
        pl.when(step == pl.num_programs(0) - 1)(store_state)
    else:
        store_state()


RGLRU_STEP_ROWS = 256


def rglru_rows_path(p, row0, b, t, buf0, h0, conv_w, conv_b, wa4, ba, wx4, bx, lam):
    assert t >= CONV_W - 1 and b % SUBLANES == 0 and RGLRU_STEP_ROWS % SUBLANES == 0
    d = D_RNN
    hist = CONV_W - 1
    nblk, wdt = wa4.shape[0], wa4.shape[1]
    chained = b * t > RGLRU_STEP_ROWS and b == SUBLANES
    if chained:
        bg, tc, nin = b, RGLRU_STEP_ROWS // b, b
        steps = t // tc
        assert t % tc == 0 and row0 % tc == 0
        blk0 = row0 // tc
        row_map = lambda bi, col: (lambda s: (blk0 + bi * steps + s, col))
        state2, state3 = (lambda s: (0, 0)), (lambda s: (0, 0, 0))
        out_map = lambda s: (0, s, 0)
    else:
        bg, tc, nin = max(SUBLANES, RGLRU_STEP_ROWS // t), t, 1
        steps = b // bg
        assert b % bg == 0 and row0 % (bg * t) == 0
        blk0 = row0 // (bg * t)
        row_map = lambda bi, col: (lambda s: (blk0 + s, col))
        state2, state3 = (lambda s: (s, 0)), (lambda s: (s, 0, 0))
        out_map = lambda s: (s, 0, 0)
    in_rows = (bg // nin) * tc
    const2 = lambda s: (0, 0)
    const3 = lambda s: (0, 0, 0)
    out, h_last, new_buf = pl.pallas_call(
        functools.partial(_rglru_rows_kernel, bg, tc, nin, chained),
        grid=(steps,),
        in_specs=[pl.BlockSpec((in_rows, d), row_map(bi, COL_XR)) for bi in range(nin)]
        + [pl.BlockSpec((in_rows, d), row_map(bi, COL_GRNN)) for bi in range(nin)]
        + [
            pl.BlockSpec((bg, hist, d), state3),
            pl.BlockSpec((bg, d), state2),
            pl.BlockSpec((CONV_W, d), const2),
            pl.BlockSpec((1, d), const2),
            pl.BlockSpec((nblk, wdt, wdt), const3),
            pl.BlockSpec((1, d), const2),
            pl.BlockSpec((nblk, wdt, wdt), const3),
            pl.BlockSpec((1, d), const2),
            pl.BlockSpec((1, d), const2),
        ],
        out_specs=[
            pl.BlockSpec((bg, tc, d), out_map),
            pl.BlockSpec((bg, d), state2),
            pl.BlockSpec((bg, hist, d), state3),
        ],
        out_shape=[
            jax.ShapeDtypeStruct((b, t, d), BF16),
            jax.ShapeDtypeStruct((b, d), F32),
            jax.ShapeDtypeStruct((b, hist, d), F32),
        ],
        scratch_shapes=[
            pltpu.VMEM(((CONV_W - 1 + tc) * bg, d), F32),
            pltpu.VMEM((tc, bg, d), F32),
            pltpu.VMEM((tc, bg, d), F32),
            pltpu.VMEM((tc, bg, d), F32),
            pltpu.VMEM((bg, d), F32),
        ],
        compiler_params=_params("arbitrary"),
        name="rglru_rows",
    )(*([p] * (2 * nin)), buf0, h0, conv_w, conv_b.reshape(1, d), wa4, ba.reshape(1, d), wx4, bx.reshape(1, d),
      lam.reshape(1, d))
    return out.reshape(b * t, d), h_last, new_buf


def _block_diag_tiles(w, tile):
    nb, bs, _ = w.shape
    per = tile // bs
    w = w.reshape(nb // per, per, bs, bs)
    eye = jnp.eye(per, dtype=w.dtype)
    dense = jnp.einsum("gpcd,pq->gpcqd", w, eye).reshape(nb // per, tile, tile)
    return dense.astype(BF16)


def _merge_kernel(blocks_a, nx, *refs):
    x_refs, refs = refs[:nx], refs[nx:]
    ro_refs, rn_refs = refs[0:2], refs[2:4]
    ga_ref, gb_ref, wr_ref, wn_ref, wo_ref, yt_ref = refs[4:]
    ret_out = jnp.dot(_pick_rows(ro_refs, blocks_a), wr_ref[...], preferred_element_type=F32)
    rnn_out = jnp.dot(_pick_rows(rn_refs, blocks_a), wn_ref[...], preferred_element_type=F32)
    merged = jax.nn.sigmoid(ga_ref[...]) * ret_out + jax.nn.sigmoid(gb_ref[...]) * rnn_out
    y = _pick_rows(x_refs, blocks_a) + jnp.dot(merged.astype(BF16), wo_ref[...], preferred_element_type=F32)
    yt_ref[...] = y.T


def merge_proj(x_parts, ret_parts, rnn_parts, p, w_ret_out, w_rnn_out, w_o):
    n, d = p.shape[0], x_parts[0].shape[1]
    tm = _pick(math.gcd(*[r.shape[0] for r in ret_parts]), (512, 256, 128))
    blocks_a, part_maps = _row_part_maps(ret_parts, tm)
    assert len(x_parts) == 1 or x_parts[0].shape[0] == ret_parts[0].shape[0]
    _, x_maps = _row_part_maps(x_parts, tm) if len(x_parts) == 2 else (None, [lambda i: (i, 0)])
    const = lambda i: (0, 0)
    return pl.pallas_call(
        functools.partial(_merge_kernel, blocks_a, len(x_parts)),
        grid=(n // tm,),
        in_specs=[pl.BlockSpec((tm, d), m) for m in x_maps + part_maps + part_maps] + [
            pl.BlockSpec((tm, d), lambda i: (i, COL_GA)),
            pl.BlockSpec((tm, d), lambda i: (i, COL_GB)),
            pl.BlockSpec((d, d), const),
            pl.BlockSpec((d, d), const),
            pl.BlockSpec((d, d), const),
        ],
        out_specs=pl.BlockSpec((d, tm), lambda i: (0, i)),
        out_shape=jax.ShapeDtypeStruct((d, n), F32),
        compiler_params=_params("parallel"),
        name="merge_out_proj",
    )(*x_parts, *ret_parts, *rnn_parts, p, p, w_ret_out, w_rnn_out, w_o)


def _peer_query_kernel(xt_ref, g_ref, wq_ref, xn_ref, q_ref):
    x = xt_ref[...]
    ms = jnp.mean(x * x, axis=0, keepdims=True)
    xn = ((x * lax.rsqrt(ms + EPS)) * g_ref[...]).astype(BF16)
    xn_ref[...] = xn
    q_ref[...] = jnp.dot(wq_ref[...], xn, preferred_element_type=F32)


def peer_query(xt, g, wq_t):
    d, n = xt.shape
    m = wq_t.shape[0]
    tn = _pick(n, (512, 256, 128))
    return pl.pallas_call(
        _peer_query_kernel,
        grid=(n // tn,),
        in_specs=[
            pl.BlockSpec((d, tn), lambda i: (0, i)),
            pl.BlockSpec((d, 1), lambda i: (0, 0)),
            pl.BlockSpec((m, d), lambda i: (0, 0)),
        ],
        out_specs=[pl.BlockSpec((d, tn), lambda i: (0, i)), pl.BlockSpec((m, tn), lambda i: (0, i))],
        out_shape=[jax.ShapeDtypeStruct((d, n), BF16), jax.ShapeDtypeStruct((m, n), F32)],
        compiler_params=_params("parallel"),
        name="peer_query",
    )(xt, g.reshape(d, 1), wq_t)


def _sort_pairs(n):
    def merge(lo, hi, r):
        step = r * 2
        if step < hi - lo:
            yield from merge(lo, hi, step)
            yield from merge(lo + r, hi, step)
            yield from [(i, i + r) for i in range(lo + r, hi - r, step)]
        else:
            yield (lo, lo + r)

    def sort(lo, hi):
        if hi - lo >= 1:
            mid = lo + (hi - lo) // 2
            yield from sort(lo, mid)
            yield from sort(mid + 1, hi)
            yield from merge(lo, hi, 1)

    return list(sort(0, n - 1))


_SORT16 = _sort_pairs(PEER_TOPK)


def _cmpx(vals, i, j):
    a, b = vals[i], vals[j]
    if b is None:
        return
    if a is None:
        vals[i], vals[j] = b, None
        return
    vals[i], vals[j] = jnp.maximum(a, b), jnp.minimum(a, b)


def _sort_desc(vals):
    vals = list(vals)
    for i, j in _SORT16:
        _cmpx(vals, i, j)
    return vals


def _merge_top(a, b):
    k = PEER_TOPK
    a = list(a) + [None] * (k - len(a))
    b = list(b) + [None] * (k - len(b))
    out = []
    for r in range(k):
        x, y = a[r], b[k - 1 - r]
        out.append(y if x is None else (x if y is None else jnp.maximum(x, y)))
    d = k // 2
    while d >= 1:
        for i in range(k):
            if not i & d:
                _cmpx(out, i, i + d)
        d //= 2
    return out


def _top_sorted(ref):
    groups = []
    for g0 in range(0, N_KEYS, PEER_TOPK):
        vals = [ref[pl.ds((g0 + j) * PEER_HEADS, PEER_HEADS), :] for j in range(PEER_TOPK)]
        groups.append(_sort_desc(vals))
    while len(groups) > 1:
        groups = [_merge_top(groups[i], groups[i + 1]) for i in range(0, len(groups), 2)]
    return groups[0]


def _peer_select_kernel(q_ref, k1_ref, k2_ref, c1_ref, e1_ref, r2_ref, e2_ref, s1_scr, s2_scr, r2_scr, e2_scr):
    nh = PEER_HEADS
    half_rows = q_ref.shape[0] // 2
    tn = q_ref.shape[1]
    s1 = jnp.dot(k1_ref[...], q_ref[pl.ds(0, half_rows), :].astype(BF16), preferred_element_type=F32)
    s2 = jnp.dot(k2_ref[...], q_ref[pl.ds(half_rows, half_rows), :].astype(BF16), preferred_element_type=F32)
    for lt in range(tn // LANES):
        s1_scr[lt] = s1[:, lt * LANES:(lt + 1) * LANES]
        s2_scr[lt] = s2[:, lt * LANES:(lt + 1) * LANES]
    for lt in range(tn // LANES):
        lanes = pl.ds(lt * LANES, LANES)
        s1_t, s2_t, r2_t, e2_t = s1_scr.at[lt], s2_scr.at[lt], r2_scr.at[lt], e2_scr.at[lt]
        a = _top_sorted(s1_t)
        b = _top_sorted(s2_t)
        k = PEER_TOPK
        lists = []
        for j in range(1, k + 1):
            col = [a[r - 1] + b[j - 1] for r in range(j, k // j + 1)]
            row = [a[j - 1] + b[s - 1] for s in range(j + 1, k // j + 1)]
            if col:
                lists.append(col)
            if row:
                lists.append(row)
        top = lists[0]
        for other in lists[1:]:
            top = _merge_top(top, other)
        tau = top[k - 1]
        z = jnp.ones_like(tau)
        for r in range(1, k):
            z = z + jnp.exp(top[r] - top[0])
        zinv = 1.0 / z
        inf = jnp.full((nh, LANES), jnp.inf, F32)
        phi = []
        for s in range(1, k + 1):
            p = inf
            for r in range(1, k // s + 1):
                p = jnp.where(a[r - 1] + b[s - 1] >= tau, a[r - 1], p)
            phi.append(p)

        def per_key(kk, carry):
            rows = pl.ds(pl.multiple_of(kk * nh, nh), nh)
            s1k = s1_t[rows, :]
            s2k = s2_t[rows, :]
            cnt = jnp.ones((nh, LANES), F32)
            for s in range(k):
                cnt = jnp.where(s1k >= phi[s], float(s + 2), cnt)
            rank = jnp.full((nh, LANES), float(k + 1), F32)
            for s in range(k - 1, -1, -1):
                rank = jnp.where(s2k >= b[s], float(s + 1), rank)
            c1_ref[rows, lanes] = cnt
            e1_ref[rows, lanes] = jnp.exp(s1k - a[0])
            r2_t[rows, :] = rank
            e2_t[rows, :] = jnp.exp(s2k - b[0]) * zinv
            return carry

        lax.fori_loop(0, N_KEYS, per_key, 0, unroll=2)
        pack = 2 * SUBLANES
        for h in range(nh):
            for kt in range(N_KEYS // pack):
                lo = pl.ds(kt * pack * nh + h, SUBLANES, stride=nh)
                hi = pl.ds((kt * pack + SUBLANES) * nh + h, SUBLANES, stride=nh)
                dst = pl.ds(h * N_KEYS + kt * pack, pack)
                r2_ref[dst, lanes] = jnp.concatenate([r2_t[lo, :], r2_t[hi, :]], axis=0).astype(BF16)
                e2_ref[dst, lanes] = jnp.concatenate([e2_t[lo, :], e2_t[hi, :]], axis=0).astype(BF16)


def peer_select(q_t, k1, k2):
    m, n = q_t.shape
    rows = PEER_HEADS * N_KEYS
    tn = _pick(n, (256, 128))
    tok = lambda i: (0, i)
    const = lambda i: (0, 0)
    return pl.pallas_call(
        _peer_select_kernel,
        grid=(n // tn,),
        in_specs=[pl.BlockSpec((m, tn), tok), pl.BlockSpec(k1.shape, const), pl.BlockSpec(k2.shape, const)],
        out_specs=[pl.BlockSpec((rows, tn), tok)] * 4,
        out_shape=[jax.ShapeDtypeStruct((rows, n), F32), jax.ShapeDtypeStruct((rows, n), F32),
                   jax.ShapeDtypeStruct((rows, n), BF16), jax.ShapeDtypeStruct((rows, n), BF16)],
        scratch_shapes=[pltpu.VMEM((tn // LANES, rows, LANES), F32)] * 4,
        compiler_params=_params("parallel"),
        name="peer_select",
    )(q_t, k1, k2)


PEER_EXPERT_BLOCK = 1024


def _mxu_dot(lhs, rhs):
    return lax.dot_general(lhs, rhs, (((1,), (0,)), ((), ())), preferred_element_type=F32)


def _peer_dense_kernel(xt_ref, xn_ref, u_ref, vt_ref, c1_ref, e1_ref, r2_ref, e2_ref, y_ref,
                       acc_scr, act_scr):
    j = pl.program_id(1)
    nh = PEER_HEADS
    tn = xn_ref.shape[1]
    pack = 2 * SUBLANES
    i1_per_blk = u_ref.shape[0] // N_KEYS

    @pl.when(j == 0)
    def _():
        acc_scr[...] = jnp.zeros_like(acc_scr)

    wide = 2 * LANES

    def gate_block(il, nc):
        erows = pl.ds(il * N_KEYS, N_KEYS)
        hblk = _mxu_dot(u_ref[erows, :], xn_ref[:, pl.ds(nc * wide, wide)])
        for lw in range(wide // LANES):
            lc = nc * (wide // LANES) + lw
            lanes = pl.ds(lc * LANES, LANES)
            c1b = [jnp.broadcast_to(c1_ref[pl.ds(il * nh + h, 1), lanes], (pack, LANES)).astype(BF16)
                   for h in range(nh)]
            e1b = [jnp.broadcast_to(e1_ref[pl.ds(il * nh + h, 1), lanes], (pack, LANES)).astype(BF16)
                   for h in range(nh)]
            zero = jnp.zeros((pack, LANES), BF16)
            half = jnp.full((), 0.5, BF16)
            gelu_c0 = jnp.full((), GELU_C0, BF16)
            gelu_c1 = jnp.full((), GELU_C1, BF16)
            for it in range(N_KEYS // pack):
                rows = pl.ds(il * N_KEYS + it * pack, pack)
                gate = None
                for h in range(nh):
                    krows = pl.ds(h * N_KEYS + it * pack, pack)
                    sel = jnp.minimum(jnp.maximum(c1b[h] - r2_ref[krows, lanes], zero), e2_ref[krows, lanes])
                    term = sel * e1b[h]
                    gate = term if gate is None else gate + term
                x = hblk[it * pack:(it + 1) * pack, lw * LANES:(lw + 1) * LANES].astype(BF16)
                t = jnp.tanh(x * (gelu_c0 + gelu_c1 * (x * x)))
                act_scr[rows, lanes] = (x * (half + half * t)) * gate

    for nc in range(tn // wide):
        for il in range(i1_per_blk):
            gate_block(il, nc)
        cols = pl.ds(nc * wide, wide)
        acc_scr[:, cols] += _mxu_dot(vt_ref[...], act_scr[:, cols])

    @pl.when(j == pl.num_programs(1) - 1)
    def _():
        y_ref[...] = (xt_ref[...] + acc_scr[...]).T


def peer_dense(xt, xn_t, u_layers, layer, vt_bf16, c1, e1, r2, e2):
    d, n = xt.shape
    ne = u_layers.shape[1]
    tn = _pick(n, (512, 256, 128))
    te = PEER_EXPERT_BLOCK
    i1_per_blk = te // N_KEYS
    tok = lambda i, j: (0, i)
    return pl.pallas_call(
        _peer_dense_kernel,
        grid=(n // tn, ne // te),
        in_specs=[
            pl.BlockSpec((d, tn), tok),
            pl.BlockSpec((d, tn), tok),
            pl.BlockSpec((None, te, d), lambda i, j: (layer, j, 0)),
            pl.BlockSpec((d, te), lambda i, j: (0, j)),
            pl.BlockSpec((i1_per_blk * PEER_HEADS, tn), lambda i, j: (j, i)),
            pl.BlockSpec((i1_per_blk * PEER_HEADS, tn), lambda i, j: (j, i)),
            pl.BlockSpec((PEER_HEADS * N_KEYS, tn), tok),
            pl.BlockSpec((PEER_HEADS * N_KEYS, tn), tok),
        ],
        out_specs=pl.BlockSpec((tn, d), lambda i, j: (i, 0)),
        out_shape=jax.ShapeDtypeStruct((n, d), F32),
        scratch_shapes=[pltpu.VMEM((d, tn), F32), pltpu.VMEM((te, tn), BF16)],
        compiler_params=_params("parallel", "arbitrary"),
        name="peer_dense",
    )(xt, xn_t, u_layers, vt_bf16, c1, e1, r2, e2)


def _interleaved_keys(keys_p):
    nh, nk, half = keys_p.shape
    eye = jnp.eye(nh, dtype=keys_p.dtype)
    return jnp.einsum("hkd,hg->khgd", keys_p, eye).reshape(nk * nh, nh * half).astype(BF16)


def peer_layer(xt, norm_g, wq, keys, u_layers, layer, v_tab):
    d = xt.shape[0]
    wq_t = wq.reshape(d, PEER_HEADS, 2, PEER_HALF).transpose(2, 1, 3, 0).reshape(2 * PEER_HEADS * PEER_HALF, d)
    xn_t, q_t = peer_query(xt, norm_g, wq_t.astype(BF16))
    k1 = _interleaved_keys(keys[:, 0])
    k2 = _interleaved_keys(keys[:, 1])
    c1, e1, r2, e2 = peer_select(q_t, k1, k2)
    return peer_dense(xt, xn_t, u_layers, layer, v_tab.T.astype(BF16), c1, e1, r2, e2)


def _rmsnorm_kernel(x_ref, g_ref, y_ref):
    x = x_ref[...]
    ms = jnp.mean(x * x, axis=-1, keepdims=True)
    y_ref[...] = (x * lax.rsqrt(ms + EPS)) * g_ref[...]


def final_norm(x, row0, rows, g):
    d = x.shape[1]
    tm = _pick(math.gcd(rows, row0) if row0 else rows, (512, 256, 128, 64, 32, 16, 8))
    blk0 = row0 // tm
    return pl.pallas_call(
        _rmsnorm_kernel,
        grid=(rows // tm,),
        in_specs=[pl.BlockSpec((tm, d), lambda i: (blk0 + i, 0)), pl.BlockSpec((1, d), lambda i: (0, 0))],
        out_specs=pl.BlockSpec((tm, d), lambda i: (i, 0)),
        out_shape=jax.ShapeDtypeStruct((rows, d), F32),
        compiler_params=_params("parallel"),
        name="final_norm",
    )(x, g.reshape(1, d))


def _trunk(groups, norm1_g, norm2_g, normf_g, w_in, ret_gn_g, w_ret_out, conv_w, conv_b, rg_wa, rg_ba, rg_wx,
           rg_bx, rg_lambda, w_rnn_out, w_o, peer_wq, peer_keys, peer_u, peer_v):
    shapes = [(g[0].shape[0], g[0].shape[1]) for g in groups]
    x_parts = [g[0].reshape(-1, D_MODEL) for g in groups]
    row0s = np.cumsum([0] + [b * t for b, t in shapes]).tolist()
    states = [([], [], []) for _ in groups]
    for l in range(DEPTH):
        p = norm_matmul(x_parts, norm1_g[l], w_in[l].astype(BF16))
        wa4 = _block_diag_tiles(rg_wa[l], 256)
        wx4 = _block_diag_tiles(rg_wx[l], 256)
        ret_parts, rnn_parts = [], []
        for gi, (xg, r0, h0, buf0, pos0) in enumerate(groups):
            b, t = shapes[gi]
            row0 = row0s[gi]
            o, r_new = retention_path(p, row0, b, t, pos0, r0, l, ret_gn_g[l])
            rg_args = (conv_w[l], conv_b[l], wa4, rg_ba[l], wx4, rg_bx[l], rg_lambda[l])
            hg, h_last, nb = rglru_rows_path(p, row0, b, t, buf0[l], h0[l], *rg_args)
            ret_parts.append(o)
            rnn_parts.append(hg)
            states[gi][0].append(r_new)
            states[gi][1].append(h_last)
            states[gi][2].append(nb)
        xt = merge_proj(x_parts, ret_parts, rnn_parts, p, w_ret_out[l].astype(BF16), w_rnn_out[l].astype(BF16),
                        w_o[l].astype(BF16))
        x = peer_layer(xt, norm2_g[l], peer_wq[l], peer_keys[l], peer_u, l, peer_v[l])
        x_parts = [x]
    outs = []
    for gi, (b, t) in enumerate(shapes):
        y = final_norm(x, row0s[gi], b * t, normf_g).reshape(b, t, D_MODEL)
        outs.append((y, jnp.stack(states[gi][0]), jnp.stack(states[gi][1]), jnp.stack(states[gi][2])))
    return outs


def kernel(x_prompt, x_sample, state_ret, state_rnn, state_conv, norm1_g, norm2_g, normf_g, w_in, ret_gn_g,
           w_ret_out, conv_w, conv_b, rg_wa, rg_ba, rg_wx, rg_bx, rg_lambda, w_rnn_out, w_o, peer_wq, peer_keys,
           peer_u, peer_v):
    bp = x_prompt.shape[0]
    dt = x_prompt.dtype
    zr = jnp.zeros((DEPTH, bp, RET_HEADS, RET_DK, RET_DV), dt)
    zh = jnp.zeros((DEPTH, bp, D_RNN), dt)
    zc = jnp.zeros((DEPTH, bp, CONV_W - 1, D_RNN), dt)
    groups = [(x_prompt, zr, zh, zc, 0.0), (x_sample, state_ret, state_rnn, state_conv, float(PAST_LEN))]
    (yp, rp, hp, cp), (ys, rs, hs, cs) = _trunk(
        groups, norm1_g, norm2_g, normf_g, w_in, ret_gn_g, w_ret_out, conv_w, conv_b, rg_wa, rg_ba, rg_wx, rg_bx,
        rg_lambda, w_rnn_out, w_o, peer_wq, peer_keys, peer_u, peer_v)
    return (yp, ys, rp, hp, cp, rs, hs, cs)
```

```python
import functools
import math

import jax
import jax.numpy as jnp
import numpy as np
from jax import lax
from jax.experimental import pallas as pl
from jax.experimental.pallas import tpu as pltpu

D_MODEL = 1024
DEPTH = 2
PAST_LEN = 16384
RET_HEADS = 8
RET_DK = 64
RET_DV = 128
RET_QK = RET_HEADS * RET_DK
RET_V = RET_HEADS * RET_DV
RET_CHUNK = 128
ROPE_BASE = 10000.0
D_RNN = 1024
CONV_W = 4
RG_C = 8.0
COL_V, COL_GRET, COL_XR, COL_GRNN, COL_GA, COL_GB = 1, 2, 3, 4, 5, 6
PEER_HEADS = 8
N_KEYS = 128
PEER_DKEY = 256
PEER_HALF = PEER_DKEY // 2
PEER_TOPK = 16
EPS = 1e-6

SUBLANES = 8
LANES = 128
VMEM_LIMIT = 56 * 1024 * 1024

F32 = jnp.float32
BF16 = jnp.bfloat16


def _params(*sem):
    return pltpu.CompilerParams(dimension_semantics=sem, vmem_limit_bytes=VMEM_LIMIT)


def _pick(n, prefs):
    for p in prefs:
        if n % p == 0:
            return p
    return n


def _row_part_maps(parts, tm):
    assert len(parts) in (1, 2) and all(p.shape[0] % tm == 0 for p in parts)
    blocks_a = parts[0].shape[0] // tm
    maps = [lambda i, *_: (jnp.minimum(i, blocks_a - 1), 0)]
    if len(parts) == 2:
        maps.append(lambda i, *_: (jnp.maximum(i - blocks_a, 0), 0))
    return blocks_a, maps


def _pick_rows(refs, blocks_a):
    x = refs[0][...]
    if len(refs) == 2:
        x = jnp.where(pl.program_id(0) < blocks_a, x, refs[1][...])
    return x


def _norm_matmul_kernel(blocks_a, nparts, *refs):
    x_refs = refs[:nparts]
    g_ref, w_ref, y_ref, xn_scr = refs[nparts:]

    @pl.when(pl.program_id(1) == 0)
    def _():
        x = _pick_rows(x_refs, blocks_a)
        ms = jnp.mean(x * x, axis=-1, keepdims=True)
        xn_scr[...] = ((x * lax.rsqrt(ms + EPS)) * g_ref[...]).astype(BF16)

    y_ref[...] = jnp.dot(xn_scr[...], w_ref[...], preferred_element_type=F32)


def norm_matmul(x_parts, g, w_bf16):
    n = sum(p.shape[0] for p in x_parts)
    d = x_parts[0].shape[1]
    m = w_bf16.shape[1]
    tm = _pick(math.gcd(*[p.shape[0] for p in x_parts]), (1024, 512, 256, 128))
    tn = _pick(m, (1792, 1024, 512, 256, 128))
    blocks_a, row_maps = _row_part_maps(x_parts, tm)
    return pl.pallas_call(
        functools.partial(_norm_matmul_kernel, blocks_a, len(x_parts)),
        grid=(n // tm, m // tn),
        in_specs=[pl.BlockSpec((tm, d), rm) for rm in row_maps] + [
            pl.BlockSpec((1, d), lambda i, j: (0, 0)),
            pl.BlockSpec((d, tn), lambda i, j: (0, j)),
        ],
        out_specs=pl.BlockSpec((tm, tn), lambda i, j: (i, j)),
        out_shape=jax.ShapeDtypeStruct((n, m), F32),
        scratch_shapes=[pltpu.VMEM((tm, d), BF16)],
        compiler_params=_params("parallel", "arbitrary"),
        name="norm_in_proj",
    )(*x_parts, g.reshape(1, d), w_bf16)


def _rot_half(x):
    n = x.shape[-1]
    half = RET_DK // 2
    fwd = pltpu.roll(x, half, axis=1)
    bwd = pltpu.roll(x, n - half, axis=1)
    lane = lax.broadcasted_iota(jnp.int32, x.shape, 1)
    return jnp.where((lane % RET_DK) < half, bwd, fwd)


def _retention_kernel(q_ref, k_ref, v_ref, g_ref, r0_ref, cos_ref, sin_ref, mask_ref, qw_ref, kw_ref,
                      gc_ref, gn_ref, o_ref, r_out_ref, r_scr):
    c = pl.program_id(1)
    rows_c = mask_ref.shape[1]
    nb = q_ref.shape[0] // rows_c

    @pl.when(c == 0)
    def _():
        r_scr[...] = r0_ref[...]

    cos = cos_ref[...]
    sin = sin_ref[...]
    q = q_ref[...]
    k = k_ref[...]
    qr = q * cos + _rot_half(q) * sin
    kr = (k * cos + _rot_half(k) * sin) * (RET_DK ** -0.5)
    qd = (qr * qw_ref[...]).astype(BF16)
    kd = (kr * kw_ref[...]).astype(BF16)
    qb = qr.astype(BF16)
    kb = kr.astype(BF16)
    v = v_ref[...].astype(BF16)
    g = g_ref[...]
    for bb in range(nb):
        rs = slice(bb * rows_c, (bb + 1) * rows_c)
        for h in range(RET_HEADS):
            ks = slice(h * RET_DK, (h + 1) * RET_DK)
            vs = slice(h * RET_DV, (h + 1) * RET_DV)
            vh = v[rs, vs]
            s = lax.dot_general(qb[rs, ks], kb[rs, ks], (((1,), (1,)), ((), ())),
                                preferred_element_type=F32) * mask_ref[h]
            r_h = r_scr[bb, h]
            o = jnp.dot(s.astype(BF16), vh, preferred_element_type=F32)
            o = o + jnp.dot(qd[rs, ks], r_h.astype(BF16), preferred_element_type=F32)
            kv = lax.dot_general(kd[rs, ks], vh, (((0,), (0,)), ((), ())), preferred_element_type=F32)
            r_scr[bb, h] = gc_ref[h] * r_h + kv
            mu = jnp.mean(o, axis=-1, keepdims=True)
            var = jnp.mean(jnp.square(o - mu), axis=-1, keepdims=True)
            on = ((o - mu) * lax.rsqrt(var + EPS)) * gn_ref[:, vs]
            gh = g[rs, vs]
            o_ref[rs, vs] = ((gh * jax.nn.sigmoid(gh)) * on).astype(BF16)

    @pl.when(c == pl.num_programs(1) - 1)
    def _():
        r_out_ref[...] = r_scr[...]


def retention_path(p, row0, b, t, pos0, r0_layers, layer, gn_g):
    c = RET_CHUNK if t % RET_CHUNK == 0 else t
    nc = t // c
    blk0 = row0 // c
    assert row0 % c == 0
    log_g = jnp.log1p(-(2.0 ** (-5.0 - jnp.arange(RET_HEADS, dtype=F32))))
    idx = jnp.arange(c, dtype=F32)
    diff = idx[:, None] - idx[None, :]
    mask = jnp.where(diff[None] >= 0, jnp.exp(jnp.maximum(diff, 0.0)[None] * log_g[:, None, None]), 0.0)
    k_w = jnp.exp((c - 1 - idx)[:, None] * log_g[None, :])
    q_w = jnp.exp((idx + 1.0)[:, None] * log_g[None, :])
    g_c = jnp.exp(c * log_g)
    qw_tab = jnp.repeat(q_w, RET_DK, axis=1)
    kw_tab = jnp.repeat(k_w, RET_DK, axis=1)
    gc_tab = jnp.broadcast_to(g_c[:, None, None], (RET_HEADS, 1, RET_DV))
    half = RET_DK // 2
    pos = pos0 + jnp.arange(t, dtype=F32)
    freq = ROPE_BASE ** (-jnp.arange(half, dtype=F32) / half)
    ang = pos[:, None] * freq[None, :]
    cos_h = jnp.concatenate([jnp.cos(ang), jnp.cos(ang)], axis=1)
    sin_h = jnp.concatenate([-jnp.sin(ang), jnp.sin(ang)], axis=1)
    cos_tab = jnp.tile(cos_h, (1, RET_HEADS))
    sin_tab = jnp.tile(sin_h, (1, RET_HEADS))
    nb = _pick(b, (8, 4, 2, 1)) if nc == 1 and c * 8 <= RET_CHUNK else 1
    if nb > 1:
        cos_tab, sin_tab, qw_tab, kw_tab = [jnp.tile(tab, (nb, 1)) for tab in (cos_tab, sin_tab, qw_tab, kw_tab)]
    rb = nb * c
    blk0 = row0 // rb
    assert row0 % rb == 0

    rows = lambda bi, ci: blk0 + bi * nc + ci
    state_spec = pl.BlockSpec((nb, RET_HEADS, RET_DK, RET_DV), lambda bi, ci: (bi, 0, 0, 0))
    o, r_new = pl.pallas_call(
        _retention_kernel,
        grid=(b // nb, nc),
        in_specs=[
            pl.BlockSpec((rb, RET_QK), lambda bi, ci: (rows(bi, ci), 0)),
            pl.BlockSpec((rb, RET_QK), lambda bi, ci: (rows(bi, ci), 1)),
            pl.BlockSpec((rb, RET_V), lambda bi, ci: (rows(bi, ci), COL_V)),
            pl.BlockSpec((rb, RET_V), lambda bi, ci: (rows(bi, ci), COL_GRET)),
            pl.BlockSpec((None, nb, RET_HEADS, RET_DK, RET_DV), lambda bi, ci: (layer, bi, 0, 0, 0)),
            pl.BlockSpec((rb, RET_QK), lambda bi, ci: (ci, 0)),
            pl.BlockSpec((rb, RET_QK), lambda bi, ci: (ci, 0)),
            pl.BlockSpec((RET_HEADS, c, c), lambda bi, ci: (0, 0, 0)),
            pl.BlockSpec((rb, RET_QK), lambda bi, ci: (0, 0)),
            pl.BlockSpec((rb, RET_QK), lambda bi, ci: (0, 0)),
            pl.BlockSpec((RET_HEADS, 1, RET_DV), lambda bi, ci: (0, 0, 0)),
            pl.BlockSpec((1, RET_V), lambda bi, ci: (0, 0)),
        ],
        out_specs=[
            pl.BlockSpec((rb, RET_V), lambda bi, ci: (bi * nc + ci, 0)),
            state_spec,
        ],
        out_shape=[
            jax.ShapeDtypeStruct((b * t, RET_V), BF16),
            jax.ShapeDtypeStruct((b, RET_HEADS, RET_DK, RET_DV), F32),
        ],
        scratch_shapes=[pltpu.VMEM((nb, RET_HEADS, RET_DK, RET_DV), F32)],
        compiler_params=_params("parallel", "arbitrary"),
        name="retention",
    )(p, p, p, p, r0_layers, cos_tab, sin_tab, mask, qw_tab, kw_tab, gc_tab, gn_g.reshape(1, RET_V))
    return o, r_new


GELU_C0 = math.sqrt(2.0 / math.pi)
GELU_C1 = GELU_C0 * 0.044715


def _gelu_tanh(x):
    return x * (0.5 * (1.0 + jnp.tanh(math.sqrt(2.0 / math.pi) * (x + 0.044715 * (x * x * x)))))


def _rglru_gates(xc, wa_ref, ba_ref, wx_ref, bx_ref, lam_ref):
    xcb = xc.astype(BF16)
    nblk = wa_ref.shape[0]
    wdt = wa_ref.shape[1]
    ra = jnp.concatenate(
        [jnp.dot(xcb[:, j * wdt:(j + 1) * wdt], wa_ref[j], preferred_element_type=F32) for j in range(nblk)],
        axis=1)
    ri = jnp.concatenate(
        [jnp.dot(xcb[:, j * wdt:(j + 1) * wdt], wx_ref[j], preferred_element_type=F32) for j in range(nblk)],
        axis=1)
    r = jax.nn.sigmoid(ra + ba_ref[...])
    i = jax.nn.sigmoid(ri + bx_ref[...])
    z = -lam_ref[...]
    softplus = jnp.maximum(z, 0.0) + jnp.log1p(jnp.exp(-jnp.abs(z)))
    log_a = (-RG_C * r) * softplus
    a = jnp.exp(log_a)
    one_m_a2 = -jnp.tanh(log_a) * (a * a + 1.0)
    return a, jnp.sqrt(one_m_a2) * (i * xc)


def _rglru_rows_kernel(bsz, tc, nin, chained, *refs):
    xr_refs, g_refs = refs[:nin], refs[nin:2 * nin]
    (buf_ref, h0_ref, cw_ref, cb_ref, wa_ref, ba_ref, wx_ref, bx_ref, lam_ref, o_ref, hl_ref, nb_ref,
     xcat_scr, a_scr, u_scr, hs_scr, h_scr) = refs[2 * nin:]
    step = pl.program_id(0)
    rows = tc * bsz
    hist = (CONV_W - 1) * bsz
    bpi = bsz // nin

    def time_major(x):
        return jnp.swapaxes(x, 0, 1).reshape(x.shape[0] * x.shape[1], D_RNN)

    def batch_major(blocks):
        return jnp.concatenate([blk[...].reshape(bpi, tc, D_RNN) for blk in blocks], axis=0)

    def load_state():
        xcat_scr[pl.ds(rows, hist), :] = time_major(buf_ref[...])
        h_scr[...] = h0_ref[...]

    if chained:
        pl.when(step == 0)(load_state)
    else:
        load_state()

    xcat_scr[pl.ds(0, hist), :] = xcat_scr[pl.ds(rows, hist), :]
    xcat_scr[pl.ds(hist, rows), :] = time_major(batch_major(xr_refs))
    xc = cb_ref[...] + xcat_scr[pl.ds(0, rows), :] * cw_ref[0:1, :]
    for w in range(1, CONV_W):
        xc = xc + xcat_scr[pl.ds(w * bsz, rows), :] * cw_ref[w:w + 1, :]
    a, u = _rglru_gates(xc, wa_ref, ba_ref, wx_ref, bx_ref, lam_ref)
    a_scr[...] = a.reshape(tc, bsz, D_RNN)
    u_scr[...] = u.reshape(tc, bsz, D_RNN)

    def scan_step(t, h):
        h = a_scr[t] * h + u_scr[t]
        hs_scr[t] = h
        return h

    h_scr[...] = lax.fori_loop(0, tc, scan_step, h_scr[...])
    gate = _gelu_tanh(batch_major(g_refs))
    o_ref[...] = (jnp.swapaxes(hs_scr[...], 0, 1) * gate).astype(BF16)

    def store_state():
        hl_ref[...] = h_scr[...]
        nb_ref[...] = jnp.swapaxes(xcat_scr[pl.ds(rows, hist), :].reshape(CONV_W - 1, bsz, D_RNN), 0, 1)

    if chained:
        pl.when(step == pl.num_programs(0) - 1)(store_state)
    else:
        store_state()


RGLRU_STEP_ROWS = 256


def rglru_rows_path(p, row0, b, t, buf0, h0, conv_w, conv_b, wa4, ba, wx4, bx, lam):
    assert t >= CONV_W - 1 and b % SUBLANES == 0 and RGLRU_STEP_ROWS % SUBLANES == 0
    d = D_RNN
    hist = CONV_W - 1
    nblk, wdt = wa4.shape[0], wa4.shape[1]
    chained = b * t > RGLRU_STEP_ROWS and b == SUBLANES
    if chained:
        bg, tc, nin = b, RGLRU_STEP_ROWS // b, b
        steps = t // tc
        assert t % tc == 0 and row0 % tc == 0
        blk0 = row0 // tc
        row_map = lambda bi, col: (lambda s: (blk0 + bi * steps + s, col))
        state2, state3 = (lambda s: (0, 0)), (lambda s: (0, 0, 0))
        out_map = lambda s: (0, s, 0)
    else:
        bg, tc, nin = max(SUBLANES, RGLRU_STEP_ROWS // t), t, 1
        steps = b // bg
        assert b % bg == 0 and row0 % (bg * t) == 0
        blk0 = row0 // (bg * t)
        row_map = lambda bi, col: (lambda s: (blk0 + s, col))
        state2, state3 = (lambda s: (s, 0)), (lambda s: (s, 0, 0))
        out_map = lambda s: (s, 0, 0)
    in_rows = (bg // nin) * tc
    const2 = lambda s: (0, 0)
    const3 = lambda s: (0, 0, 0)
    out, h_last, new_buf = pl.pallas_call(
        functools.partial(_rglru_rows_kernel, bg, tc, nin, chained),
        grid=(steps,),
        in_specs=[pl.BlockSpec((in_rows, d), row_map(bi, COL_XR)) for bi in range(nin)]
        + [pl.BlockSpec((in_rows, d), row_map(bi, COL_GRNN)) for bi in range(nin)]
        + [
            pl.BlockSpec((bg, hist, d), state3),
            pl.BlockSpec((bg, d), state2),
            pl.BlockSpec((CONV_W, d), const2),
            pl.BlockSpec((1, d), const2),
            pl.BlockSpec((nblk, wdt, wdt), const3),
            pl.BlockSpec((1, d), const2),
            pl.BlockSpec((nblk, wdt, wdt), const3),
            pl.BlockSpec((1, d), const2),
            pl.BlockSpec((1, d), const2),
        ],
        out_specs=[
            pl.BlockSpec((bg, tc, d), out_map),
            pl.BlockSpec((bg, d), state2),
            pl.BlockSpec((bg, hist, d), state3),
        ],
        out_shape=[
            jax.ShapeDtypeStruct((b, t, d), BF16),
            jax.ShapeDtypeStruct((b, d), F32),
            jax.ShapeDtypeStruct((b, hist, d), F32),
        ],
        scratch_shapes=[
            pltpu.VMEM(((CONV_W - 1 + tc) * bg, d), F32),
            pltpu.VMEM((tc, bg, d), F32),
            pltpu.VMEM((tc, bg, d), F32),
            pltpu.VMEM((tc, bg, d), F32),
            pltpu.VMEM((bg, d), F32),
        ],
        compiler_params=_params("arbitrary"),
        name="rglru_rows",
    )(*([p] * (2 * nin)), buf0, h0, conv_w, conv_b.reshape(1, d), wa4, ba.reshape(1, d), wx4, bx.reshape(1, d),
      lam.reshape(1, d))
    return out.reshape(b * t, d), h_last, new_buf


def _block_diag_tiles(w, tile):
    nb, bs, _ = w.shape
    per = tile // bs
    w = w.reshape(nb // per, per, bs, bs)
    eye = jnp.eye(per, dtype=w.dtype)
    dense = jnp.einsum("gpcd,pq->gpcqd", w, eye).reshape(nb // per, tile, tile)
    return dense.astype(BF16)


def _merge_kernel(blocks_a, nx, *refs):
    x_refs, refs = refs[:nx], refs[nx:]
    ro_refs, rn_refs = refs[0:2], refs[2:4]
    ga_ref, gb_ref, wr_ref, wn_ref, wo_ref, g2_ref, wq_ref, yt_ref, xn_ref, q_ref = refs[4:]
    ret_out = jnp.dot(_pick_rows(ro_refs, blocks_a), wr_ref[...], preferred_element_type=F32)
    rnn_out = jnp.dot(_pick_rows(rn_refs, blocks_a), wn_ref[...], preferred_element_type=F32)
    merged = jax.nn.sigmoid(ga_ref[...]) * ret_out + jax.nn.sigmoid(gb_ref[...]) * rnn_out
    y = _pick_rows(x_refs, blocks_a) + jnp.dot(merged.astype(BF16), wo_ref[...], preferred_element_type=F32)
    yt = y.T
    yt_ref[...] = yt
    ms = jnp.mean(yt * yt, axis=0, keepdims=True)
    xn = ((yt * lax.rsqrt(ms + EPS)) * g2_ref[...]).astype(BF16)
    xn_ref[...] = xn
    q_ref[...] = jnp.dot(wq_ref[...], xn, preferred_element_type=F32)


def merge_proj(x_parts, ret_parts, rnn_parts, p, w_ret_out, w_rnn_out, w_o, norm2_g, wq_t):
    n, d = p.shape[0], x_parts[0].shape[1]
    m = wq_t.shape[0]
    tm = _pick(math.gcd(*[r.shape[0] for r in ret_parts]), (512, 256, 128))
    blocks_a, part_maps = _row_part_maps(ret_parts, tm)
    assert len(x_parts) == 1 or x_parts[0].shape[0] == ret_parts[0].shape[0]
    _, x_maps = _row_part_maps(x_parts, tm) if len(x_parts) == 2 else (None, [lambda i: (i, 0)])
    const = lambda i: (0, 0)
    tok = lambda i: (0, i)
    return pl.pallas_call(
        functools.partial(_merge_kernel, blocks_a, len(x_parts)),
        grid=(n // tm,),
        in_specs=[pl.BlockSpec((tm, d), mp) for mp in x_maps + part_maps + part_maps] + [
            pl.BlockSpec((tm, d), lambda i: (i, COL_GA)),
            pl.BlockSpec((tm, d), lambda i: (i, COL_GB)),
            pl.BlockSpec((d, d), const),
            pl.BlockSpec((d, d), const),
            pl.BlockSpec((d, d), const),
            pl.BlockSpec((d, 1), const),
            pl.BlockSpec((m, d), const),
        ],
        out_specs=[pl.BlockSpec((d, tm), tok), pl.BlockSpec((d, tm), tok), pl.BlockSpec((m, tm), tok)],
        out_shape=[jax.ShapeDtypeStruct((d, n), F32), jax.ShapeDtypeStruct((d, n), BF16),
                   jax.ShapeDtypeStruct((m, n), F32)],
        compiler_params=_params("parallel"),
        name="merge_out_proj",
    )(*x_parts, *ret_parts, *rnn_parts, p, p, w_ret_out, w_rnn_out, w_o, norm2_g.reshape(d, 1), wq_t)


def _sort_pairs(n):
    def merge(lo, hi, r):
        step = r * 2
        if step < hi - lo:
            yield from merge(lo, hi, step)
            yield from merge(lo + r, hi, step)
            yield from [(i, i + r) for i in range(lo + r, hi - r, step)]
        else:
            yield (lo, lo + r)

    def sort(lo, hi):
        if hi - lo >= 1:
            mid = lo + (hi - lo) // 2
            yield from sort(lo, mid)
            yield from sort(mid + 1, hi)
            yield from merge(lo, hi, 1)

    return list(sort(0, n - 1))


_SORT16 = _sort_pairs(PEER_TOPK)


def _cmpx(vals, i, j):
    a, b = vals[i], vals[j]
    if b is None:
        return
    if a is None:
        vals[i], vals[j] = b, None
        return
    vals[i], vals[j] = jnp.maximum(a, b), jnp.minimum(a, b)


def _sort_desc(vals):
    vals = list(vals)
    for i, j in _SORT16:
        _cmpx(vals, i, j)
    return vals


def _merge_top(a, b):
    k = PEER_TOPK
    a = list(a) + [None] * (k - len(a))
    b = list(b) + [None] * (k - len(b))
    out = []
    for r in range(k):
        x, y = a[r], b[k - 1 - r]
        out.append(y if x is None else (x if y is None else jnp.maximum(x, y)))
    d = k // 2
    while d >= 1:
        for i in range(k):
            if not i & d:
                _cmpx(out, i, i + d)
        d //= 2
    return out


def _top_sorted(ref):
    groups = []
    for g0 in range(0, N_KEYS, PEER_TOPK):
        vals = [ref[pl.ds((g0 + j) * PEER_HEADS, PEER_HEADS), :] for j in range(PEER_TOPK)]
        groups.append(_sort_desc(vals))
    while len(groups) > 1:
        groups = [_merge_top(groups[i], groups[i + 1]) for i in range(0, len(groups), 2)]
    return groups[0]


def _peer_select_kernel(q_ref, k1_ref, k2_ref, c1_ref, e1_ref, r2_ref, e2_ref, s1_scr, s2_scr, r2_scr, e2_scr):
    nh = PEER_HEADS
    half_rows = q_ref.shape[0] // 2
    tn = q_ref.shape[1]
    s1 = jnp.dot(k1_ref[...], q_ref[pl.ds(0, half_rows), :].astype(BF16), preferred_element_type=F32)
    s2 = jnp.dot(k2_ref[...], q_ref[pl.ds(half_rows, half_rows), :].astype(BF16), preferred_element_type=F32)
    for lt in range(tn // LANES):
        s1_scr[lt] = s1[:, lt * LANES:(lt + 1) * LANES]
        s2_scr[lt] = s2[:, lt * LANES:(lt + 1) * LANES]
    for lt in range(tn // LANES):
        lanes = pl.ds(lt * LANES, LANES)
        s1_t, s2_t, r2_t, e2_t = s1_scr.at[lt], s2_scr.at[lt], r2_scr.at[lt], e2_scr.at[lt]
        a = _top_sorted(s1_t)
        b = _top_sorted(s2_t)
        k = PEER_TOPK
        lists = []
        for j in range(1, k + 1):
            col = [a[r - 1] + b[j - 1] for r in range(j, k // j + 1)]
            row = [a[j - 1] + b[s - 1] for s in range(j + 1, k // j + 1)]
            if col:
                lists.append(col)
            if row:
                lists.append(row)
        top = lists[0]
        for other in lists[1:]:
            top = _merge_top(top, other)
        tau = top[k - 1]
        z = jnp.ones_like(tau)
        for r in range(1, k):
            z = z + jnp.exp(top[r] - top[0])
        zinv = 1.0 / z
        inf = jnp.full((nh, LANES), jnp.inf, F32)
        phi = []
        for s in range(1, k + 1):
            p = inf
            for r in range(1, k // s + 1):
                p = jnp.where(a[r - 1] + b[s - 1] >= tau, a[r - 1], p)
            phi.append(p)

        def per_key(kk, carry):
            rows = pl.ds(pl.multiple_of(kk * nh, nh), nh)
            s1k = s1_t[rows, :]
            s2k = s2_t[rows, :]
            cnt = jnp.ones((nh, LANES), F32)
            for s in range(k):
                cnt = jnp.where(s1k >= phi[s], float(s + 2), cnt)
            rank = jnp.full((nh, LANES), float(k + 1), F32)
            for s in range(k - 1, -1, -1):
                rank = jnp.where(s2k >= b[s], float(s + 1), rank)
            c1_ref[rows, lanes] = cnt
            e1_ref[rows, lanes] = jnp.exp(s1k - a[0])
            r2_t[rows, :] = rank
            e2_t[rows, :] = jnp.exp(s2k - b[0]) * zinv
            return carry

        lax.fori_loop(0, N_KEYS, per_key, 0, unroll=2)
        pack = 2 * SUBLANES
        for h in range(nh):
            for kt in range(N_KEYS // pack):
                lo = pl.ds(kt * pack * nh + h, SUBLANES, stride=nh)
                hi = pl.ds((kt * pack + SUBLANES) * nh + h, SUBLANES, stride=nh)
                dst = pl.ds(h * N_KEYS + kt * pack, pack)
                r2_ref[dst, lanes] = jnp.concatenate([r2_t[lo, :], r2_t[hi, :]], axis=0).astype(BF16)
                e2_ref[dst, lanes] = jnp.concatenate([e2_t[lo, :], e2_t[hi, :]], axis=0).astype(BF16)


def peer_select(q_t, k1, k2):
    m, n = q_t.shape
    rows = PEER_HEADS * N_KEYS
    tn = _pick(n, (256, 128))
    tok = lambda i: (0, i)
    const = lambda i: (0, 0)
    return pl.pallas_call(
        _peer_select_kernel,
        grid=(n // tn,),
        in_specs=[pl.BlockSpec((m, tn), tok), pl.BlockSpec(k1.shape, const), pl.BlockSpec(k2.shape, const)],
        out_specs=[pl.BlockSpec((rows, tn), tok)] * 4,
        out_shape=[jax.ShapeDtypeStruct((rows, n), F32), jax.ShapeDtypeStruct((rows, n), F32),
                   jax.ShapeDtypeStruct((rows, n), BF16), jax.ShapeDtypeStruct((rows, n), BF16)],
        scratch_shapes=[pltpu.VMEM((tn // LANES, rows, LANES), F32)] * 4,
        compiler_params=_params("parallel"),
        name="peer_select",
    )(q_t, k1, k2)


PEER_EXPERT_BLOCK = 1024


def _mxu_dot(lhs, rhs):
    return lax.dot_general(lhs, rhs, (((1,), (0,)), ((), ())), preferred_element_type=F32)


def _peer_dense_kernel(xt_ref, xn_ref, u_ref, vt_ref, c1_ref, e1_ref, r2_ref, e2_ref, y_ref,
                       acc_scr, act_scr):
    j = pl.program_id(1)
    nh = PEER_HEADS
    tn = xn_ref.shape[1]
    pack = 2 * SUBLANES
    i1_per_blk = u_ref.shape[0] // N_KEYS

    @pl.when(j == 0)
    def _():
        acc_scr[...] = jnp.zeros_like(acc_scr)

    wide = 2 * LANES

    def gate_block(il, nc):
        erows = pl.ds(il * N_KEYS, N_KEYS)
        hblk = _mxu_dot(u_ref[erows, :], xn_ref[:, pl.ds(nc * wide, wide)])
        for lw in range(wide // LANES):
            lc = nc * (wide // LANES) + lw
            lanes = pl.ds(lc * LANES, LANES)
            c1b = [jnp.broadcast_to(c1_ref[pl.ds(il * nh + h, 1), lanes], (pack, LANES)).astype(BF16)
                   for h in range(nh)]
            e1b = [jnp.broadcast_to(e1_ref[pl.ds(il * nh + h, 1), lanes], (pack, LANES)).astype(BF16)
                   for h in range(nh)]
            zero = jnp.zeros((pack, LANES), BF16)
            half = jnp.full((), 0.5, BF16)
            gelu_c0 = jnp.full((), GELU_C0, BF16)
            gelu_c1 = jnp.full((), GELU_C1, BF16)
            for it in range(N_KEYS // pack):
                rows = pl.ds(il * N_KEYS + it * pack, pack)
                gate = None
                for h in range(nh):
                    krows = pl.ds(h * N_KEYS + it * pack, pack)
                    sel = jnp.minimum(jnp.maximum(c1b[h] - r2_ref[krows, lanes], zero), e2_ref[krows, lanes])
                    term = sel * e1b[h]
                    gate = term if gate is None else gate + term
                x = hblk[it * pack:(it + 1) * pack, lw * LANES:(lw + 1) * LANES].astype(BF16)
                t = jnp.tanh(x * (gelu_c0 + gelu_c1 * (x * x)))
                act_scr[rows, lanes] = (x * (half + half * t)) * gate

    for nc in range(tn // wide):
        for il in range(i1_per_blk):
            gate_block(il, nc)
        cols = pl.ds(nc * wide, wide)
        acc_scr[:, cols] += _mxu_dot(vt_ref[...], act_scr[:, cols])

    @pl.when(j == pl.num_programs(1) - 1)
    def _():
        y_ref[...] = (xt_ref[...] + acc_scr[...]).T


def peer_dense(xt, xn_t, u_layers, layer, vt_bf16, c1, e1, r2, e2):
    d, n = xt.shape
    ne = u_layers.shape[1]
    tn = _pick(n, (512, 256, 128))
    te = PEER_EXPERT_BLOCK
    i1_per_blk = te // N_KEYS
    tok = lambda i, j: (0, i)
    return pl.pallas_call(
        _peer_dense_kernel,
        grid=(n // tn, ne // te),
        in_specs=[
            pl.BlockSpec((d, tn), tok),
            pl.BlockSpec((d, tn), tok),
            pl.BlockSpec((None, te, d), lambda i, j: (layer, j, 0)),
            pl.BlockSpec((d, te), lambda i, j: (0, j)),
            pl.BlockSpec((i1_per_blk * PEER_HEADS, tn), lambda i, j: (j, i)),
            pl.BlockSpec((i1_per_blk * PEER_HEADS, tn), lambda i, j: (j, i)),
            pl.BlockSpec((PEER_HEADS * N_KEYS, tn), tok),
            pl.BlockSpec((PEER_HEADS * N_KEYS, tn), tok),
        ],
        out_specs=pl.BlockSpec((tn, d), lambda i, j: (i, 0)),
        out_shape=jax.ShapeDtypeStruct((n, d), F32),
        scratch_shapes=[pltpu.VMEM((d, tn), F32), pltpu.VMEM((te, tn), BF16)],
        compiler_params=_params("parallel", "arbitrary"),
        name="peer_dense",
    )(xt, xn_t, u_layers, vt_bf16, c1, e1, r2, e2)


def _interleaved_keys(keys_p):
    nh, nk, half = keys_p.shape
    eye = jnp.eye(nh, dtype=keys_p.dtype)
    return jnp.einsum("hkd,hg->khgd", keys_p, eye).reshape(nk * nh, nh * half).astype(BF16)


def peer_query_weights(wq):
    d = wq.shape[0]
    wq_t = wq.reshape(d, PEER_HEADS, 2, PEER_HALF).transpose(2, 1, 3, 0).reshape(2 * PEER_HEADS * PEER_HALF, d)
    return wq_t.astype(BF16)


def peer_layer(xt, xn_t, q_t, keys, u_layers, layer, v_tab):
    k1 = _interleaved_keys(keys[:, 0])
    k2 = _interleaved_keys(keys[:, 1])
    c1, e1, r2, e2 = peer_select(q_t, k1, k2)
    return peer_dense(xt, xn_t, u_layers, layer, v_tab.T.astype(BF16), c1, e1, r2, e2)


def _rmsnorm_kernel(x_ref, g_ref, y_ref):
    x = x_ref[...]
    ms = jnp.mean(x * x, axis=-1, keepdims=True)
    y_ref[...] = (x * lax.rsqrt(ms + EPS)) * g_ref[...]


def final_norm(x, row0, rows, g):
    d = x.shape[1]
    tm = _pick(math.gcd(rows, row0) if row0 else rows, (512, 256, 128, 64, 32, 16, 8))
    blk0 = row0 // tm
    return pl.pallas_call(
        _rmsnorm_kernel,
        grid=(rows // tm,),
        in_specs=[pl.BlockSpec((tm, d), lambda i: (blk0 + i, 0)), pl.BlockSpec((1, d), lambda i: (0, 0))],
        out_specs=pl.BlockSpec((tm, d), lambda i: (i, 0)),
        out_shape=jax.ShapeDtypeStruct((rows, d), F32),
        compiler_params=_params("parallel"),
        name="final_norm",
    )(x, g.reshape(1, d))


def _trunk(groups, norm1_g, norm2_g, normf_g, w_in, ret_gn_g, w_ret_out, conv_w, conv_b, rg_wa, rg_ba, rg_wx,
           rg_bx, rg_lambda, w_rnn_out, w_o, peer_wq, peer_keys, peer_u, peer_v):
    shapes = [(g[0].shape[0], g[0].shape[1]) for g in groups]
    x_parts = [g[0].reshape(-1, D_MODEL) for g in groups]
    row0s = np.cumsum([0] + [b * t for b, t in shapes]).tolist()
    states = [([], [], []) for _ in groups]
    for l in range(DEPTH):
        p = norm_matmul(x_parts, norm1_g[l], w_in[l].astype(BF16))
        wa4 = _block_diag_tiles(rg_wa[l], 256)
        wx4 = _block_diag_tiles(rg_wx[l], 256)
        ret_parts, rnn_parts = [], []
        for gi, (xg, r0, h0, buf0, pos0) in enumerate(groups):
            b, t = shapes[gi]
            row0 = row0s[gi]
            o, r_new = retention_path(p, row0, b, t, pos0, r0, l, ret_gn_g[l])
            rg_args = (conv_w[l], conv_b[l], wa4, rg_ba[l], wx4, rg_bx[l], rg_lambda[l])
            hg, h_last, nb = rglru_rows_path(p, row0, b, t, buf0[l], h0[l], *rg_args)
            ret_parts.append(o)
            rnn_parts.append(hg)
            states[gi][0].append(r_new)
            states[gi][1].append(h_last)
            states[gi][2].append(nb)
        xt, xn_t, q_t = merge_proj(x_parts, ret_parts, rnn_parts, p, w_ret_out[l].astype(BF16),
                                   w_rnn_out[l].astype(BF16), w_o[l].astype(BF16), norm2_g[l],
                                   peer_query_weights(peer_wq[l]))
        x = peer_layer(xt, xn_t, q_t, peer_keys[l], peer_u, l, peer_v[l])
        x_parts = [x]
    outs = []
    for gi, (b, t) in enumerate(shapes):
        y = final_norm(x, row0s[gi], b * t, normf_g).reshape(b, t, D_MODEL)
        outs.append((y, jnp.stack(states[gi][0]), jnp.stack(states[gi][1]), jnp.stack(states[gi][2])))
    return outs


def kernel(x_prompt, x_sample, state_ret, state_rnn, state_conv, norm1_g, norm2_g, normf_g, w_in, ret_gn_g,
           w_ret_out, conv_w, conv_b, rg_wa, rg_ba, rg_wx, rg_bx, rg_lambda, w_rnn_out, w_o, peer_wq, peer_keys,
           peer_u, peer_v):
    bp = x_prompt.shape[0]
    dt = x_prompt.dtype
    zr = jnp.zeros((DEPTH, bp, RET_HEADS, RET_DK, RET_DV), dt)
    zh = jnp.zeros((DEPTH, bp, D_RNN), dt)
    zc = jnp.zeros((DEPTH, bp, CONV_W - 1, D_RNN), dt)
    groups = [(x_prompt, zr, zh, zc, 0.0), (x_sample, state_ret, state_rnn, state_conv, float(PAST_LEN))]
    (yp, rp, hp, cp), (ys, rs, hs, cs) = _trunk(
        groups, norm1_g, norm2_g, normf_g, w_in, ret_gn_g, w_ret_out, conv_w, conv_b, rg_wa, rg_ba, rg_wx, rg_bx,
        rg_lambda, w_rnn_out, w_o, peer_wq, peer_keys, peer_u, peer_v)
    return (yp, ys, rp, hp, cp, rs, hs, cs)
```

```python
import functools
import math

import jax
import jax.numpy as jnp
import numpy as np
from jax import lax
from jax.experimental import pallas as pl
from jax.experimental.pallas import tpu as pltpu

D_MODEL = 1024
DEPTH = 2
PAST_LEN = 16384
RET_HEADS = 8
RET_DK = 64
RET_DV = 128
RET_QK = RET_HEADS * RET_DK
RET_V = RET_HEADS * RET_DV
RET_CHUNK = 128
ROPE_BASE = 10000.0
D_RNN = 1024
CONV_W = 4
RG_C = 8.0
COL_V, COL_GRET, COL_XR, COL_GRNN, COL_GA, COL_GB = 1, 2, 3, 4, 5, 6
PEER_HEADS = 8
N_KEYS = 128
PEER_DKEY = 256
PEER_HALF = PEER_DKEY // 2
PEER_TOPK = 16
EPS = 1e-6

SUBLANES = 8
LANES = 128
VMEM_LIMIT = 56 * 1024 * 1024

F32 = jnp.float32
BF16 = jnp.bfloat16


def _params(*sem):
    return pltpu.CompilerParams(dimension_semantics=sem, vmem_limit_bytes=VMEM_LIMIT)


def _pick(n, prefs):
    for p in prefs:
        if n % p == 0:
            return p
    return n


def _row_part_maps(parts, tm):
    assert len(parts) in (1, 2) and all(p.shape[0] % tm == 0 for p in parts)
    blocks_a = parts[0].shape[0] // tm
    maps = [lambda i, *_: (jnp.minimum(i, blocks_a - 1), 0)]
    if len(parts) == 2:
        maps.append(lambda i, *_: (jnp.maximum(i - blocks_a, 0), 0))
    return blocks_a, maps


def _pick_rows(refs, blocks_a):
    x = refs[0][...]
    if len(refs) == 2:
        x = jnp.where(pl.program_id(0) < blocks_a, x, refs[1][...])
    return x


def _norm_matmul_kernel(blocks_a, nparts, *refs):
    x_refs = refs[:nparts]
    g_ref, w_ref, y_ref, xn_scr = refs[nparts:]

    @pl.when(pl.program_id(1) == 0)
    def _():
        x = _pick_rows(x_refs, blocks_a)
        ms = jnp.mean(x * x, axis=-1, keepdims=True)
        xn_scr[...] = ((x * lax.rsqrt(ms + EPS)) * g_ref[...]).astype(BF16)

    y_ref[...] = jnp.dot(xn_scr[...], w_ref[...], preferred_element_type=F32)


def norm_matmul(x_parts, g, w_bf16):
    n = sum(p.shape[0] for p in x_parts)
    d = x_parts[0].shape[1]
    m = w_bf16.shape[1]
    tm = _pick(math.gcd(*[p.shape[0] for p in x_parts]), (1024, 512, 256, 128))
    tn = _pick(m, (1792, 1024, 512, 256, 128))
    blocks_a, row_maps = _row_part_maps(x_parts, tm)
    return pl.pallas_call(
        functools.partial(_norm_matmul_kernel, blocks_a, len(x_parts)),
        grid=(n // tm, m // tn),
        in_specs=[pl.BlockSpec((tm, d), rm) for rm in row_maps] + [
            pl.BlockSpec((1, d), lambda i, j: (0, 0)),
            pl.BlockSpec((d, tn), lambda i, j: (0, j)),
        ],
        out_specs=pl.BlockSpec((tm, tn), lambda i, j: (i, j)),
        out_shape=jax.ShapeDtypeStruct((n, m), F32),
        scratch_shapes=[pltpu.VMEM((tm, d), BF16)],
        compiler_params=_params("parallel", "arbitrary"),
        name="norm_in_proj",
    )(*x_parts, g.reshape(1, d), w_bf16)


def _rot_half(x):
    n = x.shape[-1]
    half = RET_DK // 2
    fwd = pltpu.roll(x, half, axis=1)
    bwd = pltpu.roll(x, n - half, axis=1)
    lane = lax.broadcasted_iota(jnp.int32, x.shape, 1)
    return jnp.where((lane % RET_DK) < half, bwd, fwd)


def _retention_kernel(q_ref, k_ref, v_ref, g_ref, r0_ref, cos_ref, sin_ref, mask_ref, qw_ref, kw_ref,
                      gc_ref, gn_ref, o_ref, r_out_ref, r_scr):
    c = pl.program_id(1)
    rows_c = mask_ref.shape[1]
    nb = q_ref.shape[0] // rows_c

    @pl.when(c == 0)
    def _():
        r_scr[...] = r0_ref[...]

    cos = cos_ref[...]
    sin = sin_ref[...]
    q = q_ref[...]
    k = k_ref[...]
    qr = q * cos + _rot_half(q) * sin
    kr = (k * cos + _rot_half(k) * sin) * (RET_DK ** -0.5)
    qd = (qr * qw_ref[...]).astype(BF16)
    kd = (kr * kw_ref[...]).astype(BF16)
    qb = qr.astype(BF16)
    kb = kr.astype(BF16)
    v = v_ref[...].astype(BF16)
    g = g_ref[...]
    for bb in range(nb):
        rs = slice(bb * rows_c, (bb + 1) * rows_c)
        for h in range(RET_HEADS):
            ks = slice(h * RET_DK, (h + 1) * RET_DK)
            vs = slice(h * RET_DV, (h + 1) * RET_DV)
            vh = v[rs, vs]
            s = lax.dot_general(qb[rs, ks], kb[rs, ks], (((1,), (1,)), ((), ())),
                                preferred_element_type=F32) * mask_ref[h]
            r_h = r_scr[bb, h]
            o = jnp.dot(s.astype(BF16), vh, preferred_element_type=F32)
            o = o + jnp.dot(qd[rs, ks], r_h.astype(BF16), preferred_element_type=F32)
            kv = lax.dot_general(kd[rs, ks], vh, (((0,), (0,)), ((), ())), preferred_element_type=F32)
            r_scr[bb, h] = gc_ref[h] * r_h + kv
            mu = jnp.mean(o, axis=-1, keepdims=True)
            var = jnp.mean(jnp.square(o - mu), axis=-1, keepdims=True)
            on = ((o - mu) * lax.rsqrt(var + EPS)) * gn_ref[:, vs]
            gh = g[rs, vs]
            o_ref[rs, vs] = ((gh * jax.nn.sigmoid(gh)) * on).astype(BF16)

    @pl.when(c == pl.num_programs(1) - 1)
    def _():
        r_out_ref[...] = r_scr[...]


def retention_path(p, row0, b, t, pos0, r0_layers, layer, gn_g):
    c = RET_CHUNK if t % RET_CHUNK == 0 else t
    nc = t // c
    blk0 = row0 // c
    assert row0 % c == 0
    log_g = jnp.log1p(-(2.0 ** (-5.0 - jnp.arange(RET_HEADS, dtype=F32))))
    idx = jnp.arange(c, dtype=F32)
    diff = idx[:, None] - idx[None, :]
    mask = jnp.where(diff[None] >= 0, jnp.exp(jnp.maximum(diff, 0.0)[None] * log_g[:, None, None]), 0.0)
    k_w = jnp.exp((c - 1 - idx)[:, None] * log_g[None, :])
    q_w = jnp.exp((idx + 1.0)[:, None] * log_g[None, :])
    g_c = jnp.exp(c * log_g)
    qw_tab = jnp.repeat(q_w, RET_DK, axis=1)
    kw_tab = jnp.repeat(k_w, RET_DK, axis=1)
    gc_tab = jnp.broadcast_to(g_c[:, None, None], (RET_HEADS, 1, RET_DV))
    half = RET_DK // 2
    pos = pos0 + jnp.arange(t, dtype=F32)
    freq = ROPE_BASE ** (-jnp.arange(half, dtype=F32) / half)
    ang = pos[:, None] * freq[None, :]
    cos_h = jnp.concatenate([jnp.cos(ang), jnp.cos(ang)], axis=1)
    sin_h = jnp.concatenate([-jnp.sin(ang), jnp.sin(ang)], axis=1)
    cos_tab = jnp.tile(cos_h, (1, RET_HEADS))
    sin_tab = jnp.tile(sin_h, (1, RET_HEADS))
    nb = _pick(b, (8, 4, 2, 1)) if nc == 1 and c * 8 <= RET_CHUNK else 1
    if nb > 1:
        cos_tab, sin_tab, qw_tab, kw_tab = [jnp.tile(tab, (nb, 1)) for tab in (cos_tab, sin_tab, qw_tab, kw_tab)]
    rb = nb * c
    blk0 = row0 // rb
    assert row0 % rb == 0

    rows = lambda bi, ci: blk0 + bi * nc + ci
    state_spec = pl.BlockSpec((nb, RET_HEADS, RET_DK, RET_DV), lambda bi, ci: (bi, 0, 0, 0))
    o, r_new = pl.pallas_call(
        _retention_kernel,
        grid=(b // nb, nc),
        in_specs=[
            pl.BlockSpec((rb, RET_QK), lambda bi, ci: (rows(bi, ci), 0)),
            pl.BlockSpec((rb, RET_QK), lambda bi, ci: (rows(bi, ci), 1)),
            pl.BlockSpec((rb, RET_V), lambda bi, ci: (rows(bi, ci), COL_V)),
            pl.BlockSpec((rb, RET_V), lambda bi, ci: (rows(bi, ci), COL_GRET)),
            pl.BlockSpec((None, nb, RET_HEADS, RET_DK, RET_DV), lambda bi, ci: (layer, bi, 0, 0, 0)),
            pl.BlockSpec((rb, RET_QK), lambda bi, ci: (ci, 0)),
            pl.BlockSpec((rb, RET_QK), lambda bi, ci: (ci, 0)),
            pl.BlockSpec((RET_HEADS, c, c), lambda bi, ci: (0, 0, 0)),
            pl.BlockSpec((rb, RET_QK), lambda bi, ci: (0, 0)),
            pl.BlockSpec((rb, RET_QK), lambda bi, ci: (0, 0)),
            pl.BlockSpec((RET_HEADS, 1, RET_DV), lambda bi, ci: (0, 0, 0)),
            pl.BlockSpec((1, RET_V), lambda bi, ci: (0, 0)),
        ],
        out_specs=[
            pl.BlockSpec((rb, RET_V), lambda bi, ci: (bi * nc + ci, 0)),
            state_spec,
        ],
        out_shape=[
            jax.ShapeDtypeStruct((b * t, RET_V), BF16),
            jax.ShapeDtypeStruct((b, RET_HEADS, RET_DK, RET_DV), F32),
        ],
        scratch_shapes=[pltpu.VMEM((nb, RET_HEADS, RET_DK, RET_DV), F32)],
        compiler_params=_params("parallel", "arbitrary"),
        name="retention",
    )(p, p, p, p, r0_layers, cos_tab, sin_tab, mask, qw_tab, kw_tab, gc_tab, gn_g.reshape(1, RET_V))
    return o, r_new


GELU_C0 = math.sqrt(2.0 / math.pi)
GELU_C1 = GELU_C0 * 0.044715


def _gelu_tanh(x):
    return x * (0.5 * (1.0 + jnp.tanh(math.sqrt(2.0 / math.pi) * (x + 0.044715 * (x * x * x)))))


def _rglru_gates(xc, wa_ref, ba_ref, wx_ref, bx_ref, lam_ref):
    xcb = xc.astype(BF16)
    nblk = wa_ref.shape[0]
    wdt = wa_ref.shape[1]
    ra = jnp.concatenate(
        [jnp.dot(xcb[:, j * wdt:(j + 1) * wdt], wa_ref[j], preferred_element_type=F32) for j in range(nblk)],
        axis=1)
    ri = jnp.concatenate(
        [jnp.dot(xcb[:, j * wdt:(j + 1) * wdt], wx_ref[j], preferred_element_type=F32) for j in range(nblk)],
        axis=1)
    r = jax.nn.sigmoid(ra + ba_ref[...])
    i = jax.nn.sigmoid(ri + bx_ref[...])
    z = -lam_ref[...]
    softplus = jnp.maximum(z, 0.0) + jnp.log1p(jnp.exp(-jnp.abs(z)))
    log_a = (-RG_C * r) * softplus
    a = jnp.exp(log_a)
    one_m_a2 = -jnp.tanh(log_a) * (a * a + 1.0)
    return a, jnp.sqrt(one_m_a2) * (i * xc)


def _rglru_rows_kernel(bsz, tc, nin, chained, *refs):
    xr_refs, g_refs = refs[:nin], refs[nin:2 * nin]
    (buf_ref, h0_ref, cw_ref, cb_ref, wa_ref, ba_ref, wx_ref, bx_ref, lam_ref, o_ref, hl_ref, nb_ref,
     xcat_scr, a_scr, u_scr, hs_scr, h_scr) = refs[2 * nin:]
    step = pl.program_id(0)
    rows = tc * bsz
    hist = (CONV_W - 1) * bsz
    bpi = bsz // nin

    def time_major(x):
        return jnp.swapaxes(x, 0, 1).reshape(x.shape[0] * x.shape[1], D_RNN)

    def batch_major(blocks):
        return jnp.concatenate([blk[...].reshape(bpi, tc, D_RNN) for blk in blocks], axis=0)

    def load_state():
        xcat_scr[pl.ds(rows, hist), :] = time_major(buf_ref[...])
        h_scr[...] = h0_ref[...]

    if chained:
        pl.when(step == 0)(load_state)
    else:
        load_state()

    xcat_scr[pl.ds(0, hist), :] = xcat_scr[pl.ds(rows, hist), :]
    xcat_scr[pl.ds(hist, rows), :] = time_major(batch_major(xr_refs))
    xc = cb_ref[...] + xcat_scr[pl.ds(0, rows), :] * cw_ref[0:1, :]
    for w in range(1, CONV_W):
        xc = xc + xcat_scr[pl.ds(w * bsz, rows), :] * cw_ref[w:w + 1, :]
    a, u = _rglru_gates(xc, wa_ref, ba_ref, wx_ref, bx_ref, lam_ref)
    a_scr[...] = a.reshape(tc, bsz, D_RNN)
    u_scr[...] = u.reshape(tc, bsz, D_RNN)

    def scan_step(t, h):
        h = a_scr[t] * h + u_scr[t]
        hs_scr[t] = h
        return h

    h_scr[...] = lax.fori_loop(0, tc, scan_step, h_scr[...])
    gate = _gelu_tanh(batch_major(g_refs))
    o_ref[...] = (jnp.swapaxes(hs_scr[...], 0, 1) * gate).astype(BF16)

    def store_state():
        hl_ref[...] = h_scr[...]
        nb_ref[...] = jnp.swapaxes(xcat_scr[pl.ds(rows, hist), :].reshape(CONV_W - 1, bsz, D_RNN), 0, 1)

    if chained:
        pl.when(step == pl.num_programs(0) - 1)(store_state)
    else:
        store_state()


RGLRU_STEP_ROWS = 256


def rglru_rows_path(p, row0, b, t, buf0, h0, conv_w, conv_b, wa4, ba, wx4, bx, lam):
    assert t >= CONV_W - 1 and b % SUBLANES == 0 and RGLRU_STEP_ROWS % SUBLANES == 0
    d = D_RNN
    hist = CONV_W - 1
    nblk, wdt = wa4.shape[0], wa4.shape[1]
    chained = b * t > RGLRU_STEP_ROWS and b == SUBLANES
    if chained:
        bg, tc, nin = b, RGLRU_STEP_ROWS // b, b
        steps = t // tc
        assert t % tc == 0 and row0 % tc == 0
        blk0 = row0 // tc
        row_map = lambda bi, col: (lambda s: (blk0 + bi * steps + s, col))
        state2, state3 = (lambda s: (0, 0)), (lambda s: (0, 0, 0))
        out_map = lambda s: (0, s, 0)
    else:
        bg, tc, nin = max(SUBLANES, RGLRU_STEP_ROWS // t), t, 1
        steps = b // bg
        assert b % bg == 0 and row0 % (bg * t) == 0
        blk0 = row0 // (bg * t)
        row_map = lambda bi, col: (lambda s: (blk0 + s, col))
        state2, state3 = (lambda s: (s, 0)), (lambda s: (s, 0, 0))
        out_map = lambda s: (s, 0, 0)
    in_rows = (bg // nin) * tc
    const2 = lambda s: (0, 0)
    const3 = lambda s: (0, 0, 0)
    out, h_last, new_buf = pl.pallas_call(
        functools.partial(_rglru_rows_kernel, bg, tc, nin, chained),
        grid=(steps,),
        in_specs=[pl.BlockSpec((in_rows, d), row_map(bi, COL_XR)) for bi in range(nin)]
        + [pl.BlockSpec((in_rows, d), row_map(bi, COL_GRNN)) for bi in range(nin)]
        + [
            pl.BlockSpec((bg, hist, d), state3),
            pl.BlockSpec((bg, d), state2),
            pl.BlockSpec((CONV_W, d), const2),
            pl.BlockSpec((1, d), const2),
            pl.BlockSpec((nblk, wdt, wdt), const3),
            pl.BlockSpec((1, d), const2),
            pl.BlockSpec((nblk, wdt, wdt), const3),
            pl.BlockSpec((1, d), const2),
            pl.BlockSpec((1, d), const2),
        ],
        out_specs=[
            pl.BlockSpec((bg, tc, d), out_map),
            pl.BlockSpec((bg, d), state2),
            pl.BlockSpec((bg, hist, d), state3),
        ],
        out_shape=[
            jax.ShapeDtypeStruct((b, t, d), BF16),
            jax.ShapeDtypeStruct((b, d), F32),
            jax.ShapeDtypeStruct((b, hist, d), F32),
        ],
        scratch_shapes=[
            pltpu.VMEM(((CONV_W - 1 + tc) * bg, d), F32),
            pltpu.VMEM((tc, bg, d), F32),
            pltpu.VMEM((tc, bg, d), F32),
            pltpu.VMEM((tc, bg, d), F32),
            pltpu.VMEM((bg, d), F32),
        ],
        compiler_params=_params("arbitrary"),
        name="rglru_rows",
    )(*([p] * (2 * nin)), buf0, h0, conv_w, conv_b.reshape(1, d), wa4, ba.reshape(1, d), wx4, bx.reshape(1, d),
      lam.reshape(1, d))
    return out.reshape(b * t, d), h_last, new_buf


def _block_diag_tiles(w, tile):
    nb, bs, _ = w.shape
    per = tile // bs
    w = w.reshape(nb // per, per, bs, bs)
    eye = jnp.eye(per, dtype=w.dtype)
    dense = jnp.einsum("gpcd,pq->gpcqd", w, eye).reshape(nb // per, tile, tile)
    return dense.astype(BF16)


def _merge_kernel(blocks_a, nx, *refs):
    x_refs, refs = refs[:nx], refs[nx:]
    ro_refs, rn_refs = refs[0:2], refs[2:4]
    ga_ref, gb_ref, wr_ref, wn_ref, wo_ref, g2_ref, wq_ref, yt_ref, xn_ref, q_ref = refs[4:]
    ret_out = jnp.dot(_pick_rows(ro_refs, blocks_a), wr_ref[...], preferred_element_type=F32)
    rnn_out = jnp.dot(_pick_rows(rn_refs, blocks_a), wn_ref[...], preferred_element_type=F32)
    merged = jax.nn.sigmoid(ga_ref[...]) * ret_out + jax.nn.sigmoid(gb_ref[...]) * rnn_out
    y = _pick_rows(x_refs, blocks_a) + jnp.dot(merged.astype(BF16), wo_ref[...], preferred_element_type=F32)
    yt = y.T
    yt_ref[...] = yt
    ms = jnp.mean(yt * yt, axis=0, keepdims=True)
    xn = ((yt * lax.rsqrt(ms + EPS)) * g2_ref[...]).astype(BF16)
    xn_ref[...] = xn
    q_ref[...] = jnp.dot(wq_ref[...], xn, preferred_element_type=F32)


def merge_proj(x_parts, ret_parts, rnn_parts, p, w_ret_out, w_rnn_out, w_o, norm2_g, wq_t):
    n, d = p.shape[0], x_parts[0].shape[1]
    m = wq_t.shape[0]
    tm = _pick(math.gcd(*[r.shape[0] for r in ret_parts]), (512, 256, 128))
    blocks_a, part_maps = _row_part_maps(ret_parts, tm)
    assert len(x_parts) == 1 or x_parts[0].shape[0] == ret_parts[0].shape[0]
    _, x_maps = _row_part_maps(x_parts, tm) if len(x_parts) == 2 else (None, [lambda i: (i, 0)])
    const = lambda i: (0, 0)
    tok = lambda i: (0, i)
    return pl.pallas_call(
        functools.partial(_merge_kernel, blocks_a, len(x_parts)),
        grid=(n // tm,),
        in_specs=[pl.BlockSpec((tm, d), mp) for mp in x_maps + part_maps + part_maps] + [
            pl.BlockSpec((tm, d), lambda i: (i, COL_GA)),
            pl.BlockSpec((tm, d), lambda i: (i, COL_GB)),
            pl.BlockSpec((d, d), const),
            pl.BlockSpec((d, d), const),
            pl.BlockSpec((d, d), const),
            pl.BlockSpec((d, 1), const),
            pl.BlockSpec((m, d), const),
        ],
        out_specs=[pl.BlockSpec((d, tm), tok), pl.BlockSpec((d, tm), tok), pl.BlockSpec((m, tm), tok)],
        out_shape=[jax.ShapeDtypeStruct((d, n), F32), jax.ShapeDtypeStruct((d, n), BF16),
                   jax.ShapeDtypeStruct((m, n), F32)],
        compiler_params=_params("parallel"),
        name="merge_out_proj",
    )(*x_parts, *ret_parts, *rnn_parts, p, p, w_ret_out, w_rnn_out, w_o, norm2_g.reshape(d, 1), wq_t)


def _sort_pairs(n):
    def merge(lo, hi, r):
        step = r * 2
        if step < hi - lo:
            yield from merge(lo, hi, step)
            yield from merge(lo + r, hi, step)
            yield from [(i, i + r) for i in range(lo + r, hi - r, step)]
        else:
            yield (lo, lo + r)

    def sort(lo, hi):
        if hi - lo >= 1:
            mid = lo + (hi - lo) // 2
            yield from sort(lo, mid)
            yield from sort(mid + 1, hi)
            yield from merge(lo, hi, 1)

    return list(sort(0, n - 1))


_SORT16 = _sort_pairs(PEER_TOPK)


def _cmpx(vals, i, j):
    a, b = vals[i], vals[j]
    if b is None:
        return
    if a is None:
        vals[i], vals[j] = b, None
        return
    vals[i], vals[j] = jnp.maximum(a, b), jnp.minimum(a, b)


def _sort_desc(vals):
    vals = list(vals)
    for i, j in _SORT16:
        _cmpx(vals, i, j)
    return vals


def _merge_top(a, b):
    k = PEER_TOPK
    a = list(a) + [None] * (k - len(a))
    b = list(b) + [None] * (k - len(b))
    out = []
    for r in range(k):
        x, y = a[r], b[k - 1 - r]
        out.append(y if x is None else (x if y is None else jnp.maximum(x, y)))
    d = k // 2
    while d >= 1:
        for i in range(k):
            if not i & d:
                _cmpx(out, i, i + d)
        d //= 2
    return out


def _top_sorted(ref):
    groups = []
    for g0 in range(0, N_KEYS, PEER_TOPK):
        vals = [ref[pl.ds((g0 + j) * PEER_HEADS, PEER_HEADS), :] for j in range(PEER_TOPK)]
        groups.append(_sort_desc(vals))
    while len(groups) > 1:
        groups = [_merge_top(groups[i], groups[i + 1]) for i in range(0, len(groups), 2)]
    return groups[0]


def _peer_select_kernel(q_ref, k1_ref, k2_ref, c1_ref, e1_ref, r2_ref, e2_ref, s1_scr, s2_scr, r2_scr, e2_scr):
    nh = PEER_HEADS
    half_rows = q_ref.shape[0] // 2
    tn = q_ref.shape[1]
    s1 = jnp.dot(k1_ref[...], q_ref[pl.ds(0, half_rows), :].astype(BF16), preferred_element_type=F32)
    s2 = jnp.dot(k2_ref[...], q_ref[pl.ds(half_rows, half_rows), :].astype(BF16), preferred_element_type=F32)
    for lt in range(tn // LANES):
        s1_scr[lt] = s1[:, lt * LANES:(lt + 1) * LANES]
        s2_scr[lt] = s2[:, lt * LANES:(lt + 1) * LANES]
    for lt in range(tn // LANES):
        lanes = pl.ds(lt * LANES, LANES)
        s1_t, s2_t, r2_t, e2_t = s1_scr.at[lt], s2_scr.at[lt], r2_scr.at[lt], e2_scr.at[lt]
        a = _top_sorted(s1_t)
        b = _top_sorted(s2_t)
        k = PEER_TOPK
        lists = []
        for j in range(1, k + 1):
            col = [a[r - 1] + b[j - 1] for r in range(j, k // j + 1)]
            row = [a[j - 1] + b[s - 1] for s in range(j + 1, k // j + 1)]
            if col:
                lists.append(col)
            if row:
                lists.append(row)
        top = lists[0]
        for other in lists[1:]:
            top = _merge_top(top, other)
        tau = top[k - 1]
        z = jnp.ones_like(tau)
        for r in range(1, k):
            z = z + jnp.exp(top[r] - top[0])
        zinv = 1.0 / z
        inf = jnp.full((nh, LANES), jnp.inf, F32)
        phi = []
        for s in range(1, k + 1):
            p = inf
            for r in range(1, k // s + 1):
                p = jnp.where(a[r - 1] + b[s - 1] >= tau, a[r - 1], p)
            phi.append(p)

        def per_key(kk, carry):
            rows = pl.ds(pl.multiple_of(kk * nh, nh), nh)
            s1k = s1_t[rows, :]
            s2k = s2_t[rows, :]
            cnt = jnp.ones((nh, LANES), F32)
            for s in range(k):
                cnt = jnp.where(s1k >= phi[s], float(s + 2), cnt)
            rank = jnp.full((nh, LANES), float(k + 1), F32)
            for s in range(k - 1, -1, -1):
                rank = jnp.where(s2k >= b[s], float(s + 1), rank)
            c1_ref[rows, lanes] = cnt
            e1_ref[rows, lanes] = 0.5 * jnp.exp(s1k - a[0])
            r2_t[rows, :] = rank
            e2_t[rows, :] = jnp.exp(s2k - b[0]) * zinv
            return carry

        lax.fori_loop(0, N_KEYS, per_key, 0, unroll=2)
        pack = 2 * SUBLANES
        for h in range(nh):
            for kt in range(N_KEYS // pack):
                lo = pl.ds(kt * pack * nh + h, SUBLANES, stride=nh)
                hi = pl.ds((kt * pack + SUBLANES) * nh + h, SUBLANES, stride=nh)
                dst = pl.ds(h * N_KEYS + kt * pack, pack)
                r2_ref[dst, lanes] = jnp.concatenate([r2_t[lo, :], r2_t[hi, :]], axis=0).astype(BF16)
                e2_ref[dst, lanes] = jnp.concatenate([e2_t[lo, :], e2_t[hi, :]], axis=0).astype(BF16)


def peer_select(q_t, k1, k2):
    m, n = q_t.shape
    rows = PEER_HEADS * N_KEYS
    tn = _pick(n, (256, 128))
    tok = lambda i: (0, i)
    const = lambda i: (0, 0)
    return pl.pallas_call(
        _peer_select_kernel,
        grid=(n // tn,),
        in_specs=[pl.BlockSpec((m, tn), tok), pl.BlockSpec(k1.shape, const), pl.BlockSpec(k2.shape, const)],
        out_specs=[pl.BlockSpec((rows, tn), tok)] * 4,
        out_shape=[jax.ShapeDtypeStruct((rows, n), F32), jax.ShapeDtypeStruct((rows, n), F32),
                   jax.ShapeDtypeStruct((rows, n), BF16), jax.ShapeDtypeStruct((rows, n), BF16)],
        scratch_shapes=[pltpu.VMEM((tn // LANES, rows, LANES), F32)] * 4,
        compiler_params=_params("parallel"),
        name="peer_select",
    )(q_t, k1, k2)


PEER_EXPERT_BLOCK = 1024


def _mxu_dot(lhs, rhs):
    return lax.dot_general(lhs, rhs, (((1,), (0,)), ((), ())), preferred_element_type=F32)


def _peer_dense_kernel(xt_ref, xn_ref, u_ref, vt_ref, c1_ref, e1_ref, r2_ref, e2_ref, y_ref,
                       acc_scr, act_scr):
    j = pl.program_id(1)
    nh = PEER_HEADS
    tn = xn_ref.shape[1]
    pack = 2 * SUBLANES
    i1_per_blk = u_ref.shape[0] // N_KEYS

    @pl.when(j == 0)
    def _():
        acc_scr[...] = jnp.zeros_like(acc_scr)

    wide = 2 * LANES

    def gate_block(il, nc):
        erows = pl.ds(il * N_KEYS, N_KEYS)
        hblk = _mxu_dot(u_ref[erows, :], xn_ref[:, pl.ds(nc * wide, wide)])
        for lw in range(wide // LANES):
            lc = nc * (wide // LANES) + lw
            lanes = pl.ds(lc * LANES, LANES)
            c1b = [jnp.broadcast_to(c1_ref[pl.ds(il * nh + h, 1), lanes], (pack, LANES)).astype(BF16)
                   for h in range(nh)]
            e1b = [jnp.broadcast_to(e1_ref[pl.ds(il * nh + h, 1), lanes], (pack, LANES)).astype(BF16)
                   for h in range(nh)]
            zero = jnp.zeros((pack, LANES), BF16)
            one = jnp.full((), 1.0, BF16)
            gelu_c0 = jnp.full((), GELU_C0, BF16)
            gelu_c1 = jnp.full((), GELU_C1, BF16)
            for it in range(N_KEYS // pack):
                rows = pl.ds(il * N_KEYS + it * pack, pack)
                gate = None
                for h in range(nh):
                    krows = pl.ds(h * N_KEYS + it * pack, pack)
                    sel = jnp.minimum(jnp.maximum(c1b[h] - r2_ref[krows, lanes], zero), e2_ref[krows, lanes])
                    term = sel * e1b[h]
                    gate = term if gate is None else gate + term
                x = hblk[it * pack:(it + 1) * pack, lw * LANES:(lw + 1) * LANES].astype(BF16)
                t = jnp.tanh(x * (gelu_c0 + gelu_c1 * (x * x)))
                act_scr[rows, lanes] = (x * (one + t)) * gate

    for nc in range(tn // wide):
        for il in range(i1_per_blk):
            gate_block(il, nc)
        cols = pl.ds(nc * wide, wide)
        acc_scr[:, cols] += _mxu_dot(vt_ref[...], act_scr[:, cols])

    @pl.when(j == pl.num_programs(1) - 1)
    def _():
        y_ref[...] = (xt_ref[...] + acc_scr[...]).T


def peer_dense(xt, xn_t, u_layers, layer, vt_bf16, c1, e1, r2, e2):
    d, n = xt.shape
    ne = u_layers.shape[1]
    tn = _pick(n, (512, 256, 128))
    te = PEER_EXPERT_BLOCK
    i1_per_blk = te // N_KEYS
    tok = lambda i, j: (0, i)
    return pl.pallas_call(
        _peer_dense_kernel,
        grid=(n // tn, ne // te),
        in_specs=[
            pl.BlockSpec((d, tn), tok),
            pl.BlockSpec((d, tn), tok),
            pl.BlockSpec((None, te, d), lambda i, j: (layer, j, 0)),
            pl.BlockSpec((d, te), lambda i, j: (0, j)),
            pl.BlockSpec((i1_per_blk * PEER_HEADS, tn), lambda i, j: (j, i)),
            pl.BlockSpec((i1_per_blk * PEER_HEADS, tn), lambda i, j: (j, i)),
            pl.BlockSpec((PEER_HEADS * N_KEYS, tn), tok),
            pl.BlockSpec((PEER_HEADS * N_KEYS, tn), tok),
        ],
        out_specs=pl.BlockSpec((tn, d), lambda i, j: (i, 0)),
        out_shape=jax.ShapeDtypeStruct((n, d), F32),
        scratch_shapes=[pltpu.VMEM((d, tn), F32), pltpu.VMEM((te, tn), BF16)],
        compiler_params=_params("parallel", "arbitrary"),
        name="peer_dense",
    )(xt, xn_t, u_layers, vt_bf16, c1, e1, r2, e2)


def _interleaved_keys(keys_p):
    nh, nk, half = keys_p.shape
    eye = jnp.eye(nh, dtype=keys_p.dtype)
    return jnp.einsum("hkd,hg->khgd", keys_p, eye).reshape(nk * nh, nh * half).astype(BF16)


def peer_query_weights(wq):
    d = wq.shape[0]
    wq_t = wq.reshape(d, PEER_HEADS, 2, PEER_HALF).transpose(2, 1, 3, 0).reshape(2 * PEER_HEADS * PEER_HALF, d)
    return wq_t.astype(BF16)


def peer_layer(xt, xn_t, q_t, keys, u_layers, layer, v_tab):
    k1 = _interleaved_keys(keys[:, 0])
    k2 = _interleaved_keys(keys[:, 1])
    c1, e1, r2, e2 = peer_select(q_t, k1, k2)
    return peer_dense(xt, xn_t, u_layers, layer, v_tab.T.astype(BF16), c1, e1, r2, e2)


def _rmsnorm_kernel(x_ref, g_ref, y_ref):
    x = x_ref[...]
    ms = jnp.mean(x * x, axis=-1, keepdims=True)
    y_ref[...] = (x * lax.rsqrt(ms + EPS)) * g_ref[...]


def final_norm(x, row0, rows, g):
    d = x.shape[1]
    tm = _pick(math.gcd(rows, row0) if row0 else rows, (512, 256, 128, 64, 32, 16, 8))
    blk0 = row0 // tm
    return pl.pallas_call(
        _rmsnorm_kernel,
        grid=(rows // tm,),
        in_specs=[pl.BlockSpec((tm, d), lambda i: (blk0 + i, 0)), pl.BlockSpec((1, d), lambda i: (0, 0))],
        out_specs=pl.BlockSpec((tm, d), lambda i: (i, 0)),
        out_shape=jax.ShapeDtypeStruct((rows, d), F32),
        compiler_params=_params("parallel"),
        name="final_norm",
    )(x, g.reshape(1, d))


def _trunk(groups, norm1_g, norm2_g, normf_g, w_in, ret_gn_g, w_ret_out, conv_w, conv_b, rg_wa, rg_ba, rg_wx,
           rg_bx, rg_lambda, w_rnn_out, w_o, peer_wq, peer_keys, peer_u, peer_v):
    shapes = [(g[0].shape[0], g[0].shape[1]) for g in groups]
    x_parts = [g[0].reshape(-1, D_MODEL) for g in groups]
    row0s = np.cumsum([0] + [b * t for b, t in shapes]).tolist()
    states = [([], [], []) for _ in groups]
    for l in range(DEPTH):
        p = norm_matmul(x_parts, norm1_g[l], w_in[l].astype(BF16))
        wa4 = _block_diag_tiles(rg_wa[l], 256)
        wx4 = _block_diag_tiles(rg_wx[l], 256)
        ret_parts, rnn_parts = [], []
        for gi, (xg, r0, h0, buf0, pos0) in enumerate(groups):
            b, t = shapes[gi]
            row0 = row0s[gi]
            o, r_new = retention_path(p, row0, b, t, pos0, r0, l, ret_gn_g[l])
            rg_args = (conv_w[l], conv_b[l], wa4, rg_ba[l], wx4, rg_bx[l], rg_lambda[l])
            hg, h_last, nb = rglru_rows_path(p, row0, b, t, buf0[l], h0[l], *rg_args)
            ret_parts.append(o)
            rnn_parts.append(hg)
            states[gi][0].append(r_new)
            states[gi][1].append(h_last)
            states[gi][2].append(nb)
        xt, xn_t, q_t = merge_proj(x_parts, ret_parts, rnn_parts, p, w_ret_out[l].astype(BF16),
                                   w_rnn_out[l].astype(BF16), w_o[l].astype(BF16), norm2_g[l],
                                   peer_query_weights(peer_wq[l]))
        x = peer_layer(xt, xn_t, q_t, peer_keys[l], peer_u, l, peer_v[l])
        x_parts = [x]
    outs = []
    for gi, (b, t) in enumerate(shapes):
        y = final_norm(x, row0s[gi], b * t, normf_g).reshape(b, t, D_MODEL)
        outs.append((y, jnp.stack(states[gi][0]), jnp.stack(states[gi][1]), jnp.stack(states[gi][2])))
    return outs


def kernel(x_prompt, x_sample, state_ret, state_rnn, state_conv, norm1_g, norm2_g, normf_g, w_in, ret_gn_g,
           w_ret_out, conv_w, conv_b, rg_wa, rg_ba, rg_wx, rg_bx, rg_lambda, w_rnn_out, w_o, peer_wq, peer_keys,
           peer_u, peer_v):
    bp = x_prompt.shape[0]
    dt = x_prompt.dtype
    zr = jnp.zeros((DEPTH, bp, RET_HEADS, RET_DK, RET_DV), dt)
    zh = jnp.zeros((DEPTH, bp, D_RNN), dt)
    zc = jnp.zeros((DEPTH, bp, CONV_W - 1, D_RNN), dt)
    groups = [(x_prompt, zr, zh, zc, 0.0), (x_sample, state_ret, state_rnn, state_conv, float(PAST_LEN))]
    (yp, rp, hp, cp), (ys, rs, hs, cs) = _trunk(
        groups, norm1_g, norm2_g, normf_g, w_in, ret_gn_g, w_ret_out, conv_w, conv_b, rg_wa, rg_ba, rg_wx, rg_bx,
        rg_lambda, w_rnn_out, w_o, peer_wq, peer_keys, peer_u, peer_v)
    return (yp, ys, rp, hp, cp, rs, hs, cs)
```

```python
import functools
import math

import jax
import jax.numpy as jnp
import numpy as np
from jax import lax
from jax.experimental import pallas as pl
from jax.experimental.pallas import tpu as pltpu

D_MODEL = 1024
DEPTH = 2
PAST_LEN = 16384
RET_HEADS = 8
RET_DK = 64
RET_DV = 128
RET_QK = RET_HEADS * RET_DK
RET_V = RET_HEADS * RET_DV
RET_CHUNK = 128
ROPE_BASE = 10000.0
D_RNN = 1024
CONV_W = 4
RG_C = 8.0
COL_V, COL_GRET, COL_XR, COL_GRNN, COL_GA, COL_GB = 1, 2, 3, 4, 5, 6
PEER_HEADS = 8
N_KEYS = 128
PEER_DKEY = 256
PEER_HALF = PEER_DKEY // 2
PEER_TOPK = 16
EPS = 1e-6

SUBLANES = 8
LANES = 128
VMEM_LIMIT = 56 * 1024 * 1024

F32 = jnp.float32
BF16 = jnp.bfloat16


def _params(*sem):
    return pltpu.CompilerParams(dimension_semantics=sem, vmem_limit_bytes=VMEM_LIMIT)


def _pick(n, prefs):
    for p in prefs:
        if n % p == 0:
            return p
    return n


def _row_part_maps(parts, tm):
    assert len(parts) in (1, 2) and all(p.shape[0] % tm == 0 for p in parts)
    blocks_a = parts[0].shape[0] // tm
    maps = [lambda i, *_: (jnp.minimum(i, blocks_a - 1), 0)]
    if len(parts) == 2:
        maps.append(lambda i, *_: (jnp.maximum(i - blocks_a, 0), 0))
    return blocks_a, maps


def _pick_rows(refs, blocks_a):
    x = refs[0][...]
    if len(refs) == 2:
        x = jnp.where(pl.program_id(0) < blocks_a, x, refs[1][...])
    return x


def _norm_matmul_kernel(blocks_a, nparts, *refs):
    x_refs = refs[:nparts]
    g_ref, w_ref, y_ref, xn_scr = refs[nparts:]

    @pl.when(pl.program_id(1) == 0)
    def _():
        x = _pick_rows(x_refs, blocks_a)
        ms = jnp.mean(x * x, axis=-1, keepdims=True)
        xn_scr[...] = ((x * lax.rsqrt(ms + EPS)) * g_ref[...]).astype(BF16)

    y_ref[...] = jnp.dot(xn_scr[...], w_ref[...], preferred_element_type=F32)


def norm_matmul(x_parts, g, w_bf16):
    n = sum(p.shape[0] for p in x_parts)
    d = x_parts[0].shape[1]
    m = w_bf16.shape[1]
    tm = _pick(math.gcd(*[p.shape[0] for p in x_parts]), (1024, 512, 256, 128))
    tn = _pick(m, (1792, 1024, 512, 256, 128))
    blocks_a, row_maps = _row_part_maps(x_parts, tm)
    return pl.pallas_call(
        functools.partial(_norm_matmul_kernel, blocks_a, len(x_parts)),
        grid=(n // tm, m // tn),
        in_specs=[pl.BlockSpec((tm, d), rm) for rm in row_maps] + [
            pl.BlockSpec((1, d), lambda i, j: (0, 0)),
            pl.BlockSpec((d, tn), lambda i, j: (0, j)),
        ],
        out_specs=pl.BlockSpec((tm, tn), lambda i, j: (i, j)),
        out_shape=jax.ShapeDtypeStruct((n, m), F32),
        scratch_shapes=[pltpu.VMEM((tm, d), BF16)],
        compiler_params=_params("parallel", "arbitrary"),
        name="norm_in_proj",
    )(*x_parts, g.reshape(1, d), w_bf16)


def _rot_half(x):
    n = x.shape[-1]
    half = RET_DK // 2
    fwd = pltpu.roll(x, half, axis=1)
    bwd = pltpu.roll(x, n - half, axis=1)
    lane = lax.broadcasted_iota(jnp.int32, x.shape, 1)
    return jnp.where((lane % RET_DK) < half, bwd, fwd)


def _retention_kernel(q_ref, k_ref, v_ref, g_ref, r0_ref, cos_ref, sin_ref, mask_ref, qw_ref, kw_ref,
                      gc_ref, gn_ref, o_ref, r_out_ref, r_scr):
    c = pl.program_id(1)
    rows_c = mask_ref.shape[1]
    nb = q_ref.shape[0] // rows_c

    @pl.when(c == 0)
    def _():
        r_scr[...] = r0_ref[...]

    cos = cos_ref[...]
    sin = sin_ref[...]
    q = q_ref[...]
    k = k_ref[...]
    qr = q * cos + _rot_half(q) * sin
    kr = (k * cos + _rot_half(k) * sin) * (RET_DK ** -0.5)
    qd = (qr * qw_ref[...]).astype(BF16)
    kd = (kr * kw_ref[...]).astype(BF16)
    qb = qr.astype(BF16)
    kb = kr.astype(BF16)
    v = v_ref[...].astype(BF16)
    g = g_ref[...]
    for bb in range(nb):
        rs = slice(bb * rows_c, (bb + 1) * rows_c)
        for h in range(RET_HEADS):
            ks = slice(h * RET_DK, (h + 1) * RET_DK)
            vs = slice(h * RET_DV, (h + 1) * RET_DV)
            vh = v[rs, vs]
            s = lax.dot_general(qb[rs, ks], kb[rs, ks], (((1,), (1,)), ((), ())),
                                preferred_element_type=F32) * mask_ref[h]
            r_h = r_scr[bb, h]
            o = jnp.dot(s.astype(BF16), vh, preferred_element_type=F32)
            o = o + jnp.dot(qd[rs, ks], r_h.astype(BF16), preferred_element_type=F32)
            kv = lax.dot_general(kd[rs, ks], vh, (((0,), (0,)), ((), ())), preferred_element_type=F32)
            r_scr[bb, h] = gc_ref[h] * r_h + kv
            mu = jnp.mean(o, axis=-1, keepdims=True)
            var = jnp.mean(jnp.square(o - mu), axis=-1, keepdims=True)
            on = ((o - mu) * lax.rsqrt(var + EPS)) * gn_ref[:, vs]
            gh = g[rs, vs]
            o_ref[rs, vs] = ((gh * jax.nn.sigmoid(gh)) * on).astype(BF16)

    @pl.when(c == pl.num_programs(1) - 1)
    def _():
        r_out_ref[...] = r_scr[...]


def retention_path(p, row0, b, t, pos0, r0_layers, layer, gn_g):
    c = RET_CHUNK if t % RET_CHUNK == 0 else t
    nc = t // c
    blk0 = row0 // c
    assert row0 % c == 0
    log_g = jnp.log1p(-(2.0 ** (-5.0 - jnp.arange(RET_HEADS, dtype=F32))))
    idx = jnp.arange(c, dtype=F32)
    diff = idx[:, None] - idx[None, :]
    mask = jnp.where(diff[None] >= 0, jnp.exp(jnp.maximum(diff, 0.0)[None] * log_g[:, None, None]), 0.0)
    k_w = jnp.exp((c - 1 - idx)[:, None] * log_g[None, :])
    q_w = jnp.exp((idx + 1.0)[:, None] * log_g[None, :])
    g_c = jnp.exp(c * log_g)
    qw_tab = jnp.repeat(q_w, RET_DK, axis=1)
    kw_tab = jnp.repeat(k_w, RET_DK, axis=1)
    gc_tab = jnp.broadcast_to(g_c[:, None, None], (RET_HEADS, 1, RET_DV))
    half = RET_DK // 2
    pos = pos0 + jnp.arange(t, dtype=F32)
    freq = ROPE_BASE ** (-jnp.arange(half, dtype=F32) / half)
    ang = pos[:, None] * freq[None, :]
    cos_h = jnp.concatenate([jnp.cos(ang), jnp.cos(ang)], axis=1)
    sin_h = jnp.concatenate([-jnp.sin(ang), jnp.sin(ang)], axis=1)
    cos_tab = jnp.tile(cos_h, (1, RET_HEADS))
    sin_tab = jnp.tile(sin_h, (1, RET_HEADS))
    nb = _pick(b, (8, 4, 2, 1)) if nc == 1 and c * 8 <= RET_CHUNK else 1
    if nb > 1:
        cos_tab, sin_tab, qw_tab, kw_tab = [jnp.tile(tab, (nb, 1)) for tab in (cos_tab, sin_tab, qw_tab, kw_tab)]
    rb = nb * c
    blk0 = row0 // rb
    assert row0 % rb == 0

    rows = lambda bi, ci: blk0 + bi * nc + ci
    state_spec = pl.BlockSpec((nb, RET_HEADS, RET_DK, RET_DV), lambda bi, ci: (bi, 0, 0, 0))
    o, r_new = pl.pallas_call(
        _retention_kernel,
        grid=(b // nb, nc),
        in_specs=[
            pl.BlockSpec((rb, RET_QK), lambda bi, ci: (rows(bi, ci), 0)),
            pl.BlockSpec((rb, RET_QK), lambda bi, ci: (rows(bi, ci), 1)),
            pl.BlockSpec((rb, RET_V), lambda bi, ci: (rows(bi, ci), COL_V)),
            pl.BlockSpec((rb, RET_V), lambda bi, ci: (rows(bi, ci), COL_GRET)),
            pl.BlockSpec((None, nb, RET_HEADS, RET_DK, RET_DV), lambda bi, ci: (layer, bi, 0, 0, 0)),
            pl.BlockSpec((rb, RET_QK), lambda bi, ci: (ci, 0)),
            pl.BlockSpec((rb, RET_QK), lambda bi, ci: (ci, 0)),
            pl.BlockSpec((RET_HEADS, c, c), lambda bi, ci: (0, 0, 0)),
            pl.BlockSpec((rb, RET_QK), lambda bi, ci: (0, 0)),
            pl.BlockSpec((rb, RET_QK), lambda bi, ci: (0, 0)),
            pl.BlockSpec((RET_HEADS, 1, RET_DV), lambda bi, ci: (0, 0, 0)),
            pl.BlockSpec((1, RET_V), lambda bi, ci: (0, 0)),
        ],
        out_specs=[
            pl.BlockSpec((rb, RET_V), lambda bi, ci: (bi * nc + ci, 0)),
            state_spec,
        ],
        out_shape=[
            jax.ShapeDtypeStruct((b * t, RET_V), BF16),
            jax.ShapeDtypeStruct((b, RET_HEADS, RET_DK, RET_DV), F32),
        ],
        scratch_shapes=[pltpu.VMEM((nb, RET_HEADS, RET_DK, RET_DV), F32)],
        compiler_params=_params("parallel", "arbitrary"),
        name="retention",
    )(p, p, p, p, r0_layers, cos_tab, sin_tab, mask, qw_tab, kw_tab, gc_tab, gn_g.reshape(1, RET_V))
    return o, r_new


GELU_C0 = math.sqrt(2.0 / math.pi)
GELU_C1 = GELU_C0 * 0.044715


def _gelu_tanh(x):
    return x * (0.5 * (1.0 + jnp.tanh(math.sqrt(2.0 / math.pi) * (x + 0.044715 * (x * x * x)))))


def _rglru_gates(xc, wa_ref, ba_ref, wx_ref, bx_ref, lam_ref):
    xcb = xc.astype(BF16)
    nblk = wa_ref.shape[0]
    wdt = wa_ref.shape[1]
    ra = jnp.concatenate(
        [jnp.dot(xcb[:, j * wdt:(j + 1) * wdt], wa_ref[j], preferred_element_type=F32) for j in range(nblk)],
        axis=1)
    ri = jnp.concatenate(
        [jnp.dot(xcb[:, j * wdt:(j + 1) * wdt], wx_ref[j], preferred_element_type=F32) for j in range(nblk)],
        axis=1)
    r = jax.nn.sigmoid(ra + ba_ref[...])
    i = jax.nn.sigmoid(ri + bx_ref[...])
    z = -lam_ref[...]
    softplus = jnp.maximum(z, 0.0) + jnp.log1p(jnp.exp(-jnp.abs(z)))
    log_a = (-RG_C * r) * softplus
    a = jnp.exp(log_a)
    one_m_a2 = -jnp.tanh(log_a) * (a * a + 1.0)
    return a, jnp.sqrt(one_m_a2) * (i * xc)


def _rglru_rows_kernel(bsz, tc, nin, chained, *refs):
    xr_refs, g_refs = refs[:nin], refs[nin:2 * nin]
    (buf_ref, h0_ref, cw_ref, cb_ref, wa_ref, ba_ref, wx_ref, bx_ref, lam_ref, o_ref, hl_ref, nb_ref,
     xcat_scr, a_scr, u_scr, hs_scr, h_scr) = refs[2 * nin:]
    step = pl.program_id(0)
    rows = tc * bsz
    hist = (CONV_W - 1) * bsz
    bpi = bsz // nin

    def time_major(x):
        return jnp.swapaxes(x, 0, 1).reshape(x.shape[0] * x.shape[1], D_RNN)

    def batch_major(blocks):
        return jnp.concatenate([blk[...].reshape(bpi, tc, D_RNN) for blk in blocks], axis=0)

    def load_state():
        xcat_scr[pl.ds(rows, hist), :] = time_major(buf_ref[...])
        h_scr[...] = h0_ref[...]

    if chained:
        pl.when(step == 0)(load_state)
    else:
        load_state()

    xcat_scr[pl.ds(0, hist), :] = xcat_scr[pl.ds(rows, hist), :]
    xcat_scr[pl.ds(hist, rows), :] = time_major(batch_major(xr_refs))
    xc = cb_ref[...] + xcat_scr[pl.ds(0, rows), :] * cw_ref[0:1, :]
    for w in range(1, CONV_W):
        xc = xc + xcat_scr[pl.ds(w * bsz, rows), :] * cw_ref[w:w + 1, :]
    a, u = _rglru_gates(xc, wa_ref, ba_ref, wx_ref, bx_ref, lam_ref)
    a_scr[...] = a.reshape(tc, bsz, D_RNN)
    u_scr[...] = u.reshape(tc, bsz, D_RNN)

    def scan_step(t, h):
        h = a_scr[t] * h + u_scr[t]
        hs_scr[t] = h
        return h

    h_scr[...] = lax.fori_loop(0, tc, scan_step, h_scr[...])
    gate = _gelu_tanh(batch_major(g_refs))
    o_ref[...] = (jnp.swapaxes(hs_scr[...], 0, 1) * gate).astype(BF16)

    def store_state():
        hl_ref[...] = h_scr[...]
        nb_ref[...] = jnp.swapaxes(xcat_scr[pl.ds(rows, hist), :].reshape(CONV_W - 1, bsz, D_RNN), 0, 1)

    if chained:
        pl.when(step == pl.num_programs(0) - 1)(store_state)
    else:
        store_state()


RGLRU_STEP_ROWS = 256


def rglru_rows_path(p, row0, b, t, buf0, h0, conv_w, conv_b, wa4, ba, wx4, bx, lam):
    assert t >= CONV_W - 1 and b % SUBLANES == 0 and RGLRU_STEP_ROWS % SUBLANES == 0
    d = D_RNN
    hist = CONV_W - 1
    nblk, wdt = wa4.shape[0], wa4.shape[1]
    chained = b * t > RGLRU_STEP_ROWS and b == SUBLANES
    if chained:
        bg, tc, nin = b, RGLRU_STEP_ROWS // b, b
        steps = t // tc
        assert t % tc == 0 and row0 % tc == 0
        blk0 = row0 // tc
        row_map = lambda bi, col: (lambda s: (blk0 + bi * steps + s, col))
        state2, state3 = (lambda s: (0, 0)), (lambda s: (0, 0, 0))
        out_map = lambda s: (0, s, 0)
    else:
        bg, tc, nin = max(SUBLANES, RGLRU_STEP_ROWS // t), t, 1
        steps = b // bg
        assert b % bg == 0 and row0 % (bg * t) == 0
        blk0 = row0 // (bg * t)
        row_map = lambda bi, col: (lambda s: (blk0 + s, col))
        state2, state3 = (lambda s: (s, 0)), (lambda s: (s, 0, 0))
        out_map = lambda s: (s, 0, 0)
    in_rows = (bg // nin) * tc
    const2 = lambda s: (0, 0)
    const3 = lambda s: (0, 0, 0)
    out, h_last, new_buf = pl.pallas_call(
        functools.partial(_rglru_rows_kernel, bg, tc, nin, chained),
        grid=(steps,),
        in_specs=[pl.BlockSpec((in_rows, d), row_map(bi, COL_XR)) for bi in range(nin)]
        + [pl.BlockSpec((in_rows, d), row_map(bi, COL_GRNN)) for bi in range(nin)]
        + [
            pl.BlockSpec((bg, hist, d), state3),
            pl.BlockSpec((bg, d), state2),
            pl.BlockSpec((CONV_W, d), const2),
            pl.BlockSpec((1, d), const2),
            pl.BlockSpec((nblk, wdt, wdt), const3),
            pl.BlockSpec((1, d), const2),
            pl.BlockSpec((nblk, wdt, wdt), const3),
            pl.BlockSpec((1, d), const2),
            pl.BlockSpec((1, d), const2),
        ],
        out_specs=[
            pl.BlockSpec((bg, tc, d), out_map),
            pl.BlockSpec((bg, d), state2),
            pl.BlockSpec((bg, hist, d), state3),
        ],
        out_shape=[
            jax.ShapeDtypeStruct((b, t, d), BF16),
            jax.ShapeDtypeStruct((b, d), F32),
            jax.ShapeDtypeStruct((b, hist, d), F32),
        ],
        scratch_shapes=[
            pltpu.VMEM(((CONV_W - 1 + tc) * bg, d), F32),
            pltpu.VMEM((tc, bg, d), F32),
            pltpu.VMEM((tc, bg, d), F32),
            pltpu.VMEM((tc, bg, d), F32),
            pltpu.VMEM((bg, d), F32),
        ],
        compiler_params=_params("arbitrary"),
        name="rglru_rows",
    )(*([p] * (2 * nin)), buf0, h0, conv_w, conv_b.reshape(1, d), wa4, ba.reshape(1, d), wx4, bx.reshape(1, d),
      lam.reshape(1, d))
    return out.reshape(b * t, d), h_last, new_buf


def _block_diag_tiles(w, tile):
    nb, bs, _ = w.shape
    per = tile // bs
    w = w.reshape(nb // per, per, bs, bs)
    eye = jnp.eye(per, dtype=w.dtype)
    dense = jnp.einsum("gpcd,pq->gpcqd", w, eye).reshape(nb // per, tile, tile)
    return dense.astype(BF16)


def _merge_kernel(blocks_a, nx, *refs):
    x_refs, refs = refs[:nx], refs[nx:]
    ro_refs, rn_refs = refs[0:2], refs[2:4]
    ga_ref, gb_ref, wr_ref, wn_ref, wo_ref, g2_ref, wq_ref, yt_ref, xn_ref, q_ref = refs[4:]
    ret_out = jnp.dot(_pick_rows(ro_refs, blocks_a), wr_ref[...], preferred_element_type=F32)
    rnn_out = jnp.dot(_pick_rows(rn_refs, blocks_a), wn_ref[...], preferred_element_type=F32)
    merged = jax.nn.sigmoid(ga_ref[...]) * ret_out + jax.nn.sigmoid(gb_ref[...]) * rnn_out
    y = _pick_rows(x_refs, blocks_a) + jnp.dot(merged.astype(BF16), wo_ref[...], preferred_element_type=F32)
    yt = y.T
    yt_ref[...] = yt
    ms = jnp.mean(yt * yt, axis=0, keepdims=True)
    xn = ((yt * lax.rsqrt(ms + EPS)) * g2_ref[...]).astype(BF16)
    xn_ref[...] = xn
    q_ref[...] = jnp.dot(wq_ref[...], xn, preferred_element_type=F32)


def merge_proj(x_parts, ret_parts, rnn_parts, p, w_ret_out, w_rnn_out, w_o, norm2_g, wq_t):
    n, d = p.shape[0], x_parts[0].shape[1]
    m = wq_t.shape[0]
    tm = _pick(math.gcd(*[r.shape[0] for r in ret_parts]), (512, 256, 128))
    blocks_a, part_maps = _row_part_maps(ret_parts, tm)
    assert len(x_parts) == 1 or x_parts[0].shape[0] == ret_parts[0].shape[0]
    _, x_maps = _row_part_maps(x_parts, tm) if len(x_parts) == 2 else (None, [lambda i: (i, 0)])
    const = lambda i: (0, 0)
    tok = lambda i: (0, i)
    return pl.pallas_call(
        functools.partial(_merge_kernel, blocks_a, len(x_parts)),
        grid=(n // tm,),
        in_specs=[pl.BlockSpec((tm, d), mp) for mp in x_maps + part_maps + part_maps] + [
            pl.BlockSpec((tm, d), lambda i: (i, COL_GA)),
            pl.BlockSpec((tm, d), lambda i: (i, COL_GB)),
            pl.BlockSpec((d, d), const),
            pl.BlockSpec((d, d), const),
            pl.BlockSpec((d, d), const),
            pl.BlockSpec((d, 1), const),
            pl.BlockSpec((m, d), const),
        ],
        out_specs=[pl.BlockSpec((d, tm), tok), pl.BlockSpec((d, tm), tok), pl.BlockSpec((m, tm), tok)],
        out_shape=[jax.ShapeDtypeStruct((d, n), F32), jax.ShapeDtypeStruct((d, n), BF16),
                   jax.ShapeDtypeStruct((m, n), F32)],
        compiler_params=_params("parallel"),
        name="merge_out_proj",
    )(*x_parts, *ret_parts, *rnn_parts, p, p, w_ret_out, w_rnn_out, w_o, norm2_g.reshape(d, 1), wq_t)


def _sort_pairs(n):
    def merge(lo, hi, r):
        step = r * 2
        if step < hi - lo:
            yield from merge(lo, hi, step)
            yield from merge(lo + r, hi, step)
            yield from [(i, i + r) for i in range(lo + r, hi - r, step)]
        else:
            yield (lo, lo + r)

    def sort(lo, hi):
        if hi - lo >= 1:
            mid = lo + (hi - lo) // 2
            yield from sort(lo, mid)
            yield from sort(mid + 1, hi)
            yield from merge(lo, hi, 1)

    return list(sort(0, n - 1))


_SORT16 = _sort_pairs(PEER_TOPK)


def _cmpx(vals, i, j):
    a, b = vals[i], vals[j]
    if b is None:
        return
    if a is None:
        vals[i], vals[j] = b, None
        return
    vals[i], vals[j] = jnp.maximum(a, b), jnp.minimum(a, b)


def _sort_desc(vals):
    vals = list(vals)
    for i, j in _SORT16:
        _cmpx(vals, i, j)
    return vals


def _merge_top(a, b):
    k = PEER_TOPK
    a = list(a) + [None] * (k - len(a))
    b = list(b) + [None] * (k - len(b))
    out = []
    for r in range(k):
        x, y = a[r], b[k - 1 - r]
        out.append(y if x is None else (x if y is None else jnp.maximum(x, y)))
    d = k // 2
    while d >= 1:
        for i in range(k):
            if not i & d:
                _cmpx(out, i, i + d)
        d //= 2
    return out


def _top_sorted(ref):
    groups = []
    for g0 in range(0, N_KEYS, PEER_TOPK):
        vals = [ref[pl.ds((g0 + j) * PEER_HEADS, PEER_HEADS), :] for j in range(PEER_TOPK)]
        groups.append(_sort_desc(vals))
    while len(groups) > 1:
        groups = [_merge_top(groups[i], groups[i + 1]) for i in range(0, len(groups), 2)]
    return groups[0]


def _peer_select_kernel(q_ref, k1_ref, k2_ref, c1_ref, e1_ref, r2_ref, e2_ref, s1_scr, s2_scr, r2_scr, e2_scr):
    nh = PEER_HEADS
    half_rows = q_ref.shape[0] // 2
    tn = q_ref.shape[1]
    s1 = jnp.dot(k1_ref[...], q_ref[pl.ds(0, half_rows), :].astype(BF16), preferred_element_type=F32)
    s2 = jnp.dot(k2_ref[...], q_ref[pl.ds(half_rows, half_rows), :].astype(BF16), preferred_element_type=F32)
    for lt in range(tn // LANES):
        s1_scr[lt] = s1[:, lt * LANES:(lt + 1) * LANES]
        s2_scr[lt] = s2[:, lt * LANES:(lt + 1) * LANES]
    for lt in range(tn // LANES):
        lanes = pl.ds(lt * LANES, LANES)
        s1_t, s2_t, r2_t, e2_t = s1_scr.at[lt], s2_scr.at[lt], r2_scr.at[lt], e2_scr.at[lt]
        a = _top_sorted(s1_t)
        b = _top_sorted(s2_t)
        k = PEER_TOPK
        lists = []
        for j in range(1, k + 1):
            col = [a[r - 1] + b[j - 1] for r in range(j, k // j + 1)]
            row = [a[j - 1] + b[s - 1] for s in range(j + 1, k // j + 1)]
            if col:
                lists.append(col)
            if row:
                lists.append(row)
        top = lists[0]
        for other in lists[1:]:
            top = _merge_top(top, other)
        tau = top[k - 1]
        z = jnp.ones_like(tau)
        for r in range(1, k):
            z = z + jnp.exp(top[r] - top[0])
        zinv = 1.0 / z
        inf = jnp.full((nh, LANES), jnp.inf, F32)
        phi = []
        for s in range(1, k + 1):
            p = inf
            for r in range(1, k // s + 1):
                p = jnp.where(a[r - 1] + b[s - 1] >= tau, a[r - 1], p)
            phi.append(p)

        def per_key(kk, carry):
            rows = pl.ds(pl.multiple_of(kk * nh, nh), nh)
            s1k = s1_t[rows, :]
            s2k = s2_t[rows, :]
            cnt = jnp.ones((nh, LANES), F32)
            for s in range(k):
                cnt = jnp.where(s1k >= phi[s], float(s + 2), cnt)
            rank = jnp.full((nh, LANES), float(k + 1), F32)
            for s in range(k - 1, -1, -1):
                rank = jnp.where(s2k >= b[s], float(s + 1), rank)
            c1_ref[rows, lanes] = cnt
            e1_ref[rows, lanes] = jnp.exp(s1k - a[0])
            r2_t[rows, :] = rank
            e2_t[rows, :] = jnp.exp(s2k - b[0]) * zinv
            return carry

        lax.fori_loop(0, N_KEYS, per_key, 0, unroll=4)
        pack = 2 * SUBLANES
        for h in range(nh):
            for kt in range(N_KEYS // pack):
                lo = pl.ds(kt * pack * nh + h, SUBLANES, stride=nh)
                hi = pl.ds((kt * pack + SUBLANES) * nh + h, SUBLANES, stride=nh)
                dst = pl.ds(h * N_KEYS + kt * pack, pack)
                r2_ref[dst, lanes] = jnp.concatenate([r2_t[lo, :], r2_t[hi, :]], axis=0).astype(BF16)
                e2_ref[dst, lanes] = jnp.concatenate([e2_t[lo, :], e2_t[hi, :]], axis=0).astype(BF16)


def peer_select(q_t, k1, k2):
    m, n = q_t.shape
    rows = PEER_HEADS * N_KEYS
    tn = _pick(n, (256, 128))
    tok = lambda i: (0, i)
    const = lambda i: (0, 0)
    return pl.pallas_call(
        _peer_select_kernel,
        grid=(n // tn,),
        in_specs=[pl.BlockSpec((m, tn), tok), pl.BlockSpec(k1.shape, const), pl.BlockSpec(k2.shape, const)],
        out_specs=[pl.BlockSpec((rows, tn), tok)] * 4,
        out_shape=[jax.ShapeDtypeStruct((rows, n), F32), jax.ShapeDtypeStruct((rows, n), F32),
                   jax.ShapeDtypeStruct((rows, n), BF16), jax.ShapeDtypeStruct((rows, n), BF16)],
        scratch_shapes=[pltpu.VMEM((tn // LANES, rows, LANES), F32)] * 4,
        compiler_params=_params("parallel"),
        name="peer_select",
    )(q_t, k1, k2)


PEER_EXPERT_BLOCK = 1024


def _mxu_dot(lhs, rhs):
    return lax.dot_general(lhs, rhs, (((1,), (0,)), ((), ())), preferred_element_type=F32)


def _peer_dense_kernel(xt_ref, xn_ref, u_ref, vt_ref, c1_ref, e1_ref, r2_ref, e2_ref, y_ref,
                       acc_scr, act_scr):
    j = pl.program_id(1)
    nh = PEER_HEADS
    tn = xn_ref.shape[1]
    pack = 2 * SUBLANES
    i1_per_blk = u_ref.shape[0] // N_KEYS

    @pl.when(j == 0)
    def _():
        acc_scr[...] = jnp.zeros_like(acc_scr)

    wide = 2 * LANES

    def gate_block(il, nc):
        erows = pl.ds(il * N_KEYS, N_KEYS)
        hblk = _mxu_dot(u_ref[erows, :], xn_ref[:, pl.ds(nc * wide, wide)])
        for lw in range(wide // LANES):
            lc = nc * (wide // LANES) + lw
            lanes = pl.ds(lc * LANES, LANES)
            c1b = [jnp.broadcast_to(c1_ref[pl.ds(il * nh + h, 1), lanes], (pack, LANES)).astype(BF16)
                   for h in range(nh)]
            e1b = [jnp.broadcast_to(e1_ref[pl.ds(il * nh + h, 1), lanes], (pack, LANES)).astype(BF16)
                   for h in range(nh)]
            zero = jnp.zeros((pack, LANES), BF16)
            half = jnp.full((), 0.5, BF16)
            gelu_c0 = jnp.full((), GELU_C0, BF16)
            gelu_c1 = jnp.full((), GELU_C1, BF16)
            for it in range(N_KEYS // pack):
                rows = pl.ds(il * N_KEYS + it * pack, pack)
                gate = None
                for h in range(nh):
                    krows = pl.ds(h * N_KEYS + it * pack, pack)
                    sel = jnp.minimum(jnp.maximum(c1b[h] - r2_ref[krows, lanes], zero), e2_ref[krows, lanes])
                    term = sel * e1b[h]
                    gate = term if gate is None else gate + term
                x = hblk[it * pack:(it + 1) * pack, lw * LANES:(lw + 1) * LANES].astype(BF16)
                t = jnp.tanh(x * (gelu_c0 + gelu_c1 * (x * x)))
                act_scr[rows, lanes] = (x * (half + half * t)) * gate

    for nc in range(tn // wide):
        for il in range(i1_per_blk):
            gate_block(il, nc)
        cols = pl.ds(nc * wide, wide)
        acc_scr[:, cols] += _mxu_dot(vt_ref[...], act_scr[:, cols])

    @pl.when(j == pl.num_programs(1) - 1)
    def _():
        y_ref[...] = (xt_ref[...] + acc_scr[...]).T


def peer_dense(xt, xn_t, u_layers, layer, vt_bf16, c1, e1, r2, e2):
    d, n = xt.shape
    ne = u_layers.shape[1]
    tn = _pick(n, (512, 256, 128))
    te = PEER_EXPERT_BLOCK
    i1_per_blk = te // N_KEYS
    tok = lambda i, j: (0, i)
    return pl.pallas_call(
        _peer_dense_kernel,
        grid=(n // tn, ne // te),
        in_specs=[
            pl.BlockSpec((d, tn), tok),
            pl.BlockSpec((d, tn), tok),
            pl.BlockSpec((None, te, d), lambda i, j: (layer, j, 0)),
            pl.BlockSpec((d, te), lambda i, j: (0, j)),
            pl.BlockSpec((i1_per_blk * PEER_HEADS, tn), lambda i, j: (j, i)),
            pl.BlockSpec((i1_per_blk * PEER_HEADS, tn), lambda i, j: (j, i)),
            pl.BlockSpec((PEER_HEADS * N_KEYS, tn), tok),
            pl.BlockSpec((PEER_HEADS * N_KEYS, tn), tok),
        ],
        out_specs=pl.BlockSpec((tn, d), lambda i, j: (i, 0)),
        out_shape=jax.ShapeDtypeStruct((n, d), F32),
        scratch_shapes=[pltpu.VMEM((d, tn), F32), pltpu.VMEM((te, tn), BF16)],
        compiler_params=_params("parallel", "arbitrary"),
        name="peer_dense",
    )(xt, xn_t, u_layers, vt_bf16, c1, e1, r2, e2)


def _interleaved_keys(keys_p):
    nh, nk, half = keys_p.shape
    eye = jnp.eye(nh, dtype=keys_p.dtype)
    return jnp.einsum("hkd,hg->khgd", keys_p, eye).reshape(nk * nh, nh * half).astype(BF16)


def peer_query_weights(wq):
    d = wq.shape[0]
    wq_t = wq.reshape(d, PEER_HEADS, 2, PEER_HALF).transpose(2, 1, 3, 0).reshape(2 * PEER_HEADS * PEER_HALF, d)
    return wq_t.astype(BF16)


def peer_layer(xt, xn_t, q_t, keys, u_layers, layer, v_tab):
    k1 = _interleaved_keys(keys[:, 0])
    k2 = _interleaved_keys(keys[:, 1])
    c1, e1, r2, e2 = peer_select(q_t, k1, k2)
    return peer_dense(xt, xn_t, u_layers, layer, v_tab.T.astype(BF16), c1, e1, r2, e2)


def _rmsnorm_kernel(x_ref, g_ref, y_ref):
    x = x_ref[...]
    ms = jnp.mean(x * x, axis=-1, keepdims=True)
    y_ref[...] = (x * lax.rsqrt(ms + EPS)) * g_ref[...]


def final_norm(x, row0, rows, g):
    d = x.shape[1]
    tm = _pick(math.gcd(rows, row0) if row0 else rows, (512, 256, 128, 64, 32, 16, 8))
    blk0 = row0 // tm
    return pl.pallas_call(
        _rmsnorm_kernel,
        grid=(rows // tm,),
        in_specs=[pl.BlockSpec((tm, d), lambda i: (blk0 + i, 0)), pl.BlockSpec((1, d), lambda i: (0, 0))],
        out_specs=pl.BlockSpec((tm, d), lambda i: (i, 0)),
        out_shape=jax.ShapeDtypeStruct((rows, d), F32),
        compiler_params=_params("parallel"),
        name="final_norm",
    )(x, g.reshape(1, d))


def _trunk(groups, norm1_g, norm2_g, normf_g, w_in, ret_gn_g, w_ret_out, conv_w, conv_b, rg_wa, rg_ba, rg_wx,
           rg_bx, rg_lambda, w_rnn_out, w_o, peer_wq, peer_keys, peer_u, peer_v):
    shapes = [(g[0].shape[0], g[0].shape[1]) for g in groups]
    x_parts = [g[0].reshape(-1, D_MODEL) for g in groups]
    row0s = np.cumsum([0] + [b * t for b, t in shapes]).tolist()
    states = [([], [], []) for _ in groups]
    for l in range(DEPTH):
        p = norm_matmul(x_parts, norm1_g[l], w_in[l].astype(BF16))
        wa4 = _block_diag_tiles(rg_wa[l], 256)
        wx4 = _block_diag_tiles(rg_wx[l], 256)
        ret_parts, rnn_parts = [], []
        for gi, (xg, r0, h0, buf0, pos0) in enumerate(groups):
            b, t = shapes[gi]
            row0 = row0s[gi]
            o, r_new = retention_path(p, row0, b, t, pos0, r0, l, ret_gn_g[l])
            rg_args = (conv_w[l], conv_b[l], wa4, rg_ba[l], wx4, rg_bx[l], rg_lambda[l])
            hg, h_last, nb = rglru_rows_path(p, row0, b, t, buf0[l], h0[l], *rg_args)
            ret_parts.append(o)
            rnn_parts.append(hg)
            states[gi][0].append(r_new)
            states[gi][1].append(h_last)
            states[gi][2].append(nb)
        xt, xn_t, q_t = merge_proj(x_parts, ret_parts, rnn_parts, p, w_ret_out[l].astype(BF16),
                                   w_rnn_out[l].astype(BF16), w_o[l].astype(BF16), norm2_g[l],
                                   peer_query_weights(peer_wq[l]))
        x = peer_layer(xt, xn_t, q_t, peer_keys[l], peer_u, l, peer_v[l])
        x_parts = [x]
    outs = []
    for gi, (b, t) in enumerate(shapes):
        y = final_norm(x, row0s[gi], b * t, normf_g).reshape(b, t, D_MODEL)
        outs.append((y, jnp.stack(states[gi][0]), jnp.stack(states[gi][1]), jnp.stack(states[gi][2])))
    return outs


def kernel(x_prompt, x_sample, state_ret, state_rnn, state_conv, norm1_g, norm2_g, normf_g, w_in, ret_gn_g,
           w_ret_out, conv_w, conv_b, rg_wa, rg_ba, rg_wx, rg_bx, rg_lambda, w_rnn_out, w_o, peer_wq, peer_keys,
           peer_u, peer_v):
    bp = x_prompt.shape[0]
    dt = x_prompt.dtype
    zr = jnp.zeros((DEPTH, bp, RET_HEADS, RET_DK, RET_DV), dt)
    zh = jnp.zeros((DEPTH, bp, D_RNN), dt)
    zc = jnp.zeros((DEPTH, bp, CONV_W - 1, D_RNN), dt)
    groups = [(x_prompt, zr, zh, zc, 0.0), (x_sample, state_ret, state_rnn, state_conv, float(PAST_LEN))]
    (yp, rp, hp, cp), (ys, rs, hs, cs) = _trunk(
        groups, norm1_g, norm2_g, normf_g, w_in, ret_gn_g, w_ret_out, conv_w, conv_b, rg_wa, rg_ba, rg_wx, rg_bx,
        rg_lambda, w_rnn_out, w_o, peer_wq, peer_keys, peer_u, peer_v)
    return (yp, ys, rp, hp, cp, rs, hs, cs)
```

```python
import functools
import math

import jax
import jax.numpy as jnp
import numpy as np
from jax import lax
from jax.experimental import pallas as pl
from jax.experimental.pallas import tpu as pltpu

D_MODEL = 1024
DEPTH = 2
PAST_LEN = 16384
RET_HEADS = 8
RET_DK = 64
RET_DV = 128
RET_QK = RET_HEADS * RET_DK
RET_V = RET_HEADS * RET_DV
RET_CHUNK = 128
ROPE_BASE = 10000.0
D_RNN = 1024
CONV_W = 4
RG_C = 8.0
COL_V, COL_GRET, COL_XR, COL_GRNN, COL_GA, COL_GB = 1, 2, 3, 4, 5, 6
PEER_HEADS = 8
N_KEYS = 128
PEER_DKEY = 256
PEER_HALF = PEER_DKEY // 2
PEER_TOPK = 16
EPS = 1e-6

SUBLANES = 8
LANES = 128
VMEM_LIMIT = 56 * 1024 * 1024

F32 = jnp.float32
BF16 = jnp.bfloat16


def _params(*sem):
    return pltpu.CompilerParams(dimension_semantics=sem, vmem_limit_bytes=VMEM_LIMIT)


def _pick(n, prefs):
    for p in prefs:
        if n % p == 0:
            return p
    return n


def _row_part_maps(parts, tm):
    assert len(parts) in (1, 2) and all(p.shape[0] % tm == 0 for p in parts)
    blocks_a = parts[0].shape[0] // tm
    maps = [lambda i, *_: (jnp.minimum(i, blocks_a - 1), 0)]
    if len(parts) == 2:
        maps.append(lambda i, *_: (jnp.maximum(i - blocks_a, 0), 0))
    return blocks_a, maps


def _pick_rows(refs, blocks_a):
    x = refs[0][...]
    if len(refs) == 2:
        x = jnp.where(pl.program_id(0) < blocks_a, x, refs[1][...])
    return x


def _norm_matmul_kernel(blocks_a, nparts, *refs):
    x_refs = refs[:nparts]
    g_ref, w_ref, y_ref, xn_scr = refs[nparts:]

    @pl.when(pl.program_id(1) == 0)
    def _():
        x = _pick_rows(x_refs, blocks_a)
        ms = jnp.mean(x * x, axis=-1, keepdims=True)
        xn_scr[...] = ((x * lax.rsqrt(ms + EPS)) * g_ref[...]).astype(BF16)

    y_ref[...] = jnp.dot(xn_scr[...], w_ref[...], preferred_element_type=F32)


def norm_matmul(x_parts, g, w_bf16):
    n = sum(p.shape[0] for p in x_parts)
    d = x_parts[0].shape[1]
    m = w_bf16.shape[1]
    tm = _pick(math.gcd(*[p.shape[0] for p in x_parts]), (1024, 512, 256, 128))
    tn = _pick(m, (1792, 1024, 512, 256, 128))
    blocks_a, row_maps = _row_part_maps(x_parts, tm)
    return pl.pallas_call(
        functools.partial(_norm_matmul_kernel, blocks_a, len(x_parts)),
        grid=(n // tm, m // tn),
        in_specs=[pl.BlockSpec((tm, d), rm) for rm in row_maps] + [
            pl.BlockSpec((1, d), lambda i, j: (0, 0)),
            pl.BlockSpec((d, tn), lambda i, j: (0, j)),
        ],
        out_specs=pl.BlockSpec((tm, tn), lambda i, j: (i, j)),
        out_shape=jax.ShapeDtypeStruct((n, m), F32),
        scratch_shapes=[pltpu.VMEM((tm, d), BF16)],
        compiler_params=_params("parallel", "arbitrary"),
        name="norm_in_proj",
    )(*x_parts, g.reshape(1, d), w_bf16)


def _rot_half(x):
    n = x.shape[-1]
    half = RET_DK // 2
    fwd = pltpu.roll(x, half, axis=1)
    bwd = pltpu.roll(x, n - half, axis=1)
    lane = lax.broadcasted_iota(jnp.int32, x.shape, 1)
    return jnp.where((lane % RET_DK) < half, bwd, fwd)


def _retention_kernel(q_ref, k_ref, v_ref, g_ref, r0_ref, cos_ref, sin_ref, mask_ref, qw_ref, kw_ref,
                      gc_ref, gn_ref, o_ref, r_out_ref, r_scr):
    c = pl.program_id(1)
    rows_c = mask_ref.shape[1]
    nb = q_ref.shape[0] // rows_c

    @pl.when(c == 0)
    def _():
        r_scr[...] = r0_ref[...]

    cos = cos_ref[...]
    sin = sin_ref[...]
    q = q_ref[...]
    k = k_ref[...]
    qr = q * cos + _rot_half(q) * sin
    kr = (k * cos + _rot_half(k) * sin) * (RET_DK ** -0.5)
    qd = (qr * qw_ref[...]).astype(BF16)
    kd = (kr * kw_ref[...]).astype(BF16)
    qb = qr.astype(BF16)
    kb = kr.astype(BF16)
    v = v_ref[...].astype(BF16)
    g = g_ref[...]
    for bb in range(nb):
        rs = slice(bb * rows_c, (bb + 1) * rows_c)
        for h in range(RET_HEADS):
            ks = slice(h * RET_DK, (h + 1) * RET_DK)
            vs = slice(h * RET_DV, (h + 1) * RET_DV)
            vh = v[rs, vs]
            s = lax.dot_general(qb[rs, ks], kb[rs, ks], (((1,), (1,)), ((), ())),
                                preferred_element_type=F32) * mask_ref[h]
            r_h = r_scr[bb, h]
            o = jnp.dot(s.astype(BF16), vh, preferred_element_type=F32)
            o = o + jnp.dot(qd[rs, ks], r_h.astype(BF16), preferred_element_type=F32)
            kv = lax.dot_general(kd[rs, ks], vh, (((0,), (0,)), ((), ())), preferred_element_type=F32)
            r_scr[bb, h] = gc_ref[h] * r_h + kv
            mu = jnp.mean(o, axis=-1, keepdims=True)
            var = jnp.mean(jnp.square(o - mu), axis=-1, keepdims=True)
            on = ((o - mu) * lax.rsqrt(var + EPS)) * gn_ref[:, vs]
            gh = g[rs, vs]
            o_ref[rs, vs] = ((gh * jax.nn.sigmoid(gh)) * on).astype(BF16)

    @pl.when(c == pl.num_programs(1) - 1)
    def _():
        r_out_ref[...] = r_scr[...]


def retention_path(p, row0, b, t, pos0, r0_layers, layer, gn_g):
    c = RET_CHUNK if t % RET_CHUNK == 0 else t
    nc = t // c
    blk0 = row0 // c
    assert row0 % c == 0
    log_g = jnp.log1p(-(2.0 ** (-5.0 - jnp.arange(RET_HEADS, dtype=F32))))
    idx = jnp.arange(c, dtype=F32)
    diff = idx[:, None] - idx[None, :]
    mask = jnp.where(diff[None] >= 0, jnp.exp(jnp.maximum(diff, 0.0)[None] * log_g[:, None, None]), 0.0)
    k_w = jnp.exp((c - 1 - idx)[:, None] * log_g[None, :])
    q_w = jnp.exp((idx + 1.0)[:, None] * log_g[None, :])
    g_c = jnp.exp(c * log_g)
    qw_tab = jnp.repeat(q_w, RET_DK, axis=1)
    kw_tab = jnp.repeat(k_w, RET_DK, axis=1)
    gc_tab = jnp.broadcast_to(g_c[:, None, None], (RET_HEADS, 1, RET_DV))
    half = RET_DK // 2
    pos = pos0 + jnp.arange(t, dtype=F32)
    freq = ROPE_BASE ** (-jnp.arange(half, dtype=F32) / half)
    ang = pos[:, None] * freq[None, :]
    cos_h = jnp.concatenate([jnp.cos(ang), jnp.cos(ang)], axis=1)
    sin_h = jnp.concatenate([-jnp.sin(ang), jnp.sin(ang)], axis=1)
    cos_tab = jnp.tile(cos_h, (1, RET_HEADS))
    sin_tab = jnp.tile(sin_h, (1, RET_HEADS))
    nb = _pick(b, (8, 4, 2, 1)) if nc == 1 and c * 8 <= RET_CHUNK else 1
    if nb > 1:
        cos_tab, sin_tab, qw_tab, kw_tab = [jnp.tile(tab, (nb, 1)) for tab in (cos_tab, sin_tab, qw_tab, kw_tab)]
    rb = nb * c
    blk0 = row0 // rb
    assert row0 % rb == 0

    rows = lambda bi, ci: blk0 + bi * nc + ci
    state_spec = pl.BlockSpec((nb, RET_HEADS, RET_DK, RET_DV), lambda bi, ci: (bi, 0, 0, 0))
    o, r_new = pl.pallas_call(
        _retention_kernel,
        grid=(b // nb, nc),
        in_specs=[
            pl.BlockSpec((rb, RET_QK), lambda bi, ci: (rows(bi, ci), 0)),
            pl.BlockSpec((rb, RET_QK), lambda bi, ci: (rows(bi, ci), 1)),
            pl.BlockSpec((rb, RET_V), lambda bi, ci: (rows(bi, ci), COL_V)),
            pl.BlockSpec((rb, RET_V), lambda bi, ci: (rows(bi, ci), COL_GRET)),
            pl.BlockSpec((None, nb, RET_HEADS, RET_DK, RET_DV), lambda bi, ci: (layer, bi, 0, 0, 0)),
            pl.BlockSpec((rb, RET_QK), lambda bi, ci: (ci, 0)),
            pl.BlockSpec((rb, RET_QK), lambda bi, ci: (ci, 0)),
            pl.BlockSpec((RET_HEADS, c, c), lambda bi, ci: (0, 0, 0)),
            pl.BlockSpec((rb, RET_QK), lambda bi, ci: (0, 0)),
            pl.BlockSpec((rb, RET_QK), lambda bi, ci: (0, 0)),
            pl.BlockSpec((RET_HEADS, 1, RET_DV), lambda bi, ci: (0, 0, 0)),
            pl.BlockSpec((1, RET_V), lambda bi, ci: (0, 0)),
        ],
        out_specs=[
            pl.BlockSpec((rb, RET_V), lambda bi, ci: (bi * nc + ci, 0)),
            state_spec,
        ],
        out_shape=[
            jax.ShapeDtypeStruct((b * t, RET_V), BF16),
            jax.ShapeDtypeStruct((b, RET_HEADS, RET_DK, RET_DV), F32),
        ],
        scratch_shapes=[pltpu.VMEM((nb, RET_HEADS, RET_DK, RET_DV), F32)],
        compiler_params=_params("parallel", "arbitrary"),
        name="retention",
    )(p, p, p, p, r0_layers, cos_tab, sin_tab, mask, qw_tab, kw_tab, gc_tab, gn_g.reshape(1, RET_V))
    return o, r_new


GELU_C0 = math.sqrt(2.0 / math.pi)
GELU_C1 = GELU_C0 * 0.044715


def _gelu_tanh(x):
    return x * (0.5 * (1.0 + jnp.tanh(math.sqrt(2.0 / math.pi) * (x + 0.044715 * (x * x * x)))))


def _rglru_gates(xc, wa_ref, ba_ref, wx_ref, bx_ref, lam_ref):
    xcb = xc.astype(BF16)
    nblk = wa_ref.shape[0]
    wdt = wa_ref.shape[1]
    ra = jnp.concatenate(
        [jnp.dot(xcb[:, j * wdt:(j + 1) * wdt], wa_ref[j], preferred_element_type=F32) for j in range(nblk)],
        axis=1)
    ri = jnp.concatenate(
        [jnp.dot(xcb[:, j * wdt:(j + 1) * wdt], wx_ref[j], preferred_element_type=F32) for j in range(nblk)],
        axis=1)
    r = jax.nn.sigmoid(ra + ba_ref[...])
    i = jax.nn.sigmoid(ri + bx_ref[...])
    z = -lam_ref[...]
    softplus = jnp.maximum(z, 0.0) + jnp.log1p(jnp.exp(-jnp.abs(z)))
    log_a = (-RG_C * r) * softplus
    a = jnp.exp(log_a)
    one_m_a2 = -jnp.tanh(log_a) * (a * a + 1.0)
    return a, jnp.sqrt(one_m_a2) * (i * xc)


def _rglru_rows_kernel(bsz, tc, nin, chained, *refs):
    xr_refs, g_refs = refs[:nin], refs[nin:2 * nin]
    (buf_ref, h0_ref, cw_ref, cb_ref, wa_ref, ba_ref, wx_ref, bx_ref, lam_ref, o_ref, hl_ref, nb_ref,
     xcat_scr, a_scr, u_scr, hs_scr, h_scr) = refs[2 * nin:]
    step = pl.program_id(0)
    rows = tc * bsz
    hist = (CONV_W - 1) * bsz
    bpi = bsz // nin

    def time_major(x):
        return jnp.swapaxes(x, 0, 1).reshape(x.shape[0] * x.shape[1], D_RNN)

    def batch_major(blocks):
        return jnp.concatenate([blk[...].reshape(bpi, tc, D_RNN) for blk in blocks], axis=0)

    def load_state():
        xcat_scr[pl.ds(rows, hist), :] = time_major(buf_ref[...])
        h_scr[...] = h0_ref[...]

    if chained:
        pl.when(step == 0)(load_state)
    else:
        load_state()

    xcat_scr[pl.ds(0, hist), :] = xcat_scr[pl.ds(rows, hist), :]
    xcat_scr[pl.ds(hist, rows), :] = time_major(batch_major(xr_refs))
    xc = cb_ref[...] + xcat_scr[pl.ds(0, rows), :] * cw_ref[0:1, :]
    for w in range(1, CONV_W):
        xc = xc + xcat_scr[pl.ds(w * bsz, rows), :] * cw_ref[w:w + 1, :]
    a, u = _rglru_gates(xc, wa_ref, ba_ref, wx_ref, bx_ref, lam_ref)
    a_scr[...] = a.reshape(tc, bsz, D_RNN)
    u_scr[...] = u.reshape(tc, bsz, D_RNN)

    def scan_step(t, h):
        h = a_scr[t] * h + u_scr[t]
        hs_scr[t] = h
        return h

    h_scr[...] = lax.fori_loop(0, tc, scan_step, h_scr[...])
    gate = _gelu_tanh(batch_major(g_refs))
    o_ref[...] = (jnp.swapaxes(hs_scr[...], 0, 1) * gate).astype(BF16)

    def store_state():
        hl_ref[...] = h_scr[...]
        nb_ref[...] = jnp.swapaxes(xcat_scr[pl.ds(rows, hist), :].reshape(CONV_W - 1, bsz, D_RNN), 0, 1)

    if chained:
        pl.when(step == pl.num_programs(0) - 1)(store_state)
    else:
        store_state()


RGLRU_STEP_ROWS = 256


def rglru_rows_path(p, row0, b, t, buf0, h0, conv_w, conv_b, wa4, ba, wx4, bx, lam):
    assert t >= CONV_W - 1 and b % SUBLANES == 0 and RGLRU_STEP_ROWS % SUBLANES == 0
    d = D_RNN
    hist = CONV_W - 1
    nblk, wdt = wa4.shape[0], wa4.shape[1]
    chained = b * t > RGLRU_STEP_ROWS and b == SUBLANES
    if chained:
        bg, tc, nin = b, RGLRU_STEP_ROWS // b, b
        steps = t // tc
        assert t % tc == 0 and row0 % tc == 0
        blk0 = row0 // tc
        row_map = lambda bi, col: (lambda s: (blk0 + bi * steps + s, col))
        state2, state3 = (lambda s: (0, 0)), (lambda s: (0, 0, 0))
        out_map = lambda s: (0, s, 0)
    else:
        bg, tc, nin = max(SUBLANES, RGLRU_STEP_ROWS // t), t, 1
        steps = b // bg
        assert b % bg == 0 and row0 % (bg * t) == 0
        blk0 = row0 // (bg * t)
        row_map = lambda bi, col: (lambda s: (blk0 + s, col))
        state2, state3 = (lambda s: (s, 0)), (lambda s: (s, 0, 0))
        out_map = lambda s: (s, 0, 0)
    in_rows = (bg // nin) * tc
    const2 = lambda s: (0, 0)
    const3 = lambda s: (0, 0, 0)
    out, h_last, new_buf = pl.pallas_call(
        functools.partial(_rglru_rows_kernel, bg, tc, nin, chained),
        grid=(steps,),
        in_specs=[pl.BlockSpec((in_rows, d), row_map(bi, COL_XR)) for bi in range(nin)]
        + [pl.BlockSpec((in_rows, d), row_map(bi, COL_GRNN)) for bi in range(nin)]
        + [
            pl.BlockSpec((bg, hist, d), state3),
            pl.BlockSpec((bg, d), state2),
            pl.BlockSpec((CONV_W, d), const2),
            pl.BlockSpec((1, d), const2),
            pl.BlockSpec((nblk, wdt, wdt), const3),
            pl.BlockSpec((1, d), const2),
            pl.BlockSpec((nblk, wdt, wdt), const3),
            pl.BlockSpec((1, d), const2),
            pl.BlockSpec((1, d), const2),
        ],
        out_specs=[
            pl.BlockSpec((bg, tc, d), out_map),
            pl.BlockSpec((bg, d), state2),
            pl.BlockSpec((bg, hist, d), state3),
        ],
        out_shape=[
            jax.ShapeDtypeStruct((b, t, d), BF16),
            jax.ShapeDtypeStruct((b, d), F32),
            jax.ShapeDtypeStruct((b, hist, d), F32),
        ],
        scratch_shapes=[
            pltpu.VMEM(((CONV_W - 1 + tc) * bg, d), F32),
            pltpu.VMEM((tc, bg, d), F32),
            pltpu.VMEM((tc, bg, d), F32),
            pltpu.VMEM((tc, bg, d), F32),
            pltpu.VMEM((bg, d), F32),
        ],
        compiler_params=_params("arbitrary"),
        name="rglru_rows",
    )(*([p] * (2 * nin)), buf0, h0, conv_w, conv_b.reshape(1, d), wa4, ba.reshape(1, d), wx4, bx.reshape(1, d),
      lam.reshape(1, d))
    return out.reshape(b * t, d), h_last, new_buf


def _block_diag_tiles(w, tile):
    nb, bs, _ = w.shape
    per = tile // bs
    w = w.reshape(nb // per, per, bs, bs)
    eye = jnp.eye(per, dtype=w.dtype)
    dense = jnp.einsum("gpcd,pq->gpcqd", w, eye).reshape(nb // per, tile, tile)
    return dense.astype(BF16)


def _merge_kernel(blocks_a, nx, *refs):
    x_refs, refs = refs[:nx], refs[nx:]
    ro_refs, rn_refs = refs[0:2], refs[2:4]
    ga_ref, gb_ref, wr_ref, wn_ref, wo_ref, g2_ref, wq_ref, yt_ref, xn_ref, q_ref = refs[4:]
    ret_out = jnp.dot(_pick_rows(ro_refs, blocks_a), wr_ref[...], preferred_element_type=F32)
    rnn_out = jnp.dot(_pick_rows(rn_refs, blocks_a), wn_ref[...], preferred_element_type=F32)
    merged = jax.nn.sigmoid(ga_ref[...]) * ret_out + jax.nn.sigmoid(gb_ref[...]) * rnn_out
    y = _pick_rows(x_refs, blocks_a) + jnp.dot(merged.astype(BF16), wo_ref[...], preferred_element_type=F32)
    yt = y.T
    yt_ref[...] = yt
    ms = jnp.mean(yt * yt, axis=0, keepdims=True)
    xn = ((yt * lax.rsqrt(ms + EPS)) * g2_ref[...]).astype(BF16)
    xn_ref[...] = xn
    q_ref[...] = jnp.dot(wq_ref[...], xn, preferred_element_type=F32)


def merge_proj(x_parts, ret_parts, rnn_parts, p, w_ret_out, w_rnn_out, w_o, norm2_g, wq_t):
    n, d = p.shape[0], x_parts[0].shape[1]
    m = wq_t.shape[0]
    tm = _pick(math.gcd(*[r.shape[0] for r in ret_parts]), (512, 256, 128))
    blocks_a, part_maps = _row_part_maps(ret_parts, tm)
    assert len(x_parts) == 1 or x_parts[0].shape[0] == ret_parts[0].shape[0]
    _, x_maps = _row_part_maps(x_parts, tm) if len(x_parts) == 2 else (None, [lambda i: (i, 0)])
    const = lambda i: (0, 0)
    tok = lambda i: (0, i)
    return pl.pallas_call(
        functools.partial(_merge_kernel, blocks_a, len(x_parts)),
        grid=(n // tm,),
        in_specs=[pl.BlockSpec((tm, d), mp) for mp in x_maps + part_maps + part_maps] + [
            pl.BlockSpec((tm, d), lambda i: (i, COL_GA)),
            pl.BlockSpec((tm, d), lambda i: (i, COL_GB)),
            pl.BlockSpec((d, d), const),
            pl.BlockSpec((d, d), const),
            pl.BlockSpec((d, d), const),
            pl.BlockSpec((d, 1), const),
            pl.BlockSpec((m, d), const),
        ],
        out_specs=[pl.BlockSpec((d, tm), tok), pl.BlockSpec((d, tm), tok), pl.BlockSpec((m, tm), tok)],
        out_shape=[jax.ShapeDtypeStruct((d, n), F32), jax.ShapeDtypeStruct((d, n), BF16),
                   jax.ShapeDtypeStruct((m, n), F32)],
        compiler_params=_params("parallel"),
        name="merge_out_proj",
    )(*x_parts, *ret_parts, *rnn_parts, p, p, w_ret_out, w_rnn_out, w_o, norm2_g.reshape(d, 1), wq_t)


def _sort_pairs(n):
    def merge(lo, hi, r):
        step = r * 2
        if step < hi - lo:
            yield from merge(lo, hi, step)
            yield from merge(lo + r, hi, step)
            yield from [(i, i + r) for i in range(lo + r, hi - r, step)]
        else:
            yield (lo, lo + r)

    def sort(lo, hi):
        if hi - lo >= 1:
            mid = lo + (hi - lo) // 2
            yield from sort(lo, mid)
            yield from sort(mid + 1, hi)
            yield from merge(lo, hi, 1)

    return list(sort(0, n - 1))


_SORT16 = _sort_pairs(PEER_TOPK)


def _cmpx(vals, i, j):
    a, b = vals[i], vals[j]
    if b is None:
        return
    if a is None:
        vals[i], vals[j] = b, None
        return
    vals[i], vals[j] = jnp.maximum(a, b), jnp.minimum(a, b)


def _sort_desc(vals):
    vals = list(vals)
    for i, j in _SORT16:
        _cmpx(vals, i, j)
    return vals


def _merge_top(a, b):
    k = PEER_TOPK
    a = list(a) + [None] * (k - len(a))
    b = list(b) + [None] * (k - len(b))
    out = []
    for r in range(k):
        x, y = a[r], b[k - 1 - r]
        out.append(y if x is None else (x if y is None else jnp.maximum(x, y)))
    d = k // 2
    while d >= 1:
        for i in range(k):
            if not i & d:
                _cmpx(out, i, i + d)
        d //= 2
    return out


def _top_sorted(ref):
    groups = []
    for g0 in range(0, N_KEYS, PEER_TOPK):
        vals = [ref[pl.ds((g0 + j) * PEER_HEADS, PEER_HEADS), :] for j in range(PEER_TOPK)]
        groups.append(_sort_desc(vals))
    while len(groups) > 1:
        groups = [_merge_top(groups[i], groups[i + 1]) for i in range(0, len(groups), 2)]
    return groups[0]


def _peer_select_kernel(q_ref, k1_ref, k2_ref, c1_ref, e1_ref, r2_ref, e2_ref, s1_scr, s2_scr, r2_scr, e2_scr):
    nh = PEER_HEADS
    half_rows = q_ref.shape[0] // 2
    tn = q_ref.shape[1]
    s1 = jnp.dot(k1_ref[...], q_ref[pl.ds(0, half_rows), :].astype(BF16), preferred_element_type=F32)
    s2 = jnp.dot(k2_ref[...], q_ref[pl.ds(half_rows, half_rows), :].astype(BF16), preferred_element_type=F32)
    for lt in range(tn // LANES):
        s1_scr[lt] = s1[:, lt * LANES:(lt + 1) * LANES]
        s2_scr[lt] = s2[:, lt * LANES:(lt + 1) * LANES]
    for lt in range(tn // LANES):
        lanes = pl.ds(lt * LANES, LANES)
        s1_t, s2_t, r2_t, e2_t = s1_scr.at[lt], s2_scr.at[lt], r2_scr.at[lt], e2_scr.at[lt]
        a = _top_sorted(s1_t)
        b = _top_sorted(s2_t)
        k = PEER_TOPK
        lists = []
        for j in range(1, k + 1):
            col = [a[r - 1] + b[j - 1] for r in range(j, k // j + 1)]
            row = [a[j - 1] + b[s - 1] for s in range(j + 1, k // j + 1)]
            if col:
                lists.append(col)
            if row:
                lists.append(row)
        top = lists[0]
        for other in lists[1:]:
            top = _merge_top(top, other)
        tau = top[k - 1]
        z = jnp.ones_like(tau)
        for r in range(1, k):
            z = z + jnp.exp(top[r] - top[0])
        zinv = 1.0 / z
        inf = jnp.full((nh, LANES), jnp.inf, F32)
        phi = []
        for s in range(1, k + 1):
            p = inf
            for r in range(1, k // s + 1):
                p = jnp.where(a[r - 1] + b[s - 1] >= tau, a[r - 1], p)
            phi.append(p)

        def per_key(kk, carry):
            rows = pl.ds(pl.multiple_of(kk * nh, nh), nh)
            s1k = s1_t[rows, :]
            s2k = s2_t[rows, :]
            cnt = jnp.ones((nh, LANES), F32)
            for s in range(k):
                cnt = jnp.where(s1k >= phi[s], float(s + 2), cnt)
            rank = jnp.full((nh, LANES), float(k + 1), F32)
            for s in range(k - 1, -1, -1):
                rank = jnp.where(s2k >= b[s], float(s + 1), rank)
            c1_ref[rows, lanes] = cnt
            e1_ref[rows, lanes] = jnp.exp(s1k - a[0])
            r2_t[rows, :] = rank
            e2_t[rows, :] = jnp.exp(s2k - b[0]) * zinv
            return carry

        lax.fori_loop(0, N_KEYS, per_key, 0, unroll=8)
        pack = 2 * SUBLANES
        for h in range(nh):
            for kt in range(N_KEYS // pack):
                lo = pl.ds(kt * pack * nh + h, SUBLANES, stride=nh)
                hi = pl.ds((kt * pack + SUBLANES) * nh + h, SUBLANES, stride=nh)
                dst = pl.ds(h * N_KEYS + kt * pack, pack)
                r2_ref[dst, lanes] = jnp.concatenate([r2_t[lo, :], r2_t[hi, :]], axis=0).astype(BF16)
                e2_ref[dst, lanes] = jnp.concatenate([e2_t[lo, :], e2_t[hi, :]], axis=0).astype(BF16)


def peer_select(q_t, k1, k2):
    m, n = q_t.shape
    rows = PEER_HEADS * N_KEYS
    tn = _pick(n, (256, 128))
    tok = lambda i: (0, i)
    const = lambda i: (0, 0)
    return pl.pallas_call(
        _peer_select_kernel,
        grid=(n // tn,),
        in_specs=[pl.BlockSpec((m, tn), tok), pl.BlockSpec(k1.shape, const), pl.BlockSpec(k2.shape, const)],
        out_specs=[pl.BlockSpec((rows, tn), tok)] * 4,
        out_shape=[jax.ShapeDtypeStruct((rows, n), F32), jax.ShapeDtypeStruct((rows, n), F32),
                   jax.ShapeDtypeStruct((rows, n), BF16), jax.ShapeDtypeStruct((rows, n), BF16)],
        scratch_shapes=[pltpu.VMEM((tn // LANES, rows, LANES), F32)] * 4,
        compiler_params=_params("parallel"),
        name="peer_select",
    )(q_t, k1, k2)


PEER_EXPERT_BLOCK = 1024


def _mxu_dot(lhs, rhs):
    return lax.dot_general(lhs, rhs, (((1,), (0,)), ((), ())), preferred_element_type=F32)


def _peer_dense_kernel(xt_ref, xn_ref, u_ref, vt_ref, c1_ref, e1_ref, r2_ref, e2_ref, y_ref,
                       acc_scr, act_scr):
    j = pl.program_id(1)
    nh = PEER_HEADS
    tn = xn_ref.shape[1]
    pack = 2 * SUBLANES
    i1_per_blk = u_ref.shape[0] // N_KEYS

    @pl.when(j == 0)
    def _():
        acc_scr[...] = jnp.zeros_like(acc_scr)

    wide = 2 * LANES

    def gate_block(il, nc):
        erows = pl.ds(il * N_KEYS, N_KEYS)
        hblk = _mxu_dot(u_ref[erows, :], xn_ref[:, pl.ds(nc * wide, wide)])
        for lw in range(wide // LANES):
            lc = nc * (wide // LANES) + lw
            lanes = pl.ds(lc * LANES, LANES)
            c1b = [jnp.broadcast_to(c1_ref[pl.ds(il * nh + h, 1), lanes], (pack, LANES)).astype(BF16)
                   for h in range(nh)]
            e1b = [jnp.broadcast_to(e1_ref[pl.ds(il * nh + h, 1), lanes], (pack, LANES)).astype(BF16)
                   for h in range(nh)]
            zero = jnp.zeros((pack, LANES), BF16)
            half = jnp.full((), 0.5, BF16)
            gelu_c0 = jnp.full((), GELU_C0, BF16)
            gelu_c1 = jnp.full((), GELU_C1, BF16)
            for it in range(N_KEYS // pack):
                rows = pl.ds(il * N_KEYS + it * pack, pack)
                gate = None
                for h in range(nh):
                    krows = pl.ds(h * N_KEYS + it * pack, pack)
                    sel = jnp.minimum(jnp.maximum(c1b[h] - r2_ref[krows, lanes], zero), e2_ref[krows, lanes])
                    term = sel * e1b[h]
                    gate = term if gate is None else gate + term
                x = hblk[it * pack:(it + 1) * pack, lw * LANES:(lw + 1) * LANES].astype(BF16)
                t = jnp.tanh(x * (gelu_c0 + gelu_c1 * (x * x)))
                act_scr[rows, lanes] = (x * (half + half * t)) * gate

    for nc in range(tn // wide):
        for il in range(i1_per_blk):
            gate_block(il, nc)
        cols = pl.ds(nc * wide, wide)
        acc_scr[:, cols] += _mxu_dot(vt_ref[...], act_scr[:, cols])

    @pl.when(j == pl.num_programs(1) - 1)
    def _():
        y_ref[...] = (xt_ref[...] + acc_scr[...]).T


def peer_dense(xt, xn_t, u_layers, layer, vt_bf16, c1, e1, r2, e2):
    d, n = xt.shape
    ne = u_layers.shape[1]
    tn = _pick(n, (512, 256, 128))
    te = PEER_EXPERT_BLOCK
    i1_per_blk = te // N_KEYS
    tok = lambda i, j: (0, i)
    return pl.pallas_call(
        _peer_dense_kernel,
        grid=(n // tn, ne // te),
        in_specs=[
            pl.BlockSpec((d, tn), tok),
            pl.BlockSpec((d, tn), tok),
            pl.BlockSpec((None, te, d), lambda i, j: (layer, j, 0)),
            pl.BlockSpec((d, te), lambda i, j: (0, j)),
            pl.BlockSpec((i1_per_blk * PEER_HEADS, tn), lambda i, j: (j, i)),
            pl.BlockSpec((i1_per_blk * PEER_HEADS, tn), lambda i, j: (j, i)),
            pl.BlockSpec((PEER_HEADS * N_KEYS, tn), tok),
            pl.BlockSpec((PEER_HEADS * N_KEYS, tn), tok),
        ],
        out_specs=pl.BlockSpec((tn, d), lambda i, j: (i, 0)),
        out_shape=jax.ShapeDtypeStruct((n, d), F32),
        scratch_shapes=[pltpu.VMEM((d, tn), F32), pltpu.VMEM((te, tn), BF16)],
        compiler_params=_params("parallel", "arbitrary"),
        name="peer_dense",
    )(xt, xn_t, u_layers, vt_bf16, c1, e1, r2, e2)


def _interleaved_keys(keys_p):
    nh, nk, half = keys_p.shape
    eye = jnp.eye(nh, dtype=keys_p.dtype)
    return jnp.einsum("hkd,hg->khgd", keys_p, eye).reshape(nk * nh, nh * half).astype(BF16)


def peer_query_weights(wq):
    d = wq.shape[0]
    wq_t = wq.reshape(d, PEER_HEADS, 2, PEER_HALF).transpose(2, 1, 3, 0).reshape(2 * PEER_HEADS * PEER_HALF, d)
    return wq_t.astype(BF16)


def peer_layer(xt, xn_t, q_t, keys, u_layers, layer, v_tab):
    k1 = _interleaved_keys(keys[:, 0])
    k2 = _interleaved_keys(keys[:, 1])
    c1, e1, r2, e2 = peer_select(q_t, k1, k2)
    return peer_dense(xt, xn_t, u_layers, layer, v_tab.T.astype(BF16), c1, e1, r2, e2)


def _rmsnorm_kernel(x_ref, g_ref, y_ref):
    x = x_ref[...]
    ms = jnp.mean(x * x, axis=-1, keepdims=True)
    y_ref[...] = (x * lax.rsqrt(ms + EPS)) * g_ref[...]


def final_norm(x, row0, rows, g):
    d = x.shape[1]
    tm = _pick(math.gcd(rows, row0) if row0 else rows, (512, 256, 128, 64, 32, 16, 8))
    blk0 = row0 // tm
    return pl.pallas_call(
        _rmsnorm_kernel,
        grid=(rows // tm,),
        in_specs=[pl.BlockSpec((tm, d), lambda i: (blk0 + i, 0)), pl.BlockSpec((1, d), lambda i: (0, 0))],
        out_specs=pl.BlockSpec((tm, d), lambda i: (i, 0)),
        out_shape=jax.ShapeDtypeStruct((rows, d), F32),
        compiler_params=_params("parallel"),
        name="final_norm",
    )(x, g.reshape(1, d))


def _trunk(groups, norm1_g, norm2_g, normf_g, w_in, ret_gn_g, w_ret_out, conv_w, conv_b, rg_wa, rg_ba, rg_wx,
           rg_bx, rg_lambda, w_rnn_out, w_o, peer_wq, peer_keys, peer_u, peer_v):
    shapes = [(g[0].shape[0], g[0].shape[1]) for g in groups]
    x_parts = [g[0].reshape(-1, D_MODEL) for g in groups]
    row0s = np.cumsum([0] + [b * t for b, t in shapes]).tolist()
    states = [([], [], []) for _ in groups]
    for l in range(DEPTH):
        p = norm_matmul(x_parts, norm1_g[l], w_in[l].astype(BF16))
        wa4 = _block_diag_tiles(rg_wa[l], 256)
        wx4 = _block_diag_tiles(rg_wx[l], 256)
        ret_parts, rnn_parts = [], []
        for gi, (xg, r0, h0, buf0, pos0) in enumerate(groups):
            b, t = shapes[gi]
            row0 = row0s[gi]
            o, r_new = retention_path(p, row0, b, t, pos0, r0, l, ret_gn_g[l])
            rg_args = (conv_w[l], conv_b[l], wa4, rg_ba[l], wx4, rg_bx[l], rg_lambda[l])
            hg, h_last, nb = rglru_rows_path(p, row0, b, t, buf0[l], h0[l], *rg_args)
            ret_parts.append(o)
            rnn_parts.append(hg)
            states[gi][0].append(r_new)
            states[gi][1].append(h_last)
            states[gi][2].append(nb)
        xt, xn_t, q_t = merge_proj(x_parts, ret_parts, rnn_parts, p, w_ret_out[l].astype(BF16),
                                   w_rnn_out[l].astype(BF16), w_o[l].astype(BF16), norm2_g[l],
                                   peer_query_weights(peer_wq[l]))
        x = peer_layer(xt, xn_t, q_t, peer_keys[l], peer_u, l, peer_v[l])
        x_parts = [x]
    outs = []
    for gi, (b, t) in enumerate(shapes):
        y = final_norm(x, row0s[gi], b * t, normf_g).reshape(b, t, D_MODEL)
        outs.append((y, jnp.stack(states[gi][0]), jnp.stack(states[gi][1]), jnp.stack(states[gi][2])))
    return outs


def kernel(x_prompt, x_sample, state_ret, state_rnn, state_conv, norm1_g, norm2_g, normf_g, w_in, ret_gn_g,
           w_ret_out, conv_w, conv_b, rg_wa, rg_ba, rg_wx, rg_bx, rg_lambda, w_rnn_out, w_o, peer_wq, peer_keys,
           peer_u, peer_v):
    bp = x_prompt.shape[0]
    dt = x_prompt.dtype
    zr = jnp.zeros((DEPTH, bp, RET_HEADS, RET_DK, RET_DV), dt)
    zh = jnp.zeros((DEPTH, bp, D_RNN), dt)
    zc = jnp.zeros((DEPTH, bp, CONV_W - 1, D_RNN), dt)
    groups = [(x_prompt, zr, zh, zc, 0.0), (x_sample, state_ret, state_rnn, state_conv, float(PAST_LEN))]
    (yp, rp, hp, cp), (ys, rs, hs, cs) = _trunk(
        groups, norm1_g, norm2_g, normf_g, w_in, ret_gn_g, w_ret_out, conv_w, conv_b, rg_wa, rg_ba, rg_wx, rg_bx,
        rg_lambda, w_rnn_out, w_o, peer_wq, peer_keys, peer_u, peer_v)
    return (yp, ys, rp, hp, cp, rs, hs, cs)
```

```python
import functools
import math

import jax
import jax.numpy as jnp
import numpy as np
from jax import lax
from jax.experimental import pallas as pl
from jax.experimental.pallas import tpu as pltpu

D_MODEL = 1024
DEPTH = 2
PAST_LEN = 16384
RET_HEADS = 8
RET_DK = 64
RET_DV = 128
RET_QK = RET_HEADS * RET_DK
RET_V = RET_HEADS * RET_DV
RET_CHUNK = 128
ROPE_BASE = 10000.0
D_RNN = 1024
CONV_W = 4
RG_C = 8.0
COL_V, COL_GRET, COL_XR, COL_GRNN, COL_GA, COL_GB = 1, 2, 3, 4, 5, 6
PEER_HEADS = 8
N_KEYS = 128
PEER_DKEY = 256
PEER_HALF = PEER_DKEY // 2
PEER_TOPK = 16
EPS = 1e-6

SUBLANES = 8
LANES = 128
VMEM_LIMIT = 56 * 1024 * 1024

F32 = jnp.float32
BF16 = jnp.bfloat16


def _params(*sem):
    return pltpu.CompilerParams(dimension_semantics=sem, vmem_limit_bytes=VMEM_LIMIT)


def _pick(n, prefs):
    for p in prefs:
        if n % p == 0:
            return p
    return n


def _row_part_maps(parts, tm):
    assert len(parts) in (1, 2) and all(p.shape[0] % tm == 0 for p in parts)
    blocks_a = parts[0].shape[0] // tm
    maps = [lambda i, *_: (jnp.minimum(i, blocks_a - 1), 0)]
    if len(parts) == 2:
        maps.append(lambda i, *_: (jnp.maximum(i - blocks_a, 0), 0))
    return blocks_a, maps


def _pick_rows(refs, blocks_a):
    x = refs[0][...]
    if len(refs) == 2:
        x = jnp.where(pl.program_id(0) < blocks_a, x, refs[1][...])
    return x


def _norm_matmul_kernel(blocks_a, nparts, *refs):
    x_refs = refs[:nparts]
    g_ref, w_ref, y_ref, xn_scr = refs[nparts:]

    @pl.when(pl.program_id(1) == 0)
    def _():
        x = _pick_rows(x_refs, blocks_a)
        ms = jnp.mean(x * x, axis=-1, keepdims=True)
        xn_scr[...] = ((x * lax.rsqrt(ms + EPS)) * g_ref[...]).astype(BF16)

    y_ref[...] = jnp.dot(xn_scr[...], w_ref[...], preferred_element_type=F32)


def norm_matmul(x_parts, g, w_bf16):
    n = sum(p.shape[0] for p in x_parts)
    d = x_parts[0].shape[1]
    m = w_bf16.shape[1]
    tm = _pick(math.gcd(*[p.shape[0] for p in x_parts]), (1024, 512, 256, 128))
    tn = _pick(m, (1792, 1024, 512, 256, 128))
    blocks_a, row_maps = _row_part_maps(x_parts, tm)
    return pl.pallas_call(
        functools.partial(_norm_matmul_kernel, blocks_a, len(x_parts)),
        grid=(n // tm, m // tn),
        in_specs=[pl.BlockSpec((tm, d), rm) for rm in row_maps] + [
            pl.BlockSpec((1, d), lambda i, j: (0, 0)),
            pl.BlockSpec((d, tn), lambda i, j: (0, j)),
        ],
        out_specs=pl.BlockSpec((tm, tn), lambda i, j: (i, j)),
        out_shape=jax.ShapeDtypeStruct((n, m), F32),
        scratch_shapes=[pltpu.VMEM((tm, d), BF16)],
        compiler_params=_params("parallel", "arbitrary"),
        name="norm_in_proj",
    )(*x_parts, g.reshape(1, d), w_bf16)


def _rot_half(x):
    n = x.shape[-1]
    half = RET_DK // 2
    fwd = pltpu.roll(x, half, axis=1)
    bwd = pltpu.roll(x, n - half, axis=1)
    lane = lax.broadcasted_iota(jnp.int32, x.shape, 1)
    return jnp.where((lane % RET_DK) < half, bwd, fwd)


def _retention_kernel(q_ref, k_ref, v_ref, g_ref, r0_ref, cos_ref, sin_ref, mask_ref, qw_ref, kw_ref,
                      gc_ref, gn_ref, o_ref, r_out_ref, r_scr):
    c = pl.program_id(1)
    rows_c = mask_ref.shape[1]
    nb = q_ref.shape[0] // rows_c

    @pl.when(c == 0)
    def _():
        r_scr[...] = r0_ref[...]

    cos = cos_ref[...]
    sin = sin_ref[...]
    q = q_ref[...]
    k = k_ref[...]
    qr = q * cos + _rot_half(q) * sin
    kr = (k * cos + _rot_half(k) * sin) * (RET_DK ** -0.5)
    qd = (qr * qw_ref[...]).astype(BF16)
    kd = (kr * kw_ref[...]).astype(BF16)
    qb = qr.astype(BF16)
    kb = kr.astype(BF16)
    v = v_ref[...].astype(BF16)
    g = g_ref[...]
    for bb in range(nb):
        rs = slice(bb * rows_c, (bb + 1) * rows_c)
        for h in range(RET_HEADS):
            ks = slice(h * RET_DK, (h + 1) * RET_DK)
            vs = slice(h * RET_DV, (h + 1) * RET_DV)
            vh = v[rs, vs]
            s = lax.dot_general(qb[rs, ks], kb[rs, ks], (((1,), (1,)), ((), ())),
                                preferred_element_type=F32) * mask_ref[h]
            r_h = r_scr[bb, h]
            o = jnp.dot(s.astype(BF16), vh, preferred_element_type=F32)
            o = o + jnp.dot(qd[rs, ks], r_h.astype(BF16), preferred_element_type=F32)
            kv = lax.dot_general(kd[rs, ks], vh, (((0,), (0,)), ((), ())), preferred_element_type=F32)
            r_scr[bb, h] = gc_ref[h] * r_h + kv
            mu = jnp.mean(o, axis=-1, keepdims=True)
            var = jnp.mean(jnp.square(o - mu), axis=-1, keepdims=True)
            on = ((o - mu) * lax.rsqrt(var + EPS)) * gn_ref[:, vs]
            gh = g[rs, vs]
            o_ref[rs, vs] = ((gh * jax.nn.sigmoid(gh)) * on).astype(BF16)

    @pl.when(c == pl.num_programs(1) - 1)
    def _():
        r_out_ref[...] = r_scr[...]


def retention_path(p, row0, b, t, pos0, r0_layers, layer, gn_g):
    c = RET_CHUNK if t % RET_CHUNK == 0 else t
    nc = t // c
    blk0 = row0 // c
    assert row0 % c == 0
    log_g = jnp.log1p(-(2.0 ** (-5.0 - jnp.arange(RET_HEADS, dtype=F32))))
    idx = jnp.arange(c, dtype=F32)
    diff = idx[:, None] - idx[None, :]
    mask = jnp.where(diff[None] >= 0, jnp.exp(jnp.maximum(diff, 0.0)[None] * log_g[:, None, None]), 0.0)
    k_w = jnp.exp((c - 1 - idx)[:, None] * log_g[None, :])
    q_w = jnp.exp((idx + 1.0)[:, None] * log_g[None, :])
    g_c = jnp.exp(c * log_g)
    qw_tab = jnp.repeat(q_w, RET_DK, axis=1)
    kw_tab = jnp.repeat(k_w, RET_DK, axis=1)
    gc_tab = jnp.broadcast_to(g_c[:, None, None], (RET_HEADS, 1, RET_DV))
    half = RET_DK // 2
    pos = pos0 + jnp.arange(t, dtype=F32)
    freq = ROPE_BASE ** (-jnp.arange(half, dtype=F32) / half)
    ang = pos[:, None] * freq[None, :]
    cos_h = jnp.concatenate([jnp.cos(ang), jnp.cos(ang)], axis=1)
    sin_h = jnp.concatenate([-jnp.sin(ang), jnp.sin(ang)], axis=1)
    cos_tab = jnp.tile(cos_h, (1, RET_HEADS))
    sin_tab = jnp.tile(sin_h, (1, RET_HEADS))
    nb = _pick(b, (8, 4, 2, 1)) if nc == 1 and c * 8 <= RET_CHUNK else 1
    if nb > 1:
        cos_tab, sin_tab, qw_tab, kw_tab = [jnp.tile(tab, (nb, 1)) for tab in (cos_tab, sin_tab, qw_tab, kw_tab)]
    rb = nb * c
    blk0 = row0 // rb
    assert row0 % rb == 0

    rows = lambda bi, ci: blk0 + bi * nc + ci
    state_spec = pl.BlockSpec((nb, RET_HEADS, RET_DK, RET_DV), lambda bi, ci: (bi, 0, 0, 0))
    o, r_new = pl.pallas_call(
        _retention_kernel,
        grid=(b // nb, nc),
        in_specs=[
            pl.BlockSpec((rb, RET_QK), lambda bi, ci: (rows(bi, ci), 0)),
            pl.BlockSpec((rb, RET_QK), lambda bi, ci: (rows(bi, ci), 1)),
            pl.BlockSpec((rb, RET_V), lambda bi, ci: (rows(bi, ci), COL_V)),
            pl.BlockSpec((rb, RET_V), lambda bi, ci: (rows(bi, ci), COL_GRET)),
            pl.BlockSpec((None, nb, RET_HEADS, RET_DK, RET_DV), lambda bi, ci: (layer, bi, 0, 0, 0)),
            pl.BlockSpec((rb, RET_QK), lambda bi, ci: (ci, 0)),
            pl.BlockSpec((rb, RET_QK), lambda bi, ci: (ci, 0)),
            pl.BlockSpec((RET_HEADS, c, c), lambda bi, ci: (0, 0, 0)),
            pl.BlockSpec((rb, RET_QK), lambda bi, ci: (0, 0)),
            pl.BlockSpec((rb, RET_QK), lambda bi, ci: (0, 0)),
            pl.BlockSpec((RET_HEADS, 1, RET_DV), lambda bi, ci: (0, 0, 0)),
            pl.BlockSpec((1, RET_V), lambda bi, ci: (0, 0)),
        ],
        out_specs=[
            pl.BlockSpec((rb, RET_V), lambda bi, ci: (bi * nc + ci, 0)),
            state_spec,
        ],
        out_shape=[
            jax.ShapeDtypeStruct((b * t, RET_V), BF16),
            jax.ShapeDtypeStruct((b, RET_HEADS, RET_DK, RET_DV), F32),
        ],
        scratch_shapes=[pltpu.VMEM((nb, RET_HEADS, RET_DK, RET_DV), F32)],
        compiler_params=_params("parallel", "arbitrary"),
        name="retention",
    )(p, p, p, p, r0_layers, cos_tab, sin_tab, mask, qw_tab, kw_tab, gc_tab, gn_g.reshape(1, RET_V))
    return o, r_new


GELU_C0 = math.sqrt(2.0 / math.pi)
GELU_C1 = GELU_C0 * 0.044715


def _gelu_tanh(x):
    return x * (0.5 * (1.0 + jnp.tanh(math.sqrt(2.0 / math.pi) * (x + 0.044715 * (x * x * x)))))


def _rglru_gates(xc, wa_ref, ba_ref, wx_ref, bx_ref, lam_ref):
    xcb = xc.astype(BF16)
    nblk = wa_ref.shape[0]
    wdt = wa_ref.shape[1]
    ra = jnp.concatenate(
        [jnp.dot(xcb[:, j * wdt:(j + 1) * wdt], wa_ref[j], preferred_element_type=F32) for j in range(nblk)],
        axis=1)
    ri = jnp.concatenate(
        [jnp.dot(xcb[:, j * wdt:(j + 1) * wdt], wx_ref[j], preferred_element_type=F32) for j in range(nblk)],
        axis=1)
    r = jax.nn.sigmoid(ra + ba_ref[...])
    i = jax.nn.sigmoid(ri + bx_ref[...])
    z = -lam_ref[...]
    softplus = jnp.maximum(z, 0.0) + jnp.log1p(jnp.exp(-jnp.abs(z)))
    log_a = (-RG_C * r) * softplus
    a = jnp.exp(log_a)
    one_m_a2 = -jnp.tanh(log_a) * (a * a + 1.0)
    return a, jnp.sqrt(one_m_a2) * (i * xc)


def _rglru_rows_kernel(bsz, tc, nin, chained, *refs):
    xr_refs, g_refs = refs[:nin], refs[nin:2 * nin]
    (buf_ref, h0_ref, cw_ref, cb_ref, wa_ref, ba_ref, wx_ref, bx_ref, lam_ref, o_ref, hl_ref, nb_ref,
     xcat_scr, a_scr, u_scr, hs_scr, h_scr) = refs[2 * nin:]
    step = pl.program_id(0)
    rows = tc * bsz
    hist = (CONV_W - 1) * bsz
    bpi = bsz // nin

    def time_major(x):
        return jnp.swapaxes(x, 0, 1).reshape(x.shape[0] * x.shape[1], D_RNN)

    def batch_major(blocks):
        return jnp.concatenate([blk[...].reshape(bpi, tc, D_RNN) for blk in blocks], axis=0)

    def load_state():
        xcat_scr[pl.ds(rows, hist), :] = time_major(buf_ref[...])
        h_scr[...] = h0_ref[...]

    if chained:
        pl.when(step == 0)(load_state)
    else:
        load_state()

    xcat_scr[pl.ds(0, hist), :] = xcat_scr[pl.ds(rows, hist), :]
    xcat_scr[pl.ds(hist, rows), :] = time_major(batch_major(xr_refs))
    xc = cb_ref[...] + xcat_scr[pl.ds(0, rows), :] * cw_ref[0:1, :]
    for w in range(1, CONV_W):
        xc = xc + xcat_scr[pl.ds(w * bsz, rows), :] * cw_ref[w:w + 1, :]
    a, u = _rglru_gates(xc, wa_ref, ba_ref, wx_ref, bx_ref, lam_ref)
    a_scr[...] = a.reshape(tc, bsz, D_RNN)
    u_scr[...] = u.reshape(tc, bsz, D_RNN)

    def scan_step(t, h):
        h = a_scr[t] * h + u_scr[t]
        hs_scr[t] = h
        return h

    h_scr[...] = lax.fori_loop(0, tc, scan_step, h_scr[...])
    gate = _gelu_tanh(batch_major(g_refs))
    o_ref[...] = (jnp.swapaxes(hs_scr[...], 0, 1) * gate).astype(BF16)

    def store_state():
        hl_ref[...] = h_scr[...]
        nb_ref[...] = jnp.swapaxes(xcat_scr[pl.ds(rows, hist), :].reshape(CONV_W - 1, bsz, D_RNN), 0, 1)

    if chained:
        pl.when(step == pl.num_programs(0) - 1)(store_state)
    else:
        store_state()


RGLRU_STEP_ROWS = 256


def rglru_rows_path(p, row0, b, t, buf0, h0, conv_w, conv_b, wa4, ba, wx4, bx, lam):
    assert t >= CONV_W - 1 and b % SUBLANES == 0 and RGLRU_STEP_ROWS % SUBLANES == 0
    d = D_RNN
    hist = CONV_W - 1
    nblk, wdt = wa4.shape[0], wa4.shape[1]
    chained = b * t > RGLRU_STEP_ROWS and b == SUBLANES
    if chained:
        bg, tc, nin = b, RGLRU_STEP_ROWS // b, b
        steps = t // tc
        assert t % tc == 0 and row0 % tc == 0
        blk0 = row0 // tc
        row_map = lambda bi, col: (lambda s: (blk0 + bi * steps + s, col))
        state2, state3 = (lambda s: (0, 0)), (lambda s: (0, 0, 0))
        out_map = lambda s: (0, s, 0)
    else:
        bg, tc, nin = max(SUBLANES, RGLRU_STEP_ROWS // t), t, 1
        steps = b // bg
        assert b % bg == 0 and row0 % (bg * t) == 0
        blk0 = row0 // (bg * t)
        row_map = lambda bi, col: (lambda s: (blk0 + s, col))
        state2, state3 = (lambda s: (s, 0)), (lambda s: (s, 0, 0))
        out_map = lambda s: (s, 0, 0)
    in_rows = (bg // nin) * tc
    const2 = lambda s: (0, 0)
    const3 = lambda s: (0, 0, 0)
    out, h_last, new_buf = pl.pallas_call(
        functools.partial(_rglru_rows_kernel, bg, tc, nin, chained),
        grid=(steps,),
        in_specs=[pl.BlockSpec((in_rows, d), row_map(bi, COL_XR)) for bi in range(nin)]
        + [pl.BlockSpec((in_rows, d), row_map(bi, COL_GRNN)) for bi in range(nin)]
        + [
            pl.BlockSpec((bg, hist, d), state3),
            pl.BlockSpec((bg, d), state2),
            pl.BlockSpec((CONV_W, d), const2),
            pl.BlockSpec((1, d), const2),
            pl.BlockSpec((nblk, wdt, wdt), const3),
            pl.BlockSpec((1, d), const2),
            pl.BlockSpec((nblk, wdt, wdt), const3),
            pl.BlockSpec((1, d), const2),
            pl.BlockSpec((1, d), const2),
        ],
        out_specs=[
            pl.BlockSpec((bg, tc, d), out_map),
            pl.BlockSpec((bg, d), state2),
            pl.BlockSpec((bg, hist, d), state3),
        ],
        out_shape=[
            jax.ShapeDtypeStruct((b, t, d), BF16),
            jax.ShapeDtypeStruct((b, d), F32),
            jax.ShapeDtypeStruct((b, hist, d), F32),
        ],
        scratch_shapes=[
            pltpu.VMEM(((CONV_W - 1 + tc) * bg, d), F32),
            pltpu.VMEM((tc, bg, d), F32),
            pltpu.VMEM((tc, bg, d), F32),
            pltpu.VMEM((tc, bg, d), F32),
            pltpu.VMEM((bg, d), F32),
        ],
        compiler_params=_params("arbitrary"),
        name="rglru_rows",
    )(*([p] * (2 * nin)), buf0, h0, conv_w, conv_b.reshape(1, d), wa4, ba.reshape(1, d), wx4, bx.reshape(1, d),
      lam.reshape(1, d))
    return out.reshape(b * t, d), h_last, new_buf


def _block_diag_tiles(w, tile):
    nb, bs, _ = w.shape
    per = tile // bs
    w = w.reshape(nb // per, per, bs, bs)
    eye = jnp.eye(per, dtype=w.dtype)
    dense = jnp.einsum("gpcd,pq->gpcqd", w, eye).reshape(nb // per, tile, tile)
    return dense.astype(BF16)


def _merge_kernel(blocks_a, nx, *refs):
    x_refs, refs = refs[:nx], refs[nx:]
    ro_refs, rn_refs = refs[0:2], refs[2:4]
    ga_ref, gb_ref, wr_ref, wn_ref, wo_ref, g2_ref, wq_ref, yt_ref, xn_ref, q_ref = refs[4:]
    ret_out = jnp.dot(_pick_rows(ro_refs, blocks_a), wr_ref[...], preferred_element_type=F32)
    rnn_out = jnp.dot(_pick_rows(rn_refs, blocks_a), wn_ref[...], preferred_element_type=F32)
    merged = jax.nn.sigmoid(ga_ref[...]) * ret_out + jax.nn.sigmoid(gb_ref[...]) * rnn_out
    y = _pick_rows(x_refs, blocks_a) + jnp.dot(merged.astype(BF16), wo_ref[...], preferred_element_type=F32)
    yt = y.T
    yt_ref[...] = yt
    ms = jnp.mean(yt * yt, axis=0, keepdims=True)
    xn = ((yt * lax.rsqrt(ms + EPS)) * g2_ref[...]).astype(BF16)
    xn_ref[...] = xn
    q_ref[...] = jnp.dot(wq_ref[...], xn, preferred_element_type=F32)


def merge_proj(x_parts, ret_parts, rnn_parts, p, w_ret_out, w_rnn_out, w_o, norm2_g, wq_t):
    n, d = p.shape[0], x_parts[0].shape[1]
    m = wq_t.shape[0]
    tm = _pick(math.gcd(*[r.shape[0] for r in ret_parts]), (512, 256, 128))
    blocks_a, part_maps = _row_part_maps(ret_parts, tm)
    assert len(x_parts) == 1 or x_parts[0].shape[0] == ret_parts[0].shape[0]
    _, x_maps = _row_part_maps(x_parts, tm) if len(x_parts) == 2 else (None, [lambda i: (i, 0)])
    const = lambda i: (0, 0)
    tok = lambda i: (0, i)
    return pl.pallas_call(
        functools.partial(_merge_kernel, blocks_a, len(x_parts)),
        grid=(n // tm,),
        in_specs=[pl.BlockSpec((tm, d), mp) for mp in x_maps + part_maps + part_maps] + [
            pl.BlockSpec((tm, d), lambda i: (i, COL_GA)),
            pl.BlockSpec((tm, d), lambda i: (i, COL_GB)),
            pl.BlockSpec((d, d), const),
            pl.BlockSpec((d, d), const),
            pl.BlockSpec((d, d), const),
            pl.BlockSpec((d, 1), const),
            pl.BlockSpec((m, d), const),
        ],
        out_specs=[pl.BlockSpec((d, tm), tok), pl.BlockSpec((d, tm), tok), pl.BlockSpec((m, tm), tok)],
        out_shape=[jax.ShapeDtypeStruct((d, n), F32), jax.ShapeDtypeStruct((d, n), BF16),
                   jax.ShapeDtypeStruct((m, n), F32)],
        compiler_params=_params("parallel"),
        name="merge_out_proj",
    )(*x_parts, *ret_parts, *rnn_parts, p, p, w_ret_out, w_rnn_out, w_o, norm2_g.reshape(d, 1), wq_t)


def _sort_pairs(n):
    def merge(lo, hi, r):
        step = r * 2
        if step < hi - lo:
            yield from merge(lo, hi, step)
            yield from merge(lo + r, hi, step)
            yield from [(i, i + r) for i in range(lo + r, hi - r, step)]
        else:
            yield (lo, lo + r)

    def sort(lo, hi):
        if hi - lo >= 1:
            mid = lo + (hi - lo) // 2
            yield from sort(lo, mid)
            yield from sort(mid + 1, hi)
            yield from merge(lo, hi, 1)

    return list(sort(0, n - 1))


_SORT16 = _sort_pairs(PEER_TOPK)


def _cmpx(vals, i, j):
    a, b = vals[i], vals[j]
    if b is None:
        return
    if a is None:
        vals[i], vals[j] = b, None
        return
    vals[i], vals[j] = jnp.maximum(a, b), jnp.minimum(a, b)


def _sort_desc(vals):
    vals = list(vals)
    for i, j in _SORT16:
        _cmpx(vals, i, j)
    return vals


def _merge_top(a, b):
    k = PEER_TOPK
    a = list(a) + [None] * (k - len(a))
    b = list(b) + [None] * (k - len(b))
    out = []
    for r in range(k):
        x, y = a[r], b[k - 1 - r]
        out.append(y if x is None else (x if y is None else jnp.maximum(x, y)))
    d = k // 2
    while d >= 1:
        for i in range(k):
            if not i & d:
                _cmpx(out, i, i + d)
        d //= 2
    return out


def _top_sorted(ref):
    groups = []
    for g0 in range(0, N_KEYS, PEER_TOPK):
        vals = [ref[pl.ds((g0 + j) * PEER_HEADS, PEER_HEADS), :] for j in range(PEER_TOPK)]
        groups.append(_sort_desc(vals))
    while len(groups) > 1:
        groups = [_merge_top(groups[i], groups[i + 1]) for i in range(0, len(groups), 2)]
    return groups[0]


def _peer_select_kernel(q_ref, k1_ref, k2_ref, c1_ref, e1_ref, r2_ref, e2_ref, s1_scr, s2_scr, r2_scr, e2_scr):
    nh = PEER_HEADS
    half_rows = q_ref.shape[0] // 2
    tn = q_ref.shape[1]
    s1 = jnp.dot(k1_ref[...], q_ref[pl.ds(0, half_rows), :].astype(BF16), preferred_element_type=F32)
    s2 = jnp.dot(k2_ref[...], q_ref[pl.ds(half_rows, half_rows), :].astype(BF16), preferred_element_type=F32)
    for lt in range(tn // LANES):
        s1_scr[lt] = s1[:, lt * LANES:(lt + 1) * LANES]
        s2_scr[lt] = s2[:, lt * LANES:(lt + 1) * LANES]
    for lt in range(tn // LANES):
        lanes = pl.ds(lt * LANES, LANES)
        s1_t, s2_t, r2_t, e2_t = s1_scr.at[lt], s2_scr.at[lt], r2_scr.at[lt], e2_scr.at[lt]
        a = _top_sorted(s1_t)
        b = _top_sorted(s2_t)
        k = PEER_TOPK
        lists = []
        for j in range(1, k + 1):
            col = [a[r - 1] + b[j - 1] for r in range(j, k // j + 1)]
            row = [a[j - 1] + b[s - 1] for s in range(j + 1, k // j + 1)]
            if col:
                lists.append(col)
            if row:
                lists.append(row)
        top = lists[0]
        for other in lists[1:]:
            top = _merge_top(top, other)
        tau = top[k - 1]
        z = jnp.ones_like(tau)
        for r in range(1, k):
            z = z + jnp.exp(top[r] - top[0])
        zinv = 1.0 / z
        inf = jnp.full((nh, LANES), jnp.inf, F32)
        phi = []
        for s in range(1, k + 1):
            p = inf
            for r in range(1, k // s + 1):
                p = jnp.where(a[r - 1] + b[s - 1] >= tau, a[r - 1], p)
            phi.append(p)

        def per_key(kk, carry):
            rows = pl.ds(pl.multiple_of(kk * nh, nh), nh)
            s1k = s1_t[rows, :]
            s2k = s2_t[rows, :]
            cnt = jnp.ones((nh, LANES), F32)
            for s in range(k):
                cnt = jnp.where(s1k >= phi[s], float(s + 2), cnt)
            rank = jnp.full((nh, LANES), float(k + 1), F32)
            for s in range(k - 1, -1, -1):
                rank = jnp.where(s2k >= b[s], float(s + 1), rank)
            c1_ref[rows, lanes] = cnt
            e1_ref[rows, lanes] = jnp.exp(s1k - a[0])
            r2_t[rows, :] = rank
            e2_t[rows, :] = jnp.exp(s2k - b[0]) * zinv
            return carry

        lax.fori_loop(0, N_KEYS, per_key, 0, unroll=8)
        pack = 2 * SUBLANES
        for h in range(nh):
            for kt in range(N_KEYS // pack):
                lo = pl.ds(kt * pack * nh + h, SUBLANES, stride=nh)
                hi = pl.ds((kt * pack + SUBLANES) * nh + h, SUBLANES, stride=nh)
                dst = pl.ds(h * N_KEYS + kt * pack, pack)
                r2_ref[dst, lanes] = jnp.concatenate([r2_t[lo, :], r2_t[hi, :]], axis=0).astype(BF16)
                e2_ref[dst, lanes] = jnp.concatenate([e2_t[lo, :], e2_t[hi, :]], axis=0).astype(BF16)


def peer_select(q_t, k1, k2):
    m, n = q_t.shape
    rows = PEER_HEADS * N_KEYS
    tn = _pick(n, (256, 128))
    tok = lambda i: (0, i)
    const = lambda i: (0, 0)
    return pl.pallas_call(
        _peer_select_kernel,
        grid=(n // tn,),
        in_specs=[pl.BlockSpec((m, tn), tok), pl.BlockSpec(k1.shape, const), pl.BlockSpec(k2.shape, const)],
        out_specs=[pl.BlockSpec((rows, tn), tok)] * 4,
        out_shape=[jax.ShapeDtypeStruct((rows, n), F32), jax.ShapeDtypeStruct((rows, n), F32),
                   jax.ShapeDtypeStruct((rows, n), BF16), jax.ShapeDtypeStruct((rows, n), BF16)],
        scratch_shapes=[pltpu.VMEM((tn // LANES, rows, LANES), F32)] * 4,
        compiler_params=_params("parallel"),
        name="peer_select",
    )(q_t, k1, k2)


PEER_EXPERT_BLOCK = 1024


def _mxu_dot(lhs, rhs):
    return lax.dot_general(lhs, rhs, (((1,), (0,)), ((), ())), preferred_element_type=F32)


def _peer_dense_kernel(xt_ref, xn_ref, u_ref, vt_ref, c1_ref, e1_ref, r2_ref, e2_ref, y_ref,
                       acc_scr, act_scr):
    j = pl.program_id(1)
    nh = PEER_HEADS
    tn = xn_ref.shape[1]
    pack = 2 * SUBLANES
    i1_per_blk = u_ref.shape[0] // N_KEYS

    @pl.when(j == 0)
    def _():
        acc_scr[...] = jnp.zeros_like(acc_scr)

    wide = 2 * LANES

    def gate_block(il, nc):
        erows = pl.ds(il * N_KEYS, N_KEYS)
        hblk = _mxu_dot(u_ref[erows, :], xn_ref[:, pl.ds(nc * wide, wide)])
        for lw in range(wide // LANES):
            lc = nc * (wide // LANES) + lw
            lanes = pl.ds(lc * LANES, LANES)
            c1b = [jnp.broadcast_to(c1_ref[pl.ds(il * nh + h, 1), lanes], (pack, LANES)).astype(BF16)
                   for h in range(nh)]
            e1b = [jnp.broadcast_to(e1_ref[pl.ds(il * nh + h, 1), lanes], (pack, LANES)).astype(BF16)
                   for h in range(nh)]
            zero = jnp.zeros((pack, LANES), BF16)
            half = jnp.full((), 0.5, BF16)
            gelu_c0 = jnp.full((), GELU_C0, BF16)
            gelu_c1 = jnp.full((), GELU_C1, BF16)
            for it in range(N_KEYS // pack):
                rows = pl.ds(il * N_KEYS + it * pack, pack)
                x = hblk[it * pack:(it + 1) * pack, lw * LANES:(lw + 1) * LANES].astype(BF16)
                t = jnp.tanh(x * (gelu_c0 + gelu_c1 * (x * x)))
                act = x * (half + half * t)
                gate = None
                for h in range(nh):
                    krows = pl.ds(h * N_KEYS + it * pack, pack)
                    sel = jnp.minimum(jnp.maximum(c1b[h] - r2_ref[krows, lanes], zero), e2_ref[krows, lanes])
                    term = sel * e1b[h]
                    gate = term if gate is None else gate + term
                act_scr[rows, lanes] = act * gate

    for nc in range(tn // wide):
        for il in range(i1_per_blk):
            gate_block(il, nc)
        cols = pl.ds(nc * wide, wide)
        acc_scr[:, cols] += _mxu_dot(vt_ref[...], act_scr[:, cols])

    @pl.when(j == pl.num_programs(1) - 1)
    def _():
        y_ref[...] = (xt_ref[...] + acc_scr[...]).T


def peer_dense(xt, xn_t, u_layers, layer, vt_bf16, c1, e1, r2, e2):
    d, n = xt.shape
    ne = u_layers.shape[1]
    tn = _pick(n, (512, 256, 128))
    te = PEER_EXPERT_BLOCK
    i1_per_blk = te // N_KEYS
    tok = lambda i, j: (0, i)
    return pl.pallas_call(
        _peer_dense_kernel,
        grid=(n // tn, ne // te),
        in_specs=[
            pl.BlockSpec((d, tn), tok),
            pl.BlockSpec((d, tn), tok),
            pl.BlockSpec((None, te, d), lambda i, j: (layer, j, 0)),
            pl.BlockSpec((d, te), lambda i, j: (0, j)),
            pl.BlockSpec((i1_per_blk * PEER_HEADS, tn), lambda i, j: (j, i)),
            pl.BlockSpec((i1_per_blk * PEER_HEADS, tn), lambda i, j: (j, i)),
            pl.BlockSpec((PEER_HEADS * N_KEYS, tn), tok),
            pl.BlockSpec((PEER_HEADS * N_KEYS, tn), tok),
        ],
        out_specs=pl.BlockSpec((tn, d), lambda i, j: (i, 0)),
        out_shape=jax.ShapeDtypeStruct((n, d), F32),
        scratch_shapes=[pltpu.VMEM((d, tn), F32), pltpu.VMEM((te, tn), BF16)],
        compiler_params=_params("parallel", "arbitrary"),
        name="peer_dense",
    )(xt, xn_t, u_layers, vt_bf16, c1, e1, r2, e2)


def _interleaved_keys(keys_p):
    nh, nk, half = keys_p.shape
    eye = jnp.eye(nh, dtype=keys_p.dtype)
    return jnp.einsum("hkd,hg->khgd", keys_p, eye).reshape(nk * nh, nh * half).astype(BF16)


def peer_query_weights(wq):
    d = wq.shape[0]
    wq_t = wq.reshape(d, PEER_HEADS, 2, PEER_HALF).transpose(2, 1, 3, 0).reshape(2 * PEER_HEADS * PEER_HALF, d)
    return wq_t.astype(BF16)


def peer_layer(xt, xn_t, q_t, keys, u_layers, layer, v_tab):
    k1 = _interleaved_keys(keys[:, 0])
    k2 = _interleaved_keys(keys[:, 1])
    c1, e1, r2, e2 = peer_select(q_t, k1, k2)
    return peer_dense(xt, xn_t, u_layers, layer, v_tab.T.astype(BF16), c1, e1, r2, e2)


def _rmsnorm_kernel(x_ref, g_ref, y_ref):
    x = x_ref[...]
    ms = jnp.mean(x * x, axis=-1, keepdims=True)
    y_ref[...] = (x * lax.rsqrt(ms + EPS)) * g_ref[...]


def final_norm(x, row0, rows, g):
    d = x.shape[1]
    tm = _pick(math.gcd(rows, row0) if row0 else rows, (512, 256, 128, 64, 32, 16, 8))
    blk0 = row0 // tm
    return pl.pallas_call(
        _rmsnorm_kernel,
        grid=(rows // tm,),
        in_specs=[pl.BlockSpec((tm, d), lambda i: (blk0 + i, 0)), pl.BlockSpec((1, d), lambda i: (0, 0))],
        out_specs=pl.BlockSpec((tm, d), lambda i: (i, 0)),
        out_shape=jax.ShapeDtypeStruct((rows, d), F32),
        compiler_params=_params("parallel"),
        name="final_norm",
    )(x, g.reshape(1, d))


def _trunk(groups, norm1_g, norm2_g, normf_g, w_in, ret_gn_g, w_ret_out, conv_w, conv_b, rg_wa, rg_ba, rg_wx,
           rg_bx, rg_lambda, w_rnn_out, w_o, peer_wq, peer_keys, peer_u, peer_v):
    shapes = [(g[0].shape[0], g[0].shape[1]) for g in groups]
    x_parts = [g[0].reshape(-1, D_MODEL) for g in groups]
    row0s = np.cumsum([0] + [b * t for b, t in shapes]).tolist()
    states = [([], [], []) for _ in groups]
    for l in range(DEPTH):
        p = norm_matmul(x_parts, norm1_g[l], w_in[l].astype(BF16))
        wa4 = _block_diag_tiles(rg_wa[l], 256)
        wx4 = _block_diag_tiles(rg_wx[l], 256)
        ret_parts, rnn_parts = [], []
        for gi, (xg, r0, h0, buf0, pos0) in enumerate(groups):
            b, t = shapes[gi]
            row0 = row0s[gi]
            o, r_new = retention_path(p, row0, b, t, pos0, r0, l, ret_gn_g[l])
            rg_args = (conv_w[l], conv_b[l], wa4, rg_ba[l], wx4, rg_bx[l], rg_lambda[l])
            hg, h_last, nb = rglru_rows_path(p, row0, b, t, buf0[l], h0[l], *rg_args)
            ret_parts.append(o)
            rnn_parts.append(hg)
            states[gi][0].append(r_new)
            states[gi][1].append(h_last)
            states[gi][2].append(nb)
        xt, xn_t, q_t = merge_proj(x_parts, ret_parts, rnn_parts, p, w_ret_out[l].astype(BF16),
                                   w_rnn_out[l].astype(BF16), w_o[l].astype(BF16), norm2_g[l],
                                   peer_query_weights(peer_wq[l]))
        x = peer_layer(xt, xn_t, q_t, peer_keys[l], peer_u, l, peer_v[l])
        x_parts = [x]
    outs = []
    for gi, (b, t) in enumerate(shapes):
        y = final_norm(x, row0s[gi], b * t, normf_g).reshape(b, t, D_MODEL)
        outs.append((y, jnp.stack(states[gi][0]), jnp.stack(states[gi][1]), jnp.stack(states[gi][2])))
    return outs


def kernel(x_prompt, x_sample, state_ret, state_rnn, state_conv, norm1_g, norm2_g, normf_g, w_in, ret_gn_g,
           w_ret_out, conv_w, conv_b, rg_wa, rg_ba, rg_wx, rg_bx, rg_lambda, w_rnn_out, w_o, peer_wq, peer_keys,
           peer_u, peer_v):
    bp = x_prompt.shape[0]
    dt = x_prompt.dtype
    zr = jnp.zeros((DEPTH, bp, RET_HEADS, RET_DK, RET_DV), dt)
    zh = jnp.zeros((DEPTH, bp, D_RNN), dt)
    zc = jnp.zeros((DEPTH, bp, CONV_W - 1, D_RNN), dt)
    groups = [(x_prompt, zr, zh, zc, 0.0), (x_sample, state_ret, state_rnn, state_conv, float(PAST_LEN))]
    (yp, rp, hp, cp), (ys, rs, hs, cs) = _trunk(
        groups, norm1_g, norm2_g, normf_g, w_in, ret_gn_g, w_ret_out, conv_w, conv_b, rg_wa, rg_ba, rg_wx, rg_bx,
        rg_lambda, w_rnn_out, w_o, peer_wq, peer_keys, peer_u, peer_v)
    return (yp, ys, rp, hp, cp, rs, hs, cs)
```
